```python
import math
import functools
import jax
import jax.numpy as jnp
from jax import lax
import numpy as np

D_MODEL = 1024
BATCH = 8
SEQ = 2048
DEPTH = 4
DEC_BATCH = 128
DEC_SEQ = 4
PAST_LEN = 2048
PAGE_SIZE = 128

NSA_HEADS = 8
NSA_KV_HEADS = 2
NSA_HPG = NSA_HEADS // NSA_KV_HEADS
NSA_HEAD_DIM = 64
CMP_BLOCK = 32
SEL_BLOCK = 64
N_SELECT = 16
WINDOW = 512
SEL_Q_BLOCK = 64
WIN_Q_BLOCK = 128
FORCE_SCORE = 1e4
REL_BUCKETS = 32
REL_MAX_DIST = 128
LRU_WIDTH = 512
LRU_BLOCKS = 8
LRU_BLOCK_DIM = LRU_WIDTH // LRU_BLOCKS
CONV_WIDTH = 4
LRU_C = 8.0
GLA_HEADS = 4
GLA_DK = 64
GLA_DV = 128
GLA_RANK = 16
GLA_TAU = 16.0
GLA_CHUNK = 64
D_FF = 4 * D_MODEL
N_BRANCH = 3
BRANCH_WIDTH = 512
NORM_EPS = 1e-6

PROJ_SPLITS = (
    ('nsa_q', NSA_HEADS * NSA_HEAD_DIM),
    ('nsa_cmp_kv', 2 * NSA_KV_HEADS * NSA_HEAD_DIM),
    ('nsa_sel_kv', 2 * NSA_KV_HEADS * NSA_HEAD_DIM),
    ('nsa_win_kv', 2 * NSA_KV_HEADS * NSA_HEAD_DIM),
    ('nsa_gate', 3 * NSA_HEADS),
    ('lru_x', LRU_WIDTH),
    ('lru_gate', LRU_WIDTH),
    ('gla_q', GLA_HEADS * GLA_DK),
    ('gla_k', GLA_HEADS * GLA_DK),
    ('gla_v', GLA_HEADS * GLA_DV),
    ('gla_alpha', GLA_RANK),
    ('gla_gate', GLA_HEADS * GLA_DV),
    ('merge_gate', N_BRANCH * D_MODEL),
)
PROJ_WIDTH = (NSA_HEADS * NSA_HEAD_DIM + 6 * NSA_KV_HEADS * NSA_HEAD_DIM + 3 * NSA_HEADS
              + 2 * LRU_WIDTH + 2 * GLA_HEADS * GLA_DK + 2 * GLA_HEADS * GLA_DV + GLA_RANK
              + N_BRANCH * D_MODEL)

kernel_name = 'hybrid_nsa_rglru_gla_decoder_step'


def _rmsnorm(x, g):
    xf = x.astype(jnp.float32)
    y = xf * lax.rsqrt(jnp.mean(xf * xf, axis=-1, keepdims=True) + NORM_EPS)
    return (y * g.astype(jnp.float32)).astype(x.dtype)


def _split_proj(p):
    out = {}
    off = 0
    for name, w in PROJ_SPLITS:
        out[name] = p[..., off:off + w]
        off += w
    return out


def _rel_bucket(dist):
    max_exact = REL_BUCKETS // 2
    n = jnp.maximum(dist, 0)
    nf = jnp.maximum(n, 1).astype(jnp.float32)
    large = max_exact + (jnp.log(nf / max_exact) / math.log(REL_MAX_DIST / max_exact)
                         * (REL_BUCKETS - max_exact)).astype(jnp.int32)
    return jnp.where(n < max_exact, n, jnp.minimum(large, REL_BUCKETS - 1))


def _bias_table(rel_bias):
    return rel_bias.astype(jnp.float32).reshape(REL_BUCKETS, NSA_KV_HEADS, NSA_HPG)


def _masked_softmax(s, mask):
    s = jnp.where(mask, s.astype(jnp.float32), -jnp.inf)
    m = jnp.max(s, axis=-1, keepdims=True)
    m = jnp.where(jnp.isfinite(m), m, 0.0)
    p = jnp.exp(s - m)
    return p / jnp.maximum(jnp.sum(p, axis=-1, keepdims=True), 1e-30)


def _nsa_compress(rows, pe, w1, w2):
    B, Tk = rows.shape[:2]
    nc = Tk // CMP_BLOCK
    blk = rows.reshape(B, nc, CMP_BLOCK, 2, NSA_KV_HEADS, NSA_HEAD_DIM) + pe[:, :, None, :]
    flat = jnp.transpose(blk, (0, 1, 3, 4, 2, 5)).reshape(B, nc, 2, NSA_KV_HEADS, CMP_BLOCK * NSA_HEAD_DIM)
    hid = jax.nn.gelu(jnp.einsum('bnzgf,zfe->bnzge', flat, w1))
    out = jnp.einsum('bnzge,zef->bnzgf', hid, w2)
    return out[:, :, 0], out[:, :, 1]


def _sel_blocks(rows):
    B, Tk = rows.shape[:2]
    r = rows.reshape(B, Tk // SEL_BLOCK, SEL_BLOCK, 2, NSA_KV_HEADS, NSA_HEAD_DIM)
    k = jnp.transpose(r[:, :, :, 0], (0, 3, 1, 2, 4))
    v = jnp.transpose(r[:, :, :, 1], (0, 3, 1, 2, 4))
    return k, v


def _nsa_cmp_sel(q, pos_q, k_c, v_c, k_sb, v_sb, rel_bias):
    B, Tq = q.shape[:2]
    nc = k_c.shape[1]
    ns = k_sb.shape[2]
    tbl = _bias_table(rel_bias)
    scale = NSA_HEAD_DIM ** -0.5
    end_c = (jnp.arange(nc, dtype=jnp.int32) + 1) * CMP_BLOCK - 1
    dist_c = pos_q[:, None] - end_c[None, :]
    bias_c = jnp.transpose(tbl[_rel_bucket(dist_c)], (2, 3, 0, 1))
    s_c = jnp.einsum('btghd,bngd->bghtn', q, k_c) * scale + bias_c
    p_c = _masked_softmax(s_c, dist_c >= 0)
    o_c = jnp.einsum('bghtn,bngd->btghd', p_c.astype(v_c.dtype), v_c)
    imp = jnp.sum(p_c, axis=2).reshape(B, NSA_KV_HEADS, Tq, ns, SEL_BLOCK // CMP_BLOCK).sum(-1)
    blk = jnp.arange(ns, dtype=jnp.int32)[None, :]
    cur = (pos_q // SEL_BLOCK)[:, None]
    forced = (blk == 0) | (blk == cur) | (blk == cur - 1)
    imp = jnp.where(forced, FORCE_SCORE, jnp.where(blk <= cur, imp, -jnp.inf))
    _, idx = lax.top_k(imp, min(N_SELECT, ns))
    kk = idx.shape[-1]
    b_ix = jnp.arange(B)[:, None, None, None]
    g_ix = jnp.arange(NSA_KV_HEADS)[None, :, None, None]
    k_g = k_sb[b_ix, g_ix, idx]
    v_g = v_sb[b_ix, g_ix, idx]
    pos_k = idx[..., None] * SEL_BLOCK + jnp.arange(SEL_BLOCK, dtype=jnp.int32)
    dist_s = pos_q[None, None, :, None, None] - pos_k
    bias_s = jnp.moveaxis(tbl[_rel_bucket(dist_s), g_ix[..., None]], -1, 2)
    s_s = jnp.einsum('btghd,bgtkld->bghtkl', q, k_g) * scale + bias_s
    p_s = _masked_softmax(s_s.reshape(B, NSA_KV_HEADS, NSA_HPG, Tq, kk * SEL_BLOCK),
                          (dist_s >= 0).reshape(B, NSA_KV_HEADS, 1, Tq, kk * SEL_BLOCK))
    p_s = p_s.reshape(B, NSA_KV_HEADS, NSA_HPG, Tq, kk, SEL_BLOCK)
    o_s = jnp.einsum('bghtkl,bgtkld->btghd', p_s.astype(v_g.dtype), v_g)
    return o_c, o_s


def _nsa_window(q, pos_q, k, v, pos_k, rel_bias):
    tbl = _bias_table(rel_bias)
    dist = pos_q[:, None] - pos_k[None, :]
    mask = (dist >= 0) & (dist < WINDOW) & (pos_k[None, :] >= 0)
    bias = jnp.transpose(tbl[_rel_bucket(dist)], (2, 3, 0, 1))
    s = jnp.einsum('btghd,bsgd->bghts', q, k) * (NSA_HEAD_DIM ** -0.5) + bias
    p = _masked_softmax(s, mask)
    return jnp.einsum('bghts,bsgd->btghd', p.astype(v.dtype), v)


def _nsa_gate_merge(pr, o_c, o_s, o_w):
    B, T = o_c.shape[:2]
    g = jax.nn.sigmoid(pr['nsa_gate'].reshape(B, T, 3, NSA_KV_HEADS, NSA_HPG, 1))
    o = g[:, :, 0] * o_c + g[:, :, 1] * o_s + g[:, :, 2] * o_w
    return o.reshape(B, T, NSA_HEADS * NSA_HEAD_DIM)


def _unblock(o, B, T):
    return jnp.moveaxis(o, 0, 1).reshape(B, T, NSA_KV_HEADS, NSA_HPG, NSA_HEAD_DIM)


def _nsa_prompt(pr, lw, rel_bias):
    B, T = pr['nsa_q'].shape[:2]
    q = pr['nsa_q'].reshape(B, T, NSA_KV_HEADS, NSA_HPG, NSA_HEAD_DIM)
    cmp_rows = pr['nsa_cmp_kv'].reshape(B, T, 2, NSA_KV_HEADS, NSA_HEAD_DIM)
    sel_rows = pr['nsa_sel_kv'].reshape(B, T, 2, NSA_KV_HEADS, NSA_HEAD_DIM)
    win_rows = pr['nsa_win_kv'].reshape(B, T, 2, NSA_KV_HEADS, NSA_HEAD_DIM)
    k_c, v_c = _nsa_compress(cmp_rows, lw['cmp_pe'], lw['cmp_w1'], lw['cmp_w2'])
    k_sb, v_sb = _sel_blocks(sel_rows)

    def cmp_sel_block(i):
        start = i * SEL_Q_BLOCK
        qb = lax.dynamic_slice_in_dim(q, start, SEL_Q_BLOCK, axis=1)
        pos = start + jnp.arange(SEL_Q_BLOCK, dtype=jnp.int32)
        return _nsa_cmp_sel(qb, pos, k_c, v_c, k_sb, v_sb, rel_bias)

    o_c, o_s = lax.map(cmp_sel_block, jnp.arange(T // SEL_Q_BLOCK, dtype=jnp.int32))
    kvp = jnp.pad(win_rows, ((0, 0), (WINDOW, 0), (0, 0), (0, 0), (0, 0)))

    def win_block(i):
        start = i * WIN_Q_BLOCK
        qb = lax.dynamic_slice_in_dim(q, start, WIN_Q_BLOCK, axis=1)
        kvb = lax.dynamic_slice_in_dim(kvp, start, WIN_Q_BLOCK + WINDOW, axis=1)
        pos_q = start + jnp.arange(WIN_Q_BLOCK, dtype=jnp.int32)
        pos_k = start - WINDOW + jnp.arange(WIN_Q_BLOCK + WINDOW, dtype=jnp.int32)
        return _nsa_window(qb, pos_q, kvb[:, :, 0], kvb[:, :, 1], pos_k, rel_bias)

    o_w = lax.map(win_block, jnp.arange(T // WIN_Q_BLOCK, dtype=jnp.int32))
    o = _nsa_gate_merge(pr, _unblock(o_c, B, T), _unblock(o_s, B, T), _unblock(o_w, B, T))
    return o, (cmp_rows, sel_rows, win_rows[:, T - min(WINDOW, T):])


def _nsa_sample(pr, lw, rel_bias, cache_cmp, cache_sel, win_buf, page_table):
    DB, S = pr['nsa_q'].shape[:2]
    past_len = page_table.shape[1] * cache_cmp.shape[1]
    tk = -(-(past_len + S) // SEL_BLOCK) * SEL_BLOCK
    q = pr['nsa_q'].reshape(DB, S, NSA_KV_HEADS, NSA_HPG, NSA_HEAD_DIM)
    cmp_rows = pr['nsa_cmp_kv'].reshape(DB, S, 2, NSA_KV_HEADS, NSA_HEAD_DIM)
    sel_rows = pr['nsa_sel_kv'].reshape(DB, S, 2, NSA_KV_HEADS, NSA_HEAD_DIM)
    win_rows = pr['nsa_win_kv'].reshape(DB, S, 2, NSA_KV_HEADS, NSA_HEAD_DIM)

    def full_rows(cache, new):
        past = cache[page_table].reshape(DB, past_len, 2, NSA_KV_HEADS, NSA_HEAD_DIM).astype(new.dtype)
        rows = jnp.concatenate([past, new], axis=1)
        return jnp.pad(rows, ((0, 0), (0, tk - past_len - S), (0, 0), (0, 0), (0, 0)))

    k_c, v_c = _nsa_compress(full_rows(cache_cmp, cmp_rows), lw['cmp_pe'], lw['cmp_w1'], lw['cmp_w2'])
    k_sb, v_sb = _sel_blocks(full_rows(cache_sel, sel_rows))
    pos_q = past_len + jnp.arange(S, dtype=jnp.int32)
    o_c, o_s = _nsa_cmp_sel(q, pos_q, k_c, v_c, k_sb, v_sb, rel_bias)
    w_buf = win_buf.shape[1]
    kv_w = jnp.concatenate([win_buf.astype(win_rows.dtype), win_rows], axis=1)
    pos_k = past_len - w_buf + jnp.arange(w_buf + S, dtype=jnp.int32)
    o_w = _nsa_window(q, pos_q, kv_w[:, :, 0], kv_w[:, :, 1], pos_k, rel_bias)
    o = _nsa_gate_merge(pr, o_c, o_s, o_w)
    return o, (cmp_rows, sel_rows, kv_w[:, S:])


def _linear_combine(e1, e2):
    a1, b1 = e1
    a2, b2 = e2
    return a1 * a2, a2 * b1 + b2


def _rglru(xb, gb, conv0, h0, lw):
    B, T, R = xb.shape
    f32 = jnp.float32
    xin = jnp.concatenate([conv0.astype(xb.dtype), xb], axis=1)
    w = lw['lru_conv_w']
    xc = lw['lru_conv_b'] + xin[:, 0:T] * w[0]
    for j in range(1, CONV_WIDTH):
        xc = xc + xin[:, j:j + T] * w[j]
    gates = jnp.einsum('btnc,znce->zbtne', xc.reshape(B, T, LRU_BLOCKS, LRU_BLOCK_DIM),
                       lw['lru_gate_w']).reshape(2, B, T, R) + lw['lru_gate_b'][:, None, None, :]
    gates = gates.astype(f32)
    r = jax.nn.sigmoid(gates[0])
    i = jax.nn.sigmoid(gates[1])
    log_a = -LRU_C * r * jax.nn.softplus(-lw['lru_lambda'].astype(f32))
    a = jnp.exp(log_a)
    b = jnp.sqrt(-jnp.expm1(2.0 * log_a)) * (i * xc.astype(f32))
    a_cum, b_cum = lax.associative_scan(_linear_combine, (a, b), axis=1)
    h = a_cum * h0.astype(f32)[:, None, :] + b_cum
    out = (jax.nn.gelu(gb.astype(f32)) * h).astype(xb.dtype)
    return out, xin[:, T:], h[:, -1].astype(h0.dtype)


def _gla(q, k, v, alr, og, s0, lw):
    B, T = q.shape[:2]
    f32 = jnp.float32
    q = q.astype(f32).reshape(B, T, GLA_HEADS, GLA_DK) * (GLA_DK ** -0.5)
    k = k.astype(f32).reshape(B, T, GLA_HEADS, GLA_DK)
    v = v.astype(f32).reshape(B, T, GLA_HEADS, GLA_DV)
    g = jax.nn.log_sigmoid((jnp.einsum('btr,re->bte', alr, lw['gla_alpha_w'])
                            + lw['gla_alpha_b']).astype(f32)) / GLA_TAU
    g = g.reshape(B, T, GLA_HEADS, GLA_DK)
    c = GLA_CHUNK if T >= GLA_CHUNK else T
    tp = -(-T // c) * c
    n = tp // c
    pad = ((0, 0), (0, tp - T), (0, 0), (0, 0))
    q, k, v, g = [jnp.pad(t, pad).reshape(B, n, c, GLA_HEADS, t.shape[-1]) for t in (q, k, v, g)]
    bc = jnp.cumsum(g, axis=2)
    bl = bc[:, :, -1]
    qi = q * jnp.exp(bc)
    ki = k * jnp.exp(-bc)
    kd = k * jnp.exp(bl[:, :, None] - bc)
    att = jnp.einsum('bnthk,bnshk->bnhts', qi, ki)
    att = jnp.where(jnp.tril(jnp.ones((c, c), dtype=bool)), att, 0.0)
    o = jnp.einsum('bnhts,bnshv->bnthv', att, v)
    ds = jnp.einsum('bnshk,bnshv->bnhkv', kd, v)

    def step(s, inp):
        ds_c, dec_c = inp
        return dec_c[..., None] * s + ds_c, s

    s_fin, s_prev = lax.scan(step, s0.astype(f32), (jnp.moveaxis(ds, 1, 0), jnp.moveaxis(jnp.exp(bl), 1, 0)))
    o = o + jnp.einsum('bnthk,bnhkv->bnthv', qi, jnp.moveaxis(s_prev, 0, 1))
    o = o.reshape(B, tp, GLA_HEADS, GLA_DV)[:, :T]
    o = _rmsnorm(o, lw['gla_norm_g'].reshape(GLA_HEADS, GLA_DV))
    o = o * jax.nn.silu(og.astype(f32).reshape(B, T, GLA_HEADS, GLA_DV))
    return o.reshape(B, T, GLA_HEADS * GLA_DV).astype(og.dtype), s_fin.astype(s0.dtype)


def _merge(o_a, o_b, o_c, gate_pre, w_branch, w_out):
    B, T = o_a.shape[:2]
    u = jnp.einsum('btzc,zcd->btzd', jnp.stack([o_a, o_b, o_c], axis=2), w_branch)
    g = jax.nn.sigmoid(gate_pre.reshape(B, T, N_BRANCH, D_MODEL))
    return jnp.einsum('btd,de->bte', jnp.sum(g * u, axis=2), w_out)


def _mlp(h, w1, w2):
    a = jax.nn.relu(jnp.einsum('btd,df->btf', h, w1))
    return jnp.einsum('btf,fd->btd', a * a, w2)


def _layer(x, lw, rel_bias, nsa_fn, conv0, h0, s0):
    h = _rmsnorm(x, lw['norm_mix_g'])
    pr = _split_proj(jnp.einsum('btd,de->bte', h, lw['w_in']) + lw['b_in'])
    o_a, nsa_state = nsa_fn(pr, lw, rel_bias)
    o_b, conv_new, h_new = _rglru(pr['lru_x'], pr['lru_gate'], conv0, h0, lw)
    o_c, s_new = _gla(pr['gla_q'], pr['gla_k'], pr['gla_v'], pr['gla_alpha'], pr['gla_gate'], s0, lw)
    x = x + _merge(o_a, o_b, o_c, pr['merge_gate'], lw['w_branch'], lw['w_out'])
    x = x + _mlp(_rmsnorm(x, lw['norm_mlp_g']), lw['mlp_w1'], lw['mlp_w2'])
    return x, nsa_state[0], nsa_state[1], nsa_state[2], conv_new, h_new, s_new


def setup_inputs(seed: int = 0) -> dict:
    key = jax.random.key(seed)
    ks = jax.random.split(key, 32)
    d = NSA_HEAD_DIM
    G = NSA_KV_HEADS

    def nrm(k, shape, scale):
        return jax.random.normal(k, shape, jnp.float32) * scale

    n_pages = PAST_LEN // PAGE_SIZE
    n_used = DEC_BATCH * n_pages
    n_pool = (5 * n_used) // 4
    w_buf = min(WINDOW, PAST_LEN)
    page_table = jax.random.permutation(ks[8], n_pool)[:n_used].reshape(DEC_BATCH, n_pages).astype(jnp.int32)
    a0 = jax.random.uniform(ks[15], (DEPTH, LRU_WIDTH), jnp.float32, 0.9, 0.999)
    sig = a0 ** (1.0 / LRU_C)
    lru_lambda = jnp.log(sig) - jnp.log1p(-sig)
    return {
        'x_prompt': nrm(ks[0], (BATCH, SEQ, D_MODEL), 1.0),
        'x_sample': nrm(ks[1], (DEC_BATCH, DEC_SEQ, D_MODEL), 1.0),
        'cache_nsa_cmp_kv': nrm(ks[2], (DEPTH, n_pool, PAGE_SIZE, 2, G, d), 1.0),
        'cache_nsa_sel_kv': nrm(ks[3], (DEPTH, n_pool, PAGE_SIZE, 2, G, d), 1.0),
        'state_nsa_win_kv': nrm(ks[4], (DEPTH, DEC_BATCH, w_buf, 2, G, d), 1.0),
        'state_lru_conv': nrm(ks[5], (DEPTH, DEC_BATCH, CONV_WIDTH - 1, LRU_WIDTH), 1.0),
        'state_lru_h': nrm(ks[6], (DEPTH, DEC_BATCH, LRU_WIDTH), 0.5),
        'state_gla': nrm(ks[7], (DEPTH, DEC_BATCH, GLA_HEADS, GLA_DK, GLA_DV), 0.5),
        'page_table': page_table,
        'rel_bias': nrm(ks[9], (REL_BUCKETS, NSA_HEADS), 0.1),
        'norm_mix_g': 1.0 + nrm(ks[10], (DEPTH, D_MODEL), 0.02),
        'norm_mlp_g': 1.0 + nrm(ks[11], (DEPTH, D_MODEL), 0.02),
        'norm_final_g': 1.0 + nrm(ks[12], (D_MODEL,), 0.02),
        'w_in': nrm(ks[13], (DEPTH, D_MODEL, PROJ_WIDTH), D_MODEL ** -0.5),
        'b_in': nrm(ks[14], (DEPTH, PROJ_WIDTH), 0.01),
        'nsa_cmp_pe': nrm(ks[16], (DEPTH, CMP_BLOCK, 2, d), 0.1),
        'nsa_cmp_w1': nrm(ks[17], (DEPTH, 2, CMP_BLOCK * d, d), (CMP_BLOCK * d) ** -0.5),
        'nsa_cmp_w2': nrm(ks[18], (DEPTH, 2, d, d), d ** -0.5),
        'lru_gate_w': nrm(ks[19], (DEPTH, 2, LRU_BLOCKS, LRU_BLOCK_DIM, LRU_BLOCK_DIM), LRU_BLOCK_DIM ** -0.5),
        'lru_gate_b': nrm(ks[20], (DEPTH, 2, LRU_WIDTH), 0.01),
        'lru_lambda': lru_lambda,
        'lru_conv_w': nrm(ks[21], (DEPTH, CONV_WIDTH, LRU_WIDTH), CONV_WIDTH ** -0.5),
        'lru_conv_b': nrm(ks[22], (DEPTH, LRU_WIDTH), 0.01),
        'gla_alpha_w': nrm(ks[23], (DEPTH, GLA_RANK, GLA_HEADS * GLA_DK), GLA_RANK ** -0.5),
        'gla_alpha_b': nrm(ks[24], (DEPTH, GLA_HEADS * GLA_DK), 0.01),
        'gla_norm_g': 1.0 + nrm(ks[25], (DEPTH, GLA_HEADS * GLA_DV), 0.02),
        'w_branch': nrm(ks[26], (DEPTH, N_BRANCH, BRANCH_WIDTH, D_MODEL), BRANCH_WIDTH ** -0.5),
        'w_out': nrm(ks[27], (DEPTH, D_MODEL, D_MODEL), D_MODEL ** -0.5),
        'mlp_w1': nrm(ks[28], (DEPTH, D_MODEL, D_FF), D_MODEL ** -0.5),
        'mlp_w2': nrm(ks[29], (DEPTH, D_FF, D_MODEL), D_FF ** -0.5),
    }


def reference(x_prompt, x_sample, cache_nsa_cmp_kv, cache_nsa_sel_kv, state_nsa_win_kv, state_lru_conv,
              state_lru_h, state_gla, page_table, rel_bias, norm_mix_g, norm_mlp_g, norm_final_g, w_in, b_in,
              nsa_cmp_pe, nsa_cmp_w1, nsa_cmp_w2, lru_gate_w, lru_gate_b, lru_lambda, lru_conv_w, lru_conv_b,
              gla_alpha_w, gla_alpha_b, gla_norm_g, w_branch, w_out, mlp_w1, mlp_w2):
    B = x_prompt.shape[0]
    xp, xs = x_prompt, x_sample
    conv0_p = jnp.zeros((B, CONV_WIDTH - 1, LRU_WIDTH), state_lru_conv.dtype)
    h0_p = jnp.zeros((B, LRU_WIDTH), state_lru_h.dtype)
    s0_p = jnp.zeros((B, GLA_HEADS, GLA_DK, GLA_DV), state_gla.dtype)
    outs_p = [[] for _ in range(6)]
    outs_s = [[] for _ in range(6)]
    for l in range(DEPTH):
        lw = {
            'norm_mix_g': norm_mix_g[l], 'norm_mlp_g': norm_mlp_g[l], 'w_in': w_in[l], 'b_in': b_in[l],
            'cmp_pe': nsa_cmp_pe[l], 'cmp_w1': nsa_cmp_w1[l], 'cmp_w2': nsa_cmp_w2[l],
            'lru_gate_w': lru_gate_w[l], 'lru_gate_b': lru_gate_b[l], 'lru_lambda': lru_lambda[l],
            'lru_conv_w': lru_conv_w[l], 'lru_conv_b': lru_conv_b[l],
            'gla_alpha_w': gla_alpha_w[l], 'gla_alpha_b': gla_alpha_b[l], 'gla_norm_g': gla_norm_g[l],
            'w_branch': w_branch[l], 'w_out': w_out[l], 'mlp_w1': mlp_w1[l], 'mlp_w2': mlp_w2[l],
        }
        xp, *st_p = _layer(xp, lw, rel_bias, _nsa_prompt, conv0_p, h0_p, s0_p)
        nsa_s = functools.partial(_nsa_sample, cache_cmp=cache_nsa_cmp_kv[l], cache_sel=cache_nsa_sel_kv[l],
                                  win_buf=state_nsa_win_kv[l], page_table=page_table)
        xs, *st_s = _layer(xs, lw, rel_bias, nsa_s, state_lru_conv[l], state_lru_h[l], state_gla[l])
        for j in range(6):
            outs_p[j].append(st_p[j])
            outs_s[j].append(st_s[j])
    y_prompt = _rmsnorm(xp, norm_final_g)
    y_sample = _rmsnorm(xs, norm_final_g)
    cmp_kv_prompt = jnp.stack(outs_p[0])
    cmp_kv_sample = jnp.stack(outs_s[0])
    sel_kv_prompt = jnp.stack(outs_p[1])
    sel_kv_sample = jnp.stack(outs_s[1])
    win_kv_prompt = jnp.stack(outs_p[2])
    win_kv_sample = jnp.stack(outs_s[2])
    lru_conv_prompt = jnp.stack(outs_p[3])
    lru_conv_sample = jnp.stack(outs_s[3])
    lru_h_prompt = jnp.stack(outs_p[4])
    lru_h_sample = jnp.stack(outs_s[4])
    gla_state_prompt = jnp.stack(outs_p[5])
    gla_state_sample = jnp.stack(outs_s[5])
    return (y_prompt, y_sample, cmp_kv_prompt, cmp_kv_sample, sel_kv_prompt, sel_kv_sample,
            win_kv_prompt, win_kv_sample, lru_conv_prompt, lru_conv_sample, lru_h_prompt, lru_h_sample,
            gla_state_prompt, gla_state_sample)
```

```python
import functools
import math

import jax
import jax.numpy as jnp
import numpy as np
from jax import lax
from jax.experimental import pallas as pl
from jax.experimental.pallas import tpu as pltpu

F32 = jnp.float32
BF16 = jnp.bfloat16

D_MODEL = 1024
DEPTH = 4
PAGE_SIZE = 128
NSA_HEADS = 8
NSA_KV_HEADS = 2
NSA_HPG = NSA_HEADS // NSA_KV_HEADS
NSA_HEAD_DIM = 64
CMP_BLOCK = 32
SEL_BLOCK = 64
N_SELECT = 16
WINDOW = 512
FORCE_SCORE = 1e4
REL_BUCKETS = 32
REL_MAX_DIST = 128
LRU_WIDTH = 512
LRU_BLOCKS = 8
LRU_BLOCK_DIM = LRU_WIDTH // LRU_BLOCKS
CONV_WIDTH = 4
LRU_C = 8.0
GLA_HEADS = 4
GLA_DK = 64
GLA_DV = 128
GLA_RANK = 16
GLA_TAU = 16.0
GLA_CHUNK = 64
D_FF = 4 * D_MODEL
N_BRANCH = 3
BRANCH_WIDTH = 512
NORM_EPS = 1e-6

KV_W = 2 * NSA_KV_HEADS * NSA_HEAD_DIM
Q_W = NSA_HEADS * NSA_HEAD_DIM
GROUP_W = NSA_HPG * NSA_HEAD_DIM
GLA_QK_W = GLA_HEADS * GLA_DK
GLA_V_W = GLA_HEADS * GLA_DV
MISC_W = 128
N_GATE = 3 * NSA_HEADS

COL_MERGE = 0
COL_Q = COL_MERGE + N_BRANCH * D_MODEL
COL_LRU_X = COL_Q + Q_W
COL_LRU_G = COL_LRU_X + LRU_WIDTH
COL_GLA_V = COL_LRU_G + LRU_WIDTH
COL_GLA_G = COL_GLA_V + GLA_V_W
COL_CMP = COL_GLA_G + GLA_V_W
COL_SEL = COL_CMP + KV_W
COL_WIN = COL_SEL + KV_W
COL_GLA_Q = COL_WIN + KV_W
COL_GLA_K = COL_GLA_Q + GLA_QK_W
COL_MISC = COL_GLA_K + GLA_QK_W
PROJ_COLS = COL_MISC + MISC_W

_SRC = {}
_off = 0
for _name, _w in (('nsa_q', Q_W), ('nsa_cmp_kv', KV_W), ('nsa_sel_kv', KV_W), ('nsa_win_kv', KV_W),
                  ('nsa_gate', N_GATE), ('lru_x', LRU_WIDTH), ('lru_gate', LRU_WIDTH),
                  ('gla_q', GLA_QK_W), ('gla_k', GLA_QK_W), ('gla_v', GLA_V_W),
                  ('gla_alpha', GLA_RANK), ('gla_gate', GLA_V_W), ('merge_gate', N_BRANCH * D_MODEL)):
    _SRC[_name] = (_off, _w)
    _off += _w
_DST_ORDER = ('merge_gate', 'nsa_q', 'lru_x', 'lru_gate', 'gla_v', 'gla_gate', 'nsa_cmp_kv',
              'nsa_sel_kv', 'nsa_win_kv', 'gla_q', 'gla_k', 'nsa_gate', 'gla_alpha')

NSA_TQ = 128
NSA_LANES = NSA_HPG * NSA_TQ
N_TOEP = 5
VMEM_LIMIT = 56 * 1024 * 1024


def _cparams(sem):
    return pltpu.CompilerParams(dimension_semantics=sem, vmem_limit_bytes=VMEM_LIMIT)


def _gelu(x):
    return x * (0.5 * (1.0 + jnp.tanh(math.sqrt(2.0 / math.pi) * (x + 0.044715 * (x * x * x)))))


def _softplus(x):
    return jnp.maximum(x, 0.0) + jnp.log1p(jnp.exp(-jnp.abs(x)))


def _rms(x, g):
    return x * lax.rsqrt(jnp.mean(x * x, axis=-1, keepdims=True) + NORM_EPS) * g


def _dot(a, b):
    return jnp.dot(a, b, preferred_element_type=F32)


def _dot_nt(a, b):
    return lax.dot_general(a, b, (((1,), (1,)), ((), ())), preferred_element_type=F32)


def _dot_tn(a, b):
    return lax.dot_general(a, b, (((0,), (0,)), ((), ())), preferred_element_type=F32)


def _proj_kernel(x_ref, g_ref, w_ref, b_ref, o_ref, h_ref):
    @pl.when(pl.program_id(1) == 0)
    def _():
        h_ref[...] = _rms(x_ref[...], g_ref[...]).astype(BF16)

    o_ref[...] = _dot(h_ref[...], w_ref[...]) + b_ref[...]


def _proj(x, g, w, b, tm, tn):
    n = x.shape[0]
    return pl.pallas_call(
        _proj_kernel,
        grid=(n // tm, PROJ_COLS // tn),
        in_specs=[pl.BlockSpec((tm, D_MODEL), lambda i, j: (i, 0)),
                  pl.BlockSpec((1, D_MODEL), lambda i, j: (0, 0)),
                  pl.BlockSpec((D_MODEL, tn), lambda i, j: (0, j)),
                  pl.BlockSpec((1, tn), lambda i, j: (0, j))],
        out_specs=pl.BlockSpec((tm, tn), lambda i, j: (i, j)),
        out_shape=jax.ShapeDtypeStruct((n, PROJ_COLS), F32),
        scratch_shapes=[pltpu.VMEM((tm, D_MODEL), BF16)],
        compiler_params=_cparams(("parallel", "arbitrary")),
        name="proj",
    )(x, g, w, b)


def _merge_kernel(x_ref, oa_ref, ob_ref, oc_ref, g0_ref, g1_ref, g2_ref, wb_ref, wo_ref, o_ref):
    m = jax.nn.sigmoid(g0_ref[...]) * _dot(oa_ref[...].astype(BF16), wb_ref[0])
    m = m + jax.nn.sigmoid(g1_ref[...]) * _dot(ob_ref[...].astype(BF16), wb_ref[1])
    m = m + jax.nn.sigmoid(g2_ref[...]) * _dot(oc_ref[...].astype(BF16), wb_ref[2])
    o_ref[...] = x_ref[...] + _dot(m.astype(BF16), wo_ref[...])


def _merge(x, oa, ob, oc, pr, wb, wo, tm):
    n = x.shape[0]
    row = lambda w: pl.BlockSpec((tm, w), lambda i: (i, 0))
    gate = lambda z: pl.BlockSpec((tm, D_MODEL), lambda i: (i, COL_MERGE // D_MODEL + z))
    return pl.pallas_call(
        _merge_kernel,
        grid=(n // tm,),
        in_specs=[row(D_MODEL), row(BRANCH_WIDTH), row(BRANCH_WIDTH), row(BRANCH_WIDTH),
                  gate(0), gate(1), gate(2),
                  pl.BlockSpec((N_BRANCH, BRANCH_WIDTH, D_MODEL), lambda i: (0, 0, 0)),
                  pl.BlockSpec((D_MODEL, D_MODEL), lambda i: (0, 0))],
        out_specs=row(D_MODEL),
        out_shape=jax.ShapeDtypeStruct((n, D_MODEL), F32),
        compiler_params=_cparams(("parallel",)),
        name="merge",
    )(x, oa, ob, oc, pr, pr, pr, wb, wo)


def _mlp_kernel(x_ref, g_ref, w1_ref, w2_ref, o_ref, h_ref, acc_ref):
    f = pl.program_id(1)

    @pl.when(f == 0)
    def _():
        h_ref[...] = _rms(x_ref[...], g_ref[...]).astype(BF16)
        acc_ref[...] = jnp.zeros_like(acc_ref)

    a = jnp.maximum(_dot(h_ref[...], w1_ref[...]), 0.0)
    acc_ref[...] += _dot((a * a).astype(BF16), w2_ref[...])

    @pl.when(f == pl.num_programs(1) - 1)
    def _():
        o_ref[...] = x_ref[...] + acc_ref[...]


def _mlp(x, g, w1, w2, tm, tf):
    n = x.shape[0]
    return pl.pallas_call(
        _mlp_kernel,
        grid=(n // tm, D_FF // tf),
        in_specs=[pl.BlockSpec((tm, D_MODEL), lambda i, f: (i, 0)),
                  pl.BlockSpec((1, D_MODEL), lambda i, f: (0, 0)),
                  pl.BlockSpec((D_MODEL, tf), lambda i, f: (0, f)),
                  pl.BlockSpec((tf, D_MODEL), lambda i, f: (f, 0))],
        out_specs=pl.BlockSpec((tm, D_MODEL), lambda i, f: (i, 0)),
        out_shape=jax.ShapeDtypeStruct((n, D_MODEL), F32),
        scratch_shapes=[pltpu.VMEM((tm, D_MODEL), BF16), pltpu.VMEM((tm, D_MODEL), F32)],
        compiler_params=_cparams(("parallel", "arbitrary")),
        name="mlp",
    )(x, g, w1, w2)


def _norm_kernel(x_ref, g_ref, o_ref):
    o_ref[...] = _rms(x_ref[...], g_ref[...])


def _final_norm(x, g, tm):
    n = x.shape[0]
    return pl.pallas_call(
        _norm_kernel,
        grid=(n // tm,),
        in_specs=[pl.BlockSpec((tm, D_MODEL), lambda i: (i, 0)),
                  pl.BlockSpec((1, D_MODEL), lambda i: (0, 0))],
        out_specs=pl.BlockSpec((tm, D_MODEL), lambda i: (i, 0)),
        out_shape=jax.ShapeDtypeStruct((n, D_MODEL), F32),
        compiler_params=_cparams(("parallel",)),
        name="final_norm",
    )(x, g)


_XB = 8


def _lru_kernel(x_ref, gb_ref, conv0_ref, h0_ref, cw_ref, cb_ref, gw_ref, gbias_ref, lam_ref,
                o_ref, convn_ref, hn_ref, xbuf, hcar, *, tc, tv):
    @pl.when(pl.program_id(1) == 0)
    def _():
        xbuf[0:_XB, :] = jnp.zeros((_XB, LRU_WIDTH), F32)
        xbuf[_XB - 3:_XB, :] = conv0_ref[0]
        hcar[...] = h0_ref[0]

    x = x_ref[0]
    xbuf[_XB:_XB + tc, :] = x
    w = cw_ref[...]
    xc = cb_ref[...] + xbuf[_XB - 3:_XB - 3 + tc, :] * w[0:1]
    xc = xc + xbuf[_XB - 2:_XB - 2 + tc, :] * w[1:2]
    xc = xc + xbuf[_XB - 1:_XB - 1 + tc, :] * w[2:3]
    xc = xc + x * w[3:4]
    tail = xbuf[_XB - 3 + tv:_XB + tv, :]
    convn_ref[0] = tail
    xbuf[_XB - 3:_XB, :] = tail

    xcb = xc.astype(BF16)
    r = jax.nn.sigmoid(_dot(xcb, gw_ref[0]) + gbias_ref[0:1])
    i = jax.nn.sigmoid(_dot(xcb, gw_ref[1]) + gbias_ref[1:2])
    log_a = (-LRU_C * r) * _softplus(-lam_ref[...])
    a = jnp.exp(log_a)
    b = jnp.sqrt(-jnp.tanh(log_a) * (a * a + 1.0)) * (i * xc)

    rows = lax.broadcasted_iota(jnp.int32, (tc, LRU_WIDTH), 0)
    s = 1
    while s < tc:
        a_sh = pltpu.roll(a, s, 0)
        b_sh = pltpu.roll(b, s, 0)
        m = rows >= s
        b = jnp.where(m, a * b_sh + b, b)
        a = jnp.where(m, a * a_sh, a)
        s *= 2
    h = a * hcar[...] + b
    hlast = h[tv - 1:tv]
    hcar[...] = hlast
    hn_ref[0] = hlast
    o_ref[0] = _gelu(gb_ref[0]) * h


def _lru(x_arr, x_blk, gb_arr, gb_blk, conv0, h0, cw, cb, gw, gbias, lam, tc, tv):
    b, t = x_arr.shape[:2]
    r = LRU_WIDTH
    const2 = lambda shape: pl.BlockSpec(shape, lambda i, c: (0, 0))
    return pl.pallas_call(
        functools.partial(_lru_kernel, tc=tc, tv=tv),
        grid=(b, t // tc),
        in_specs=[pl.BlockSpec((1, tc, r), lambda i, c: (i, c, x_blk)),
                  pl.BlockSpec((1, tc, r), lambda i, c: (i, c, gb_blk)),
                  pl.BlockSpec((1, CONV_WIDTH - 1, r), lambda i, c: (i, 0, 0)),
                  pl.BlockSpec((1, 1, r), lambda i, c: (i, 0, 0)),
                  const2((CONV_WIDTH, r)), const2((1, r)),
                  pl.BlockSpec((2, r, r), lambda i, c: (0, 0, 0)),
                  const2((2, r)), const2((1, r))],
        out_specs=[pl.BlockSpec((1, tc, r), lambda i, c: (i, c, 0)),
                   pl.BlockSpec((1, CONV_WIDTH - 1, r), lambda i, c: (i, 0, 0)),
                   pl.BlockSpec((1, 1, r), lambda i, c: (i, 0, 0))],
        out_shape=[jax.ShapeDtypeStruct((b, t, r), F32),
                   jax.ShapeDtypeStruct((b, CONV_WIDTH - 1, r), F32),
                   jax.ShapeDtypeStruct((b, 1, r), F32)],
        scratch_shapes=[pltpu.VMEM((_XB + tc, r), F32), pltpu.VMEM((1, r), F32)],
        compiler_params=_cparams(("parallel", "arbitrary")),
        name="rglru",
    )(x_arr, gb_arr, conv0, h0, cw, cb, gw, gbias, lam)


def _gla_kernel(q_ref, k_ref, v_ref, og_ref, misc_ref, aw_ref, ab_ref, ng_ref, s0_ref,
                o_ref, sn_ref, st, *, tg, ck, tv):
    @pl.when(pl.program_id(1) == 0)
    def _():
        st[...] = s0_ref[0]

    pre = _dot(misc_ref[0].astype(BF16), aw_ref[...]) + ab_ref[...]
    g = -_softplus(-pre) * (1.0 / GLA_TAU)
    if tv < tg:
        g = jnp.where(lax.broadcasted_iota(jnp.int32, g.shape, 0) < tv, g, 0.0)
    q = q_ref[0] * (GLA_DK ** -0.5)
    k = k_ref[0]
    v = v_ref[0]
    og = og_ref[0]
    ng = ng_ref[...]
    rows = lax.broadcasted_iota(jnp.int32, (ck, GLA_QK_W), 0)
    tril = (lax.broadcasted_iota(jnp.int32, (ck, ck), 0) >= lax.broadcasted_iota(jnp.int32, (ck, ck), 1))
    for c in range(tg // ck):
        sl = slice(c * ck, (c + 1) * ck)
        bc = g[sl]
        s = 1
        while s < ck:
            bc = bc + jnp.where(rows >= s, pltpu.roll(bc, s, 0), 0.0)
            s *= 2
        bl = bc[ck - 1:ck]
        e = jnp.exp(bc)
        qi = (q[sl] * e).astype(BF16)
        ki = (k[sl] * jnp.exp(-bc)).astype(BF16)
        kd = (k[sl] * jnp.exp(bl - bc)).astype(BF16)
        dec = jnp.exp(bl)
        vb = v[sl].astype(BF16)
        outs = []
        for h in range(GLA_HEADS):
            ks = slice(h * GLA_DK, (h + 1) * GLA_DK)
            vs = slice(h * GLA_DV, (h + 1) * GLA_DV)
            att = jnp.where(tril, _dot_nt(qi[:, ks], ki[:, ks]), 0.0)
            s_prev = st[h]
            o = _dot(att.astype(BF16), vb[:, vs]) + _dot_nt(qi[:, ks], s_prev.astype(BF16))
            st[h] = s_prev * dec[:, ks] + _dot_tn(vb[:, vs], kd[:, ks])
            o = _rms(o, ng[:, vs])
            ogh = og[sl, vs]
            outs.append(o * (ogh * jax.nn.sigmoid(ogh)))
        o_ref[0, sl, :] = jnp.concatenate(outs, axis=-1)
    sn_ref[0] = st[...]


def _gla(q_arr, q_blk, k_arr, k_blk, v_arr, v_blk, og_arr, og_blk, misc_arr, misc_blk,
         aw, ab, ng, s0t, tg, ck, tv):
    b, t = q_arr.shape[:2]
    col = lambda w, blk: pl.BlockSpec((1, tg, w), lambda i, c: (i, c, blk))
    const2 = lambda shape: pl.BlockSpec(shape, lambda i, c: (0, 0))
    state = pl.BlockSpec((1, GLA_HEADS, GLA_DV, GLA_DK), lambda i, c: (i, 0, 0, 0))
    return pl.pallas_call(
        functools.partial(_gla_kernel, tg=tg, ck=ck, tv=tv),
        grid=(b, t // tg),
        in_specs=[col(GLA_QK_W, q_blk), col(GLA_QK_W, k_blk), col(GLA_V_W, v_blk), col(GLA_V_W, og_blk),
                  col(MISC_W, misc_blk), const2((MISC_W, GLA_QK_W)), const2((1, GLA_QK_W)),
                  const2((1, GLA_V_W)), state],
        out_specs=[pl.BlockSpec((1, tg, GLA_V_W), lambda i, c: (i, c, 0)), state],
        out_shape=[jax.ShapeDtypeStruct((b, t, GLA_V_W), F32),
                   jax.ShapeDtypeStruct((b, GLA_HEADS, GLA_DV, GLA_DK), F32)],
        scratch_shapes=[pltpu.VMEM((GLA_HEADS, GLA_DV, GLA_DK), F32)],
        compiler_params=_cparams(("parallel", "arbitrary")),
        name="gla",
    )(q_arr, k_arr, v_arr, og_arr, misc_arr, aw, ab, ng, s0t)


def _bias_kernel(tbl_ref, dist_ref, o_ref):
    h = pl.program_id(0)
    max_exact = REL_BUCKETS // 2
    n = jnp.maximum(dist_ref[...], 0)
    nf = jnp.maximum(n, 1).astype(F32)
    large = max_exact + (jnp.log(nf / max_exact) / math.log(REL_MAX_DIST / max_exact)
                         * (REL_BUCKETS - max_exact)).astype(jnp.int32)
    bucket = jnp.where(n < max_exact, n, jnp.minimum(large, REL_BUCKETS - 1))
    out = jnp.zeros(bucket.shape, F32)
    for kk in range(REL_BUCKETS):
        out = jnp.where(bucket == kk, tbl_ref[kk, h], out)
    o_ref[0] = out


def _bias_lookup(rel_bias, dist, tr):
    r, c = dist.shape
    return pl.pallas_call(
        _bias_kernel,
        grid=(NSA_HEADS, r // tr),
        in_specs=[pl.BlockSpec(memory_space=pltpu.SMEM),
                  pl.BlockSpec((tr, c), lambda h, i: (i, 0))],
        out_specs=pl.BlockSpec((1, tr, c), lambda h, i: (h, i, 0)),
        out_shape=jax.ShapeDtypeStruct((NSA_HEADS, r, c), F32),
        compiler_params=_cparams(("parallel", "parallel")),
        name="rel_bias",
    )(rel_bias, dist)


def _cmp_kernel(x_ref, pe_ref, w1_ref, w2_ref, o_ref):
    hid = _gelu(_dot((x_ref[0] + pe_ref[0]).astype(BF16), w1_ref[0]))
    o_ref[0] = _dot(hid.astype(BF16), w2_ref[0])


def _row_tile(m, cap):
    for tm in range(min(cap, m) // 8 * 8, 0, -8):
        if m % tm == 0:
            return tm
    return m


def _compress(flat, pe, w1, w2, cap):
    m = flat.shape[1]
    tm = _row_tile(m, cap)
    fw = CMP_BLOCK * NSA_HEAD_DIM
    d = NSA_HEAD_DIM
    return pl.pallas_call(
        _cmp_kernel,
        grid=(2, m // tm),
        in_specs=[pl.BlockSpec((1, tm, fw), lambda z, i: (z, i, 0)),
                  pl.BlockSpec((1, 1, fw), lambda z, i: (z, 0, 0)),
                  pl.BlockSpec((1, fw, d), lambda z, i: (z, 0, 0)),
                  pl.BlockSpec((1, d, d), lambda z, i: (z, 0, 0))],
        out_specs=pl.BlockSpec((1, tm, d), lambda z, i: (z, i, 0)),
        out_shape=jax.ShapeDtypeStruct((2, m, d), F32),
        compiler_params=_cparams(("parallel", "parallel")),
        name="nsa_compress",
    )(flat, pe, w1, w2)


def _nsa_prompt_kernel(q_ref, kc_ref, vct_ref, bc_ref, ks_ref, vst_ref, kw_ref, vwt_ref, bt_ref, gt_ref,
                       o_ref, sel_ref):
    i = pl.program_id(2)
    tq = NSA_TQ
    ln = NSA_LANES
    nsel = kc_ref.shape[2] // 2
    q = (q_ref[0, 0, 0] * (NSA_HEAD_DIM ** -0.5)).astype(BF16)
    t_lane = i * tq + (lax.broadcasted_iota(jnp.int32, (1, ln), 1) & (tq - 1))

    s_c = _dot(kc_ref[0, 0], q) + bc_ref[0, 0]
    r_c = lax.broadcasted_iota(jnp.int32, (2 * nsel, ln), 0)
    n_c = jnp.where(r_c < nsel, 2 * r_c, 2 * (r_c - nsel) + 1)
    s_c = jnp.where(t_lane >= (n_c + 1) * CMP_BLOCK - 1, s_c, -jnp.inf)
    m_c = jnp.max(s_c, axis=0, keepdims=True)
    m_c = jnp.where(m_c > -jnp.inf, m_c, 0.0)
    p_c = jnp.exp(s_c - m_c)
    p_c = p_c / jnp.maximum(jnp.sum(p_c, axis=0, keepdims=True), 1e-30)
    o_c = _dot(vct_ref[0, 0], p_c.astype(BF16))

    ph = p_c[:, 0:tq]
    for h in range(1, NSA_HPG):
        ph = ph + p_c[:, h * tq:(h + 1) * tq]
    imp = ph[0:nsel] + ph[nsel:2 * nsel]
    blk = lax.broadcasted_iota(jnp.int32, (nsel, tq), 0)
    cur = (i * tq + lax.broadcasted_iota(jnp.int32, (nsel, tq), 1)) >> int(math.log2(SEL_BLOCK))
    forced = (blk == 0) | (blk == cur) | (blk == cur - 1)
    imp = jnp.where(forced, FORCE_SCORE, jnp.where(blk <= cur, imp, -jnp.inf))
    rank = jnp.zeros((nsel, tq), jnp.int32)
    for s2 in range(nsel):
        row = imp[s2:s2 + 1]
        beats = (row > imp) | ((row == imp) & (blk > s2))
        rank = rank + beats.astype(jnp.int32)
    chosen = (rank < N_SELECT).astype(F32)
    sel_ref[...] = jnp.concatenate([chosen] * NSA_HPG, axis=1)

    krow = lax.broadcasted_iota(jnp.int32, (SEL_BLOCK, ln), 0)
    t_full = i * tq + (lax.broadcasted_iota(jnp.int32, (SEL_BLOCK, ln), 1) & (tq - 1))
    blocks_per_tile = tq // SEL_BLOCK

    def attend(k_ref, vt_ref, lo, hi, use_sel):
        def body(j, carry):
            m_run, l_run, acc = carry
            e = jnp.clip(blocks_per_tile * i - j + 1, 0, N_TOEP - 1)
            s = _dot(k_ref[0, 0, j], q) + bt_ref[0, e]
            dist = t_full - (j * SEL_BLOCK + krow)
            if use_sel:
                ok = (dist >= 0) & (sel_ref[pl.ds(j, 1), :] > 0.5)
            else:
                ok = (dist >= 0) & (dist < WINDOW)
            s = jnp.where(ok, s, -jnp.inf)
            m_new = jnp.maximum(m_run, jnp.max(s, axis=0, keepdims=True))
            m_safe = jnp.where(m_new > -jnp.inf, m_new, 0.0)
            alpha = jnp.exp(m_run - m_safe)
            p = jnp.exp(s - m_safe)
            l_new = alpha * l_run + jnp.sum(p, axis=0, keepdims=True)
            acc = alpha * acc + _dot(vt_ref[0, 0, j], p.astype(BF16))
            return m_new, l_new, acc

        init = (jnp.full((1, ln), -jnp.inf, F32), jnp.zeros((1, ln), F32),
                jnp.zeros((NSA_HEAD_DIM, ln), F32))
        _, l_fin, acc = lax.fori_loop(lo, hi, body, init)
        return acc / jnp.maximum(l_fin, 1e-30)

    hi = blocks_per_tile * (i + 1)
    o_s = attend(ks_ref, vst_ref, 0, hi, True)
    o_w = attend(kw_ref, vwt_ref, jnp.maximum(blocks_per_tile * i - WINDOW // SEL_BLOCK, 0), hi, False)
    gt = jax.nn.sigmoid(gt_ref[0, 0, 0])
    o_ref[0, 0, 0] = gt[0:1] * o_c + gt[1:2] * o_s + gt[2:3] * o_w


def _nsa_prompt(qs, kc, vct, bias_c, ks, vst, kw, vwt, bias_t, gts):
    b, g, nt = qs.shape[:3]
    d = NSA_HEAD_DIM
    nb = ks.shape[2]
    nc = kc.shape[2]
    ln = NSA_LANES
    tile = lambda rows: pl.BlockSpec((1, 1, 1, rows, ln), lambda bi, gi, i: (bi, gi, i, 0, 0))
    per_bg = lambda r, c: pl.BlockSpec((1, 1, r, c), lambda bi, gi, i: (bi, gi, 0, 0))
    blocks = lambda r, c: pl.BlockSpec((1, 1, nb, r, c), lambda bi, gi, i: (bi, gi, 0, 0, 0))
    return pl.pallas_call(
        _nsa_prompt_kernel,
        grid=(b, g, nt),
        in_specs=[tile(d), per_bg(nc, d), per_bg(d, nc),
                  pl.BlockSpec((1, 1, nc, ln), lambda bi, gi, i: (gi, i, 0, 0)),
                  blocks(SEL_BLOCK, d), blocks(d, SEL_BLOCK), blocks(SEL_BLOCK, d), blocks(d, SEL_BLOCK),
                  pl.BlockSpec((1, N_TOEP, SEL_BLOCK, ln), lambda bi, gi, i: (gi, 0, 0, 0)),
                  tile(8)],
        out_specs=tile(d),
        out_shape=jax.ShapeDtypeStruct((b, g, nt, d, ln), F32),
        scratch_shapes=[pltpu.VMEM((nc // 2, ln), F32)],
        compiler_params=_cparams(("parallel", "parallel", "arbitrary")),
        name="nsa_prompt",
    )(qs, kc, vct, bias_c, ks, vst, kw, vwt, bias_t, gts)


SMP_ROWS = 16
SMP_CPAD = 128
KEY_TILE = PAGE_SIZE


def _masked_softmax(s, ok):
    s = jnp.where(ok, s, -jnp.inf)
    m = jnp.max(s, axis=-1, keepdims=True)
    m = jnp.where(m > -jnp.inf, m, 0.0)
    p = jnp.exp(s - m)
    return p / jnp.maximum(jnp.sum(p, axis=-1, keepdims=True), 1e-30)


def _nsa_sample_kernel(pt_ref, q_ref, kc_ref, vc_ref, bc_ref, okc_ref, cur_ref, *rest, n_pages, n_win, nsel):
    pages = rest[:n_pages]
    (sn_ref, bs_ref, oks_ref, e_ref, wb_ref, wn_ref, bw_ref, okw_ref, gt_ref,
     o_ref, s_scr, w_scr) = rest[n_pages:]
    del pt_ref
    d = NSA_HEAD_DIM
    s_new = q_ref.shape[2] // NSA_HPG
    for g in range(NSA_KV_HEADS):
        kcol = slice(g * d, (g + 1) * d)
        vcol = slice((NSA_KV_HEADS + g) * d, (NSA_KV_HEADS + g + 1) * d)
        q = (q_ref[0, g] * (d ** -0.5)).astype(BF16)

        s_c = _dot_nt(q, kc_ref[0, g].astype(BF16)) + bc_ref[g]
        p_c = _masked_softmax(s_c, okc_ref[...] > 0.5)
        o_c = _dot(p_c.astype(BF16), vc_ref[0, g].astype(BF16))
        ph = p_c[0:s_new]
        for h in range(1, NSA_HPG):
            ph = ph + p_c[h * s_new:(h + 1) * s_new]
        imp = ph[:, 0:SMP_CPAD] + ph[:, SMP_CPAD:2 * SMP_CPAD]
        blk = lax.broadcasted_iota(jnp.int32, imp.shape, 1)
        cur = cur_ref[...]
        forced = (blk == 0) | (blk == cur) | (blk == cur - 1)
        imp = jnp.where(forced, FORCE_SCORE, jnp.where(blk <= cur, imp, -jnp.inf))
        rank = jnp.zeros(imp.shape, jnp.int32)
        for s2 in range(nsel):
            col = imp[:, s2:s2 + 1]
            beats = (col > imp) | ((col == imp) & (blk > s2))
            rank = rank + beats.astype(jnp.int32)
        chosen = ((rank < N_SELECT) & (blk < nsel)).astype(BF16)
        chosen = jnp.concatenate([chosen] * NSA_HPG, axis=0)
        sel_keys = _dot(chosen, e_ref[...])

        for p in range(n_pages):
            s_scr[:, p * KEY_TILE:(p + 1) * KEY_TILE] = _dot_nt(q, pages[p][0, 0, :, kcol].astype(BF16))
        s_scr[:, n_pages * KEY_TILE:(n_pages + 1) * KEY_TILE] = _dot_nt(q, sn_ref[0, :, kcol].astype(BF16))
        p_s = _masked_softmax(s_scr[...] + bs_ref[g], (oks_ref[...] > 0.5) & (sel_keys > 0.5)).astype(BF16)
        o_s = _dot(p_s[:, n_pages * KEY_TILE:(n_pages + 1) * KEY_TILE], sn_ref[0, :, vcol].astype(BF16))
        for p in range(n_pages):
            o_s = o_s + _dot(p_s[:, p * KEY_TILE:(p + 1) * KEY_TILE], pages[p][0, 0, :, vcol].astype(BF16))

        for p in range(n_win):
            w_scr[:, p * KEY_TILE:(p + 1) * KEY_TILE] = _dot_nt(
                q, wb_ref[0, 0, p * KEY_TILE:(p + 1) * KEY_TILE, kcol].astype(BF16))
        w_scr[:, n_win * KEY_TILE:(n_win + 1) * KEY_TILE] = _dot_nt(q, wn_ref[0, :, kcol].astype(BF16))
        p_w = _masked_softmax(w_scr[...] + bw_ref[g], okw_ref[...] > 0.5).astype(BF16)
        o_w = _dot(p_w[:, n_win * KEY_TILE:(n_win + 1) * KEY_TILE], wn_ref[0, :, vcol].astype(BF16))
        for p in range(n_win):
            o_w = o_w + _dot(p_w[:, p * KEY_TILE:(p + 1) * KEY_TILE],
                             wb_ref[0, 0, p * KEY_TILE:(p + 1) * KEY_TILE, vcol].astype(BF16))

        gt = jax.nn.sigmoid(gt_ref[0, g])
        o_ref[0, g] = gt[:, 0:1] * o_c + gt[:, 1:2] * o_s + gt[:, 2:3] * o_w


def _nsa_sample(layer, nsel, page_table, qg, kc, vc, bias_c, ok_c, cur, cache_sel, sel_new, bias_s, ok_s,
                expand, win_state, win_new, bias_w, ok_w, gts):
    b = qg.shape[0]
    g = NSA_KV_HEADS
    d = NSA_HEAD_DIM
    rows = qg.shape[2]
    n_pages = page_table.shape[1]
    n_win = win_state.shape[2] // KEY_TILE
    ks = (n_pages + 1) * KEY_TILE
    kw = (n_win + 1) * KEY_TILE
    per_b = lambda *shape: pl.BlockSpec((1,) + shape, lambda i, pt: (i,) + (0,) * len(shape))
    const = lambda *shape: pl.BlockSpec(shape, lambda i, pt: (0,) * len(shape))

    def page_spec(p):
        return pl.BlockSpec((1, 1, PAGE_SIZE, KV_W), lambda i, pt: (layer, pt[i, p], 0, 0))

    in_specs = ([per_b(g, rows, d), per_b(g, 2 * SMP_CPAD, d), per_b(g, 2 * SMP_CPAD, d),
                 const(g, rows, 2 * SMP_CPAD), const(rows, 2 * SMP_CPAD), const(rows // NSA_HPG, SMP_CPAD)]
                + [page_spec(p) for p in range(n_pages)]
                + [per_b(KEY_TILE, KV_W), const(g, rows, ks), const(rows, ks), const(SMP_CPAD, ks),
                   pl.BlockSpec((1, 1, n_win * KEY_TILE, KV_W), lambda i, pt: (layer, i, 0, 0)),
                   per_b(KEY_TILE, KV_W), const(g, rows, kw), const(rows, kw), per_b(g, rows, 128)])
    return pl.pallas_call(
        functools.partial(_nsa_sample_kernel, n_pages=n_pages, n_win=n_win, nsel=nsel),
        grid_spec=pltpu.PrefetchScalarGridSpec(
            num_scalar_prefetch=1,
            grid=(b,),
            in_specs=in_specs,
            out_specs=pl.BlockSpec((1, g, rows, d), lambda i, pt: (i, 0, 0, 0)),
            scratch_shapes=[pltpu.VMEM((rows, ks), F32), pltpu.VMEM((rows, kw), F32)]),
        out_shape=jax.ShapeDtypeStruct((b, g, rows, d), F32),
        compiler_params=_cparams(("parallel",)),
        name="nsa_sample",
    )(page_table, qg, kc, vc, bias_c, ok_c, cur, *([cache_sel] * n_pages), sel_new, bias_s, ok_s, expand,
      win_state, win_new, bias_w, ok_w, gts)


def _pad_rows(a, rows):
    return jnp.pad(a, ((0, 0), (0, rows - a.shape[1]), (0, 0)))


def _even_odd(n):
    return np.concatenate([np.arange(0, n, 2), np.arange(1, n, 2)])


def _prompt_bias_tables(rel_bias, t):
    g, hpg, tq = NSA_KV_HEADS, NSA_HPG, NSA_TQ
    nt = t // tq
    nc = t // CMP_BLOCK
    end_c = (_even_odd(nc) + 1) * CMP_BLOCK - 1
    dist_c = jnp.asarray(np.arange(t)[None, :] - end_c[:, None], jnp.int32)
    bias_c = _bias_lookup(rel_bias, dist_c, nc)
    bias_c = bias_c.reshape(g, hpg, nc, nt, tq).transpose(0, 3, 2, 1, 4).reshape(g, nt, nc, hpg * tq)
    off = np.arange(-1, N_TOEP - 1)[:, None, None] * SEL_BLOCK
    dist_t = off + np.arange(tq)[None, None, :] - np.arange(SEL_BLOCK)[None, :, None]
    dist_t = jnp.asarray(dist_t.reshape(N_TOEP * SEL_BLOCK, tq), jnp.int32)
    bias_t = _bias_lookup(rel_bias, dist_t, N_TOEP * SEL_BLOCK)
    bias_t = bias_t.reshape(g, hpg, N_TOEP, SEL_BLOCK, tq).transpose(0, 2, 3, 1, 4)
    return bias_c, bias_t.reshape(g, N_TOEP, SEL_BLOCK, hpg * tq)


def _sample_tables(rel_bias, past_len, s_new, n_win_keys):
    g, hpg = NSA_KV_HEADS, NSA_HPG
    tk = -(-(past_len + s_new) // SEL_BLOCK) * SEL_BLOCK
    nc = tk // CMP_BLOCK
    nsel = tk // SEL_BLOCK
    half = (nc + 1) // 2
    pos_q = past_len + np.arange(s_new)
    lane = np.arange(SMP_CPAD)
    n_of_lane = np.concatenate([2 * lane, 2 * lane + 1])
    real_c = np.concatenate([lane < half, lane < nc - half])
    dist_c = pos_q[:, None] - ((n_of_lane[None, :] + 1) * CMP_BLOCK - 1)
    ok_c = real_c[None, :] & (dist_c >= 0)
    n_keys_s = (past_len // PAGE_SIZE + 1) * KEY_TILE
    key = np.arange(n_keys_s)
    dist_s = pos_q[:, None] - key[None, :]
    ok_s = (key[None, :] < past_len + s_new) & (dist_s >= 0)
    n_keys_w = n_win_keys + KEY_TILE
    i = np.arange(n_keys_w)
    pos_kw = np.where(i < n_win_keys, past_len - n_win_keys + i, past_len + i - n_win_keys)
    dist_w = pos_q[:, None] - pos_kw[None, :]
    ok_w = (i[None, :] < n_win_keys + s_new) & (dist_w >= 0) & (dist_w < WINDOW) & (pos_kw[None, :] >= 0)
    dist = np.concatenate([dist_c, dist_s, dist_w], axis=1)
    dist = np.pad(dist, ((0, 8 - s_new), (0, 0)))
    bias = _bias_lookup(rel_bias, jnp.asarray(dist, jnp.int32), 8)[:, :s_new]
    bias = bias.reshape(g, hpg * s_new, dist.shape[1])
    c0, c1 = 2 * SMP_CPAD, 2 * SMP_CPAD + n_keys_s
    tile = lambda m: jnp.asarray(np.tile(m, (hpg, 1)), F32)
    expand = (key[None, :] // SEL_BLOCK == np.arange(SMP_CPAD)[:, None]) & (np.arange(SMP_CPAD)[:, None] < nsel)
    cur = np.broadcast_to((pos_q // SEL_BLOCK)[:, None], (s_new, SMP_CPAD))
    return dict(bias_c=bias[:, :, :c0], bias_s=bias[:, :, c0:c1], bias_w=bias[:, :, c1:],
                ok_c=tile(ok_c), ok_s=tile(ok_s), ok_w=tile(ok_w),
                expand=jnp.asarray(expand, BF16), cur=jnp.asarray(cur, jnp.int32), nc=nc, half=half, tk=tk)


def _flat_blocks(rows):
    b, tk = rows.shape[:2]
    nc = tk // CMP_BLOCK
    blk = rows.reshape(b, nc, CMP_BLOCK, 2, NSA_KV_HEADS, NSA_HEAD_DIM)
    return jnp.transpose(blk, (3, 0, 1, 4, 2, 5)).reshape(2, b * nc * NSA_KV_HEADS, CMP_BLOCK * NSA_HEAD_DIM)


def _nsa_prompt_layer(pr3, lw, tabs):
    b, t = pr3.shape[:2]
    g, hpg, d, tq = NSA_KV_HEADS, NSA_HPG, NSA_HEAD_DIM, NSA_TQ
    nt = t // tq
    nc = t // CMP_BLOCK
    rows = lambda c: pr3[:, :, c:c + KV_W].reshape(b, t, 2, g, d)
    cmp_rows, sel_rows, win_rows = rows(COL_CMP), rows(COL_SEL), rows(COL_WIN)
    kcv = _compress(_flat_blocks(cmp_rows), lw['pe'], lw['cw1'], lw['cw2'], 512).reshape(2, b, nc, g, d)
    kcv = kcv[:, :, _even_odd(nc)]
    kc = kcv[0].transpose(0, 2, 1, 3).astype(BF16)
    vct = kcv[1].transpose(0, 2, 3, 1).astype(BF16)
    qs = pr3[:, :, COL_Q:COL_Q + Q_W].reshape(b, nt, tq, g, hpg, d)
    qs = qs.transpose(0, 3, 1, 5, 4, 2).reshape(b, g, nt, d, hpg * tq)
    nb = t // SEL_BLOCK
    blocked = lambda r, z: r[:, :, z].reshape(b, nb, SEL_BLOCK, g, d).astype(BF16)
    kt = lambda r: blocked(r, 0).transpose(0, 3, 1, 2, 4)
    vt = lambda r: blocked(r, 1).transpose(0, 3, 1, 4, 2)
    gts = pr3[:, :, COL_MISC:COL_MISC + N_GATE].reshape(b, nt, tq, 3, g, hpg)
    gts = gts.transpose(0, 4, 1, 3, 5, 2).reshape(b, g, nt, 3, hpg * tq)
    gts = jnp.pad(gts, ((0, 0), (0, 0), (0, 0), (0, 5), (0, 0)))
    o = _nsa_prompt(qs, kc, vct, tabs[0], kt(sel_rows), vt(sel_rows), kt(win_rows), vt(win_rows), tabs[1], gts)
    o = o.reshape(b, g, nt, d, hpg, tq).transpose(0, 2, 5, 1, 4, 3).reshape(b * t, Q_W)
    return o, cmp_rows, sel_rows, win_rows[:, t - min(WINDOW, t):]


def _nsa_sample_layer(pr3, lw, tabs, layer, cache_cmp, cache_sel, win_state, page_table):
    b, s_new = pr3.shape[:2]
    g, hpg, d = NSA_KV_HEADS, NSA_HPG, NSA_HEAD_DIM
    n_pages = page_table.shape[1]
    past_len = n_pages * PAGE_SIZE
    new = lambda c: pr3[:, :, c:c + KV_W]
    cmp_new, sel_new, win_new = new(COL_CMP), new(COL_SEL), new(COL_WIN)
    past = cache_cmp[layer][page_table].reshape(b, past_len, KV_W)
    full = jnp.concatenate([past, cmp_new, jnp.zeros((b, tabs['tk'] - past_len - s_new, KV_W), F32)], axis=1)
    nc, half = tabs['nc'], tabs['half']
    kcv = _compress(_flat_blocks(full.reshape(b, tabs['tk'], 2, g, d)), lw['pe'], lw['cw1'], lw['cw2'], 512)
    kcv = kcv.reshape(2, b, nc, g, d)
    pad_c = lambda a: jnp.pad(a, ((0, 0), (0, 0), (0, SMP_CPAD - a.shape[2]), (0, 0), (0, 0)))
    kcv = jnp.concatenate([pad_c(kcv[:, :, 0::2]), pad_c(kcv[:, :, 1::2])], axis=2)
    kcv = kcv.transpose(0, 1, 3, 2, 4)
    qg = pr3[:, :, COL_Q:COL_Q + Q_W].reshape(b, s_new, g, hpg, d).transpose(0, 2, 3, 1, 4)
    qg = qg.reshape(b, g, hpg * s_new, d)
    gts = pr3[:, :, COL_MISC:COL_MISC + N_GATE].reshape(b, s_new, 3, g, hpg).transpose(0, 3, 4, 1, 2)
    gts = jnp.pad(gts.reshape(b, g, hpg * s_new, 3), ((0, 0), (0, 0), (0, 0), (0, 128 - 3)))
    o = _nsa_sample(layer, tabs['tk'] // SEL_BLOCK, page_table, qg, kcv[0], kcv[1], tabs['bias_c'], tabs['ok_c'],
                    tabs['cur'],
                    cache_sel, _pad_rows(sel_new, KEY_TILE), tabs['bias_s'], tabs['ok_s'], tabs['expand'],
                    win_state, _pad_rows(win_new, KEY_TILE), tabs['bias_w'], tabs['ok_w'], gts)
    o = o.reshape(b, g, hpg, s_new, d).transpose(0, 3, 1, 2, 4).reshape(b * s_new, Q_W)
    kv5 = lambda a: a.reshape(b, -1, 2, g, d)
    win_out = jnp.concatenate([win_state[layer], win_new], axis=1)[:, s_new:]
    return o, kv5(cmp_new), kv5(sel_new), kv5(win_out)


def _mixers(pr, b, t, lw, conv0, h0, s0, nsa_fn):
    pr3 = pr.reshape(b, t, PROJ_COLS)
    o_a, cmp_rows, sel_rows, win_rows = nsa_fn(pr3)
    lru_w = (lw['lcw'], lw['lcb'], lw['lgw'], lw['lgb'], lw['lam'])
    gla_w = (lw['gaw'], lw['gab'], lw['gng'])
    s0t = jnp.swapaxes(s0, -1, -2)
    if t % 8 == 0:
        tc = min(t, 256)
        o_b, conv_n, h_n = _lru(pr3, COL_LRU_X // LRU_WIDTH, pr3, COL_LRU_G // LRU_WIDTH, conv0, h0[:, None],
                                *lru_w, tc, tc)
        tg = min(t, 256)
        o_c, s_nt = _gla(pr3, COL_GLA_Q // GLA_QK_W, pr3, COL_GLA_K // GLA_QK_W, pr3, COL_GLA_V // GLA_V_W,
                         pr3, COL_GLA_G // GLA_V_W, pr3, COL_MISC // MISC_W, *gla_w, s0t, tg,
                         min(tg, GLA_CHUNK), tg)
    else:
        tp = -(-t // 8) * 8
        cut = lambda c, w: _pad_rows(pr3[:, :, c:c + w], tp)
        o_b, conv_n, h_n = _lru(cut(COL_LRU_X, LRU_WIDTH), 0, cut(COL_LRU_G, LRU_WIDTH), 0, conv0, h0[:, None],
                                *lru_w, tp, t)
        o_c, s_nt = _gla(cut(COL_GLA_Q, GLA_QK_W), 0, cut(COL_GLA_K, GLA_QK_W), 0, cut(COL_GLA_V, GLA_V_W), 0,
                         cut(COL_GLA_G, GLA_V_W), 0, cut(COL_MISC, MISC_W), 0, *gla_w, s0t, tp, tp, t)
        o_b, o_c = o_b[:, :t], o_c[:, :t]
    o_b = o_b.reshape(b * t, LRU_WIDTH)
    o_c = o_c.reshape(b * t, GLA_V_W)
    return (o_a, o_b, o_c), (cmp_rows, sel_rows, win_rows, conv_n, h_n[:, 0], jnp.swapaxes(s_nt, -1, -2))


def _layer(x, b, t, lw, conv0, h0, s0, nsa_fn, tm):
    pr = _proj(x, lw['ng'], lw['w_in'], lw['b_in'], tm, PROJ_COLS // 5)
    (o_a, o_b, o_c), states = _mixers(pr, b, t, lw, conv0, h0, s0, nsa_fn)
    x = _merge(x, o_a, o_b, o_c, pr, lw['wb'], lw['wo'], min(tm, 512))
    x = _mlp(x, lw['mg'], lw['w1'], lw['w2'], tm, 1024)
    return x, states


def kernel(x_prompt, x_sample, cache_nsa_cmp_kv, cache_nsa_sel_kv, state_nsa_win_kv, state_lru_conv,
           state_lru_h, state_gla, page_table, rel_bias, norm_mix_g, norm_mlp_g, norm_final_g, w_in, b_in,
           nsa_cmp_pe, nsa_cmp_w1, nsa_cmp_w2, lru_gate_w, lru_gate_b, lru_lambda, lru_conv_w, lru_conv_b,
           gla_alpha_w, gla_alpha_b, gla_norm_g, w_branch, w_out, mlp_w1, mlp_w2):
    bp, tp = x_prompt.shape[:2]
    bs, ts = x_sample.shape[:2]
    depth = w_in.shape[0]
    n_pool = cache_nsa_cmp_kv.shape[1]
    n_pages = page_table.shape[1]
    w_buf = state_nsa_win_kv.shape[2]

    cols = [w_in[..., _SRC[n][0]:_SRC[n][0] + _SRC[n][1]] for n in _DST_ORDER]
    pad = PROJ_COLS - sum(c.shape[-1] for c in cols)
    w_in_p = jnp.concatenate(cols + [jnp.zeros(w_in.shape[:2] + (pad,), w_in.dtype)], axis=-1).astype(BF16)
    bcols = [b_in[..., _SRC[n][0]:_SRC[n][0] + _SRC[n][1]] for n in _DST_ORDER]
    b_in_p = jnp.concatenate(bcols + [jnp.zeros((depth, pad), b_in.dtype)], axis=-1)[:, None, :]
    eye = jnp.eye(LRU_BLOCKS, dtype=lru_gate_w.dtype)
    lgw = jnp.einsum('lznce,nm->lzncme', lru_gate_w, eye).reshape(depth, 2, LRU_WIDTH, LRU_WIDTH).astype(BF16)
    gaw = jnp.zeros((depth, MISC_W, GLA_QK_W), F32).at[:, N_GATE:N_GATE + GLA_RANK].set(gla_alpha_w).astype(BF16)
    pe = jnp.transpose(nsa_cmp_pe, (0, 2, 1, 3)).reshape(depth, 2, 1, CMP_BLOCK * NSA_HEAD_DIM)
    cw1 = nsa_cmp_w1.astype(BF16)
    cw2 = nsa_cmp_w2.astype(BF16)
    wb = w_branch.astype(BF16)
    wo = w_out.astype(BF16)
    w1 = mlp_w1.astype(BF16)
    w2 = mlp_w2.astype(BF16)
    cache_cmp = cache_nsa_cmp_kv.reshape(depth, n_pool, PAGE_SIZE, KV_W)
    cache_sel = cache_nsa_sel_kv.reshape(depth, n_pool, PAGE_SIZE, KV_W)
    win_state = state_nsa_win_kv.reshape(depth, bs, w_buf, KV_W)

    tabs_p = _prompt_bias_tables(rel_bias, tp)
    tabs_s = _sample_tables(rel_bias, n_pages * PAGE_SIZE, ts, w_buf)

    xp = x_prompt.reshape(bp * tp, D_MODEL)
    xs = x_sample.reshape(bs * ts, D_MODEL)
    conv0_p = jnp.zeros((bp, CONV_WIDTH - 1, LRU_WIDTH), F32)
    h0_p = jnp.zeros((bp, LRU_WIDTH), F32)
    s0_p = jnp.zeros((bp, GLA_HEADS, GLA_DK, GLA_DV), F32)
    outs_p = [[] for _ in range(6)]
    outs_s = [[] for _ in range(6)]
    for l in range(depth):
        lw = dict(ng=norm_mix_g[l][None], mg=norm_mlp_g[l][None], w_in=w_in_p[l], b_in=b_in_p[l],
                  pe=pe[l], cw1=cw1[l], cw2=cw2[l], lcw=lru_conv_w[l], lcb=lru_conv_b[l][None], lgw=lgw[l],
                  lgb=lru_gate_b[l], lam=lru_lambda[l][None], gaw=gaw[l], gab=gla_alpha_b[l][None],
                  gng=gla_norm_g[l][None], wb=wb[l], wo=wo[l], w1=w1[l], w2=w2[l])
        xp, st_p = _layer(xp, bp, tp, lw, conv0_p, h0_p, s0_p,
                          functools.partial(_nsa_prompt_layer, lw=lw, tabs=tabs_p), _row_tile(bp * tp, 1024))
        nsa_s = functools.partial(_nsa_sample_layer, lw=lw, tabs=tabs_s, layer=l, cache_cmp=cache_cmp,
                                  cache_sel=cache_sel, win_state=win_state, page_table=page_table)
        xs, st_s = _layer(xs, bs, ts, lw, state_lru_conv[l], state_lru_h[l], state_gla[l], nsa_s, bs * ts)
        for j in range(6):
            outs_p[j].append(st_p[j])
            outs_s[j].append(st_s[j])
    y_prompt = _final_norm(xp, norm_final_g[None], _row_tile(bp * tp, 1024)).reshape(bp, tp, D_MODEL)
    y_sample = _final_norm(xs, norm_final_g[None], bs * ts).reshape(bs, ts, D_MODEL)
    st = lambda outs, j: jnp.stack(outs[j])
    return (y_prompt, y_sample, st(outs_p, 0), st(outs_s, 0), st(outs_p, 1), st(outs_s, 1),
            st(outs_p, 2), st(outs_s, 2), st(outs_p, 3), st(outs_s, 3), st(outs_p, 4), st(outs_s, 4),
            st(outs_p, 5), st(outs_s, 5))
```

```python
import functools
import math

import jax
import jax.numpy as jnp
import numpy as np
from jax import lax
from jax.experimental import pallas as pl
from jax.experimental.pallas import tpu as pltpu

F32 = jnp.float32
BF16 = jnp.bfloat16

D_MODEL = 1024
DEPTH = 4
PAGE_SIZE = 128
NSA_HEADS = 8
NSA_KV_HEADS = 2
NSA_HPG = NSA_HEADS // NSA_KV_HEADS
NSA_HEAD_DIM = 64
CMP_BLOCK = 32
SEL_BLOCK = 64
N_SELECT = 16
WINDOW = 512
FORCE_SCORE = 1e4
REL_BUCKETS = 32
REL_MAX_DIST = 128
LRU_WIDTH = 512
LRU_BLOCKS = 8
LRU_BLOCK_DIM = LRU_WIDTH // LRU_BLOCKS
CONV_WIDTH = 4
LRU_C = 8.0
GLA_HEADS = 4
GLA_DK = 64
GLA_DV = 128
GLA_RANK = 16
GLA_TAU = 16.0
GLA_CHUNK = 64
D_FF = 4 * D_MODEL
N_BRANCH = 3
BRANCH_WIDTH = 512
NORM_EPS = 1e-6

KV_W = 2 * NSA_KV_HEADS * NSA_HEAD_DIM
Q_W = NSA_HEADS * NSA_HEAD_DIM
GROUP_W = NSA_HPG * NSA_HEAD_DIM
GLA_QK_W = GLA_HEADS * GLA_DK
GLA_V_W = GLA_HEADS * GLA_DV
MISC_W = 128
N_GATE = 3 * NSA_HEADS

COL_MERGE = 0
COL_Q = COL_MERGE + N_BRANCH * D_MODEL
COL_LRU_X = COL_Q + Q_W
COL_LRU_G = COL_LRU_X + LRU_WIDTH
COL_GLA_V = COL_LRU_G + LRU_WIDTH
COL_GLA_G = COL_GLA_V + GLA_V_W
COL_CMP = COL_GLA_G + GLA_V_W
COL_SEL = COL_CMP + KV_W
COL_WIN = COL_SEL + KV_W
COL_GLA_Q = COL_WIN + KV_W
COL_GLA_K = COL_GLA_Q + GLA_QK_W
COL_MISC = COL_GLA_K + GLA_QK_W
PROJ_COLS = COL_MISC + MISC_W

_SRC = {}
_off = 0
for _name, _w in (('nsa_q', Q_W), ('nsa_cmp_kv', KV_W), ('nsa_sel_kv', KV_W), ('nsa_win_kv', KV_W),
                  ('nsa_gate', N_GATE), ('lru_x', LRU_WIDTH), ('lru_gate', LRU_WIDTH),
                  ('gla_q', GLA_QK_W), ('gla_k', GLA_QK_W), ('gla_v', GLA_V_W),
                  ('gla_alpha', GLA_RANK), ('gla_gate', GLA_V_W), ('merge_gate', N_BRANCH * D_MODEL)):
    _SRC[_name] = (_off, _w)
    _off += _w
_DST_ORDER = ('merge_gate', 'nsa_q', 'lru_x', 'lru_gate', 'gla_v', 'gla_gate', 'nsa_cmp_kv',
              'nsa_sel_kv', 'nsa_win_kv', 'gla_q', 'gla_k', 'nsa_gate', 'gla_alpha')

NSA_TQ = 128
NSA_LANES = NSA_HPG * NSA_TQ
NSA_CLASS = 4
VMEM_LIMIT = 56 * 1024 * 1024


def _cparams(sem):
    return pltpu.CompilerParams(dimension_semantics=sem, vmem_limit_bytes=VMEM_LIMIT)


def _gelu(x):
    return x * (0.5 * (1.0 + jnp.tanh(math.sqrt(2.0 / math.pi) * (x + 0.044715 * (x * x * x)))))


def _softplus(x):
    return jnp.maximum(x, 0.0) + jnp.log1p(jnp.exp(-jnp.abs(x)))


def _rms(x, g):
    return x * lax.rsqrt(jnp.mean(x * x, axis=-1, keepdims=True) + NORM_EPS) * g


def _dot(a, b):
    return jnp.dot(a, b, preferred_element_type=F32)


def _dot_nt(a, b):
    return lax.dot_general(a, b, (((1,), (1,)), ((), ())), preferred_element_type=F32)


def _dot_tn(a, b):
    return lax.dot_general(a, b, (((0,), (0,)), ((), ())), preferred_element_type=F32)


def _proj_kernel(x_ref, g_ref, w_ref, b_ref, o_ref, h_ref):
    @pl.when(pl.program_id(1) == 0)
    def _():
        h_ref[...] = _rms(x_ref[...], g_ref[...]).astype(BF16)

    o_ref[...] = _dot(h_ref[...], w_ref[...]) + b_ref[...]


def _proj(x, g, w, b, tm, tn):
    n = x.shape[0]
    return pl.pallas_call(
        _proj_kernel,
        grid=(n // tm, PROJ_COLS // tn),
        in_specs=[pl.BlockSpec((tm, D_MODEL), lambda i, j: (i, 0)),
                  pl.BlockSpec((1, D_MODEL), lambda i, j: (0, 0)),
                  pl.BlockSpec((D_MODEL, tn), lambda i, j: (0, j)),
                  pl.BlockSpec((1, tn), lambda i, j: (0, j))],
        out_specs=pl.BlockSpec((tm, tn), lambda i, j: (i, j)),
        out_shape=jax.ShapeDtypeStruct((n, PROJ_COLS), F32),
        scratch_shapes=[pltpu.VMEM((tm, D_MODEL), BF16)],
        compiler_params=_cparams(("parallel", "arbitrary")),
        name="proj",
    )(x, g, w, b)


def _merge_kernel(x_ref, oa_ref, ob_ref, oc_ref, g0_ref, g1_ref, g2_ref, wb_ref, wo_ref, o_ref):
    m = jax.nn.sigmoid(g0_ref[...]) * _dot(oa_ref[...].astype(BF16), wb_ref[0])
    m = m + jax.nn.sigmoid(g1_ref[...]) * _dot(ob_ref[...].astype(BF16), wb_ref[1])
    m = m + jax.nn.sigmoid(g2_ref[...]) * _dot(oc_ref[...].astype(BF16), wb_ref[2])
    o_ref[...] = x_ref[...] + _dot(m.astype(BF16), wo_ref[...])


def _merge(x, oa, ob, oc, pr, wb, wo, tm):
    n = x.shape[0]
    row = lambda w: pl.BlockSpec((tm, w), lambda i: (i, 0))
    gate = lambda z: pl.BlockSpec((tm, D_MODEL), lambda i: (i, COL_MERGE // D_MODEL + z))
    return pl.pallas_call(
        _merge_kernel,
        grid=(n // tm,),
        in_specs=[row(D_MODEL), row(BRANCH_WIDTH), row(BRANCH_WIDTH), row(BRANCH_WIDTH),
                  gate(0), gate(1), gate(2),
                  pl.BlockSpec((N_BRANCH, BRANCH_WIDTH, D_MODEL), lambda i: (0, 0, 0)),
                  pl.BlockSpec((D_MODEL, D_MODEL), lambda i: (0, 0))],
        out_specs=row(D_MODEL),
        out_shape=jax.ShapeDtypeStruct((n, D_MODEL), F32),
        compiler_params=_cparams(("parallel",)),
        name="merge",
    )(x, oa, ob, oc, pr, pr, pr, wb, wo)


def _mlp_kernel(x_ref, g_ref, w1_ref, w2_ref, o_ref, h_ref, acc_ref):
    f = pl.program_id(1)

    @pl.when(f == 0)
    def _():
        h_ref[...] = _rms(x_ref[...], g_ref[...]).astype(BF16)
        acc_ref[...] = jnp.zeros_like(acc_ref)

    a = jnp.maximum(_dot(h_ref[...], w1_ref[...]), 0.0)
    acc_ref[...] += _dot((a * a).astype(BF16), w2_ref[...])

    @pl.when(f == pl.num_programs(1) - 1)
    def _():
        o_ref[...] = x_ref[...] + acc_ref[...]


def _mlp(x, g, w1, w2, tm, tf):
    n = x.shape[0]
    return pl.pallas_call(
        _mlp_kernel,
        grid=(n // tm, D_FF // tf),
        in_specs=[pl.BlockSpec((tm, D_MODEL), lambda i, f: (i, 0)),
                  pl.BlockSpec((1, D_MODEL), lambda i, f: (0, 0)),
                  pl.BlockSpec((D_MODEL, tf), lambda i, f: (0, f)),
                  pl.BlockSpec((tf, D_MODEL), lambda i, f: (f, 0))],
        out_specs=pl.BlockSpec((tm, D_MODEL), lambda i, f: (i, 0)),
        out_shape=jax.ShapeDtypeStruct((n, D_MODEL), F32),
        scratch_shapes=[pltpu.VMEM((tm, D_MODEL), BF16), pltpu.VMEM((tm, D_MODEL), F32)],
        compiler_params=_cparams(("parallel", "arbitrary")),
        name="mlp",
    )(x, g, w1, w2)


def _norm_kernel(x_ref, g_ref, o_ref):
    o_ref[...] = _rms(x_ref[...], g_ref[...])


def _final_norm(x, g, tm):
    n = x.shape[0]
    return pl.pallas_call(
        _norm_kernel,
        grid=(n // tm,),
        in_specs=[pl.BlockSpec((tm, D_MODEL), lambda i: (i, 0)),
                  pl.BlockSpec((1, D_MODEL), lambda i: (0, 0))],
        out_specs=pl.BlockSpec((tm, D_MODEL), lambda i: (i, 0)),
        out_shape=jax.ShapeDtypeStruct((n, D_MODEL), F32),
        compiler_params=_cparams(("parallel",)),
        name="final_norm",
    )(x, g)


_XB = 8


def _lru_kernel(x_ref, gb_ref, conv0_ref, h0_ref, cw_ref, cb_ref, gw_ref, gbias_ref, lam_ref,
                o_ref, convn_ref, hn_ref, xbuf, hcar, *, tc, tv):
    @pl.when(pl.program_id(1) == 0)
    def _():
        xbuf[0:_XB, :] = jnp.zeros((_XB, LRU_WIDTH), F32)
        xbuf[_XB - 3:_XB, :] = conv0_ref[0]
        hcar[...] = h0_ref[0]

    x = x_ref[0]
    xbuf[_XB:_XB + tc, :] = x
    w = cw_ref[...]
    xc = cb_ref[...] + xbuf[_XB - 3:_XB - 3 + tc, :] * w[0:1]
    xc = xc + xbuf[_XB - 2:_XB - 2 + tc, :] * w[1:2]
    xc = xc + xbuf[_XB - 1:_XB - 1 + tc, :] * w[2:3]
    xc = xc + x * w[3:4]
    tail = xbuf[_XB - 3 + tv:_XB + tv, :]
    convn_ref[0] = tail
    xbuf[_XB - 3:_XB, :] = tail

    xcb = xc.astype(BF16)
    r = jax.nn.sigmoid(_dot(xcb, gw_ref[0]) + gbias_ref[0:1])
    i = jax.nn.sigmoid(_dot(xcb, gw_ref[1]) + gbias_ref[1:2])
    log_a = (-LRU_C * r) * _softplus(-lam_ref[...])
    a = jnp.exp(log_a)
    b = jnp.sqrt(-jnp.tanh(log_a) * (a * a + 1.0)) * (i * xc)

    rows = lax.broadcasted_iota(jnp.int32, (tc, LRU_WIDTH), 0)
    s = 1
    while s < tc:
        a_sh = pltpu.roll(a, s, 0)
        b_sh = pltpu.roll(b, s, 0)
        m = rows >= s
        b = jnp.where(m, a * b_sh + b, b)
        a = jnp.where(m, a * a_sh, a)
        s *= 2
    h = a * hcar[...] + b
    hlast = h[tv - 1:tv]
    hcar[...] = hlast
    hn_ref[0] = hlast
    o_ref[0] = _gelu(gb_ref[0]) * h


def _lru(x_arr, x_blk, gb_arr, gb_blk, conv0, h0, cw, cb, gw, gbias, lam, tc, tv):
    b, t = x_arr.shape[:2]
    r = LRU_WIDTH
    const2 = lambda shape: pl.BlockSpec(shape, lambda i, c: (0, 0))
    return pl.pallas_call(
        functools.partial(_lru_kernel, tc=tc, tv=tv),
        grid=(b, t // tc),
        in_specs=[pl.BlockSpec((1, tc, r), lambda i, c: (i, c, x_blk)),
                  pl.BlockSpec((1, tc, r), lambda i, c: (i, c, gb_blk)),
                  pl.BlockSpec((1, CONV_WIDTH - 1, r), lambda i, c: (i, 0, 0)),
                  pl.BlockSpec((1, 1, r), lambda i, c: (i, 0, 0)),
                  const2((CONV_WIDTH, r)), const2((1, r)),
                  pl.BlockSpec((2, r, r), lambda i, c: (0, 0, 0)),
                  const2((2, r)), const2((1, r))],
        out_specs=[pl.BlockSpec((1, tc, r), lambda i, c: (i, c, 0)),
                   pl.BlockSpec((1, CONV_WIDTH - 1, r), lambda i, c: (i, 0, 0)),
                   pl.BlockSpec((1, 1, r), lambda i, c: (i, 0, 0))],
        out_shape=[jax.ShapeDtypeStruct((b, t, r), F32),
                   jax.ShapeDtypeStruct((b, CONV_WIDTH - 1, r), F32),
                   jax.ShapeDtypeStruct((b, 1, r), F32)],
        scratch_shapes=[pltpu.VMEM((_XB + tc, r), F32), pltpu.VMEM((1, r), F32)],
        compiler_params=_cparams(("parallel", "arbitrary")),
        name="rglru",
    )(x_arr, gb_arr, conv0, h0, cw, cb, gw, gbias, lam)


def _gla_kernel(q_ref, k_ref, v_ref, og_ref, misc_ref, aw_ref, ab_ref, ng_ref, s0_ref,
                o_ref, sn_ref, st, *, tg, ck, tv):
    @pl.when(pl.program_id(1) == 0)
    def _():
        st[...] = s0_ref[0]

    pre = _dot(misc_ref[0].astype(BF16), aw_ref[...]) + ab_ref[...]
    g = -_softplus(-pre) * (1.0 / GLA_TAU)
    if tv < tg:
        g = jnp.where(lax.broadcasted_iota(jnp.int32, g.shape, 0) < tv, g, 0.0)
    q = q_ref[0] * (GLA_DK ** -0.5)
    k = k_ref[0]
    v = v_ref[0]
    og = og_ref[0]
    ng = ng_ref[...]
    rows = lax.broadcasted_iota(jnp.int32, (ck, GLA_QK_W), 0)
    tril = (lax.broadcasted_iota(jnp.int32, (ck, ck), 0) >= lax.broadcasted_iota(jnp.int32, (ck, ck), 1))
    for c in range(tg // ck):
        sl = slice(c * ck, (c + 1) * ck)
        bc = g[sl]
        s = 1
        while s < ck:
            bc = bc + jnp.where(rows >= s, pltpu.roll(bc, s, 0), 0.0)
            s *= 2
        bl = bc[ck - 1:ck]
        e = jnp.exp(bc)
        qi = (q[sl] * e).astype(BF16)
        ki = (k[sl] * jnp.exp(-bc)).astype(BF16)
        kd = (k[sl] * jnp.exp(bl - bc)).astype(BF16)
        dec = jnp.exp(bl)
        vb = v[sl].astype(BF16)
        outs = []
        for h in range(GLA_HEADS):
            ks = slice(h * GLA_DK, (h + 1) * GLA_DK)
            vs = slice(h * GLA_DV, (h + 1) * GLA_DV)
            att = jnp.where(tril, _dot_nt(qi[:, ks], ki[:, ks]), 0.0)
            s_prev = st[h]
            o = _dot(att.astype(BF16), vb[:, vs]) + _dot_nt(qi[:, ks], s_prev.astype(BF16))
            st[h] = s_prev * dec[:, ks] + _dot_tn(vb[:, vs], kd[:, ks])
            o = _rms(o, ng[:, vs])
            ogh = og[sl, vs]
            outs.append(o * (ogh * jax.nn.sigmoid(ogh)))
        o_ref[0, sl, :] = jnp.concatenate(outs, axis=-1)
    sn_ref[0] = st[...]


def _gla(q_arr, q_blk, k_arr, k_blk, v_arr, v_blk, og_arr, og_blk, misc_arr, misc_blk,
         aw, ab, ng, s0t, tg, ck, tv):
    b, t = q_arr.shape[:2]
    col = lambda w, blk: pl.BlockSpec((1, tg, w), lambda i, c: (i, c, blk))
    const2 = lambda shape: pl.BlockSpec(shape, lambda i, c: (0, 0))
    state = pl.BlockSpec((1, GLA_HEADS, GLA_DV, GLA_DK), lambda i, c: (i, 0, 0, 0))
    return pl.pallas_call(
        functools.partial(_gla_kernel, tg=tg, ck=ck, tv=tv),
        grid=(b, t // tg),
        in_specs=[col(GLA_QK_W, q_blk), col(GLA_QK_W, k_blk), col(GLA_V_W, v_blk), col(GLA_V_W, og_blk),
                  col(MISC_W, misc_blk), const2((MISC_W, GLA_QK_W)), const2((1, GLA_QK_W)),
                  const2((1, GLA_V_W)), state],
        out_specs=[pl.BlockSpec((1, tg, GLA_V_W), lambda i, c: (i, c, 0)), state],
        out_shape=[jax.ShapeDtypeStruct((b, t, GLA_V_W), F32),
                   jax.ShapeDtypeStruct((b, GLA_HEADS, GLA_DV, GLA_DK), F32)],
        scratch_shapes=[pltpu.VMEM((GLA_HEADS, GLA_DV, GLA_DK), F32)],
        compiler_params=_cparams(("parallel", "arbitrary")),
        name="gla",
    )(q_arr, k_arr, v_arr, og_arr, misc_arr, aw, ab, ng, s0t)


def _bias_kernel(tbl_ref, dist_ref, o_ref):
    h = pl.program_id(0)
    max_exact = REL_BUCKETS // 2
    n = jnp.maximum(dist_ref[...], 0)
    nf = jnp.maximum(n, 1).astype(F32)
    large = max_exact + (jnp.log(nf / max_exact) / math.log(REL_MAX_DIST / max_exact)
                         * (REL_BUCKETS - max_exact)).astype(jnp.int32)
    bucket = jnp.where(n < max_exact, n, jnp.minimum(large, REL_BUCKETS - 1))
    out = jnp.zeros(bucket.shape, F32)
    for kk in range(REL_BUCKETS):
        out = jnp.where(bucket == kk, tbl_ref[kk, h], out)
    o_ref[0] = out


def _bias_lookup(rel_bias, dist, tr):
    r, c = dist.shape
    return pl.pallas_call(
        _bias_kernel,
        grid=(NSA_HEADS, r // tr),
        in_specs=[pl.BlockSpec(memory_space=pltpu.SMEM),
                  pl.BlockSpec((tr, c), lambda h, i: (i, 0))],
        out_specs=pl.BlockSpec((1, tr, c), lambda h, i: (h, i, 0)),
        out_shape=jax.ShapeDtypeStruct((NSA_HEADS, r, c), F32),
        compiler_params=_cparams(("parallel", "parallel")),
        name="rel_bias",
    )(rel_bias, dist)


def _cmp_kernel(x_ref, pe_ref, w1_ref, w2_ref, o_ref):
    hid = _gelu(_dot((x_ref[0] + pe_ref[0]).astype(BF16), w1_ref[0]))
    o_ref[0] = _dot(hid.astype(BF16), w2_ref[0])


def _row_tile(m, cap):
    for tm in range(min(cap, m) // 8 * 8, 0, -8):
        if m % tm == 0:
            return tm
    return m


def _compress(flat, pe, w1, w2, cap):
    m = flat.shape[1]
    tm = _row_tile(m, cap)
    fw = CMP_BLOCK * NSA_HEAD_DIM
    d = NSA_HEAD_DIM
    return pl.pallas_call(
        _cmp_kernel,
        grid=(2, m // tm),
        in_specs=[pl.BlockSpec((1, tm, fw), lambda z, i: (z, i, 0)),
                  pl.BlockSpec((1, 1, fw), lambda z, i: (z, 0, 0)),
                  pl.BlockSpec((1, fw, d), lambda z, i: (z, 0, 0)),
                  pl.BlockSpec((1, d, d), lambda z, i: (z, 0, 0))],
        out_specs=pl.BlockSpec((1, tm, d), lambda z, i: (z, i, 0)),
        out_shape=jax.ShapeDtypeStruct((2, m, d), F32),
        compiler_params=_cparams(("parallel", "parallel")),
        name="nsa_compress",
    )(flat, pe, w1, w2)


def _nsa_prompt_kernel(q_ref, kc_ref, vc_ref, bc_ref, kst_ref, vst_ref, kwc_ref, vwc_ref, bt_ref, e_ref, gt_ref,
                       o_ref, s_scr, p_scr, m_scr, l_scr, selk_scr, os_scr):
    i = pl.program_id(2)
    tq, d, ck, hpg = NSA_TQ, NSA_HEAD_DIM, NSA_TQ, NSA_HPG
    nchunk = kst_ref.shape[-1] // ck
    nsel = kc_ref.shape[2] // 2
    neg = -jnp.inf
    qf = q_ref[0] * (d ** -0.5)
    qs = jnp.concatenate([qf[:, h * d:(h + 1) * d] for h in range(hpg)], axis=0).astype(BF16)

    t_lane = i * tq + (lax.broadcasted_iota(jnp.int32, (1, hpg * tq), 1) & (tq - 1))
    s_c = _dot_nt(kc_ref[0, 0], qs) + bc_ref[0, 0]
    r_c = lax.broadcasted_iota(jnp.int32, (2 * nsel, hpg * tq), 0)
    n_c = jnp.where(r_c < nsel, 2 * r_c, 2 * (r_c - nsel) + 1)
    s_c = jnp.where(t_lane >= (n_c + 1) * CMP_BLOCK - 1, s_c, neg)
    m_c = jnp.max(s_c, axis=0, keepdims=True)
    m_c = jnp.where(m_c > neg, m_c, 0.0)
    p_c = jnp.exp(s_c - m_c)
    p_c = p_c / jnp.maximum(jnp.sum(p_c, axis=0, keepdims=True), 1e-30)
    o_c = _dot_tn(p_c.astype(BF16), vc_ref[0, 0])

    ph = p_c[:, 0:tq]
    for h in range(1, hpg):
        ph = ph + p_c[:, h * tq:(h + 1) * tq]
    imp = ph[0:nsel] + ph[nsel:2 * nsel]
    blk = lax.broadcasted_iota(jnp.int32, (nsel, tq), 0)
    cur = (i * tq + lax.broadcasted_iota(jnp.int32, (nsel, tq), 1)) >> int(math.log2(SEL_BLOCK))
    forced = (blk == 0) | (blk == cur) | (blk == cur - 1)
    imp = jnp.where(forced, FORCE_SCORE, jnp.where(blk <= cur, imp, neg))
    rank = jnp.zeros((nsel, tq), jnp.int32)
    for s2 in range(nsel):
        row = imp[s2:s2 + 1]
        beats = (row > imp) | ((row == imp) & (blk > s2))
        rank = rank + beats.astype(jnp.int32)
    chosen = (rank < N_SELECT).astype(BF16)
    selk_scr[...] = _dot_tn(chosen, e_ref[...])

    tk = lax.broadcasted_iota(jnp.int32, (tq, ck), 0) - lax.broadcasted_iota(jnp.int32, (tq, ck), 1)

    def mask_heads(s, ok):
        return jnp.concatenate([jnp.where(ok, s[h * tq:(h + 1) * tq], neg) for h in range(hpg)], axis=0)

    def softmax_pv(n, score_chunk, vt):
        m_scr[...] = jnp.full(m_scr.shape, neg, F32)
        for c in range(n):
            s = score_chunk(c)
            s_scr[:, c * ck:(c + 1) * ck] = s
            m_scr[...] = jnp.maximum(m_scr[...], s)
        m = jnp.max(m_scr[...], axis=1, keepdims=True)
        m = jnp.where(m > neg, m, 0.0)
        l_scr[...] = jnp.zeros(l_scr.shape, F32)
        for c in range(n):
            p = jnp.exp(s_scr[:, c * ck:(c + 1) * ck] - m)
            l_scr[...] += p
            p_scr[:, c * ck:(c + 1) * ck] = p.astype(BF16)
        l = jnp.sum(l_scr[...], axis=1, keepdims=True)
        return _dot_nt(p_scr[:, 0:n * ck], vt) / jnp.maximum(l, 1e-30)

    def sel_chunk(c):
        delta = i - c
        s = _dot(qs, kst_ref[0, 0, 0, :, c * ck:(c + 1) * ck]) + bt_ref[0, jnp.clip(delta, 0, 2)]
        ok = (tk + delta * ck >= 0) & (selk_scr[:, c * ck:(c + 1) * ck] > 0.5)
        return mask_heads(s, ok)

    sizes = list(range(NSA_CLASS, nchunk, NSA_CLASS)) + [nchunk]
    for lo, n in zip([0] + sizes[:-1], sizes):
        @pl.when((i >= lo) & (i < n))
        def _(n=n):
            os_scr[...] = softmax_pv(n, sel_chunk, vst_ref[0, 0, 0, :, 0:n * ck])

    nw = WINDOW // ck + 1

    def win_chunk(k):
        delta = nw - 1 - k
        c = i - delta
        s = _dot(qs, kwc_ref[0, 0, 0, jnp.maximum(c, 0)]) + bt_ref[0, min(delta, 2)]
        dist = tk + delta * ck
        ok = (dist >= 0) & (dist < jnp.where(c >= 0, WINDOW, 0))
        return mask_heads(s, ok)

    vw = jnp.concatenate([vwc_ref[0, 0, 0, jnp.maximum(i - (nw - 1 - k), 0)] for k in range(nw)], axis=1)
    o_w = softmax_pv(nw, win_chunk, vw)
    o_s = os_scr[...]

    sig = jax.nn.sigmoid(gt_ref[0, 0])
    outs = []
    for h in range(hpg):
        r = slice(h * tq, (h + 1) * tq)
        outs.append(sig[:, h:h + 1] * o_c[r] + sig[:, hpg + h:hpg + h + 1] * o_s[r]
                    + sig[:, 2 * hpg + h:2 * hpg + h + 1] * o_w[r])
    o_ref[0] = jnp.concatenate(outs, axis=1)


def _nsa_prompt(pr3, kc, vc, bias_c, kvt_sel, kvt_win, bias_t, expand, gts):
    b, t = pr3.shape[:2]
    g, d, tq, hpg = NSA_KV_HEADS, NSA_HEAD_DIM, NSA_TQ, NSA_HPG
    nt = t // tq
    nc = kc.shape[2]
    rows = hpg * tq
    per_bg = lambda r, c: pl.BlockSpec((1, 1, r, c), lambda bi, gi, i: (bi, gi, 0, 0))
    kv_sel = lambda z: pl.BlockSpec((1, 1, 1, d, t), lambda bi, gi, i: (bi, z, gi, 0, 0))
    kv_win = lambda z: pl.BlockSpec((1, 1, 1, nt, d, tq), lambda bi, gi, i: (bi, z, gi, 0, 0, 0))
    return pl.pallas_call(
        _nsa_prompt_kernel,
        grid=(b, g, nt),
        in_specs=[pl.BlockSpec((1, tq, GROUP_W), lambda bi, gi, i: (bi, i, COL_Q // GROUP_W + gi)),
                  per_bg(nc, d), per_bg(nc, d),
                  pl.BlockSpec((1, 1, nc, rows), lambda bi, gi, i: (gi, i, 0, 0)),
                  kv_sel(0), kv_sel(1), kv_win(0), kv_win(1),
                  pl.BlockSpec((1, 3, rows, tq), lambda bi, gi, i: (gi, 0, 0, 0)),
                  pl.BlockSpec((nc // 2, t), lambda bi, gi, i: (0, 0)),
                  pl.BlockSpec((1, 1, tq, 128), lambda bi, gi, i: (bi, gi, i, 0))],
        out_specs=pl.BlockSpec((1, tq, GROUP_W), lambda bi, gi, i: (bi, i, gi)),
        out_shape=jax.ShapeDtypeStruct((b, t, Q_W), F32),
        scratch_shapes=[pltpu.VMEM((rows, t), F32), pltpu.VMEM((rows, t), BF16),
                        pltpu.VMEM((rows, tq), F32), pltpu.VMEM((rows, tq), F32),
                        pltpu.VMEM((tq, t), F32), pltpu.VMEM((rows, d), F32)],
        compiler_params=_cparams(("parallel", "parallel", "arbitrary")),
        name="nsa_prompt",
    )(pr3, kc, vc, bias_c, kvt_sel, kvt_sel, kvt_win, kvt_win, bias_t, expand, gts)


SMP_CPAD = 128
KEY_TILE = PAGE_SIZE


def _masked_softmax(s, ok):
    s = jnp.where(ok, s, -jnp.inf)
    m = jnp.max(s, axis=-1, keepdims=True)
    m = jnp.where(m > -jnp.inf, m, 0.0)
    p = jnp.exp(s - m)
    return p / jnp.maximum(jnp.sum(p, axis=-1, keepdims=True), 1e-30)


def _nsa_sample_kernel(pt_ref, q_ref, kc_ref, vc_ref, bc_ref, okc_ref, cur_ref, *rest, n_pages, n_win, nsel):
    pages = rest[:n_pages]
    (sn_ref, bs_ref, oks_ref, e_ref, wb_ref, wn_ref, bw_ref, okw_ref, gt_ref,
     o_ref, s_scr, w_scr) = rest[n_pages:]
    del pt_ref
    d = NSA_HEAD_DIM
    s_new = q_ref.shape[2] // NSA_HPG
    wkeys = n_win * KEY_TILE
    for g in range(NSA_KV_HEADS):
        q = (q_ref[0, g] * (d ** -0.5)).astype(BF16)

        s_c = _dot_nt(q, kc_ref[0, g].astype(BF16)) + bc_ref[g]
        p_c = _masked_softmax(s_c, okc_ref[...] > 0.5)
        o_c = _dot(p_c.astype(BF16), vc_ref[0, g].astype(BF16))
        ph = p_c[0:s_new]
        for h in range(1, NSA_HPG):
            ph = ph + p_c[h * s_new:(h + 1) * s_new]
        imp = ph[:, 0:SMP_CPAD] + ph[:, SMP_CPAD:2 * SMP_CPAD]
        blk = lax.broadcasted_iota(jnp.int32, imp.shape, 1)
        cur = cur_ref[...]
        forced = (blk == 0) | (blk == cur) | (blk == cur - 1)
        imp = jnp.where(forced, FORCE_SCORE, jnp.where(blk <= cur, imp, -jnp.inf))
        rank = jnp.zeros(imp.shape, jnp.int32)
        for s2 in range(nsel):
            col = imp[:, s2:s2 + 1]
            beats = (col > imp) | ((col == imp) & (blk > s2))
            rank = rank + beats.astype(jnp.int32)
        chosen = ((rank < N_SELECT) & (blk < nsel)).astype(BF16)
        chosen = jnp.concatenate([chosen] * NSA_HPG, axis=0)
        sel_keys = _dot(chosen, e_ref[...])

        for p in range(n_pages):
            s_scr[:, p * KEY_TILE:(p + 1) * KEY_TILE] = _dot(q, pages[p][0, 0, 0, g].astype(BF16))
        s_scr[:, n_pages * KEY_TILE:(n_pages + 1) * KEY_TILE] = _dot(q, sn_ref[0, 0, g].astype(BF16))
        p_s = _masked_softmax(s_scr[...] + bs_ref[g], (oks_ref[...] > 0.5) & (sel_keys > 0.5)).astype(BF16)
        o_s = _dot_nt(p_s[:, n_pages * KEY_TILE:(n_pages + 1) * KEY_TILE], sn_ref[0, 1, g].astype(BF16))
        for p in range(n_pages):
            o_s = o_s + _dot_nt(p_s[:, p * KEY_TILE:(p + 1) * KEY_TILE], pages[p][0, 0, 1, g].astype(BF16))

        w_scr[:, 0:wkeys] = _dot(q, wb_ref[0, 0, 0, g].astype(BF16))
        w_scr[:, wkeys:wkeys + KEY_TILE] = _dot(q, wn_ref[0, 0, g].astype(BF16))
        p_w = _masked_softmax(w_scr[...] + bw_ref[g], okw_ref[...] > 0.5).astype(BF16)
        o_w = (_dot_nt(p_w[:, 0:wkeys], wb_ref[0, 0, 1, g].astype(BF16))
               + _dot_nt(p_w[:, wkeys:wkeys + KEY_TILE], wn_ref[0, 1, g].astype(BF16)))

        gt = jax.nn.sigmoid(gt_ref[0, g])
        o_ref[0, g] = gt[:, 0:1] * o_c + gt[:, 1:2] * o_s + gt[:, 2:3] * o_w


def _nsa_sample(layer, nsel, page_table, qg, kc, vc, bias_c, ok_c, cur, cache_sel, sel_new, bias_s, ok_s,
                expand, win_state, win_new, bias_w, ok_w, gts):
    b = qg.shape[0]
    g = NSA_KV_HEADS
    d = NSA_HEAD_DIM
    rows = qg.shape[2]
    n_pages = page_table.shape[1]
    n_win = win_state.shape[-1] // KEY_TILE
    ks = (n_pages + 1) * KEY_TILE
    kw = (n_win + 1) * KEY_TILE
    per_b = lambda *shape: pl.BlockSpec((1,) + shape, lambda i, pt: (i,) + (0,) * len(shape))
    const = lambda *shape: pl.BlockSpec(shape, lambda i, pt: (0,) * len(shape))

    def page_spec(p):
        return pl.BlockSpec((1, 1, 2, g, d, PAGE_SIZE), lambda i, pt: (layer, pt[i, p], 0, 0, 0, 0))

    in_specs = ([per_b(g, rows, d), per_b(g, 2 * SMP_CPAD, d), per_b(g, 2 * SMP_CPAD, d),
                 const(g, rows, 2 * SMP_CPAD), const(rows, 2 * SMP_CPAD), const(rows // NSA_HPG, SMP_CPAD)]
                + [page_spec(p) for p in range(n_pages)]
                + [per_b(2, g, d, KEY_TILE), const(g, rows, ks), const(rows, ks), const(SMP_CPAD, ks),
                   pl.BlockSpec((1, 1, 2, g, d, n_win * KEY_TILE), lambda i, pt: (layer, i, 0, 0, 0, 0)),
                   per_b(2, g, d, KEY_TILE), const(g, rows, kw), const(rows, kw), per_b(g, rows, 128)])
    return pl.pallas_call(
        functools.partial(_nsa_sample_kernel, n_pages=n_pages, n_win=n_win, nsel=nsel),
        grid_spec=pltpu.PrefetchScalarGridSpec(
            num_scalar_prefetch=1,
            grid=(b,),
            in_specs=in_specs,
            out_specs=pl.BlockSpec((1, g, rows, d), lambda i, pt: (i, 0, 0, 0)),
            scratch_shapes=[pltpu.VMEM((rows, ks), F32), pltpu.VMEM((rows, kw), F32)]),
        out_shape=jax.ShapeDtypeStruct((b, g, rows, d), F32),
        compiler_params=_cparams(("parallel",)),
        name="nsa_sample",
    )(page_table, qg, kc, vc, bias_c, ok_c, cur, *([cache_sel] * n_pages), sel_new, bias_s, ok_s, expand,
      win_state, win_new, bias_w, ok_w, gts)


def _pad_rows(a, rows):
    return jnp.pad(a, ((0, 0), (0, rows - a.shape[1]), (0, 0)))


def _even_odd(n):
    return np.concatenate([np.arange(0, n, 2), np.arange(1, n, 2)])


def _prompt_bias_tables(rel_bias, t):
    g, hpg, tq = NSA_KV_HEADS, NSA_HPG, NSA_TQ
    nt = t // tq
    nc = t // CMP_BLOCK
    end_c = (_even_odd(nc) + 1) * CMP_BLOCK - 1
    dist_c = jnp.asarray(np.arange(t)[None, :] - end_c[:, None], jnp.int32)
    bias_c = _bias_lookup(rel_bias, dist_c, nc)
    bias_c = bias_c.reshape(g, hpg, nc, nt, tq).transpose(0, 3, 2, 1, 4).reshape(g, nt, nc, hpg * tq)
    off = np.arange(3)[:, None, None] * tq
    dist_t = off + np.arange(tq)[None, :, None] - np.arange(tq)[None, None, :]
    assert dist_t[2].min() >= REL_MAX_DIST
    bias_t = _bias_lookup(rel_bias, jnp.asarray(dist_t.reshape(3 * tq, tq), jnp.int32), 3 * tq)
    bias_t = bias_t.reshape(g, hpg, 3, tq, tq).transpose(0, 2, 1, 3, 4).reshape(g, 3, hpg * tq, tq)
    expand = np.arange(t)[None, :] // SEL_BLOCK == np.arange(t // SEL_BLOCK)[:, None]
    return bias_c, bias_t, jnp.asarray(expand, BF16)


def _sample_tables(rel_bias, past_len, s_new, n_win_keys):
    g, hpg = NSA_KV_HEADS, NSA_HPG
    tk = -(-(past_len + s_new) // SEL_BLOCK) * SEL_BLOCK
    nc = tk // CMP_BLOCK
    nsel = tk // SEL_BLOCK
    half = (nc + 1) // 2
    pos_q = past_len + np.arange(s_new)
    lane = np.arange(SMP_CPAD)
    n_of_lane = np.concatenate([2 * lane, 2 * lane + 1])
    real_c = np.concatenate([lane < half, lane < nc - half])
    dist_c = pos_q[:, None] - ((n_of_lane[None, :] + 1) * CMP_BLOCK - 1)
    ok_c = real_c[None, :] & (dist_c >= 0)
    n_keys_s = (past_len // PAGE_SIZE + 1) * KEY_TILE
    key = np.arange(n_keys_s)
    dist_s = pos_q[:, None] - key[None, :]
    ok_s = (key[None, :] < past_len + s_new) & (dist_s >= 0)
    n_keys_w = n_win_keys + KEY_TILE
    i = np.arange(n_keys_w)
    pos_kw = np.where(i < n_win_keys, past_len - n_win_keys + i, past_len + i - n_win_keys)
    dist_w = pos_q[:, None] - pos_kw[None, :]
    ok_w = (i[None, :] < n_win_keys + s_new) & (dist_w >= 0) & (dist_w < WINDOW) & (pos_kw[None, :] >= 0)
    dist = np.concatenate([dist_c, dist_s, dist_w], axis=1)
    dist = np.pad(dist, ((0, 8 - s_new), (0, 0)))
    bias = _bias_lookup(rel_bias, jnp.asarray(dist, jnp.int32), 8)[:, :s_new]
    bias = bias.reshape(g, hpg * s_new, dist.shape[1])
    c0, c1 = 2 * SMP_CPAD, 2 * SMP_CPAD + n_keys_s
    tile = lambda m: jnp.asarray(np.tile(m, (hpg, 1)), F32)
    expand = (key[None, :] // SEL_BLOCK == np.arange(SMP_CPAD)[:, None]) & (np.arange(SMP_CPAD)[:, None] < nsel)
    cur = np.broadcast_to((pos_q // SEL_BLOCK)[:, None], (s_new, SMP_CPAD))
    return dict(bias_c=bias[:, :, :c0], bias_s=bias[:, :, c0:c1], bias_w=bias[:, :, c1:],
                ok_c=tile(ok_c), ok_s=tile(ok_s), ok_w=tile(ok_w),
                expand=jnp.asarray(expand, BF16), cur=jnp.asarray(cur, jnp.int32), nc=nc, half=half, tk=tk)


def _flat_blocks(rows):
    b, tk = rows.shape[:2]
    nc = tk // CMP_BLOCK
    blk = rows.reshape(b, nc, CMP_BLOCK, 2, NSA_KV_HEADS, NSA_HEAD_DIM)
    return jnp.transpose(blk, (3, 0, 1, 4, 2, 5)).reshape(2, b * nc * NSA_KV_HEADS, CMP_BLOCK * NSA_HEAD_DIM)


def _nsa_prompt_layer(pr3, lw, tabs):
    b, t = pr3.shape[:2]
    g, hpg, d, tq = NSA_KV_HEADS, NSA_HPG, NSA_HEAD_DIM, NSA_TQ
    nt = t // tq
    nc = t // CMP_BLOCK
    kv = pr3[:, :, COL_CMP:COL_CMP + 3 * KV_W].reshape(b, t, 3, 2, g, d)
    cmp_rows, sel_rows, win_rows = kv[:, :, 0], kv[:, :, 1], kv[:, :, 2]
    kcv = _compress(_flat_blocks(cmp_rows), lw['pe'], lw['cw1'], lw['cw2'], 512).reshape(2, b, nc, g, d)
    kcv = kcv[:, :, _even_odd(nc)].transpose(0, 1, 3, 2, 4).astype(BF16)
    kvt = kv[:, :, 1:].transpose(2, 0, 3, 4, 5, 1).astype(BF16)
    kvt_win = kvt[1].reshape(b, 2, g, d, nt, tq).transpose(0, 1, 2, 4, 3, 5)
    gts = pr3[:, :, COL_MISC:COL_MISC + N_GATE].reshape(b, t, 3, g, hpg).transpose(0, 3, 1, 2, 4)
    gts = jnp.pad(gts.reshape(b, g, t, 3 * hpg), ((0, 0), (0, 0), (0, 0), (0, 128 - 3 * hpg)))
    o = _nsa_prompt(pr3, kcv[0], kcv[1], tabs[0], kvt[0], kvt_win, tabs[1], tabs[2], gts)
    return o.reshape(b * t, Q_W), cmp_rows, sel_rows, win_rows[:, t - min(WINDOW, t):]


def _to_native(rows, keys):
    b, n = rows.shape[:2]
    r = rows.reshape(b, n, 2, NSA_KV_HEADS, NSA_HEAD_DIM).transpose(0, 2, 3, 4, 1)
    return jnp.pad(r, ((0, 0),) * 4 + ((0, keys - n),))


def _nsa_sample_layer(pr3, lw, tabs, layer, cache_cmp, cache_sel, win_state, page_table):
    b, s_new = pr3.shape[:2]
    g, hpg, d = NSA_KV_HEADS, NSA_HPG, NSA_HEAD_DIM
    n_pages = page_table.shape[1]
    past_len = n_pages * PAGE_SIZE
    new = lambda c: pr3[:, :, c:c + KV_W]
    cmp_new, sel_new, win_new = new(COL_CMP), new(COL_SEL), new(COL_WIN)
    nc, half = tabs['nc'], tabs['half']
    per_page = PAGE_SIZE // CMP_BLOCK
    past = cache_cmp[layer][page_table].reshape(b, n_pages, 2, g, d, per_page, CMP_BLOCK)
    past = past.transpose(2, 0, 1, 5, 3, 6, 4).reshape(2, b * n_pages * per_page * g, CMP_BLOCK * d)
    tail = jnp.pad(cmp_new, ((0, 0), (0, tabs['tk'] - past_len - s_new), (0, 0)))
    tail = _flat_blocks(tail.reshape(b, tabs['tk'] - past_len, 2, g, d))
    cmp_w = (lw['pe'], lw['cw1'], lw['cw2'], 512)
    kcv = jnp.concatenate([_compress(past, *cmp_w).reshape(2, b, past_len // CMP_BLOCK, g, d),
                           _compress(tail, *cmp_w).reshape(2, b, nc - past_len // CMP_BLOCK, g, d)], axis=2)
    pad_c = lambda a: jnp.pad(a, ((0, 0), (0, 0), (0, SMP_CPAD - a.shape[2]), (0, 0), (0, 0)))
    kcv = jnp.concatenate([pad_c(kcv[:, :, 0::2]), pad_c(kcv[:, :, 1::2])], axis=2)
    kcv = kcv.transpose(0, 1, 3, 2, 4)
    qg = pr3[:, :, COL_Q:COL_Q + Q_W].reshape(b, s_new, g, hpg, d).transpose(0, 2, 3, 1, 4)
    qg = qg.reshape(b, g, hpg * s_new, d)
    gts = pr3[:, :, COL_MISC:COL_MISC + N_GATE].reshape(b, s_new, 3, g, hpg).transpose(0, 3, 4, 1, 2)
    gts = jnp.pad(gts.reshape(b, g, hpg * s_new, 3), ((0, 0), (0, 0), (0, 0), (0, 128 - 3)))
    win_new_t = _to_native(win_new, KEY_TILE)
    o = _nsa_sample(layer, tabs['tk'] // SEL_BLOCK, page_table, qg, kcv[0], kcv[1], tabs['bias_c'], tabs['ok_c'],
                    tabs['cur'], cache_sel, _to_native(sel_new, KEY_TILE), tabs['bias_s'], tabs['ok_s'],
                    tabs['expand'], win_state, win_new_t, tabs['bias_w'], tabs['ok_w'], gts)
    o = o.reshape(b, g, hpg, s_new, d).transpose(0, 3, 1, 2, 4).reshape(b * s_new, Q_W)
    kv5 = lambda a: a.reshape(b, -1, 2, g, d)
    win_out = jnp.concatenate([win_state[layer][..., s_new:], win_new_t[..., :s_new]], axis=-1)
    return o, kv5(cmp_new), kv5(sel_new), win_out.transpose(0, 4, 1, 2, 3)


def _mixers(pr, b, t, lw, conv0, h0, s0, nsa_fn):
    pr3 = pr.reshape(b, t, PROJ_COLS)
    o_a, cmp_rows, sel_rows, win_rows = nsa_fn(pr3)
    lru_w = (lw['lcw'], lw['lcb'], lw['lgw'], lw['lgb'], lw['lam'])
    gla_w = (lw['gaw'], lw['gab'], lw['gng'])
    s0t = jnp.swapaxes(s0, -1, -2)
    if t % 8 == 0:
        tc = min(t, 256)
        o_b, conv_n, h_n = _lru(pr3, COL_LRU_X // LRU_WIDTH, pr3, COL_LRU_G // LRU_WIDTH, conv0, h0[:, None],
                                *lru_w, tc, tc)
        tg = min(t, 256)
        o_c, s_nt = _gla(pr3, COL_GLA_Q // GLA_QK_W, pr3, COL_GLA_K // GLA_QK_W, pr3, COL_GLA_V // GLA_V_W,
                         pr3, COL_GLA_G // GLA_V_W, pr3, COL_MISC // MISC_W, *gla_w, s0t, tg,
                         min(tg, GLA_CHUNK), tg)
    else:
        tp = -(-t // 8) * 8
        cut = lambda c, w: _pad_rows(pr3[:, :, c:c + w], tp)
        o_b, conv_n, h_n = _lru(cut(COL_LRU_X, LRU_WIDTH), 0, cut(COL_LRU_G, LRU_WIDTH), 0, conv0, h0[:, None],
                                *lru_w, tp, t)
        o_c, s_nt = _gla(cut(COL_GLA_Q, GLA_QK_W), 0, cut(COL_GLA_K, GLA_QK_W), 0, cut(COL_GLA_V, GLA_V_W), 0,
                         cut(COL_GLA_G, GLA_V_W), 0, cut(COL_MISC, MISC_W), 0, *gla_w, s0t, tp, tp, t)
        o_b, o_c = o_b[:, :t], o_c[:, :t]
    o_b = o_b.reshape(b * t, LRU_WIDTH)
    o_c = o_c.reshape(b * t, GLA_V_W)
    return (o_a, o_b, o_c), (cmp_rows, sel_rows, win_rows, conv_n, h_n[:, 0], jnp.swapaxes(s_nt, -1, -2))


def _layer(x, b, t, lw, conv0, h0, s0, nsa_fn, tm):
    pr = _proj(x, lw['ng'], lw['w_in'], lw['b_in'], tm, PROJ_COLS // 5)
    (o_a, o_b, o_c), states = _mixers(pr, b, t, lw, conv0, h0, s0, nsa_fn)
    x = _merge(x, o_a, o_b, o_c, pr, lw['wb'], lw['wo'], min(tm, 512))
    x = _mlp(x, lw['mg'], lw['w1'], lw['w2'], tm, 1024)
    return x, states


def kernel(x_prompt, x_sample, cache_nsa_cmp_kv, cache_nsa_sel_kv, state_nsa_win_kv, state_lru_conv,
           state_lru_h, state_gla, page_table, rel_bias, norm_mix_g, norm_mlp_g, norm_final_g, w_in, b_in,
           nsa_cmp_pe, nsa_cmp_w1, nsa_cmp_w2, lru_gate_w, lru_gate_b, lru_lambda, lru_conv_w, lru_conv_b,
           gla_alpha_w, gla_alpha_b, gla_norm_g, w_branch, w_out, mlp_w1, mlp_w2):
    bp, tp = x_prompt.shape[:2]
    bs, ts = x_sample.shape[:2]
    depth = w_in.shape[0]
    n_pages = page_table.shape[1]
    w_buf = state_nsa_win_kv.shape[2]

    cols = [w_in[..., _SRC[n][0]:_SRC[n][0] + _SRC[n][1]] for n in _DST_ORDER]
    pad = PROJ_COLS - sum(c.shape[-1] for c in cols)
    w_in_p = jnp.concatenate(cols + [jnp.zeros(w_in.shape[:2] + (pad,), w_in.dtype)], axis=-1).astype(BF16)
    bcols = [b_in[..., _SRC[n][0]:_SRC[n][0] + _SRC[n][1]] for n in _DST_ORDER]
    b_in_p = jnp.concatenate(bcols + [jnp.zeros((depth, pad), b_in.dtype)], axis=-1)[:, None, :]
    eye = jnp.eye(LRU_BLOCKS, dtype=lru_gate_w.dtype)
    lgw = jnp.einsum('lznce,nm->lzncme', lru_gate_w, eye).reshape(depth, 2, LRU_WIDTH, LRU_WIDTH).astype(BF16)
    gaw = jnp.zeros((depth, MISC_W, GLA_QK_W), F32).at[:, N_GATE:N_GATE + GLA_RANK].set(gla_alpha_w).astype(BF16)
    pe = jnp.transpose(nsa_cmp_pe, (0, 2, 1, 3)).reshape(depth, 2, 1, CMP_BLOCK * NSA_HEAD_DIM)
    cw1 = nsa_cmp_w1.astype(BF16)
    cw2 = nsa_cmp_w2.astype(BF16)
    wb = w_branch.astype(BF16)
    wo = w_out.astype(BF16)
    w1 = mlp_w1.astype(BF16)
    w2 = mlp_w2.astype(BF16)
    cache_cmp = jnp.transpose(cache_nsa_cmp_kv, (0, 1, 3, 4, 5, 2))
    cache_sel = jnp.transpose(cache_nsa_sel_kv, (0, 1, 3, 4, 5, 2))
    win_state = jnp.transpose(state_nsa_win_kv, (0, 1, 3, 4, 5, 2))

    tabs_p = _prompt_bias_tables(rel_bias, tp)
    tabs_s = _sample_tables(rel_bias, n_pages * PAGE_SIZE, ts, w_buf)

    xp = x_prompt.reshape(bp * tp, D_MODEL)
    xs = x_sample.reshape(bs * ts, D_MODEL)
    conv0_p = jnp.zeros((bp, CONV_WIDTH - 1, LRU_WIDTH), F32)
    h0_p = jnp.zeros((bp, LRU_WIDTH), F32)
    s0_p = jnp.zeros((bp, GLA_HEADS, GLA_DK, GLA_DV), F32)
    outs_p = [[] for _ in range(6)]
    outs_s = [[] for _ in range(6)]
    for l in range(depth):
        lw = dict(ng=norm_mix_g[l][None], mg=norm_mlp_g[l][None], w_in=w_in_p[l], b_in=b_in_p[l],
                  pe=pe[l], cw1=cw1[l], cw2=cw2[l], lcw=lru_conv_w[l], lcb=lru_conv_b[l][None], lgw=lgw[l],
                  lgb=lru_gate_b[l], lam=lru_lambda[l][None], gaw=gaw[l], gab=gla_alpha_b[l][None],
                  gng=gla_norm_g[l][None], wb=wb[l], wo=wo[l], w1=w1[l], w2=w2[l])
        xp, st_p = _layer(xp, bp, tp, lw, conv0_p, h0_p, s0_p,
                          functools.partial(_nsa_prompt_layer, lw=lw, tabs=tabs_p), _row_tile(bp * tp, 1024))
        nsa_s = functools.partial(_nsa_sample_layer, lw=lw, tabs=tabs_s, layer=l, cache_cmp=cache_cmp,
                                  cache_sel=cache_sel, win_state=win_state, page_table=page_table)
        xs, st_s = _layer(xs, bs, ts, lw, state_lru_conv[l], state_lru_h[l], state_gla[l], nsa_s, bs * ts)
        for j in range(6):
            outs_p[j].append(st_p[j])
            outs_s[j].append(st_s[j])
    y_prompt = _final_norm(xp, norm_final_g[None], _row_tile(bp * tp, 1024)).reshape(bp, tp, D_MODEL)
    y_sample = _final_norm(xs, norm_final_g[None], bs * ts).reshape(bs, ts, D_MODEL)
    st = lambda outs, j: jnp.stack(outs[j])
    return (y_prompt, y_sample, st(outs_p, 0), st(outs_s, 0), st(outs_p, 1), st(outs_s, 1),
            st(outs_p, 2), st(outs_s, 2), st(outs_p, 3), st(outs_s, 3), st(outs_p, 4), st(outs_s, 4),
            st(outs_p, 5), st(outs_s, 5))
```

```python
import functools
import math

import jax
import jax.numpy as jnp
import numpy as np
from jax import lax
from jax.experimental import pallas as pl
from jax.experimental.pallas import tpu as pltpu

F32 = jnp.float32
BF16 = jnp.bfloat16

D_MODEL = 1024
DEPTH = 4
PAGE_SIZE = 128
NSA_HEADS = 8
NSA_KV_HEADS = 2
NSA_HPG = NSA_HEADS // NSA_KV_HEADS
NSA_HEAD_DIM = 64
CMP_BLOCK = 32
SEL_BLOCK = 64
N_SELECT = 16
WINDOW = 512
FORCE_SCORE = 1e4
REL_BUCKETS = 32
REL_MAX_DIST = 128
LRU_WIDTH = 512
LRU_BLOCKS = 8
LRU_BLOCK_DIM = LRU_WIDTH // LRU_BLOCKS
CONV_WIDTH = 4
LRU_C = 8.0
GLA_HEADS = 4
GLA_DK = 64
GLA_DV = 128
GLA_RANK = 16
GLA_TAU = 16.0
GLA_CHUNK = 64
D_FF = 4 * D_MODEL
N_BRANCH = 3
BRANCH_WIDTH = 512
NORM_EPS = 1e-6

KV_W = 2 * NSA_KV_HEADS * NSA_HEAD_DIM
Q_W = NSA_HEADS * NSA_HEAD_DIM
GROUP_W = NSA_HPG * NSA_HEAD_DIM
GLA_QK_W = GLA_HEADS * GLA_DK
GLA_V_W = GLA_HEADS * GLA_DV
MISC_W = 128
N_GATE = 3 * NSA_HEADS

COL_MERGE = 0
COL_Q = COL_MERGE + N_BRANCH * D_MODEL
COL_LRU_X = COL_Q + Q_W
COL_LRU_G = COL_LRU_X + LRU_WIDTH
COL_GLA_V = COL_LRU_G + LRU_WIDTH
COL_GLA_G = COL_GLA_V + GLA_V_W
COL_CMP = COL_GLA_G + GLA_V_W
COL_SEL = COL_CMP + KV_W
COL_WIN = COL_SEL + KV_W
COL_GLA_Q = COL_WIN + KV_W
COL_GLA_K = COL_GLA_Q + GLA_QK_W
COL_MISC = COL_GLA_K + GLA_QK_W
PROJ_COLS = COL_MISC + MISC_W

_SRC = {}
_off = 0
for _name, _w in (('nsa_q', Q_W), ('nsa_cmp_kv', KV_W), ('nsa_sel_kv', KV_W), ('nsa_win_kv', KV_W),
                  ('nsa_gate', N_GATE), ('lru_x', LRU_WIDTH), ('lru_gate', LRU_WIDTH),
                  ('gla_q', GLA_QK_W), ('gla_k', GLA_QK_W), ('gla_v', GLA_V_W),
                  ('gla_alpha', GLA_RANK), ('gla_gate', GLA_V_W), ('merge_gate', N_BRANCH * D_MODEL)):
    _SRC[_name] = (_off, _w)
    _off += _w
_DST_ORDER = ('merge_gate', 'nsa_q', 'lru_x', 'lru_gate', 'gla_v', 'gla_gate', 'nsa_cmp_kv',
              'nsa_sel_kv', 'nsa_win_kv', 'gla_q', 'gla_k', 'nsa_gate', 'gla_alpha')

NSA_TQ = 128
NSA_LANES = NSA_HPG * NSA_TQ
NSA_CLASS = 4
VMEM_LIMIT = 56 * 1024 * 1024


def _cparams(sem):
    return pltpu.CompilerParams(dimension_semantics=sem, vmem_limit_bytes=VMEM_LIMIT)


def _gelu(x):
    return x * (0.5 * (1.0 + jnp.tanh(math.sqrt(2.0 / math.pi) * (x + 0.044715 * (x * x * x)))))


def _softplus(x):
    return jnp.maximum(x, 0.0) + jnp.log1p(jnp.exp(-jnp.abs(x)))


def _rms(x, g):
    return x * lax.rsqrt(jnp.mean(x * x, axis=-1, keepdims=True) + NORM_EPS) * g


def _dot(a, b):
    return jnp.dot(a, b, preferred_element_type=F32)


def _dot_nt(a, b):
    return lax.dot_general(a, b, (((1,), (1,)), ((), ())), preferred_element_type=F32)


def _dot_tn(a, b):
    return lax.dot_general(a, b, (((0,), (0,)), ((), ())), preferred_element_type=F32)


def _proj_kernel(x_ref, g_ref, w_ref, b_ref, o_ref, h_ref):
    @pl.when(pl.program_id(1) == 0)
    def _():
        h_ref[...] = _rms(x_ref[...], g_ref[...]).astype(BF16)

    o_ref[...] = _dot(h_ref[...], w_ref[...]) + b_ref[...]


def _proj(x, g, w, b, tm, tn):
    n = x.shape[0]
    return pl.pallas_call(
        _proj_kernel,
        grid=(n // tm, PROJ_COLS // tn),
        in_specs=[pl.BlockSpec((tm, D_MODEL), lambda i, j: (i, 0)),
                  pl.BlockSpec((1, D_MODEL), lambda i, j: (0, 0)),
                  pl.BlockSpec((D_MODEL, tn), lambda i, j: (0, j)),
                  pl.BlockSpec((1, tn), lambda i, j: (0, j))],
        out_specs=pl.BlockSpec((tm, tn), lambda i, j: (i, j)),
        out_shape=jax.ShapeDtypeStruct((n, PROJ_COLS), F32),
        scratch_shapes=[pltpu.VMEM((tm, D_MODEL), BF16)],
        compiler_params=_cparams(("parallel", "arbitrary")),
        name="proj",
    )(x, g, w, b)


def _merge_kernel(x_ref, oa_ref, ob_ref, oc_ref, g0_ref, g1_ref, g2_ref, wb_ref, wo_ref, o_ref):
    m = jax.nn.sigmoid(g0_ref[...]) * _dot(oa_ref[...].astype(BF16), wb_ref[0])
    m = m + jax.nn.sigmoid(g1_ref[...]) * _dot(ob_ref[...].astype(BF16), wb_ref[1])
    m = m + jax.nn.sigmoid(g2_ref[...]) * _dot(oc_ref[...].astype(BF16), wb_ref[2])
    o_ref[...] = x_ref[...] + _dot(m.astype(BF16), wo_ref[...])


def _merge(x, oa, ob, oc, pr, wb, wo, tm):
    n = x.shape[0]
    row = lambda w: pl.BlockSpec((tm, w), lambda i: (i, 0))
    gate = lambda z: pl.BlockSpec((tm, D_MODEL), lambda i: (i, COL_MERGE // D_MODEL + z))
    return pl.pallas_call(
        _merge_kernel,
        grid=(n // tm,),
        in_specs=[row(D_MODEL), row(BRANCH_WIDTH), row(BRANCH_WIDTH), row(BRANCH_WIDTH),
                  gate(0), gate(1), gate(2),
                  pl.BlockSpec((N_BRANCH, BRANCH_WIDTH, D_MODEL), lambda i: (0, 0, 0)),
                  pl.BlockSpec((D_MODEL, D_MODEL), lambda i: (0, 0))],
        out_specs=row(D_MODEL),
        out_shape=jax.ShapeDtypeStruct((n, D_MODEL), F32),
        compiler_params=_cparams(("parallel",)),
        name="merge",
    )(x, oa, ob, oc, pr, pr, pr, wb, wo)


def _mlp_kernel(x_ref, g_ref, w1_ref, w2_ref, o_ref, h_ref, acc_ref):
    f = pl.program_id(1)

    @pl.when(f == 0)
    def _():
        h_ref[...] = _rms(x_ref[...], g_ref[...]).astype(BF16)
        acc_ref[...] = jnp.zeros_like(acc_ref)

    a = jnp.maximum(_dot(h_ref[...], w1_ref[...]), 0.0)
    acc_ref[...] += _dot((a * a).astype(BF16), w2_ref[...])

    @pl.when(f == pl.num_programs(1) - 1)
    def _():
        o_ref[...] = x_ref[...] + acc_ref[...]


def _mlp(x, g, w1, w2, tm, tf):
    n = x.shape[0]
    return pl.pallas_call(
        _mlp_kernel,
        grid=(n // tm, D_FF // tf),
        in_specs=[pl.BlockSpec((tm, D_MODEL), lambda i, f: (i, 0)),
                  pl.BlockSpec((1, D_MODEL), lambda i, f: (0, 0)),
                  pl.BlockSpec((D_MODEL, tf), lambda i, f: (0, f)),
                  pl.BlockSpec((tf, D_MODEL), lambda i, f: (f, 0))],
        out_specs=pl.BlockSpec((tm, D_MODEL), lambda i, f: (i, 0)),
        out_shape=jax.ShapeDtypeStruct((n, D_MODEL), F32),
        scratch_shapes=[pltpu.VMEM((tm, D_MODEL), BF16), pltpu.VMEM((tm, D_MODEL), F32)],
        compiler_params=_cparams(("parallel", "arbitrary")),
        name="mlp",
    )(x, g, w1, w2)


def _norm_kernel(x_ref, g_ref, o_ref):
    o_ref[...] = _rms(x_ref[...], g_ref[...])


def _final_norm(x, g, tm):
    n = x.shape[0]
    return pl.pallas_call(
        _norm_kernel,
        grid=(n // tm,),
        in_specs=[pl.BlockSpec((tm, D_MODEL), lambda i: (i, 0)),
                  pl.BlockSpec((1, D_MODEL), lambda i: (0, 0))],
        out_specs=pl.BlockSpec((tm, D_MODEL), lambda i: (i, 0)),
        out_shape=jax.ShapeDtypeStruct((n, D_MODEL), F32),
        compiler_params=_cparams(("parallel",)),
        name="final_norm",
    )(x, g)


_XB = 8


def _lru_kernel(x_ref, gb_ref, conv0_ref, h0_ref, cw_ref, cb_ref, gw_ref, gbias_ref, lam_ref,
                o_ref, convn_ref, hn_ref, xbuf, hcar, *, tc, tv):
    @pl.when(pl.program_id(1) == 0)
    def _():
        xbuf[0:_XB, :] = jnp.zeros((_XB, LRU_WIDTH), F32)
        xbuf[_XB - 3:_XB, :] = conv0_ref[0]
        hcar[...] = h0_ref[0]

    x = x_ref[0]
    xbuf[_XB:_XB + tc, :] = x
    w = cw_ref[...]
    xc = cb_ref[...] + xbuf[_XB - 3:_XB - 3 + tc, :] * w[0:1]
    xc = xc + xbuf[_XB - 2:_XB - 2 + tc, :] * w[1:2]
    xc = xc + xbuf[_XB - 1:_XB - 1 + tc, :] * w[2:3]
    xc = xc + x * w[3:4]
    tail = xbuf[_XB - 3 + tv:_XB + tv, :]
    convn_ref[0] = tail
    xbuf[_XB - 3:_XB, :] = tail

    xcb = xc.astype(BF16)
    r = jax.nn.sigmoid(_dot(xcb, gw_ref[0]) + gbias_ref[0:1])
    i = jax.nn.sigmoid(_dot(xcb, gw_ref[1]) + gbias_ref[1:2])
    log_a = (-LRU_C * r) * _softplus(-lam_ref[...])
    a = jnp.exp(log_a)
    b = jnp.sqrt(-jnp.tanh(log_a) * (a * a + 1.0)) * (i * xc)

    rows = lax.broadcasted_iota(jnp.int32, (tc, LRU_WIDTH), 0)
    s = 1
    while s < tc:
        a_sh = pltpu.roll(a, s, 0)
        b_sh = pltpu.roll(b, s, 0)
        m = rows >= s
        b = jnp.where(m, a * b_sh + b, b)
        a = jnp.where(m, a * a_sh, a)
        s *= 2
    h = a * hcar[...] + b
    hlast = h[tv - 1:tv]
    hcar[...] = hlast
    hn_ref[0] = hlast
    o_ref[0] = _gelu(gb_ref[0]) * h


def _lru(x_arr, x_blk, gb_arr, gb_blk, conv0, h0, cw, cb, gw, gbias, lam, tc, tv):
    b, t = x_arr.shape[:2]
    r = LRU_WIDTH
    const2 = lambda shape: pl.BlockSpec(shape, lambda i, c: (0, 0))
    return pl.pallas_call(
        functools.partial(_lru_kernel, tc=tc, tv=tv),
        grid=(b, t // tc),
        in_specs=[pl.BlockSpec((1, tc, r), lambda i, c: (i, c, x_blk)),
                  pl.BlockSpec((1, tc, r), lambda i, c: (i, c, gb_blk)),
                  pl.BlockSpec((1, CONV_WIDTH - 1, r), lambda i, c: (i, 0, 0)),
                  pl.BlockSpec((1, 1, r), lambda i, c: (i, 0, 0)),
                  const2((CONV_WIDTH, r)), const2((1, r)),
                  pl.BlockSpec((2, r, r), lambda i, c: (0, 0, 0)),
                  const2((2, r)), const2((1, r))],
        out_specs=[pl.BlockSpec((1, tc, r), lambda i, c: (i, c, 0)),
                   pl.BlockSpec((1, CONV_WIDTH - 1, r), lambda i, c: (i, 0, 0)),
                   pl.BlockSpec((1, 1, r), lambda i, c: (i, 0, 0))],
        out_shape=[jax.ShapeDtypeStruct((b, t, r), F32),
                   jax.ShapeDtypeStruct((b, CONV_WIDTH - 1, r), F32),
                   jax.ShapeDtypeStruct((b, 1, r), F32)],
        scratch_shapes=[pltpu.VMEM((_XB + tc, r), F32), pltpu.VMEM((1, r), F32)],
        compiler_params=_cparams(("parallel", "arbitrary")),
        name="rglru",
    )(x_arr, gb_arr, conv0, h0, cw, cb, gw, gbias, lam)


def _gla_kernel(q_ref, k_ref, v_ref, og_ref, misc_ref, aw_ref, ab_ref, ng_ref, s0_ref,
                o_ref, sn_ref, st, *, tg, ck, tv):
    @pl.when(pl.program_id(1) == 0)
    def _():
        st[...] = s0_ref[0]

    pre = _dot(misc_ref[0].astype(BF16), aw_ref[...]) + ab_ref[...]
    g = -_softplus(-pre) * (1.0 / GLA_TAU)
    if tv < tg:
        g = jnp.where(lax.broadcasted_iota(jnp.int32, g.shape, 0) < tv, g, 0.0)
    q = q_ref[0] * (GLA_DK ** -0.5)
    k = k_ref[0]
    v = v_ref[0]
    og = og_ref[0]
    ng = ng_ref[...]
    rows = lax.broadcasted_iota(jnp.int32, (ck, GLA_QK_W), 0)
    tril = (lax.broadcasted_iota(jnp.int32, (ck, ck), 0) >= lax.broadcasted_iota(jnp.int32, (ck, ck), 1))
    for c in range(tg // ck):
        sl = slice(c * ck, (c + 1) * ck)
        bc = g[sl]
        s = 1
        while s < ck:
            bc = bc + jnp.where(rows >= s, pltpu.roll(bc, s, 0), 0.0)
            s *= 2
        bl = bc[ck - 1:ck]
        e = jnp.exp(bc)
        qi = (q[sl] * e).astype(BF16)
        ki = (k[sl] * jnp.exp(-bc)).astype(BF16)
        kd = (k[sl] * jnp.exp(bl - bc)).astype(BF16)
        dec = jnp.exp(bl)
        vb = v[sl].astype(BF16)
        outs = []
        for h in range(GLA_HEADS):
            ks = slice(h * GLA_DK, (h + 1) * GLA_DK)
            vs = slice(h * GLA_DV, (h + 1) * GLA_DV)
            att = jnp.where(tril, _dot_nt(qi[:, ks], ki[:, ks]), 0.0)
            s_prev = st[h]
            o = _dot(att.astype(BF16), vb[:, vs]) + _dot_nt(qi[:, ks], s_prev.astype(BF16))
            st[h] = s_prev * dec[:, ks] + _dot_tn(vb[:, vs], kd[:, ks])
            o = _rms(o, ng[:, vs])
            ogh = og[sl, vs]
            outs.append(o * (ogh * jax.nn.sigmoid(ogh)))
        o_ref[0, sl, :] = jnp.concatenate(outs, axis=-1)
    sn_ref[0] = st[...]


def _gla(q_arr, q_blk, k_arr, k_blk, v_arr, v_blk, og_arr, og_blk, misc_arr, misc_blk,
         aw, ab, ng, s0t, tg, ck, tv):
    b, t = q_arr.shape[:2]
    col = lambda w, blk: pl.BlockSpec((1, tg, w), lambda i, c: (i, c, blk))
    const2 = lambda shape: pl.BlockSpec(shape, lambda i, c: (0, 0))
    state = pl.BlockSpec((1, GLA_HEADS, GLA_DV, GLA_DK), lambda i, c: (i, 0, 0, 0))
    return pl.pallas_call(
        functools.partial(_gla_kernel, tg=tg, ck=ck, tv=tv),
        grid=(b, t // tg),
        in_specs=[col(GLA_QK_W, q_blk), col(GLA_QK_W, k_blk), col(GLA_V_W, v_blk), col(GLA_V_W, og_blk),
                  col(MISC_W, misc_blk), const2((MISC_W, GLA_QK_W)), const2((1, GLA_QK_W)),
                  const2((1, GLA_V_W)), state],
        out_specs=[pl.BlockSpec((1, tg, GLA_V_W), lambda i, c: (i, c, 0)), state],
        out_shape=[jax.ShapeDtypeStruct((b, t, GLA_V_W), F32),
                   jax.ShapeDtypeStruct((b, GLA_HEADS, GLA_DV, GLA_DK), F32)],
        scratch_shapes=[pltpu.VMEM((GLA_HEADS, GLA_DV, GLA_DK), F32)],
        compiler_params=_cparams(("parallel", "arbitrary")),
        name="gla",
    )(q_arr, k_arr, v_arr, og_arr, misc_arr, aw, ab, ng, s0t)


def _bias_kernel(tbl_ref, dist_ref, o_ref):
    h = pl.program_id(0)
    max_exact = REL_BUCKETS // 2
    n = jnp.maximum(dist_ref[...], 0)
    nf = jnp.maximum(n, 1).astype(F32)
    large = max_exact + (jnp.log(nf / max_exact) / math.log(REL_MAX_DIST / max_exact)
                         * (REL_BUCKETS - max_exact)).astype(jnp.int32)
    bucket = jnp.where(n < max_exact, n, jnp.minimum(large, REL_BUCKETS - 1))
    out = jnp.zeros(bucket.shape, F32)
    for kk in range(REL_BUCKETS):
        out = jnp.where(bucket == kk, tbl_ref[kk, h], out)
    o_ref[0] = out


def _bias_lookup(rel_bias, dist, tr):
    r, c = dist.shape
    return pl.pallas_call(
        _bias_kernel,
        grid=(NSA_HEADS, r // tr),
        in_specs=[pl.BlockSpec(memory_space=pltpu.SMEM),
                  pl.BlockSpec((tr, c), lambda h, i: (i, 0))],
        out_specs=pl.BlockSpec((1, tr, c), lambda h, i: (h, i, 0)),
        out_shape=jax.ShapeDtypeStruct((NSA_HEADS, r, c), F32),
        compiler_params=_cparams(("parallel", "parallel")),
        name="rel_bias",
    )(rel_bias, dist)


def _cmp_kernel(x_ref, pe_ref, w1_ref, w2_ref, o_ref):
    hid = _gelu(_dot((x_ref[0] + pe_ref[0]).astype(BF16), w1_ref[0]))
    o_ref[0] = _dot(hid.astype(BF16), w2_ref[0])


def _row_tile(m, cap):
    for tm in range(min(cap, m) // 8 * 8, 0, -8):
        if m % tm == 0:
            return tm
    return m


def _compress(flat, pe, w1, w2, cap):
    m = flat.shape[1]
    tm = _row_tile(m, cap)
    fw = CMP_BLOCK * NSA_HEAD_DIM
    d = NSA_HEAD_DIM
    return pl.pallas_call(
        _cmp_kernel,
        grid=(2, m // tm),
        in_specs=[pl.BlockSpec((1, tm, fw), lambda z, i: (z, i, 0)),
                  pl.BlockSpec((1, 1, fw), lambda z, i: (z, 0, 0)),
                  pl.BlockSpec((1, fw, d), lambda z, i: (z, 0, 0)),
                  pl.BlockSpec((1, d, d), lambda z, i: (z, 0, 0))],
        out_specs=pl.BlockSpec((1, tm, d), lambda z, i: (z, i, 0)),
        out_shape=jax.ShapeDtypeStruct((2, m, d), F32),
        compiler_params=_cparams(("parallel", "parallel")),
        name="nsa_compress",
    )(flat, pe, w1, w2)


def _cmp_pages_kernel(pt_ref, *refs, n_pages):
    del pt_ref
    pages = refs[:n_pages]
    pe_ref, w1_ref, w2_ref, o_ref, x_scr, h_scr = refs[n_pages:]
    gd = NSA_KV_HEADS * NSA_HEAD_DIM
    nblk = n_pages * PAGE_SIZE // CMP_BLOCK
    for z in range(2):
        for p in range(n_pages):
            x_scr[p * PAGE_SIZE:(p + 1) * PAGE_SIZE, :] = pages[p][0, 0, z].reshape(gd, PAGE_SIZE).T
        acc = jnp.zeros((nblk, gd), F32)
        for t in range(CMP_BLOCK):
            rows = x_scr[pl.ds(t, nblk, stride=CMP_BLOCK), :]
            acc = acc + _dot((rows + pe_ref[z, t:t + 1, :]).astype(BF16), w1_ref[z, t])
        h_scr[...] = _dot(_gelu(acc).astype(BF16), w2_ref[z])
        for par in range(2):
            o_ref[0, z, par] = h_scr[pl.ds(par, nblk // 2, stride=2), :]


def _compress_pages(layer, page_table, cache, pe, w1, w2):
    b, n_pages = page_table.shape
    g, d = NSA_KV_HEADS, NSA_HEAD_DIM
    gd = g * d
    nblk = n_pages * PAGE_SIZE // CMP_BLOCK
    eye = jnp.eye(g, dtype=w1.dtype)
    w1 = jnp.einsum('gh,ztje->ztgjhe', eye, w1).reshape(2, CMP_BLOCK, gd, gd)
    w2 = jnp.einsum('gh,zje->zgjhe', eye, w2).reshape(2, gd, gd)
    pe = jnp.tile(pe, (1, 1, g))
    const = lambda *shape: pl.BlockSpec(shape, lambda i, pt: (0,) * len(shape))

    def page_spec(p):
        return pl.BlockSpec((1, 1, 2, g, d, PAGE_SIZE), lambda i, pt: (layer, pt[i, p], 0, 0, 0, 0))

    out = pl.pallas_call(
        functools.partial(_cmp_pages_kernel, n_pages=n_pages),
        grid_spec=pltpu.PrefetchScalarGridSpec(
            num_scalar_prefetch=1,
            grid=(b,),
            in_specs=[page_spec(p) for p in range(n_pages)]
            + [const(2, CMP_BLOCK, gd), const(2, CMP_BLOCK, gd, gd), const(2, gd, gd)],
            out_specs=pl.BlockSpec((1, 2, 2, nblk // 2, gd), lambda i, pt: (i, 0, 0, 0, 0)),
            scratch_shapes=[pltpu.VMEM((n_pages * PAGE_SIZE, gd), F32), pltpu.VMEM((nblk, gd), F32)]),
        out_shape=jax.ShapeDtypeStruct((b, 2, 2, nblk // 2, gd), F32),
        compiler_params=_cparams(("parallel",)),
        name="nsa_compress_pages",
    )(page_table, *([cache] * n_pages), pe, w1, w2)
    return out.reshape(b, 2, 2, nblk // 2, g, d).transpose(0, 1, 4, 2, 3, 5)


def _nsa_prompt_kernel(q_ref, kc_ref, vc_ref, bc_ref, kst_ref, vst_ref, kwc_ref, vwc_ref, bt_ref, e_ref, gt_ref,
                       o_ref, s_scr, p_scr, m_scr, l_scr, selk_scr, os_scr):
    i = pl.program_id(2)
    tq, d, ck, hpg = NSA_TQ, NSA_HEAD_DIM, NSA_TQ, NSA_HPG
    nchunk = kst_ref.shape[-1] // ck
    nsel = kc_ref.shape[2] // 2
    neg = -jnp.inf
    qf = q_ref[0] * (d ** -0.5)
    qs = jnp.concatenate([qf[:, h * d:(h + 1) * d] for h in range(hpg)], axis=0).astype(BF16)

    t_lane = i * tq + (lax.broadcasted_iota(jnp.int32, (1, hpg * tq), 1) & (tq - 1))
    s_c = _dot_nt(kc_ref[0, 0], qs) + bc_ref[0, 0]
    r_c = lax.broadcasted_iota(jnp.int32, (2 * nsel, hpg * tq), 0)
    n_c = jnp.where(r_c < nsel, 2 * r_c, 2 * (r_c - nsel) + 1)
    s_c = jnp.where(t_lane >= (n_c + 1) * CMP_BLOCK - 1, s_c, neg)
    m_c = jnp.max(s_c, axis=0, keepdims=True)
    m_c = jnp.where(m_c > neg, m_c, 0.0)
    p_c = jnp.exp(s_c - m_c)
    p_c = p_c / jnp.maximum(jnp.sum(p_c, axis=0, keepdims=True), 1e-30)
    o_c = _dot_tn(p_c.astype(BF16), vc_ref[0, 0])

    ph = p_c[:, 0:tq]
    for h in range(1, hpg):
        ph = ph + p_c[:, h * tq:(h + 1) * tq]
    imp = ph[0:nsel] + ph[nsel:2 * nsel]
    blk = lax.broadcasted_iota(jnp.int32, (nsel, tq), 0)
    cur = (i * tq + lax.broadcasted_iota(jnp.int32, (nsel, tq), 1)) >> int(math.log2(SEL_BLOCK))
    forced = (blk == 0) | (blk == cur) | (blk == cur - 1)
    imp = jnp.where(forced, FORCE_SCORE, jnp.where(blk <= cur, imp, neg))
    rank = jnp.zeros((nsel, tq), jnp.int32)
    for s2 in range(nsel):
        row = imp[s2:s2 + 1]
        beats = (row > imp) | ((row == imp) & (blk > s2))
        rank = rank + beats.astype(jnp.int32)
    chosen = (rank < N_SELECT).astype(BF16)
    selk_scr[...] = _dot_tn(chosen, e_ref[...])

    tk = lax.broadcasted_iota(jnp.int32, (tq, ck), 0) - lax.broadcasted_iota(jnp.int32, (tq, ck), 1)

    def mask_heads(s, ok):
        return jnp.concatenate([jnp.where(ok, s[h * tq:(h + 1) * tq], neg) for h in range(hpg)], axis=0)

    def softmax_pv(n, score_chunk, vt):
        m_scr[...] = jnp.full(m_scr.shape, neg, F32)
        for c in range(n):
            s = score_chunk(c)
            s_scr[:, c * ck:(c + 1) * ck] = s
            m_scr[...] = jnp.maximum(m_scr[...], s)
        m = jnp.max(m_scr[...], axis=1, keepdims=True)
        m = jnp.where(m > neg, m, 0.0)
        l_scr[...] = jnp.zeros(l_scr.shape, F32)
        for c in range(n):
            p = jnp.exp(s_scr[:, c * ck:(c + 1) * ck] - m)
            l_scr[...] += p
            p_scr[:, c * ck:(c + 1) * ck] = p.astype(BF16)
        l = jnp.sum(l_scr[...], axis=1, keepdims=True)
        return _dot_nt(p_scr[:, 0:n * ck], vt) / jnp.maximum(l, 1e-30)

    def sel_chunk(c):
        delta = i - c
        s = _dot(qs, kst_ref[0, 0, 0, :, c * ck:(c + 1) * ck]) + bt_ref[0, jnp.clip(delta, 0, 2)]
        ok = (tk + delta * ck >= 0) & (selk_scr[:, c * ck:(c + 1) * ck] > 0.5)
        return mask_heads(s, ok)

    sizes = list(range(NSA_CLASS, nchunk, NSA_CLASS)) + [nchunk]
    for lo, n in zip([0] + sizes[:-1], sizes):
        @pl.when((i >= lo) & (i < n))
        def _(n=n):
            os_scr[...] = softmax_pv(n, sel_chunk, vst_ref[0, 0, 0, :, 0:n * ck])

    nw = WINDOW // ck + 1

    def win_chunk(k):
        delta = nw - 1 - k
        c = i - delta
        s = _dot(qs, kwc_ref[0, 0, 0, jnp.maximum(c, 0)]) + bt_ref[0, min(delta, 2)]
        dist = tk + delta * ck
        ok = (dist >= 0) & (dist < jnp.where(c >= 0, WINDOW, 0))
        return mask_heads(s, ok)

    vw = jnp.concatenate([vwc_ref[0, 0, 0, jnp.maximum(i - (nw - 1 - k), 0)] for k in range(nw)], axis=1)
    o_w = softmax_pv(nw, win_chunk, vw)
    o_s = os_scr[...]

    sig = jax.nn.sigmoid(gt_ref[0, 0])
    outs = []
    for h in range(hpg):
        r = slice(h * tq, (h + 1) * tq)
        outs.append(sig[:, h:h + 1] * o_c[r] + sig[:, hpg + h:hpg + h + 1] * o_s[r]
                    + sig[:, 2 * hpg + h:2 * hpg + h + 1] * o_w[r])
    o_ref[0] = jnp.concatenate(outs, axis=1)


def _nsa_prompt(pr3, kc, vc, bias_c, kvt_sel, kvt_win, bias_t, expand, gts):
    b, t = pr3.shape[:2]
    g, d, tq, hpg = NSA_KV_HEADS, NSA_HEAD_DIM, NSA_TQ, NSA_HPG
    nt = t // tq
    nc = kc.shape[2]
    rows = hpg * tq
    per_bg = lambda r, c: pl.BlockSpec((1, 1, r, c), lambda bi, gi, i: (bi, gi, 0, 0))
    kv_sel = lambda z: pl.BlockSpec((1, 1, 1, d, t), lambda bi, gi, i: (bi, z, gi, 0, 0))
    kv_win = lambda z: pl.BlockSpec((1, 1, 1, nt, d, tq), lambda bi, gi, i: (bi, z, gi, 0, 0, 0))
    return pl.pallas_call(
        _nsa_prompt_kernel,
        grid=(b, g, nt),
        in_specs=[pl.BlockSpec((1, tq, GROUP_W), lambda bi, gi, i: (bi, i, COL_Q // GROUP_W + gi)),
                  per_bg(nc, d), per_bg(nc, d),
                  pl.BlockSpec((1, 1, nc, rows), lambda bi, gi, i: (gi, i, 0, 0)),
                  kv_sel(0), kv_sel(1), kv_win(0), kv_win(1),
                  pl.BlockSpec((1, 3, rows, tq), lambda bi, gi, i: (gi, 0, 0, 0)),
                  pl.BlockSpec((nc // 2, t), lambda bi, gi, i: (0, 0)),
                  pl.BlockSpec((1, 1, tq, 128), lambda bi, gi, i: (bi, gi, i, 0))],
        out_specs=pl.BlockSpec((1, tq, GROUP_W), lambda bi, gi, i: (bi, i, gi)),
        out_shape=jax.ShapeDtypeStruct((b, t, Q_W), F32),
        scratch_shapes=[pltpu.VMEM((rows, t), F32), pltpu.VMEM((rows, t), BF16),
                        pltpu.VMEM((rows, tq), F32), pltpu.VMEM((rows, tq), F32),
                        pltpu.VMEM((tq, t), F32), pltpu.VMEM((rows, d), F32)],
        compiler_params=_cparams(("parallel", "parallel", "arbitrary")),
        name="nsa_prompt",
    )(pr3, kc, vc, bias_c, kvt_sel, kvt_sel, kvt_win, kvt_win, bias_t, expand, gts)


SMP_CPAD = 128
KEY_TILE = PAGE_SIZE


def _masked_softmax(s, ok):
    s = jnp.where(ok, s, -jnp.inf)
    m = jnp.max(s, axis=-1, keepdims=True)
    m = jnp.where(m > -jnp.inf, m, 0.0)
    p = jnp.exp(s - m)
    return p / jnp.maximum(jnp.sum(p, axis=-1, keepdims=True), 1e-30)


def _nsa_sample_kernel(pt_ref, q_ref, kc_ref, vc_ref, bc_ref, okc_ref, cur_ref, *rest, n_pages, n_win, nsel):
    pages = rest[:n_pages]
    (sn_ref, bs_ref, oks_ref, e_ref, wb_ref, wn_ref, bw_ref, okw_ref, gt_ref,
     o_ref, s_scr, w_scr) = rest[n_pages:]
    del pt_ref
    d = NSA_HEAD_DIM
    s_new = q_ref.shape[2] // NSA_HPG
    wkeys = n_win * KEY_TILE
    for g in range(NSA_KV_HEADS):
        q = (q_ref[0, g] * (d ** -0.5)).astype(BF16)

        s_c = _dot_nt(q, kc_ref[0, g].astype(BF16)) + bc_ref[g]
        p_c = _masked_softmax(s_c, okc_ref[...] > 0.5)
        o_c = _dot(p_c.astype(BF16), vc_ref[0, g].astype(BF16))
        ph = p_c[0:s_new]
        for h in range(1, NSA_HPG):
            ph = ph + p_c[h * s_new:(h + 1) * s_new]
        imp = ph[:, 0:SMP_CPAD] + ph[:, SMP_CPAD:2 * SMP_CPAD]
        blk = lax.broadcasted_iota(jnp.int32, imp.shape, 1)
        cur = cur_ref[...]
        forced = (blk == 0) | (blk == cur) | (blk == cur - 1)
        imp = jnp.where(forced, FORCE_SCORE, jnp.where(blk <= cur, imp, -jnp.inf))
        rank = jnp.zeros(imp.shape, jnp.int32)
        for s2 in range(nsel):
            col = imp[:, s2:s2 + 1]
            beats = (col > imp) | ((col == imp) & (blk > s2))
            rank = rank + beats.astype(jnp.int32)
        chosen = ((rank < N_SELECT) & (blk < nsel)).astype(BF16)
        chosen = jnp.concatenate([chosen] * NSA_HPG, axis=0)
        sel_keys = _dot(chosen, e_ref[...])

        for p in range(n_pages):
            s_scr[:, p * KEY_TILE:(p + 1) * KEY_TILE] = _dot(q, pages[p][0, 0, 0, g].astype(BF16))
        s_scr[:, n_pages * KEY_TILE:(n_pages + 1) * KEY_TILE] = _dot(q, sn_ref[0, 0, g].astype(BF16))
        p_s = _masked_softmax(s_scr[...] + bs_ref[g], (oks_ref[...] > 0.5) & (sel_keys > 0.5)).astype(BF16)
        o_s = _dot_nt(p_s[:, n_pages * KEY_TILE:(n_pages + 1) * KEY_TILE], sn_ref[0, 1, g].astype(BF16))
        for p in range(n_pages):
            o_s = o_s + _dot_nt(p_s[:, p * KEY_TILE:(p + 1) * KEY_TILE], pages[p][0, 0, 1, g].astype(BF16))

        w_scr[:, 0:wkeys] = _dot(q, wb_ref[0, 0, 0, g].astype(BF16))
        w_scr[:, wkeys:wkeys + KEY_TILE] = _dot(q, wn_ref[0, 0, g].astype(BF16))
        p_w = _masked_softmax(w_scr[...] + bw_ref[g], okw_ref[...] > 0.5).astype(BF16)
        o_w = (_dot_nt(p_w[:, 0:wkeys], wb_ref[0, 0, 1, g].astype(BF16))
               + _dot_nt(p_w[:, wkeys:wkeys + KEY_TILE], wn_ref[0, 1, g].astype(BF16)))

        gt = jax.nn.sigmoid(gt_ref[0, g])
        o_ref[0, g] = gt[:, 0:1] * o_c + gt[:, 1:2] * o_s + gt[:, 2:3] * o_w


def _nsa_sample(layer, nsel, page_table, qg, kc, vc, bias_c, ok_c, cur, cache_sel, sel_new, bias_s, ok_s,
                expand, win_state, win_new, bias_w, ok_w, gts):
    b = qg.shape[0]
    g = NSA_KV_HEADS
    d = NSA_HEAD_DIM
    rows = qg.shape[2]
    n_pages = page_table.shape[1]
    n_win = win_state.shape[-1] // KEY_TILE
    ks = (n_pages + 1) * KEY_TILE
    kw = (n_win + 1) * KEY_TILE
    per_b = lambda *shape: pl.BlockSpec((1,) + shape, lambda i, pt: (i,) + (0,) * len(shape))
    const = lambda *shape: pl.BlockSpec(shape, lambda i, pt: (0,) * len(shape))

    def page_spec(p):
        return pl.BlockSpec((1, 1, 2, g, d, PAGE_SIZE), lambda i, pt: (layer, pt[i, p], 0, 0, 0, 0))

    in_specs = ([per_b(g, rows, d), per_b(g, 2 * SMP_CPAD, d), per_b(g, 2 * SMP_CPAD, d),
                 const(g, rows, 2 * SMP_CPAD), const(rows, 2 * SMP_CPAD), const(rows // NSA_HPG, SMP_CPAD)]
                + [page_spec(p) for p in range(n_pages)]
                + [per_b(2, g, d, KEY_TILE), const(g, rows, ks), const(rows, ks), const(SMP_CPAD, ks),
                   pl.BlockSpec((1, 1, 2, g, d, n_win * KEY_TILE), lambda i, pt: (layer, i, 0, 0, 0, 0)),
                   per_b(2, g, d, KEY_TILE), const(g, rows, kw), const(rows, kw), per_b(g, rows, 128)])
    return pl.pallas_call(
        functools.partial(_nsa_sample_kernel, n_pages=n_pages, n_win=n_win, nsel=nsel),
        grid_spec=pltpu.PrefetchScalarGridSpec(
            num_scalar_prefetch=1,
            grid=(b,),
            in_specs=in_specs,
            out_specs=pl.BlockSpec((1, g, rows, d), lambda i, pt: (i, 0, 0, 0)),
            scratch_shapes=[pltpu.VMEM((rows, ks), F32), pltpu.VMEM((rows, kw), F32)]),
        out_shape=jax.ShapeDtypeStruct((b, g, rows, d), F32),
        compiler_params=_cparams(("parallel",)),
        name="nsa_sample",
    )(page_table, qg, kc, vc, bias_c, ok_c, cur, *([cache_sel] * n_pages), sel_new, bias_s, ok_s, expand,
      win_state, win_new, bias_w, ok_w, gts)


def _pad_rows(a, rows):
    return jnp.pad(a, ((0, 0), (0, rows - a.shape[1]), (0, 0)))


def _even_odd(n):
    return np.concatenate([np.arange(0, n, 2), np.arange(1, n, 2)])


def _prompt_bias_tables(rel_bias, t):
    g, hpg, tq = NSA_KV_HEADS, NSA_HPG, NSA_TQ
    nt = t // tq
    nc = t // CMP_BLOCK
    end_c = (_even_odd(nc) + 1) * CMP_BLOCK - 1
    dist_c = jnp.asarray(np.arange(t)[None, :] - end_c[:, None], jnp.int32)
    bias_c = _bias_lookup(rel_bias, dist_c, nc)
    bias_c = bias_c.reshape(g, hpg, nc, nt, tq).transpose(0, 3, 2, 1, 4).reshape(g, nt, nc, hpg * tq)
    off = np.arange(3)[:, None, None] * tq
    dist_t = off + np.arange(tq)[None, :, None] - np.arange(tq)[None, None, :]
    assert dist_t[2].min() >= REL_MAX_DIST
    bias_t = _bias_lookup(rel_bias, jnp.asarray(dist_t.reshape(3 * tq, tq), jnp.int32), 3 * tq)
    bias_t = bias_t.reshape(g, hpg, 3, tq, tq).transpose(0, 2, 1, 3, 4).reshape(g, 3, hpg * tq, tq)
    expand = np.arange(t)[None, :] // SEL_BLOCK == np.arange(t // SEL_BLOCK)[:, None]
    return bias_c, bias_t, jnp.asarray(expand, BF16)


def _sample_tables(rel_bias, past_len, s_new, n_win_keys):
    g, hpg = NSA_KV_HEADS, NSA_HPG
    tk = -(-(past_len + s_new) // SEL_BLOCK) * SEL_BLOCK
    nc = tk // CMP_BLOCK
    nsel = tk // SEL_BLOCK
    half = (nc + 1) // 2
    pos_q = past_len + np.arange(s_new)
    lane = np.arange(SMP_CPAD)
    n_of_lane = np.concatenate([2 * lane, 2 * lane + 1])
    real_c = np.concatenate([lane < half, lane < nc - half])
    dist_c = pos_q[:, None] - ((n_of_lane[None, :] + 1) * CMP_BLOCK - 1)
    ok_c = real_c[None, :] & (dist_c >= 0)
    n_keys_s = (past_len // PAGE_SIZE + 1) * KEY_TILE
    key = np.arange(n_keys_s)
    dist_s = pos_q[:, None] - key[None, :]
    ok_s = (key[None, :] < past_len + s_new) & (dist_s >= 0)
    n_keys_w = n_win_keys + KEY_TILE
    i = np.arange(n_keys_w)
    pos_kw = np.where(i < n_win_keys, past_len - n_win_keys + i, past_len + i - n_win_keys)
    dist_w = pos_q[:, None] - pos_kw[None, :]
    ok_w = (i[None, :] < n_win_keys + s_new) & (dist_w >= 0) & (dist_w < WINDOW) & (pos_kw[None, :] >= 0)
    dist = np.concatenate([dist_c, dist_s, dist_w], axis=1)
    dist = np.pad(dist, ((0, 8 - s_new), (0, 0)))
    bias = _bias_lookup(rel_bias, jnp.asarray(dist, jnp.int32), 8)[:, :s_new]
    bias = bias.reshape(g, hpg * s_new, dist.shape[1])
    c0, c1 = 2 * SMP_CPAD, 2 * SMP_CPAD + n_keys_s
    tile = lambda m: jnp.asarray(np.tile(m, (hpg, 1)), F32)
    expand = (key[None, :] // SEL_BLOCK == np.arange(SMP_CPAD)[:, None]) & (np.arange(SMP_CPAD)[:, None] < nsel)
    cur = np.broadcast_to((pos_q // SEL_BLOCK)[:, None], (s_new, SMP_CPAD))
    return dict(bias_c=bias[:, :, :c0], bias_s=bias[:, :, c0:c1], bias_w=bias[:, :, c1:],
                ok_c=tile(ok_c), ok_s=tile(ok_s), ok_w=tile(ok_w),
                expand=jnp.asarray(expand, BF16), cur=jnp.asarray(cur, jnp.int32), nc=nc, half=half, tk=tk)


def _flat_blocks(rows):
    b, tk = rows.shape[:2]
    nc = tk // CMP_BLOCK
    blk = rows.reshape(b, nc // 2, 2, CMP_BLOCK, 2, NSA_KV_HEADS, NSA_HEAD_DIM)
    return jnp.transpose(blk, (4, 0, 5, 2, 1, 3, 6)).reshape(2, b * NSA_KV_HEADS * nc, CMP_BLOCK * NSA_HEAD_DIM)


def _nsa_prompt_layer(pr3, lw, tabs):
    b, t = pr3.shape[:2]
    g, hpg, d, tq = NSA_KV_HEADS, NSA_HPG, NSA_HEAD_DIM, NSA_TQ
    nt = t // tq
    nc = t // CMP_BLOCK
    kv = pr3[:, :, COL_CMP:COL_CMP + 3 * KV_W].reshape(b, t, 3, 2, g, d)
    cmp_rows, sel_rows, win_rows = kv[:, :, 0], kv[:, :, 1], kv[:, :, 2]
    kcv = _compress(_flat_blocks(cmp_rows), lw['pe'], lw['cw1'], lw['cw2'], 512)
    kcv = kcv.reshape(2, b, g, nc, d).astype(BF16)
    kvt = kv[:, :, 1:].transpose(2, 0, 3, 4, 5, 1).astype(BF16)
    kvt_win = kvt[1].reshape(b, 2, g, d, nt, tq).transpose(0, 1, 2, 4, 3, 5)
    gts = pr3[:, :, COL_MISC:COL_MISC + N_GATE].reshape(b, t, 3, g, hpg).transpose(0, 3, 1, 2, 4)
    gts = jnp.pad(gts.reshape(b, g, t, 3 * hpg), ((0, 0), (0, 0), (0, 0), (0, 128 - 3 * hpg)))
    o = _nsa_prompt(pr3, kcv[0], kcv[1], tabs[0], kvt[0], kvt_win, tabs[1], tabs[2], gts)
    return o.reshape(b * t, Q_W), cmp_rows, sel_rows, win_rows[:, t - min(WINDOW, t):]


def _to_native(rows, keys):
    b, n = rows.shape[:2]
    r = rows.reshape(b, n, 2, NSA_KV_HEADS, NSA_HEAD_DIM).transpose(0, 2, 3, 4, 1)
    return jnp.pad(r, ((0, 0),) * 4 + ((0, keys - n),))


def _nsa_sample_layer(pr3, lw, tabs, layer, cache_cmp, cache_sel, win_state, page_table):
    b, s_new = pr3.shape[:2]
    g, hpg, d = NSA_KV_HEADS, NSA_HPG, NSA_HEAD_DIM
    n_pages = page_table.shape[1]
    past_len = n_pages * PAGE_SIZE
    new = lambda c: pr3[:, :, c:c + KV_W]
    cmp_new, sel_new, win_new = new(COL_CMP), new(COL_SEL), new(COL_WIN)
    n_past, n_tail = past_len // CMP_BLOCK, tabs['nc'] - past_len // CMP_BLOCK
    pe3 = lw['pe'].reshape(2, CMP_BLOCK, d)
    past = _compress_pages(layer, page_table, cache_cmp, pe3, lw['cw1'].reshape(2, CMP_BLOCK, d, d), lw['cw2'])
    tail = jnp.pad(cmp_new, ((0, 0), (0, n_tail * CMP_BLOCK - s_new), (0, 0)))
    tail = _compress(_flat_blocks(tail.reshape(b, n_tail * CMP_BLOCK, 2, g, d)), lw['pe'], lw['cw1'], lw['cw2'], 512)
    tail = tail.reshape(2, b, g, 2, n_tail // 2, d).transpose(1, 0, 2, 3, 4, 5)
    kcv = jnp.concatenate([past, tail], axis=4)
    kcv = jnp.pad(kcv, ((0, 0),) * 4 + ((0, SMP_CPAD - kcv.shape[4]), (0, 0)))
    kcv = kcv.reshape(b, 2, g, 2 * SMP_CPAD, d).transpose(1, 0, 2, 3, 4)
    qg = pr3[:, :, COL_Q:COL_Q + Q_W].reshape(b, s_new, g, hpg, d).transpose(0, 2, 3, 1, 4)
    qg = qg.reshape(b, g, hpg * s_new, d)
    gts = pr3[:, :, COL_MISC:COL_MISC + N_GATE].reshape(b, s_new, 3, g, hpg).transpose(0, 3, 4, 1, 2)
    gts = jnp.pad(gts.reshape(b, g, hpg * s_new, 3), ((0, 0), (0, 0), (0, 0), (0, 128 - 3)))
    win_new_t = _to_native(win_new, KEY_TILE)
    o = _nsa_sample(layer, tabs['tk'] // SEL_BLOCK, page_table, qg, kcv[0], kcv[1], tabs['bias_c'], tabs['ok_c'],
                    tabs['cur'], cache_sel, _to_native(sel_new, KEY_TILE), tabs['bias_s'], tabs['ok_s'],
                    tabs['expand'], win_state, win_new_t, tabs['bias_w'], tabs['ok_w'], gts)
    o = o.reshape(b, g, hpg, s_new, d).transpose(0, 3, 1, 2, 4).reshape(b * s_new, Q_W)
    kv5 = lambda a: a.reshape(b, -1, 2, g, d)
    return o, kv5(cmp_new), kv5(sel_new), win_new_t[..., :s_new]


def _mixers(pr, b, t, lw, conv0, h0, s0, nsa_fn):
    pr3 = pr.reshape(b, t, PROJ_COLS)
    o_a, cmp_rows, sel_rows, win_rows = nsa_fn(pr3)
    lru_w = (lw['lcw'], lw['lcb'], lw['lgw'], lw['lgb'], lw['lam'])
    gla_w = (lw['gaw'], lw['gab'], lw['gng'])
    s0t = jnp.swapaxes(s0, -1, -2)
    if t % 8 == 0:
        tc = min(t, 256)
        o_b, conv_n, h_n = _lru(pr3, COL_LRU_X // LRU_WIDTH, pr3, COL_LRU_G // LRU_WIDTH, conv0, h0[:, None],
                                *lru_w, tc, tc)
        tg = min(t, 256)
        o_c, s_nt = _gla(pr3, COL_GLA_Q // GLA_QK_W, pr3, COL_GLA_K // GLA_QK_W, pr3, COL_GLA_V // GLA_V_W,
                         pr3, COL_GLA_G // GLA_V_W, pr3, COL_MISC // MISC_W, *gla_w, s0t, tg,
                         min(tg, GLA_CHUNK), tg)
    else:
        tp = -(-t // 8) * 8
        cut = lambda c, w: _pad_rows(pr3[:, :, c:c + w], tp)
        o_b, conv_n, h_n = _lru(cut(COL_LRU_X, LRU_WIDTH), 0, cut(COL_LRU_G, LRU_WIDTH), 0, conv0, h0[:, None],
                                *lru_w, tp, t)
        o_c, s_nt = _gla(cut(COL_GLA_Q, GLA_QK_W), 0, cut(COL_GLA_K, GLA_QK_W), 0, cut(COL_GLA_V, GLA_V_W), 0,
                         cut(COL_GLA_G, GLA_V_W), 0, cut(COL_MISC, MISC_W), 0, *gla_w, s0t, tp, tp, t)
        o_b, o_c = o_b[:, :t], o_c[:, :t]
    o_b = o_b.reshape(b * t, LRU_WIDTH)
    o_c = o_c.reshape(b * t, GLA_V_W)
    return (o_a, o_b, o_c), (cmp_rows, sel_rows, win_rows, conv_n, h_n[:, 0], jnp.swapaxes(s_nt, -1, -2))


def _layer(x, b, t, lw, conv0, h0, s0, nsa_fn, tm):
    pr = _proj(x, lw['ng'], lw['w_in'], lw['b_in'], tm, PROJ_COLS // 5)
    (o_a, o_b, o_c), states = _mixers(pr, b, t, lw, conv0, h0, s0, nsa_fn)
    x = _merge(x, o_a, o_b, o_c, pr, lw['wb'], lw['wo'], min(tm, 512))
    x = _mlp(x, lw['mg'], lw['w1'], lw['w2'], tm, 1024)
    return x, states


def kernel(x_prompt, x_sample, cache_nsa_cmp_kv, cache_nsa_sel_kv, state_nsa_win_kv, state_lru_conv,
           state_lru_h, state_gla, page_table, rel_bias, norm_mix_g, norm_mlp_g, norm_final_g, w_in, b_in,
           nsa_cmp_pe, nsa_cmp_w1, nsa_cmp_w2, lru_gate_w, lru_gate_b, lru_lambda, lru_conv_w, lru_conv_b,
           gla_alpha_w, gla_alpha_b, gla_norm_g, w_branch, w_out, mlp_w1, mlp_w2):
    bp, tp = x_prompt.shape[:2]
    bs, ts = x_sample.shape[:2]
    depth = w_in.shape[0]
    n_pages = page_table.shape[1]
    w_buf = state_nsa_win_kv.shape[2]

    cols = [w_in[..., _SRC[n][0]:_SRC[n][0] + _SRC[n][1]] for n in _DST_ORDER]
    pad = PROJ_COLS - sum(c.shape[-1] for c in cols)
    w_in_p = jnp.concatenate(cols + [jnp.zeros(w_in.shape[:2] + (pad,), w_in.dtype)], axis=-1).astype(BF16)
    bcols = [b_in[..., _SRC[n][0]:_SRC[n][0] + _SRC[n][1]] for n in _DST_ORDER]
    b_in_p = jnp.concatenate(bcols + [jnp.zeros((depth, pad), b_in.dtype)], axis=-1)[:, None, :]
    eye = jnp.eye(LRU_BLOCKS, dtype=lru_gate_w.dtype)
    lgw = jnp.einsum('lznce,nm->lzncme', lru_gate_w, eye).reshape(depth, 2, LRU_WIDTH, LRU_WIDTH).astype(BF16)
    gaw = jnp.zeros((depth, MISC_W, GLA_QK_W), F32).at[:, N_GATE:N_GATE + GLA_RANK].set(gla_alpha_w).astype(BF16)
    pe = jnp.transpose(nsa_cmp_pe, (0, 2, 1, 3)).reshape(depth, 2, 1, CMP_BLOCK * NSA_HEAD_DIM)
    cw1 = nsa_cmp_w1.astype(BF16)
    cw2 = nsa_cmp_w2.astype(BF16)
    wb = w_branch.astype(BF16)
    wo = w_out.astype(BF16)
    w1 = mlp_w1.astype(BF16)
    w2 = mlp_w2.astype(BF16)
    cache_cmp = jnp.transpose(cache_nsa_cmp_kv, (0, 1, 3, 4, 5, 2))
    cache_sel = jnp.transpose(cache_nsa_sel_kv, (0, 1, 3, 4, 5, 2))
    win_state = jnp.transpose(state_nsa_win_kv, (0, 1, 3, 4, 5, 2))

    tabs_p = _prompt_bias_tables(rel_bias, tp)
    tabs_s = _sample_tables(rel_bias, n_pages * PAGE_SIZE, ts, w_buf)

    xp = x_prompt.reshape(bp * tp, D_MODEL)
    xs = x_sample.reshape(bs * ts, D_MODEL)
    conv0_p = jnp.zeros((bp, CONV_WIDTH - 1, LRU_WIDTH), F32)
    h0_p = jnp.zeros((bp, LRU_WIDTH), F32)
    s0_p = jnp.zeros((bp, GLA_HEADS, GLA_DK, GLA_DV), F32)
    outs_p = [[] for _ in range(6)]
    outs_s = [[] for _ in range(6)]
    for l in range(depth):
        lw = dict(ng=norm_mix_g[l][None], mg=norm_mlp_g[l][None], w_in=w_in_p[l], b_in=b_in_p[l],
                  pe=pe[l], cw1=cw1[l], cw2=cw2[l], lcw=lru_conv_w[l], lcb=lru_conv_b[l][None], lgw=lgw[l],
                  lgb=lru_gate_b[l], lam=lru_lambda[l][None], gaw=gaw[l], gab=gla_alpha_b[l][None],
                  gng=gla_norm_g[l][None], wb=wb[l], wo=wo[l], w1=w1[l], w2=w2[l])
        xp, st_p = _layer(xp, bp, tp, lw, conv0_p, h0_p, s0_p,
                          functools.partial(_nsa_prompt_layer, lw=lw, tabs=tabs_p), _row_tile(bp * tp, 1024))
        nsa_s = functools.partial(_nsa_sample_layer, lw=lw, tabs=tabs_s, layer=l, cache_cmp=cache_cmp,
                                  cache_sel=cache_sel, win_state=win_state, page_table=page_table)
        xs, st_s = _layer(xs, bs, ts, lw, state_lru_conv[l], state_lru_h[l], state_gla[l], nsa_s, bs * ts)
        for j in range(6):
            outs_p[j].append(st_p[j])
            outs_s[j].append(st_s[j])
    y_prompt = _final_norm(xp, norm_final_g[None], _row_tile(bp * tp, 1024)).reshape(bp, tp, D_MODEL)
    y_sample = _final_norm(xs, norm_final_g[None], bs * ts).reshape(bs, ts, D_MODEL)
    st = lambda outs, j: jnp.stack(outs[j])
    win_s = jnp.concatenate([win_state[..., ts:], st(outs_s, 2)], axis=-1).transpose(0, 1, 5, 2, 3, 4)
    return (y_prompt, y_sample, st(outs_p, 0), st(outs_s, 0), st(outs_p, 1), st(outs_s, 1),
            st(outs_p, 2), win_s, st(outs_p, 3), st(outs_s, 3), st(outs_p, 4), st(outs_s, 4),
            st(outs_p, 5), st(outs_s, 5))
```

```python
import functools
import math

import jax
import jax.numpy as jnp
import numpy as np
from jax import lax
from jax.experimental import pallas as pl
from jax.experimental.pallas import tpu as pltpu

F32 = jnp.float32
BF16 = jnp.bfloat16

D_MODEL = 1024
DEPTH = 4
PAGE_SIZE = 128
NSA_HEADS = 8
NSA_KV_HEADS = 2
NSA_HPG = NSA_HEADS // NSA_KV_HEADS
NSA_HEAD_DIM = 64
CMP_BLOCK = 32
SEL_BLOCK = 64
N_SELECT = 16
WINDOW = 512
FORCE_SCORE = 1e4
REL_BUCKETS = 32
REL_MAX_DIST = 128
LRU_WIDTH = 512
LRU_BLOCKS = 8
LRU_BLOCK_DIM = LRU_WIDTH // LRU_BLOCKS
CONV_WIDTH = 4
LRU_C = 8.0
GLA_HEADS = 4
GLA_DK = 64
GLA_DV = 128
GLA_RANK = 16
GLA_TAU = 16.0
GLA_CHUNK = 64
D_FF = 4 * D_MODEL
N_BRANCH = 3
BRANCH_WIDTH = 512
NORM_EPS = 1e-6

KV_W = 2 * NSA_KV_HEADS * NSA_HEAD_DIM
Q_W = NSA_HEADS * NSA_HEAD_DIM
GROUP_W = NSA_HPG * NSA_HEAD_DIM
GLA_QK_W = GLA_HEADS * GLA_DK
GLA_V_W = GLA_HEADS * GLA_DV
MISC_W = 128
N_GATE = 3 * NSA_HEADS

COL_MERGE = 0
COL_Q = COL_MERGE + N_BRANCH * D_MODEL
COL_LRU_X = COL_Q + Q_W
COL_LRU_G = COL_LRU_X + LRU_WIDTH
COL_GLA_V = COL_LRU_G + LRU_WIDTH
COL_GLA_G = COL_GLA_V + GLA_V_W
COL_CMP = COL_GLA_G + GLA_V_W
COL_SEL = COL_CMP + KV_W
COL_WIN = COL_SEL + KV_W
COL_GLA_Q = COL_WIN + KV_W
COL_GLA_K = COL_GLA_Q + GLA_QK_W
COL_MISC = COL_GLA_K + GLA_QK_W
PROJ_COLS = COL_MISC + MISC_W

_SRC = {}
_off = 0
for _name, _w in (('nsa_q', Q_W), ('nsa_cmp_kv', KV_W), ('nsa_sel_kv', KV_W), ('nsa_win_kv', KV_W),
                  ('nsa_gate', N_GATE), ('lru_x', LRU_WIDTH), ('lru_gate', LRU_WIDTH),
                  ('gla_q', GLA_QK_W), ('gla_k', GLA_QK_W), ('gla_v', GLA_V_W),
                  ('gla_alpha', GLA_RANK), ('gla_gate', GLA_V_W), ('merge_gate', N_BRANCH * D_MODEL)):
    _SRC[_name] = (_off, _w)
    _off += _w
_DST_ORDER = ('merge_gate', 'nsa_q', 'lru_x', 'lru_gate', 'gla_v', 'gla_gate', 'nsa_cmp_kv',
              'nsa_sel_kv', 'nsa_win_kv', 'gla_q', 'gla_k', 'nsa_gate', 'gla_alpha')

NSA_TQ = 128
NSA_LANES = NSA_HPG * NSA_TQ
NSA_CLASS = 2
VMEM_LIMIT = 56 * 1024 * 1024


def _cparams(sem):
    return pltpu.CompilerParams(dimension_semantics=sem, vmem_limit_bytes=VMEM_LIMIT)


def _gelu(x):
    return x * (0.5 * (1.0 + jnp.tanh(math.sqrt(2.0 / math.pi) * (x + 0.044715 * (x * x * x)))))


def _softplus(x):
    return jnp.maximum(x, 0.0) + jnp.log1p(jnp.exp(-jnp.abs(x)))


def _rms(x, g):
    return x * lax.rsqrt(jnp.mean(x * x, axis=-1, keepdims=True) + NORM_EPS) * g


def _dot(a, b):
    return jnp.dot(a, b, preferred_element_type=F32)


def _dot_nt(a, b):
    return lax.dot_general(a, b, (((1,), (1,)), ((), ())), preferred_element_type=F32)


def _dot_tn(a, b):
    return lax.dot_general(a, b, (((0,), (0,)), ((), ())), preferred_element_type=F32)


def _proj_kernel(x_ref, g_ref, w_ref, b_ref, o_ref, h_ref):
    @pl.when(pl.program_id(1) == 0)
    def _():
        h_ref[...] = _rms(x_ref[...], g_ref[...]).astype(BF16)

    o_ref[...] = _dot(h_ref[...], w_ref[...]) + b_ref[...]


def _proj(x, g, w, b, tm, tn):
    n = x.shape[0]
    return pl.pallas_call(
        _proj_kernel,
        grid=(n // tm, PROJ_COLS // tn),
        in_specs=[pl.BlockSpec((tm, D_MODEL), lambda i, j: (i, 0)),
                  pl.BlockSpec((1, D_MODEL), lambda i, j: (0, 0)),
                  pl.BlockSpec((D_MODEL, tn), lambda i, j: (0, j)),
                  pl.BlockSpec((1, tn), lambda i, j: (0, j))],
        out_specs=pl.BlockSpec((tm, tn), lambda i, j: (i, j)),
        out_shape=jax.ShapeDtypeStruct((n, PROJ_COLS), F32),
        scratch_shapes=[pltpu.VMEM((tm, D_MODEL), BF16)],
        compiler_params=_cparams(("parallel", "arbitrary")),
        name="proj",
    )(x, g, w, b)


def _merge_kernel(x_ref, oa_ref, ob_ref, oc_ref, g0_ref, g1_ref, g2_ref, wb_ref, wo_ref, o_ref):
    m = jax.nn.sigmoid(g0_ref[...]) * _dot(oa_ref[...].astype(BF16), wb_ref[0])
    m = m + jax.nn.sigmoid(g1_ref[...]) * _dot(ob_ref[...].astype(BF16), wb_ref[1])
    m = m + jax.nn.sigmoid(g2_ref[...]) * _dot(oc_ref[...].astype(BF16), wb_ref[2])
    o_ref[...] = x_ref[...] + _dot(m.astype(BF16), wo_ref[...])


def _merge(x, oa, ob, oc, pr, wb, wo, tm):
    n = x.shape[0]
    row = lambda w: pl.BlockSpec((tm, w), lambda i: (i, 0))
    gate = lambda z: pl.BlockSpec((tm, D_MODEL), lambda i: (i, COL_MERGE // D_MODEL + z))
    return pl.pallas_call(
        _merge_kernel,
        grid=(n // tm,),
        in_specs=[row(D_MODEL), row(BRANCH_WIDTH), row(BRANCH_WIDTH), row(BRANCH_WIDTH),
                  gate(0), gate(1), gate(2),
                  pl.BlockSpec((N_BRANCH, BRANCH_WIDTH, D_MODEL), lambda i: (0, 0, 0)),
                  pl.BlockSpec((D_MODEL, D_MODEL), lambda i: (0, 0))],
        out_specs=row(D_MODEL),
        out_shape=jax.ShapeDtypeStruct((n, D_MODEL), F32),
        compiler_params=_cparams(("parallel",)),
        name="merge",
    )(x, oa, ob, oc, pr, pr, pr, wb, wo)


def _mlp_kernel(x_ref, g_ref, w1_ref, w2_ref, o_ref, h_ref, acc_ref):
    f = pl.program_id(1)

    @pl.when(f == 0)
    def _():
        h_ref[...] = _rms(x_ref[...], g_ref[...]).astype(BF16)
        acc_ref[...] = jnp.zeros_like(acc_ref)

    a = jnp.maximum(_dot(h_ref[...], w1_ref[...]), 0.0)
    acc_ref[...] += _dot((a * a).astype(BF16), w2_ref[...])

    @pl.when(f == pl.num_programs(1) - 1)
    def _():
        o_ref[...] = x_ref[...] + acc_ref[...]


def _mlp(x, g, w1, w2, tm, tf):
    n = x.shape[0]
    return pl.pallas_call(
        _mlp_kernel,
        grid=(n // tm, D_FF // tf),
        in_specs=[pl.BlockSpec((tm, D_MODEL), lambda i, f: (i, 0)),
                  pl.BlockSpec((1, D_MODEL), lambda i, f: (0, 0)),
                  pl.BlockSpec((D_MODEL, tf), lambda i, f: (0, f)),
                  pl.BlockSpec((tf, D_MODEL), lambda i, f: (f, 0))],
        out_specs=pl.BlockSpec((tm, D_MODEL), lambda i, f: (i, 0)),
        out_shape=jax.ShapeDtypeStruct((n, D_MODEL), F32),
        scratch_shapes=[pltpu.VMEM((tm, D_MODEL), BF16), pltpu.VMEM((tm, D_MODEL), F32)],
        compiler_params=_cparams(("parallel", "arbitrary")),
        name="mlp",
    )(x, g, w1, w2)


def _norm_kernel(x_ref, g_ref, o_ref):
    o_ref[...] = _rms(x_ref[...], g_ref[...])


def _final_norm(x, g, tm):
    n = x.shape[0]
    return pl.pallas_call(
        _norm_kernel,
        grid=(n // tm,),
        in_specs=[pl.BlockSpec((tm, D_MODEL), lambda i: (i, 0)),
                  pl.BlockSpec((1, D_MODEL), lambda i: (0, 0))],
        out_specs=pl.BlockSpec((tm, D_MODEL), lambda i: (i, 0)),
        out_shape=jax.ShapeDtypeStruct((n, D_MODEL), F32),
        compiler_params=_cparams(("parallel",)),
        name="final_norm",
    )(x, g)


_XB = 8


def _lru_kernel(x_ref, gb_ref, conv0_ref, h0_ref, cw_ref, cb_ref, gw_ref, gbias_ref, lam_ref,
                o_ref, convn_ref, hn_ref, xbuf, hcar, *, tc, tv):
    @pl.when(pl.program_id(1) == 0)
    def _():
        xbuf[0:_XB, :] = jnp.zeros((_XB, LRU_WIDTH), F32)
        xbuf[_XB - 3:_XB, :] = conv0_ref[0]
        hcar[...] = h0_ref[0]

    x = x_ref[0]
    xbuf[_XB:_XB + tc, :] = x
    w = cw_ref[...]
    xc = cb_ref[...] + xbuf[_XB - 3:_XB - 3 + tc, :] * w[0:1]
    xc = xc + xbuf[_XB - 2:_XB - 2 + tc, :] * w[1:2]
    xc = xc + xbuf[_XB - 1:_XB - 1 + tc, :] * w[2:3]
    xc = xc + x * w[3:4]
    tail = xbuf[_XB - 3 + tv:_XB + tv, :]
    convn_ref[0] = tail
    xbuf[_XB - 3:_XB, :] = tail

    xcb = xc.astype(BF16)
    r = jax.nn.sigmoid(_dot(xcb, gw_ref[0]) + gbias_ref[0:1])
    i = jax.nn.sigmoid(_dot(xcb, gw_ref[1]) + gbias_ref[1:2])
    log_a = (-LRU_C * r) * _softplus(-lam_ref[...])
    a = jnp.exp(log_a)
    b = jnp.sqrt(-jnp.tanh(log_a) * (a * a + 1.0)) * (i * xc)

    rows = lax.broadcasted_iota(jnp.int32, (tc, LRU_WIDTH), 0)
    s = 1
    while s < tc:
        a_sh = pltpu.roll(a, s, 0)
        b_sh = pltpu.roll(b, s, 0)
        m = rows >= s
        b = jnp.where(m, a * b_sh + b, b)
        a = jnp.where(m, a * a_sh, a)
        s *= 2
    h = a * hcar[...] + b
    hlast = h[tv - 1:tv]
    hcar[...] = hlast
    hn_ref[0] = hlast
    o_ref[0] = _gelu(gb_ref[0]) * h


def _lru(x_arr, x_blk, gb_arr, gb_blk, conv0, h0, cw, cb, gw, gbias, lam, tc, tv):
    b, t = x_arr.shape[:2]
    r = LRU_WIDTH
    const2 = lambda shape: pl.BlockSpec(shape, lambda i, c: (0, 0))
    return pl.pallas_call(
        functools.partial(_lru_kernel, tc=tc, tv=tv),
        grid=(b, t // tc),
        in_specs=[pl.BlockSpec((1, tc, r), lambda i, c: (i, c, x_blk)),
                  pl.BlockSpec((1, tc, r), lambda i, c: (i, c, gb_blk)),
                  pl.BlockSpec((1, CONV_WIDTH - 1, r), lambda i, c: (i, 0, 0)),
                  pl.BlockSpec((1, 1, r), lambda i, c: (i, 0, 0)),
                  const2((CONV_WIDTH, r)), const2((1, r)),
                  pl.BlockSpec((2, r, r), lambda i, c: (0, 0, 0)),
                  const2((2, r)), const2((1, r))],
        out_specs=[pl.BlockSpec((1, tc, r), lambda i, c: (i, c, 0)),
                   pl.BlockSpec((1, CONV_WIDTH - 1, r), lambda i, c: (i, 0, 0)),
                   pl.BlockSpec((1, 1, r), lambda i, c: (i, 0, 0))],
        out_shape=[jax.ShapeDtypeStruct((b, t, r), F32),
                   jax.ShapeDtypeStruct((b, CONV_WIDTH - 1, r), F32),
                   jax.ShapeDtypeStruct((b, 1, r), F32)],
        scratch_shapes=[pltpu.VMEM((_XB + tc, r), F32), pltpu.VMEM((1, r), F32)],
        compiler_params=_cparams(("parallel", "arbitrary")),
        name="rglru",
    )(x_arr, gb_arr, conv0, h0, cw, cb, gw, gbias, lam)


def _gla_kernel(q_ref, k_ref, v_ref, og_ref, misc_ref, aw_ref, ab_ref, ng_ref, s0_ref,
                o_ref, sn_ref, st, *, tg, ck, tv):
    @pl.when(pl.program_id(1) == 0)
    def _():
        for h in range(GLA_HEADS):
            st[h] = s0_ref[0, h].T

    pre = _dot(misc_ref[0].astype(BF16), aw_ref[...]) + ab_ref[...]
    g = -_softplus(-pre) * (1.0 / GLA_TAU)
    if tv < tg:
        g = jnp.where(lax.broadcasted_iota(jnp.int32, g.shape, 0) < tv, g, 0.0)
    q = q_ref[0] * (GLA_DK ** -0.5)
    k = k_ref[0]
    v = v_ref[0]
    og = og_ref[0]
    ng = ng_ref[...]
    rows = lax.broadcasted_iota(jnp.int32, (ck, GLA_QK_W), 0)
    tril = (lax.broadcasted_iota(jnp.int32, (ck, ck), 0) >= lax.broadcasted_iota(jnp.int32, (ck, ck), 1))
    for c in range(tg // ck):
        sl = slice(c * ck, (c + 1) * ck)
        bc = g[sl]
        s = 1
        while s < ck:
            bc = bc + jnp.where(rows >= s, pltpu.roll(bc, s, 0), 0.0)
            s *= 2
        bl = bc[ck - 1:ck]
        e = jnp.exp(bc)
        qi = (q[sl] * e).astype(BF16)
        ki = (k[sl] * jnp.exp(-bc)).astype(BF16)
        kd = (k[sl] * jnp.exp(bl - bc)).astype(BF16)
        dec = jnp.exp(bl)
        vb = v[sl].astype(BF16)
        outs = []
        for h in range(GLA_HEADS):
            ks = slice(h * GLA_DK, (h + 1) * GLA_DK)
            vs = slice(h * GLA_DV, (h + 1) * GLA_DV)
            att = jnp.where(tril, _dot_nt(qi[:, ks], ki[:, ks]), 0.0)
            s_prev = st[h]
            o = _dot(att.astype(BF16), vb[:, vs]) + _dot_nt(qi[:, ks], s_prev.astype(BF16))
            st[h] = s_prev * dec[:, ks] + _dot_tn(vb[:, vs], kd[:, ks])
            o = _rms(o, ng[:, vs])
            ogh = og[sl, vs]
            outs.append(o * (ogh * jax.nn.sigmoid(ogh)))
        o_ref[0, sl, :] = jnp.concatenate(outs, axis=-1)

    @pl.when(pl.program_id(1) == pl.num_programs(1) - 1)
    def _():
        for h in range(GLA_HEADS):
            sn_ref[0, h] = st[h].T


def _gla(q_arr, q_blk, k_arr, k_blk, v_arr, v_blk, og_arr, og_blk, misc_arr, misc_blk,
         aw, ab, ng, s0, tg, ck, tv):
    b, t = q_arr.shape[:2]
    col = lambda w, blk: pl.BlockSpec((1, tg, w), lambda i, c: (i, c, blk))
    const2 = lambda shape: pl.BlockSpec(shape, lambda i, c: (0, 0))
    state = pl.BlockSpec((1, GLA_HEADS, GLA_DK, GLA_DV), lambda i, c: (i, 0, 0, 0))
    return pl.pallas_call(
        functools.partial(_gla_kernel, tg=tg, ck=ck, tv=tv),
        grid=(b, t // tg),
        in_specs=[col(GLA_QK_W, q_blk), col(GLA_QK_W, k_blk), col(GLA_V_W, v_blk), col(GLA_V_W, og_blk),
                  col(MISC_W, misc_blk), const2((MISC_W, GLA_QK_W)), const2((1, GLA_QK_W)),
                  const2((1, GLA_V_W)), state],
        out_specs=[pl.BlockSpec((1, tg, GLA_V_W), lambda i, c: (i, c, 0)), state],
        out_shape=[jax.ShapeDtypeStruct((b, t, GLA_V_W), F32),
                   jax.ShapeDtypeStruct((b, GLA_HEADS, GLA_DK, GLA_DV), F32)],
        scratch_shapes=[pltpu.VMEM((GLA_HEADS, GLA_DV, GLA_DK), F32)],
        compiler_params=_cparams(("parallel", "arbitrary")),
        name="gla",
    )(q_arr, k_arr, v_arr, og_arr, misc_arr, aw, ab, ng, s0)


def _bias_kernel(tbl_ref, dist_ref, o_ref):
    h = pl.program_id(0)
    max_exact = REL_BUCKETS // 2
    n = jnp.maximum(dist_ref[...], 0)
    nf = jnp.maximum(n, 1).astype(F32)
    large = max_exact + (jnp.log(nf / max_exact) / math.log(REL_MAX_DIST / max_exact)
                         * (REL_BUCKETS - max_exact)).astype(jnp.int32)
    bucket = jnp.where(n < max_exact, n, jnp.minimum(large, REL_BUCKETS - 1))
    out = jnp.zeros(bucket.shape, F32)
    for kk in range(REL_BUCKETS):
        out = jnp.where(bucket == kk, tbl_ref[kk, h], out)
    o_ref[0] = out


def _bias_lookup(rel_bias, dist, tr):
    r, c = dist.shape
    return pl.pallas_call(
        _bias_kernel,
        grid=(NSA_HEADS, r // tr),
        in_specs=[pl.BlockSpec(memory_space=pltpu.SMEM),
                  pl.BlockSpec((tr, c), lambda h, i: (i, 0))],
        out_specs=pl.BlockSpec((1, tr, c), lambda h, i: (h, i, 0)),
        out_shape=jax.ShapeDtypeStruct((NSA_HEADS, r, c), F32),
        compiler_params=_cparams(("parallel", "parallel")),
        name="rel_bias",
    )(rel_bias, dist)


def _cmp_kernel(x_ref, pe_ref, w1_ref, w2_ref, o_ref):
    hid = _gelu(_dot((x_ref[0] + pe_ref[0]).astype(BF16), w1_ref[0]))
    o_ref[0] = _dot(hid.astype(BF16), w2_ref[0])


def _row_tile(m, cap):
    for tm in range(min(cap, m) // 8 * 8, 0, -8):
        if m % tm == 0:
            return tm
    return m


def _compress(flat, pe, w1, w2, cap):
    m = flat.shape[1]
    tm = _row_tile(m, cap)
    fw = CMP_BLOCK * NSA_HEAD_DIM
    d = NSA_HEAD_DIM
    return pl.pallas_call(
        _cmp_kernel,
        grid=(2, m // tm),
        in_specs=[pl.BlockSpec((1, tm, fw), lambda z, i: (z, i, 0)),
                  pl.BlockSpec((1, 1, fw), lambda z, i: (z, 0, 0)),
                  pl.BlockSpec((1, fw, d), lambda z, i: (z, 0, 0)),
                  pl.BlockSpec((1, d, d), lambda z, i: (z, 0, 0))],
        out_specs=pl.BlockSpec((1, tm, d), lambda z, i: (z, i, 0)),
        out_shape=jax.ShapeDtypeStruct((2, m, d), F32),
        compiler_params=_cparams(("parallel", "parallel")),
        name="nsa_compress",
    )(flat, pe, w1, w2)


def _cmp_pages_kernel(pt_ref, *refs, n_pages):
    del pt_ref
    pages = refs[:n_pages]
    pe_ref, w1_ref, w2_ref, o_ref, x_scr, h_scr = refs[n_pages:]
    gd = NSA_KV_HEADS * NSA_HEAD_DIM
    nblk = n_pages * PAGE_SIZE // CMP_BLOCK
    for z in range(2):
        for p in range(n_pages):
            x_scr[z, p * PAGE_SIZE:(p + 1) * PAGE_SIZE, :] = pages[p][0, 0, z].reshape(gd, PAGE_SIZE).T
        acc = jnp.zeros((nblk, gd), F32)
        for t in range(CMP_BLOCK):
            rows = x_scr[z, pl.ds(t, nblk, stride=CMP_BLOCK), :]
            acc = acc + _dot((rows + pe_ref[z, t:t + 1, :]).astype(BF16), w1_ref[z, t])
        h_scr[z] = _dot(_gelu(acc).astype(BF16), w2_ref[z])
        for par in range(2):
            o_ref[0, z, par] = h_scr[z, pl.ds(par, nblk // 2, stride=2), :]


def _compress_pages(layer, page_table, cache, pe, w1, w2):
    b, n_pages = page_table.shape
    g, d = NSA_KV_HEADS, NSA_HEAD_DIM
    gd = g * d
    nblk = n_pages * PAGE_SIZE // CMP_BLOCK
    eye = jnp.eye(g, dtype=w1.dtype)
    w1 = jnp.einsum('gh,ztje->ztgjhe', eye, w1).reshape(2, CMP_BLOCK, gd, gd)
    w2 = jnp.einsum('gh,zje->zgjhe', eye, w2).reshape(2, gd, gd)
    pe = jnp.tile(pe, (1, 1, g))
    const = lambda *shape: pl.BlockSpec(shape, lambda i, pt: (0,) * len(shape))

    def page_spec(p):
        return pl.BlockSpec((1, 1, 2, g, d, PAGE_SIZE), lambda i, pt: (layer, pt[i, p], 0, 0, 0, 0))

    out = pl.pallas_call(
        functools.partial(_cmp_pages_kernel, n_pages=n_pages),
        grid_spec=pltpu.PrefetchScalarGridSpec(
            num_scalar_prefetch=1,
            grid=(b,),
            in_specs=[page_spec(p) for p in range(n_pages)]
            + [const(2, CMP_BLOCK, gd), const(2, CMP_BLOCK, gd, gd), const(2, gd, gd)],
            out_specs=pl.BlockSpec((1, 2, 2, nblk // 2, gd), lambda i, pt: (i, 0, 0, 0, 0)),
            scratch_shapes=[pltpu.VMEM((2, n_pages * PAGE_SIZE, gd), F32), pltpu.VMEM((2, nblk, gd), F32)]),
        out_shape=jax.ShapeDtypeStruct((b, 2, 2, nblk // 2, gd), F32),
        compiler_params=_cparams(("parallel",)),
        name="nsa_compress_pages",
    )(page_table, *([cache] * n_pages), pe, w1, w2)
    return out.reshape(b, 2, 2, nblk // 2, g, d).transpose(0, 1, 4, 2, 3, 5)


def _nsa_prompt_kernel(q_ref, kc_ref, vc_ref, bc_ref, ks_ref, vs_ref, kw_ref, vw_ref, bt_ref, e_ref, gt_ref,
                       o_ref, s_scr, p_scr, m_scr, l_scr, selk_scr, os_scr, ow_scr):
    i = pl.program_id(2)
    tq, d, ck, hpg = NSA_TQ, NSA_HEAD_DIM, NSA_TQ, NSA_HPG
    nchunk = ks_ref.shape[3]
    nsel = kc_ref.shape[2] // 2
    neg = -jnp.inf
    qf = q_ref[0] * (d ** -0.5)
    qs = jnp.concatenate([qf[:, h * d:(h + 1) * d] for h in range(hpg)], axis=0).astype(BF16)

    tk = lax.broadcasted_iota(jnp.int32, (tq, ck), 0) - lax.broadcasted_iota(jnp.int32, (tq, ck), 1)

    def mask_heads(s, ok):
        return jnp.concatenate([jnp.where(ok, s[h * tq:(h + 1) * tq], neg) for h in range(hpg)], axis=0)

    def attend(k_ref, v_ref, n, ok_fn):
        m_scr[...] = jnp.full(m_scr.shape, neg, F32)
        for r in range(n):
            c = i - r
            cc = jnp.maximum(c, 0)
            s = _dot(qs, k_ref[0, 0, 0, cc])
            if r < 2:
                s = s + bt_ref[0, r]
            s = mask_heads(s, ok_fn(r, c, cc))
            s_scr[:, r * ck:(r + 1) * ck] = s
            m_scr[...] = jnp.maximum(m_scr[...], s)
        m = jnp.max(m_scr[...], axis=1, keepdims=True)
        m = jnp.where(m > neg, m, 0.0)
        l_scr[...] = jnp.zeros(l_scr.shape, F32)
        for r in range(n):
            p = jnp.exp(s_scr[:, r * ck:(r + 1) * ck] - m)
            l_scr[...] += p
            p_scr[:, r * ck:(r + 1) * ck] = p.astype(BF16)
        l = jnp.sum(l_scr[...], axis=1, keepdims=True)
        vt = jnp.concatenate([v_ref[0, 0, 0, jnp.maximum(i - r, 0)] for r in range(n)], axis=1)
        return _dot_nt(p_scr[:, 0:n * ck], vt) / jnp.maximum(l, 1e-30)

    nw = WINDOW // ck + 1

    def win_ok(r, c, cc):
        if r == 0:
            return tk >= 0
        if r == nw - 1:
            return tk < jnp.where(c >= 0, 0, -tq)
        return tk > jnp.where(c >= 0, -tq, tq)

    ow_scr[...] = attend(kw_ref, vw_ref, nw, win_ok)

    t_lane = i * tq + (lax.broadcasted_iota(jnp.int32, (1, hpg * tq), 1) & (tq - 1))
    s_c = _dot_nt(kc_ref[0, 0], qs) + bc_ref[0, 0]
    r_c = lax.broadcasted_iota(jnp.int32, (2 * nsel, hpg * tq), 0)
    n_c = jnp.where(r_c < nsel, 2 * r_c, 2 * (r_c - nsel) + 1)
    s_c = jnp.where(t_lane >= (n_c + 1) * CMP_BLOCK - 1, s_c, neg)
    m_c = jnp.max(s_c, axis=0, keepdims=True)
    m_c = jnp.where(m_c > neg, m_c, 0.0)
    p_c = jnp.exp(s_c - m_c)
    p_c = p_c / jnp.maximum(jnp.sum(p_c, axis=0, keepdims=True), 1e-30)
    o_c = _dot_tn(p_c.astype(BF16), vc_ref[0, 0])

    ph = p_c[:, 0:tq]
    for h in range(1, hpg):
        ph = ph + p_c[:, h * tq:(h + 1) * tq]
    imp = ph[0:nsel] + ph[nsel:2 * nsel]
    blk = lax.broadcasted_iota(jnp.int32, (nsel, tq), 0)
    cur = (i * tq + lax.broadcasted_iota(jnp.int32, (nsel, tq), 1)) >> int(math.log2(SEL_BLOCK))
    forced = (blk == 0) | (blk == cur) | (blk == cur - 1)
    imp = jnp.where(forced, FORCE_SCORE, jnp.where(blk <= cur, imp, neg))
    rank = jnp.zeros((nsel, tq), jnp.int32)
    for s2 in range(nsel):
        row = imp[s2:s2 + 1]
        beats = (row > imp) | ((row == imp) & (blk > s2))
        rank = rank + beats.astype(jnp.int32)
    chosen = (rank < N_SELECT).astype(BF16)
    selk = _dot_tn(chosen, e_ref[...])
    for c in range(nchunk):
        selk_scr[c] = selk[:, c * ck:(c + 1) * ck]

    def sel_ok(r, c, cc):
        ok = selk_scr[cc] > jnp.where(c >= 0, 0.5, 2.0)
        return ok & (tk >= 0) if r == 0 else ok

    sizes = list(range(NSA_CLASS, nchunk, NSA_CLASS)) + [nchunk]
    for lo, n in zip([0] + sizes[:-1], sizes):
        @pl.when((i >= lo) & (i < n))
        def _(n=n):
            os_scr[...] = attend(ks_ref, vs_ref, n, sel_ok)

    o_w = ow_scr[...]
    o_s = os_scr[...]

    sig = jax.nn.sigmoid(gt_ref[0, 0])
    outs = []
    for h in range(hpg):
        r = slice(h * tq, (h + 1) * tq)
        outs.append(sig[:, h:h + 1] * o_c[r] + sig[:, hpg + h:hpg + h + 1] * o_s[r]
                    + sig[:, 2 * hpg + h:2 * hpg + h + 1] * o_w[r])
    o_ref[0] = jnp.concatenate(outs, axis=1)


def _nsa_prompt(pr3, kc, vc, bias_c, kvt, bias_t, expand, gts):
    b, t = pr3.shape[:2]
    g, d, tq, hpg = NSA_KV_HEADS, NSA_HEAD_DIM, NSA_TQ, NSA_HPG
    nt = t // tq
    nc = kc.shape[2]
    rows = hpg * tq
    per_bg = lambda r, c: pl.BlockSpec((1, 1, r, c), lambda bi, gi, i: (bi, gi, 0, 0))
    kv = lambda br, z: pl.BlockSpec((None, 1, 1, 1, nt, d, tq), lambda bi, gi, i: (br, bi, z, gi, 0, 0, 0))
    return pl.pallas_call(
        _nsa_prompt_kernel,
        grid=(b, g, nt),
        in_specs=[pl.BlockSpec((1, tq, GROUP_W), lambda bi, gi, i: (bi, i, COL_Q // GROUP_W + gi)),
                  per_bg(nc, d), per_bg(nc, d),
                  pl.BlockSpec((1, 1, nc, rows), lambda bi, gi, i: (gi, i, 0, 0)),
                  kv(0, 0), kv(0, 1), kv(1, 0), kv(1, 1),
                  pl.BlockSpec((1, 2, rows, tq), lambda bi, gi, i: (gi, 0, 0, 0)),
                  pl.BlockSpec((nc // 2, t), lambda bi, gi, i: (0, 0)),
                  pl.BlockSpec((1, 1, tq, 128), lambda bi, gi, i: (bi, gi, i, 0))],
        out_specs=pl.BlockSpec((1, tq, GROUP_W), lambda bi, gi, i: (bi, i, gi)),
        out_shape=jax.ShapeDtypeStruct((b, t, Q_W), F32),
        scratch_shapes=[pltpu.VMEM((rows, t), F32), pltpu.VMEM((rows, t), BF16),
                        pltpu.VMEM((rows, tq), F32), pltpu.VMEM((rows, tq), F32),
                        pltpu.VMEM((nt, tq, tq), F32), pltpu.VMEM((rows, d), F32), pltpu.VMEM((rows, d), F32)],
        compiler_params=_cparams(("parallel", "parallel", "arbitrary")),
        name="nsa_prompt",
    )(pr3, kc, vc, bias_c, kvt, kvt, kvt, kvt, bias_t, expand, gts)


SMP_CPAD = 128
KEY_TILE = PAGE_SIZE


def _masked_softmax(s, ok):
    s = jnp.where(ok, s, -jnp.inf)
    m = jnp.max(s, axis=-1, keepdims=True)
    m = jnp.where(m > -jnp.inf, m, 0.0)
    p = jnp.exp(s - m)
    return p / jnp.maximum(jnp.sum(p, axis=-1, keepdims=True), 1e-30)


def _nsa_sample_kernel(pt_ref, q_ref, kc_ref, vc_ref, bc_ref, okc_ref, cur_ref, *rest, nb, n_pages, nsel):
    pages = rest[:nb * n_pages]
    sn_ref, bs_ref, oks_ref, e_ref, wb_ref, wn_ref, bw_ref, okw_ref, gt_ref, o_ref = rest[nb * n_pages:]
    del pt_ref
    d = NSA_HEAD_DIM
    s_new = q_ref.shape[2] // NSA_HPG
    bf = lambda x: x.astype(BF16)
    for n in range(nb):
        seq_pages = pages[n * n_pages:(n + 1) * n_pages]
        for g in range(NSA_KV_HEADS):
            q = bf(q_ref[n, g] * (d ** -0.5))

            s_c = _dot_nt(q, bf(kc_ref[n, g])) + bc_ref[g]
            p_c = _masked_softmax(s_c, okc_ref[...] > 0.5)
            o_c = _dot(bf(p_c), bf(vc_ref[n, g]))
            ph = p_c[0:s_new]
            for h in range(1, NSA_HPG):
                ph = ph + p_c[h * s_new:(h + 1) * s_new]
            imp = ph[:, 0:SMP_CPAD] + ph[:, SMP_CPAD:2 * SMP_CPAD]
            blk = lax.broadcasted_iota(jnp.int32, imp.shape, 1)
            cur = cur_ref[...]
            forced = (blk == 0) | (blk == cur) | (blk == cur - 1)
            imp = jnp.where(forced, FORCE_SCORE, jnp.where(blk <= cur, imp, -jnp.inf))
            rank = jnp.zeros(imp.shape, jnp.int32)
            for s2 in range(nsel):
                col = imp[:, s2:s2 + 1]
                beats = (col > imp) | ((col == imp) & (blk > s2))
                rank = rank + beats.astype(jnp.int32)
            chosen = bf((rank < N_SELECT) & (blk < nsel))
            chosen = jnp.concatenate([chosen] * NSA_HPG, axis=0)
            sel_keys = _dot(chosen, e_ref[...])

            k_tiles = [pg[0, 0, 0, g] for pg in seq_pages] + [sn_ref[n, 0, g]]
            v_tiles = [pg[0, 0, 1, g] for pg in seq_pages] + [sn_ref[n, 1, g]]
            s_s = jnp.concatenate([_dot(q, bf(kt)) for kt in k_tiles], axis=1) + bs_ref[g]
            p_s = bf(_masked_softmax(s_s, (oks_ref[...] > 0.5) & (sel_keys > 0.5)))
            o_s = _dot_nt(p_s[:, 0:KEY_TILE], bf(v_tiles[0]))
            for p in range(1, len(v_tiles)):
                o_s = o_s + _dot_nt(p_s[:, p * KEY_TILE:(p + 1) * KEY_TILE], bf(v_tiles[p]))

            wkeys = wb_ref.shape[-1]
            s_w = jnp.concatenate([_dot(q, bf(wb_ref[0, n, 0, g])), _dot(q, bf(wn_ref[n, 0, g]))], axis=1)
            p_w = bf(_masked_softmax(s_w + bw_ref[g], okw_ref[...] > 0.5))
            o_w = (_dot_nt(p_w[:, 0:wkeys], bf(wb_ref[0, n, 1, g]))
                   + _dot_nt(p_w[:, wkeys:wkeys + KEY_TILE], bf(wn_ref[n, 1, g])))

            gt = jax.nn.sigmoid(gt_ref[n, g])
            o_ref[n, g] = gt[:, 0:1] * o_c + gt[:, 1:2] * o_s + gt[:, 2:3] * o_w


SMP_SEQS = 2


def _nsa_sample(layer, nsel, page_table, qg, kc, vc, bias_c, ok_c, cur, cache_sel, sel_new, bias_s, ok_s,
                expand, win_state, win_new, bias_w, ok_w, gts):
    b = qg.shape[0]
    g = NSA_KV_HEADS
    d = NSA_HEAD_DIM
    rows = qg.shape[2]
    n_pages = page_table.shape[1]
    wkeys = win_state.shape[-1]
    ks = (n_pages + 1) * KEY_TILE
    kw = wkeys + KEY_TILE
    nb = SMP_SEQS if b % SMP_SEQS == 0 else 1
    per_b = lambda *shape: pl.BlockSpec((nb,) + shape, lambda i, pt: (i,) + (0,) * len(shape))
    const = lambda *shape: pl.BlockSpec(shape, lambda i, pt: (0,) * len(shape))

    def page_spec(n, p):
        return pl.BlockSpec((1, 1, 2, g, d, PAGE_SIZE), lambda i, pt: (layer, pt[nb * i + n, p], 0, 0, 0, 0))

    in_specs = ([per_b(g, rows, d), per_b(g, 2 * SMP_CPAD, d), per_b(g, 2 * SMP_CPAD, d),
                 const(g, rows, 2 * SMP_CPAD), const(rows, 2 * SMP_CPAD), const(rows // NSA_HPG, SMP_CPAD)]
                + [page_spec(n, p) for n in range(nb) for p in range(n_pages)]
                + [per_b(2, g, d, KEY_TILE), const(g, rows, ks), const(rows, ks), const(SMP_CPAD, ks),
                   pl.BlockSpec((1, nb, 2, g, d, wkeys), lambda i, pt: (layer, i, 0, 0, 0, 0)),
                   per_b(2, g, d, KEY_TILE), const(g, rows, kw), const(rows, kw), per_b(g, rows, 128)])
    return pl.pallas_call(
        functools.partial(_nsa_sample_kernel, nb=nb, n_pages=n_pages, nsel=nsel),
        grid_spec=pltpu.PrefetchScalarGridSpec(
            num_scalar_prefetch=1,
            grid=(b // nb,),
            in_specs=in_specs,
            out_specs=pl.BlockSpec((nb, g, rows, d), lambda i, pt: (i, 0, 0, 0))),
        out_shape=jax.ShapeDtypeStruct((b, g, rows, d), F32),
        compiler_params=_cparams(("parallel",)),
        name="nsa_sample",
    )(page_table, qg, kc, vc, bias_c, ok_c, cur, *([cache_sel] * (nb * n_pages)), sel_new, bias_s, ok_s, expand,
      win_state, win_new, bias_w, ok_w, gts)


def _pad_rows(a, rows):
    return jnp.pad(a, ((0, 0), (0, rows - a.shape[1]), (0, 0)))


def _even_odd(n):
    return np.concatenate([np.arange(0, n, 2), np.arange(1, n, 2)])


def _prompt_bias_tables(rel_bias, t):
    g, hpg, tq = NSA_KV_HEADS, NSA_HPG, NSA_TQ
    nt = t // tq
    nc = t // CMP_BLOCK
    end_c = (_even_odd(nc) + 1) * CMP_BLOCK - 1
    dist_c = jnp.asarray(np.arange(t)[None, :] - end_c[:, None], jnp.int32)
    bias_c = _bias_lookup(rel_bias, dist_c, nc)
    bias_c = bias_c.reshape(g, hpg, nc, nt, tq).transpose(0, 3, 2, 1, 4).reshape(g, nt, nc, hpg * tq)
    off = np.arange(3)[:, None, None] * tq
    dist_t = off + np.arange(tq)[None, :, None] - np.arange(tq)[None, None, :]
    assert dist_t[2].min() >= REL_MAX_DIST
    bias_t = _bias_lookup(rel_bias, jnp.asarray(dist_t.reshape(3 * tq, tq), jnp.int32), 3 * tq)
    bias_t = bias_t.reshape(g, hpg, 3, tq, tq).transpose(0, 2, 1, 3, 4).reshape(g, 3, hpg * tq, tq)
    bias_t = bias_t[:, 0:2] - bias_t[:, 2:3]
    expand = np.arange(t)[None, :] // SEL_BLOCK == np.arange(t // SEL_BLOCK)[:, None]
    return bias_c, bias_t, jnp.asarray(expand, BF16)


def _sample_tables(rel_bias, past_len, s_new, n_win_keys):
    g, hpg = NSA_KV_HEADS, NSA_HPG
    tk = -(-(past_len + s_new) // SEL_BLOCK) * SEL_BLOCK
    nc = tk // CMP_BLOCK
    nsel = tk // SEL_BLOCK
    half = (nc + 1) // 2
    pos_q = past_len + np.arange(s_new)
    lane = np.arange(SMP_CPAD)
    n_of_lane = np.concatenate([2 * lane, 2 * lane + 1])
    real_c = np.concatenate([lane < half, lane < nc - half])
    dist_c = pos_q[:, None] - ((n_of_lane[None, :] + 1) * CMP_BLOCK - 1)
    ok_c = real_c[None, :] & (dist_c >= 0)
    n_keys_s = (past_len // PAGE_SIZE + 1) * KEY_TILE
    key = np.arange(n_keys_s)
    dist_s = pos_q[:, None] - key[None, :]
    ok_s = (key[None, :] < past_len + s_new) & (dist_s >= 0)
    n_keys_w = n_win_keys + KEY_TILE
    i = np.arange(n_keys_w)
    pos_kw = np.where(i < n_win_keys, past_len - n_win_keys + i, past_len + i - n_win_keys)
    dist_w = pos_q[:, None] - pos_kw[None, :]
    ok_w = (i[None, :] < n_win_keys + s_new) & (dist_w >= 0) & (dist_w < WINDOW) & (pos_kw[None, :] >= 0)
    dist = np.concatenate([dist_c, dist_s, dist_w], axis=1)
    dist = np.pad(dist, ((0, 8 - s_new), (0, 0)))
    bias = _bias_lookup(rel_bias, jnp.asarray(dist, jnp.int32), 8)[:, :s_new]
    bias = bias.reshape(g, hpg * s_new, dist.shape[1])
    c0, c1 = 2 * SMP_CPAD, 2 * SMP_CPAD + n_keys_s
    tile = lambda m: jnp.asarray(np.tile(m, (hpg, 1)), F32)
    expand = (key[None, :] // SEL_BLOCK == np.arange(SMP_CPAD)[:, None]) & (np.arange(SMP_CPAD)[:, None] < nsel)
    cur = np.broadcast_to((pos_q // SEL_BLOCK)[:, None], (s_new, SMP_CPAD))
    return dict(bias_c=bias[:, :, :c0], bias_s=bias[:, :, c0:c1], bias_w=bias[:, :, c1:],
                ok_c=tile(ok_c), ok_s=tile(ok_s), ok_w=tile(ok_w),
                expand=jnp.asarray(expand, BF16), cur=jnp.asarray(cur, jnp.int32), nc=nc, half=half, tk=tk)


def _flat_blocks(rows):
    b, tk = rows.shape[:2]
    nc = tk // CMP_BLOCK
    blk = rows.reshape(b, nc // 2, 2, CMP_BLOCK, 2, NSA_KV_HEADS, NSA_HEAD_DIM)
    return jnp.transpose(blk, (4, 0, 5, 2, 1, 3, 6)).reshape(2, b * NSA_KV_HEADS * nc, CMP_BLOCK * NSA_HEAD_DIM)


def _nsa_prompt_layer(pr3, lw, tabs):
    b, t = pr3.shape[:2]
    g, hpg, d, tq = NSA_KV_HEADS, NSA_HPG, NSA_HEAD_DIM, NSA_TQ
    nt = t // tq
    nc = t // CMP_BLOCK
    kv = pr3[:, :, COL_CMP:COL_CMP + 3 * KV_W].reshape(b, t, 3, 2, g, d)
    cmp_rows, sel_rows, win_rows = kv[:, :, 0], kv[:, :, 1], kv[:, :, 2]
    kcv = _compress(_flat_blocks(cmp_rows), lw['pe'], lw['cw1'], lw['cw2'], 512)
    kcv = kcv.reshape(2, b, g, nc, d).astype(BF16)
    kvt = kv[:, :, 1:].reshape(b, nt, tq, 2, 2, g, d).transpose(3, 0, 4, 5, 1, 6, 2).astype(BF16)
    gts = pr3[:, :, COL_MISC:COL_MISC + N_GATE].reshape(b, t, 3, g, hpg).transpose(0, 3, 1, 2, 4)
    gts = jnp.pad(gts.reshape(b, g, t, 3 * hpg), ((0, 0), (0, 0), (0, 0), (0, 128 - 3 * hpg)))
    o = _nsa_prompt(pr3, kcv[0], kcv[1], tabs[0], kvt, tabs[1], tabs[2], gts)
    return o.reshape(b * t, Q_W), cmp_rows, sel_rows, win_rows[:, t - min(WINDOW, t):]


def _to_native(rows, keys):
    b, n = rows.shape[:2]
    r = rows.reshape(b, n, 2, NSA_KV_HEADS, NSA_HEAD_DIM).transpose(0, 2, 3, 4, 1)
    return jnp.pad(r, ((0, 0),) * 4 + ((0, keys - n),))


def _nsa_sample_layer(pr3, lw, tabs, layer, cache_cmp, cache_sel, win_state, page_table):
    b, s_new = pr3.shape[:2]
    g, hpg, d = NSA_KV_HEADS, NSA_HPG, NSA_HEAD_DIM
    n_pages = page_table.shape[1]
    past_len = n_pages * PAGE_SIZE
    new = lambda c: pr3[:, :, c:c + KV_W]
    cmp_new, sel_new, win_new = new(COL_CMP), new(COL_SEL), new(COL_WIN)
    n_past, n_tail = past_len // CMP_BLOCK, tabs['nc'] - past_len // CMP_BLOCK
    pe3 = lw['pe'].reshape(2, CMP_BLOCK, d)
    past = _compress_pages(layer, page_table, cache_cmp, pe3, lw['cw1'].reshape(2, CMP_BLOCK, d, d), lw['cw2'])
    tail = jnp.pad(cmp_new, ((0, 0), (0, n_tail * CMP_BLOCK - s_new), (0, 0)))
    tail = _compress(_flat_blocks(tail.reshape(b, n_tail * CMP_BLOCK, 2, g, d)), lw['pe'], lw['cw1'], lw['cw2'], 512)
    tail = tail.reshape(2, b, g, 2, n_tail // 2, d).transpose(1, 0, 2, 3, 4, 5)
    kcv = jnp.concatenate([past, tail], axis=4)
    kcv = jnp.pad(kcv, ((0, 0),) * 4 + ((0, SMP_CPAD - kcv.shape[4]), (0, 0)))
    kcv = kcv.reshape(b, 2, g, 2 * SMP_CPAD, d).transpose(1, 0, 2, 3, 4)
    qg = pr3[:, :, COL_Q:COL_Q + Q_W].reshape(b, s_new, g, hpg, d).transpose(0, 2, 3, 1, 4)
    qg = qg.reshape(b, g, hpg * s_new, d)
    gts = pr3[:, :, COL_MISC:COL_MISC + N_GATE].reshape(b, s_new, 3, g, hpg).transpose(0, 3, 4, 1, 2)
    gts = jnp.pad(gts.reshape(b, g, hpg * s_new, 3), ((0, 0), (0, 0), (0, 0), (0, 128 - 3)))
    win_new_t = _to_native(win_new, KEY_TILE)
    o = _nsa_sample(layer, tabs['tk'] // SEL_BLOCK, page_table, qg, kcv[0], kcv[1], tabs['bias_c'], tabs['ok_c'],
                    tabs['cur'], cache_sel, _to_native(sel_new, KEY_TILE), tabs['bias_s'], tabs['ok_s'],
                    tabs['expand'], win_state, win_new_t, tabs['bias_w'], tabs['ok_w'], gts)
    o = o.reshape(b, g, hpg, s_new, d).transpose(0, 3, 1, 2, 4).reshape(b * s_new, Q_W)
    kv5 = lambda a: a.reshape(b, -1, 2, g, d)
    return o, kv5(cmp_new), kv5(sel_new), win_new_t[..., :s_new]


def _mixers(pr, b, t, lw, conv0, h0, s0, nsa_fn):
    pr3 = pr.reshape(b, t, PROJ_COLS)
    o_a, cmp_rows, sel_rows, win_rows = nsa_fn(pr3)
    lru_w = (lw['lcw'], lw['lcb'], lw['lgw'], lw['lgb'], lw['lam'])
    gla_w = (lw['gaw'], lw['gab'], lw['gng'])
    if t % 8 == 0:
        tc = min(t, 256)
        o_b, conv_n, h_n = _lru(pr3, COL_LRU_X // LRU_WIDTH, pr3, COL_LRU_G // LRU_WIDTH, conv0, h0[:, None],
                                *lru_w, tc, tc)
        tg = min(t, 256)
        o_c, s_n = _gla(pr3, COL_GLA_Q // GLA_QK_W, pr3, COL_GLA_K // GLA_QK_W, pr3, COL_GLA_V // GLA_V_W,
                         pr3, COL_GLA_G // GLA_V_W, pr3, COL_MISC // MISC_W, *gla_w, s0, tg,
                         min(tg, GLA_CHUNK), tg)
    else:
        tp = -(-t // 8) * 8
        cut = lambda c, w: _pad_rows(pr3[:, :, c:c + w], tp)
        o_b, conv_n, h_n = _lru(cut(COL_LRU_X, LRU_WIDTH), 0, cut(COL_LRU_G, LRU_WIDTH), 0, conv0, h0[:, None],
                                *lru_w, tp, t)
        o_c, s_n = _gla(cut(COL_GLA_Q, GLA_QK_W), 0, cut(COL_GLA_K, GLA_QK_W), 0, cut(COL_GLA_V, GLA_V_W), 0,
                         cut(COL_GLA_G, GLA_V_W), 0, cut(COL_MISC, MISC_W), 0, *gla_w, s0, tp, tp, t)
        o_b, o_c = o_b[:, :t], o_c[:, :t]
    o_b = o_b.reshape(b * t, LRU_WIDTH)
    o_c = o_c.reshape(b * t, GLA_V_W)
    return (o_a, o_b, o_c), (cmp_rows, sel_rows, win_rows, conv_n, h_n[:, 0], s_n)


def _layer(x, b, t, lw, conv0, h0, s0, nsa_fn, tm):
    pr = _proj(x, lw['ng'], lw['w_in'], lw['b_in'], tm, PROJ_COLS // 5)
    (o_a, o_b, o_c), states = _mixers(pr, b, t, lw, conv0, h0, s0, nsa_fn)
    x = _merge(x, o_a, o_b, o_c, pr, lw['wb'], lw['wo'], min(tm, 512))
    x = _mlp(x, lw['mg'], lw['w1'], lw['w2'], tm, 1024)
    return x, states


def kernel(x_prompt, x_sample, cache_nsa_cmp_kv, cache_nsa_sel_kv, state_nsa_win_kv, state_lru_conv,
           state_lru_h, state_gla, page_table, rel_bias, norm_mix_g, norm_mlp_g, norm_final_g, w_in, b_in,
           nsa_cmp_pe, nsa_cmp_w1, nsa_cmp_w2, lru_gate_w, lru_gate_b, lru_lambda, lru_conv_w, lru_conv_b,
           gla_alpha_w, gla_alpha_b, gla_norm_g, w_branch, w_out, mlp_w1, mlp_w2):
    bp, tp = x_prompt.shape[:2]
    bs, ts = x_sample.shape[:2]
    depth = w_in.shape[0]
    n_pages = page_table.shape[1]
    w_buf = state_nsa_win_kv.shape[2]

    cols = [w_in[..., _SRC[n][0]:_SRC[n][0] + _SRC[n][1]] for n in _DST_ORDER]
    pad = PROJ_COLS - sum(c.shape[-1] for c in cols)
    w_in_p = jnp.concatenate(cols + [jnp.zeros(w_in.shape[:2] + (pad,), w_in.dtype)], axis=-1).astype(BF16)
    bcols = [b_in[..., _SRC[n][0]:_SRC[n][0] + _SRC[n][1]] for n in _DST_ORDER]
    b_in_p = jnp.concatenate(bcols + [jnp.zeros((depth, pad), b_in.dtype)], axis=-1)[:, None, :]
    eye = jnp.eye(LRU_BLOCKS, dtype=lru_gate_w.dtype)
    lgw = jnp.einsum('lznce,nm->lzncme', lru_gate_w, eye).reshape(depth, 2, LRU_WIDTH, LRU_WIDTH).astype(BF16)
    gaw = jnp.zeros((depth, MISC_W, GLA_QK_W), F32).at[:, N_GATE:N_GATE + GLA_RANK].set(gla_alpha_w).astype(BF16)
    pe = jnp.transpose(nsa_cmp_pe, (0, 2, 1, 3)).reshape(depth, 2, 1, CMP_BLOCK * NSA_HEAD_DIM)
    cw1 = nsa_cmp_w1.astype(BF16)
    cw2 = nsa_cmp_w2.astype(BF16)
    wb = w_branch.astype(BF16)
    wo = w_out.astype(BF16)
    w1 = mlp_w1.astype(BF16)
    w2 = mlp_w2.astype(BF16)
    cache_cmp = jnp.transpose(cache_nsa_cmp_kv, (0, 1, 3, 4, 5, 2))
    cache_sel = jnp.transpose(cache_nsa_sel_kv, (0, 1, 3, 4, 5, 2))
    win_state = jnp.transpose(state_nsa_win_kv, (0, 1, 3, 4, 5, 2))

    tabs_p = _prompt_bias_tables(rel_bias, tp)
    tabs_s = _sample_tables(rel_bias, n_pages * PAGE_SIZE, ts, w_buf)

    xp = x_prompt.reshape(bp * tp, D_MODEL)
    xs = x_sample.reshape(bs * ts, D_MODEL)
    conv0_p = jnp.zeros((bp, CONV_WIDTH - 1, LRU_WIDTH), F32)
    h0_p = jnp.zeros((bp, LRU_WIDTH), F32)
    s0_p = jnp.zeros((bp, GLA_HEADS, GLA_DK, GLA_DV), F32)
    outs_p = [[] for _ in range(6)]
    outs_s = [[] for _ in range(6)]
    for l in range(depth):
        lw = dict(ng=norm_mix_g[l][None], mg=norm_mlp_g[l][None], w_in=w_in_p[l], b_in=b_in_p[l],
                  pe=pe[l], cw1=cw1[l], cw2=cw2[l], lcw=lru_conv_w[l], lcb=lru_conv_b[l][None], lgw=lgw[l],
                  lgb=lru_gate_b[l], lam=lru_lambda[l][None], gaw=gaw[l], gab=gla_alpha_b[l][None],
                  gng=gla_norm_g[l][None], wb=wb[l], wo=wo[l], w1=w1[l], w2=w2[l])
        xp, st_p = _layer(xp, bp, tp, lw, conv0_p, h0_p, s0_p,
                          functools.partial(_nsa_prompt_layer, lw=lw, tabs=tabs_p), _row_tile(bp * tp, 1024))
        nsa_s = functools.partial(_nsa_sample_layer, lw=lw, tabs=tabs_s, layer=l, cache_cmp=cache_cmp,
                                  cache_sel=cache_sel, win_state=win_state, page_table=page_table)
        xs, st_s = _layer(xs, bs, ts, lw, state_lru_conv[l], state_lru_h[l], state_gla[l], nsa_s, bs * ts)
        for j in range(6):
            outs_p[j].append(st_p[j])
            outs_s[j].append(st_s[j])
    y_prompt = _final_norm(xp, norm_final_g[None], _row_tile(bp * tp, 1024)).reshape(bp, tp, D_MODEL)
    y_sample = _final_norm(xs, norm_final_g[None], bs * ts).reshape(bs, ts, D_MODEL)
    st = lambda outs, j: jnp.stack(outs[j])
    win_s = jnp.concatenate([win_state[..., ts:], st(outs_s, 2)], axis=-1).transpose(0, 1, 5, 2, 3, 4)
    return (y_prompt, y_sample, st(outs_p, 0), st(outs_s, 0), st(outs_p, 1), st(outs_s, 1),
            st(outs_p, 2), win_s, st(outs_p, 3), st(outs_s, 3), st(outs_p, 4), st(outs_s, 4),
            st(outs_p, 5), st(outs_s, 5))
```

```python
import functools
import math

import jax
import jax.numpy as jnp
import numpy as np
from jax import lax
from jax.experimental import pallas as pl
from jax.experimental.pallas import tpu as pltpu

F32 = jnp.float32
BF16 = jnp.bfloat16

D_MODEL = 1024
DEPTH = 4
PAGE_SIZE = 128
NSA_HEADS = 8
NSA_KV_HEADS = 2
NSA_HPG = NSA_HEADS // NSA_KV_HEADS
NSA_HEAD_DIM = 64
CMP_BLOCK = 32
SEL_BLOCK = 64
N_SELECT = 16
WINDOW = 512
FORCE_SCORE = 1e4
REL_BUCKETS = 32
REL_MAX_DIST = 128
LRU_WIDTH = 512
LRU_BLOCKS = 8
LRU_BLOCK_DIM = LRU_WIDTH // LRU_BLOCKS
CONV_WIDTH = 4
LRU_C = 8.0
GLA_HEADS = 4
GLA_DK = 64
GLA_DV = 128
GLA_RANK = 16
GLA_TAU = 16.0
GLA_CHUNK = 64
D_FF = 4 * D_MODEL
N_BRANCH = 3
BRANCH_WIDTH = 512
NORM_EPS = 1e-6

KV_W = 2 * NSA_KV_HEADS * NSA_HEAD_DIM
Q_W = NSA_HEADS * NSA_HEAD_DIM
GROUP_W = NSA_HPG * NSA_HEAD_DIM
GLA_QK_W = GLA_HEADS * GLA_DK
GLA_V_W = GLA_HEADS * GLA_DV
MISC_W = 128
N_GATE = 3 * NSA_HEADS

COL_MERGE = 0
COL_Q = COL_MERGE + N_BRANCH * D_MODEL
COL_LRU_X = COL_Q + Q_W
COL_LRU_G = COL_LRU_X + LRU_WIDTH
COL_GLA_V = COL_LRU_G + LRU_WIDTH
COL_GLA_G = COL_GLA_V + GLA_V_W
COL_CMP = COL_GLA_G + GLA_V_W
COL_SEL = COL_CMP + KV_W
COL_WIN = COL_SEL + KV_W
COL_GLA_Q = COL_WIN + KV_W
COL_GLA_K = COL_GLA_Q + GLA_QK_W
COL_MISC = COL_GLA_K + GLA_QK_W
PROJ_COLS = COL_MISC + MISC_W

_SRC = {}
_off = 0
for _name, _w in (('nsa_q', Q_W), ('nsa_cmp_kv', KV_W), ('nsa_sel_kv', KV_W), ('nsa_win_kv', KV_W),
                  ('nsa_gate', N_GATE), ('lru_x', LRU_WIDTH), ('lru_gate', LRU_WIDTH),
                  ('gla_q', GLA_QK_W), ('gla_k', GLA_QK_W), ('gla_v', GLA_V_W),
                  ('gla_alpha', GLA_RANK), ('gla_gate', GLA_V_W), ('merge_gate', N_BRANCH * D_MODEL)):
    _SRC[_name] = (_off, _w)
    _off += _w
_DST_ORDER = ('merge_gate', 'nsa_q', 'lru_x', 'lru_gate', 'gla_v', 'gla_gate', 'nsa_cmp_kv',
              'nsa_sel_kv', 'nsa_win_kv', 'gla_q', 'gla_k', 'nsa_gate', 'gla_alpha')

NSA_TQ = 128
NSA_LANES = NSA_HPG * NSA_TQ
NSA_CLASS = 2
VMEM_LIMIT = 56 * 1024 * 1024


def _cparams(sem):
    return pltpu.CompilerParams(dimension_semantics=sem, vmem_limit_bytes=VMEM_LIMIT)


def _gelu(x):
    return x * (0.5 * (1.0 + jnp.tanh(math.sqrt(2.0 / math.pi) * (x + 0.044715 * (x * x * x)))))


def _softplus(x):
    return jnp.maximum(x, 0.0) + jnp.log1p(jnp.exp(-jnp.abs(x)))


def _rms(x, g):
    return x * lax.rsqrt(jnp.mean(x * x, axis=-1, keepdims=True) + NORM_EPS) * g


def _dot(a, b):
    return jnp.dot(a, b, preferred_element_type=F32)


def _dot_nt(a, b):
    return lax.dot_general(a, b, (((1,), (1,)), ((), ())), preferred_element_type=F32)


def _dot_tn(a, b):
    return lax.dot_general(a, b, (((0,), (0,)), ((), ())), preferred_element_type=F32)


def _proj_kernel(x_ref, g_ref, w_ref, b_ref, o_ref, h_ref):
    @pl.when(pl.program_id(1) == 0)
    def _():
        h_ref[...] = _rms(x_ref[...], g_ref[...]).astype(BF16)

    o_ref[...] = _dot(h_ref[...], w_ref[...]) + b_ref[...]


def _proj(x, g, w, b, tm, tn):
    n = x.shape[0]
    return pl.pallas_call(
        _proj_kernel,
        grid=(n // tm, PROJ_COLS // tn),
        in_specs=[pl.BlockSpec((tm, D_MODEL), lambda i, j: (i, 0)),
                  pl.BlockSpec((1, D_MODEL), lambda i, j: (0, 0)),
                  pl.BlockSpec((D_MODEL, tn), lambda i, j: (0, j)),
                  pl.BlockSpec((1, tn), lambda i, j: (0, j))],
        out_specs=pl.BlockSpec((tm, tn), lambda i, j: (i, j)),
        out_shape=jax.ShapeDtypeStruct((n, PROJ_COLS), F32),
        scratch_shapes=[pltpu.VMEM((tm, D_MODEL), BF16)],
        compiler_params=_cparams(("parallel", "arbitrary")),
        name="proj",
    )(x, g, w, b)


def _merge_kernel(x_ref, oa_ref, ob_ref, oc_ref, g0_ref, g1_ref, g2_ref, wb_ref, wo_ref, o_ref):
    m = jax.nn.sigmoid(g0_ref[...]) * _dot(oa_ref[...].astype(BF16), wb_ref[0])
    m = m + jax.nn.sigmoid(g1_ref[...]) * _dot(ob_ref[...].astype(BF16), wb_ref[1])
    m = m + jax.nn.sigmoid(g2_ref[...]) * _dot(oc_ref[...].astype(BF16), wb_ref[2])
    o_ref[...] = x_ref[...] + _dot(m.astype(BF16), wo_ref[...])


def _merge(x, oa, ob, oc, pr, wb, wo, tm):
    n = x.shape[0]
    row = lambda w: pl.BlockSpec((tm, w), lambda i: (i, 0))
    gate = lambda z: pl.BlockSpec((tm, D_MODEL), lambda i: (i, COL_MERGE // D_MODEL + z))
    return pl.pallas_call(
        _merge_kernel,
        grid=(n // tm,),
        in_specs=[row(D_MODEL), row(BRANCH_WIDTH), row(BRANCH_WIDTH), row(BRANCH_WIDTH),
                  gate(0), gate(1), gate(2),
                  pl.BlockSpec((N_BRANCH, BRANCH_WIDTH, D_MODEL), lambda i: (0, 0, 0)),
                  pl.BlockSpec((D_MODEL, D_MODEL), lambda i: (0, 0))],
        out_specs=row(D_MODEL),
        out_shape=jax.ShapeDtypeStruct((n, D_MODEL), F32),
        compiler_params=_cparams(("parallel",)),
        name="merge",
    )(x, oa, ob, oc, pr, pr, pr, wb, wo)


def _mlp_kernel(x_ref, g_ref, w1_ref, w2_ref, o_ref, h_ref, acc_ref):
    f = pl.program_id(1)

    @pl.when(f == 0)
    def _():
        h_ref[...] = _rms(x_ref[...], g_ref[...]).astype(BF16)
        acc_ref[...] = jnp.zeros_like(acc_ref)

    a = jnp.maximum(_dot(h_ref[...], w1_ref[...]), 0.0)
    acc_ref[...] += _dot((a * a).astype(BF16), w2_ref[...])

    @pl.when(f == pl.num_programs(1) - 1)
    def _():
        o_ref[...] = x_ref[...] + acc_ref[...]


def _mlp(x, g, w1, w2, tm, tf):
    n = x.shape[0]
    return pl.pallas_call(
        _mlp_kernel,
        grid=(n // tm, D_FF // tf),
        in_specs=[pl.BlockSpec((tm, D_MODEL), lambda i, f: (i, 0)),
                  pl.BlockSpec((1, D_MODEL), lambda i, f: (0, 0)),
                  pl.BlockSpec((D_MODEL, tf), lambda i, f: (0, f)),
                  pl.BlockSpec((tf, D_MODEL), lambda i, f: (f, 0))],
        out_specs=pl.BlockSpec((tm, D_MODEL), lambda i, f: (i, 0)),
        out_shape=jax.ShapeDtypeStruct((n, D_MODEL), F32),
        scratch_shapes=[pltpu.VMEM((tm, D_MODEL), BF16), pltpu.VMEM((tm, D_MODEL), F32)],
        compiler_params=_cparams(("parallel", "arbitrary")),
        name="mlp",
    )(x, g, w1, w2)


def _norm_kernel(x_ref, g_ref, o_ref):
    o_ref[...] = _rms(x_ref[...], g_ref[...])


def _final_norm(x, g, tm):
    n = x.shape[0]
    return pl.pallas_call(
        _norm_kernel,
        grid=(n // tm,),
        in_specs=[pl.BlockSpec((tm, D_MODEL), lambda i: (i, 0)),
                  pl.BlockSpec((1, D_MODEL), lambda i: (0, 0))],
        out_specs=pl.BlockSpec((tm, D_MODEL), lambda i: (i, 0)),
        out_shape=jax.ShapeDtypeStruct((n, D_MODEL), F32),
        compiler_params=_cparams(("parallel",)),
        name="final_norm",
    )(x, g)


_XB = 8


def _lru_kernel(x_ref, gb_ref, conv0_ref, h0_ref, cw_ref, cb_ref, gw_ref, gbias_ref, lam_ref,
                o_ref, convn_ref, hn_ref, xbuf, hcar, *, tc, tv):
    @pl.when(pl.program_id(1) == 0)
    def _():
        xbuf[0:_XB, :] = jnp.zeros((_XB, LRU_WIDTH), F32)
        xbuf[_XB - 3:_XB, :] = conv0_ref[0]
        hcar[...] = h0_ref[0]

    x = x_ref[0]
    xbuf[_XB:_XB + tc, :] = x
    w = cw_ref[...]
    xc = cb_ref[...] + xbuf[_XB - 3:_XB - 3 + tc, :] * w[0:1]
    xc = xc + xbuf[_XB - 2:_XB - 2 + tc, :] * w[1:2]
    xc = xc + xbuf[_XB - 1:_XB - 1 + tc, :] * w[2:3]
    xc = xc + x * w[3:4]
    tail = xbuf[_XB - 3 + tv:_XB + tv, :]
    convn_ref[0] = tail
    xbuf[_XB - 3:_XB, :] = tail

    xcb = xc.astype(BF16)
    r = jax.nn.sigmoid(_dot(xcb, gw_ref[0]) + gbias_ref[0:1])
    i = jax.nn.sigmoid(_dot(xcb, gw_ref[1]) + gbias_ref[1:2])
    log_a = (-LRU_C * r) * _softplus(-lam_ref[...])
    a = jnp.exp(log_a)
    b = jnp.sqrt(-jnp.tanh(log_a) * (a * a + 1.0)) * (i * xc)

    rows = lax.broadcasted_iota(jnp.int32, (tc, LRU_WIDTH), 0)
    s = 1
    while s < tc:
        a_sh = pltpu.roll(a, s, 0)
        b_sh = pltpu.roll(b, s, 0)
        m = rows >= s
        b = jnp.where(m, a * b_sh + b, b)
        a = jnp.where(m, a * a_sh, a)
        s *= 2
    h = a * hcar[...] + b
    hlast = h[tv - 1:tv]
    hcar[...] = hlast
    hn_ref[0] = hlast
    o_ref[0] = _gelu(gb_ref[0]) * h


def _lru(x_arr, x_blk, gb_arr, gb_blk, conv0, h0, cw, cb, gw, gbias, lam, tc, tv):
    b, t = x_arr.shape[:2]
    r = LRU_WIDTH
    const2 = lambda shape: pl.BlockSpec(shape, lambda i, c: (0, 0))
    return pl.pallas_call(
        functools.partial(_lru_kernel, tc=tc, tv=tv),
        grid=(b, t // tc),
        in_specs=[pl.BlockSpec((1, tc, r), lambda i, c: (i, c, x_blk)),
                  pl.BlockSpec((1, tc, r), lambda i, c: (i, c, gb_blk)),
                  pl.BlockSpec((1, CONV_WIDTH - 1, r), lambda i, c: (i, 0, 0)),
                  pl.BlockSpec((1, 1, r), lambda i, c: (i, 0, 0)),
                  const2((CONV_WIDTH, r)), const2((1, r)),
                  pl.BlockSpec((2, r, r), lambda i, c: (0, 0, 0)),
                  const2((2, r)), const2((1, r))],
        out_specs=[pl.BlockSpec((1, tc, r), lambda i, c: (i, c, 0)),
                   pl.BlockSpec((1, CONV_WIDTH - 1, r), lambda i, c: (i, 0, 0)),
                   pl.BlockSpec((1, 1, r), lambda i, c: (i, 0, 0))],
        out_shape=[jax.ShapeDtypeStruct((b, t, r), F32),
                   jax.ShapeDtypeStruct((b, CONV_WIDTH - 1, r), F32),
                   jax.ShapeDtypeStruct((b, 1, r), F32)],
        scratch_shapes=[pltpu.VMEM((_XB + tc, r), F32), pltpu.VMEM((1, r), F32)],
        compiler_params=_cparams(("parallel", "arbitrary")),
        name="rglru",
    )(x_arr, gb_arr, conv0, h0, cw, cb, gw, gbias, lam)


LRU_SEG = 8


def _lru_short_kernel(x_ref, gb_ref, h0_ref, cw_ref, cb_ref, gw_ref, gbias_ref, lam_ref, o_ref, h_ref, *, t):
    x = x_ref[...]
    pos = lax.broadcasted_iota(jnp.int32, x.shape, 0) & (LRU_SEG - 1)
    w = cw_ref[...]
    xc = cb_ref[...] + x * w[CONV_WIDTH - 1:CONV_WIDTH]
    for k in range(1, CONV_WIDTH):
        xc = xc + pltpu.roll(x, k, 0) * w[CONV_WIDTH - 1 - k:CONV_WIDTH - k]
    xcb = xc.astype(BF16)
    r = jax.nn.sigmoid(_dot(xcb, gw_ref[0]) + gbias_ref[0:1])
    i = jax.nn.sigmoid(_dot(xcb, gw_ref[1]) + gbias_ref[1:2])
    log_a = (-LRU_C * r) * _softplus(-lam_ref[...])
    a = jnp.exp(log_a)
    b = jnp.sqrt(-jnp.tanh(log_a) * (a * a + 1.0)) * (i * xc)
    real = (pos >= CONV_WIDTH - 1) & (pos < CONV_WIDTH - 1 + t)
    a = jnp.where(real, a, 1.0)
    b = jnp.where(real, b, 0.0)
    s = 1
    while s < LRU_SEG:
        m = pos >= s
        b = jnp.where(m, a * pltpu.roll(b, s, 0) + b, b)
        a = jnp.where(m, a * pltpu.roll(a, s, 0), a)
        s *= 2
    h = a * h0_ref[...] + b
    h_ref[...] = h
    o_ref[...] = _gelu(gb_ref[...]) * h


def _lru_short(x, gb, conv0, h0, cw, cb, gw, gbias, lam):
    b, t, r = x.shape
    lead = CONV_WIDTH - 1
    seg = lambda head, body: jnp.concatenate(
        [head, body, jnp.zeros((b, LRU_SEG - lead - t, r), F32)], axis=1).reshape(b * LRU_SEG, r)
    xin = seg(conv0, x)
    rows = b * LRU_SEG
    tm = _row_tile(rows, 256)
    blk = pl.BlockSpec((tm, r), lambda i: (i, 0))
    const2 = lambda shape: pl.BlockSpec(shape, lambda i: (0, 0))
    o, h = pl.pallas_call(
        functools.partial(_lru_short_kernel, t=t),
        grid=(rows // tm,),
        in_specs=[blk, blk, blk, const2((CONV_WIDTH, r)), const2((1, r)),
                  pl.BlockSpec((2, r, r), lambda i: (0, 0, 0)), const2((2, r)), const2((1, r))],
        out_specs=[blk, blk],
        out_shape=[jax.ShapeDtypeStruct((rows, r), F32), jax.ShapeDtypeStruct((rows, r), F32)],
        compiler_params=_cparams(("parallel",)),
        name="rglru_short",
    )(xin, seg(jnp.zeros((b, lead, r), F32), gb), jnp.repeat(h0, LRU_SEG, axis=0), cw, cb, gw, gbias, lam)
    o = o.reshape(b, LRU_SEG, r)[:, lead:lead + t]
    h_new = h.reshape(b, LRU_SEG, r)[:, lead + t - 1]
    conv_new = xin.reshape(b, LRU_SEG, r)[:, t:t + lead]
    return o, conv_new, h_new


def _gla_kernel(q_ref, k_ref, v_ref, og_ref, misc_ref, aw_ref, ab_ref, ng_ref, s0_ref,
                o_ref, sn_ref, st, *, tg, ck, tv):
    @pl.when(pl.program_id(1) == 0)
    def _():
        for h in range(GLA_HEADS):
            st[h] = s0_ref[0, h].T

    pre = _dot(misc_ref[0].astype(BF16), aw_ref[...]) + ab_ref[...]
    g = -_softplus(-pre) * (1.0 / GLA_TAU)
    if tv < tg:
        g = jnp.where(lax.broadcasted_iota(jnp.int32, g.shape, 0) < tv, g, 0.0)
    q = q_ref[0] * (GLA_DK ** -0.5)
    k = k_ref[0]
    v = v_ref[0]
    og = og_ref[0]
    ng = ng_ref[...]
    rows = lax.broadcasted_iota(jnp.int32, (ck, GLA_QK_W), 0)
    tril = (lax.broadcasted_iota(jnp.int32, (ck, ck), 0) >= lax.broadcasted_iota(jnp.int32, (ck, ck), 1))
    for c in range(tg // ck):
        sl = slice(c * ck, (c + 1) * ck)
        bc = g[sl]
        s = 1
        while s < ck:
            bc = bc + jnp.where(rows >= s, pltpu.roll(bc, s, 0), 0.0)
            s *= 2
        bl = bc[ck - 1:ck]
        e = jnp.exp(bc)
        qi = (q[sl] * e).astype(BF16)
        ki = (k[sl] * jnp.exp(-bc)).astype(BF16)
        kd = (k[sl] * jnp.exp(bl - bc)).astype(BF16)
        dec = jnp.exp(bl)
        vb = v[sl].astype(BF16)
        outs = []
        for h in range(GLA_HEADS):
            ks = slice(h * GLA_DK, (h + 1) * GLA_DK)
            vs = slice(h * GLA_DV, (h + 1) * GLA_DV)
            att = jnp.where(tril, _dot_nt(qi[:, ks], ki[:, ks]), 0.0)
            s_prev = st[h]
            o = _dot(att.astype(BF16), vb[:, vs]) + _dot_nt(qi[:, ks], s_prev.astype(BF16))
            st[h] = s_prev * dec[:, ks] + _dot_tn(vb[:, vs], kd[:, ks])
            o = _rms(o, ng[:, vs])
            ogh = og[sl, vs]
            outs.append(o * (ogh * jax.nn.sigmoid(ogh)))
        o_ref[0, sl, :] = jnp.concatenate(outs, axis=-1)

    @pl.when(pl.program_id(1) == pl.num_programs(1) - 1)
    def _():
        for h in range(GLA_HEADS):
            sn_ref[0, h] = st[h].T


def _gla(q_arr, q_blk, k_arr, k_blk, v_arr, v_blk, og_arr, og_blk, misc_arr, misc_blk,
         aw, ab, ng, s0, tg, ck, tv):
    b, t = q_arr.shape[:2]
    col = lambda w, blk: pl.BlockSpec((1, tg, w), lambda i, c: (i, c, blk))
    const2 = lambda shape: pl.BlockSpec(shape, lambda i, c: (0, 0))
    state = pl.BlockSpec((1, GLA_HEADS, GLA_DK, GLA_DV), lambda i, c: (i, 0, 0, 0))
    return pl.pallas_call(
        functools.partial(_gla_kernel, tg=tg, ck=ck, tv=tv),
        grid=(b, t // tg),
        in_specs=[col(GLA_QK_W, q_blk), col(GLA_QK_W, k_blk), col(GLA_V_W, v_blk), col(GLA_V_W, og_blk),
                  col(MISC_W, misc_blk), const2((MISC_W, GLA_QK_W)), const2((1, GLA_QK_W)),
                  const2((1, GLA_V_W)), state],
        out_specs=[pl.BlockSpec((1, tg, GLA_V_W), lambda i, c: (i, c, 0)), state],
        out_shape=[jax.ShapeDtypeStruct((b, t, GLA_V_W), F32),
                   jax.ShapeDtypeStruct((b, GLA_HEADS, GLA_DK, GLA_DV), F32)],
        scratch_shapes=[pltpu.VMEM((GLA_HEADS, GLA_DV, GLA_DK), F32)],
        compiler_params=_cparams(("parallel", "arbitrary")),
        name="gla",
    )(q_arr, k_arr, v_arr, og_arr, misc_arr, aw, ab, ng, s0)


def _bias_kernel(tbl_ref, dist_ref, o_ref):
    h = pl.program_id(0)
    max_exact = REL_BUCKETS // 2
    n = jnp.maximum(dist_ref[...], 0)
    nf = jnp.maximum(n, 1).astype(F32)
    large = max_exact + (jnp.log(nf / max_exact) / math.log(REL_MAX_DIST / max_exact)
                         * (REL_BUCKETS - max_exact)).astype(jnp.int32)
    bucket = jnp.where(n < max_exact, n, jnp.minimum(large, REL_BUCKETS - 1))
    out = jnp.zeros(bucket.shape, F32)
    for kk in range(REL_BUCKETS):
        out = jnp.where(bucket == kk, tbl_ref[kk, h], out)
    o_ref[0] = out


def _bias_lookup(rel_bias, dist, tr):
    r, c = dist.shape
    return pl.pallas_call(
        _bias_kernel,
        grid=(NSA_HEADS, r // tr),
        in_specs=[pl.BlockSpec(memory_space=pltpu.SMEM),
                  pl.BlockSpec((tr, c), lambda h, i: (i, 0))],
        out_specs=pl.BlockSpec((1, tr, c), lambda h, i: (h, i, 0)),
        out_shape=jax.ShapeDtypeStruct((NSA_HEADS, r, c), F32),
        compiler_params=_cparams(("parallel", "parallel")),
        name="rel_bias",
    )(rel_bias, dist)


def _cmp_kernel(x_ref, pe_ref, w1_ref, w2_ref, o_ref):
    hid = _gelu(_dot((x_ref[0] + pe_ref[0]).astype(BF16), w1_ref[0]))
    o_ref[0] = _dot(hid.astype(BF16), w2_ref[0])


def _row_tile(m, cap):
    for tm in range(min(cap, m) // 8 * 8, 0, -8):
        if m % tm == 0:
            return tm
    return m


def _compress(flat, pe, w1, w2, cap):
    m = flat.shape[1]
    tm = _row_tile(m, cap)
    fw = CMP_BLOCK * NSA_HEAD_DIM
    d = NSA_HEAD_DIM
    return pl.pallas_call(
        _cmp_kernel,
        grid=(2, m // tm),
        in_specs=[pl.BlockSpec((1, tm, fw), lambda z, i: (z, i, 0)),
                  pl.BlockSpec((1, 1, fw), lambda z, i: (z, 0, 0)),
                  pl.BlockSpec((1, fw, d), lambda z, i: (z, 0, 0)),
                  pl.BlockSpec((1, d, d), lambda z, i: (z, 0, 0))],
        out_specs=pl.BlockSpec((1, tm, d), lambda z, i: (z, i, 0)),
        out_shape=jax.ShapeDtypeStruct((2, m, d), F32),
        compiler_params=_cparams(("parallel", "parallel")),
        name="nsa_compress",
    )(flat, pe, w1, w2)


def _cmp_pages_kernel(pt_ref, *refs, n_pages):
    del pt_ref
    pages = refs[:n_pages]
    pe_ref, w1_ref, w2_ref, o_ref, x_scr, h_scr = refs[n_pages:]
    gd = NSA_KV_HEADS * NSA_HEAD_DIM
    nblk = n_pages * PAGE_SIZE // CMP_BLOCK
    for z in range(2):
        for p in range(n_pages):
            x_scr[z, p * PAGE_SIZE:(p + 1) * PAGE_SIZE, :] = pages[p][0, 0, z].reshape(gd, PAGE_SIZE).T
        acc = jnp.zeros((nblk, gd), F32)
        for t in range(CMP_BLOCK):
            rows = x_scr[z, pl.ds(t, nblk, stride=CMP_BLOCK), :]
            acc = acc + _dot((rows + pe_ref[z, t:t + 1, :]).astype(BF16), w1_ref[z, t])
        h_scr[z] = _dot(_gelu(acc).astype(BF16), w2_ref[z])
        for par in range(2):
            o_ref[0, z, par] = h_scr[z, pl.ds(par, nblk // 2, stride=2), :]


def _compress_pages(layer, page_table, cache, pe, w1, w2):
    b, n_pages = page_table.shape
    g, d = NSA_KV_HEADS, NSA_HEAD_DIM
    gd = g * d
    nblk = n_pages * PAGE_SIZE // CMP_BLOCK
    eye = jnp.eye(g, dtype=w1.dtype)
    w1 = jnp.einsum('gh,ztje->ztgjhe', eye, w1).reshape(2, CMP_BLOCK, gd, gd)
    w2 = jnp.einsum('gh,zje->zgjhe', eye, w2).reshape(2, gd, gd)
    pe = jnp.tile(pe, (1, 1, g))
    const = lambda *shape: pl.BlockSpec(shape, lambda i, pt: (0,) * len(shape))

    def page_spec(p):
        return pl.BlockSpec((1, 1, 2, g, d, PAGE_SIZE), lambda i, pt: (layer, pt[i, p], 0, 0, 0, 0))

    out = pl.pallas_call(
        functools.partial(_cmp_pages_kernel, n_pages=n_pages),
        grid_spec=pltpu.PrefetchScalarGridSpec(
            num_scalar_prefetch=1,
            grid=(b,),
            in_specs=[page_spec(p) for p in range(n_pages)]
            + [const(2, CMP_BLOCK, gd), const(2, CMP_BLOCK, gd, gd), const(2, gd, gd)],
            out_specs=pl.BlockSpec((1, 2, 2, nblk // 2, gd), lambda i, pt: (i, 0, 0, 0, 0)),
            scratch_shapes=[pltpu.VMEM((2, n_pages * PAGE_SIZE, gd), F32), pltpu.VMEM((2, nblk, gd), F32)]),
        out_shape=jax.ShapeDtypeStruct((b, 2, 2, nblk // 2, gd), F32),
        compiler_params=_cparams(("parallel",)),
        name="nsa_compress_pages",
    )(page_table, *([cache] * n_pages), pe, w1, w2)
    return out.reshape(b, 2, 2, nblk // 2, g, d).transpose(0, 1, 4, 2, 3, 5)


def _nsa_prompt_kernel(q_ref, kc_ref, vc_ref, bc_ref, ks_ref, vs_ref, kw_ref, vw_ref, bt_ref, e_ref, gt_ref,
                       o_ref, s_scr, p_scr, m_scr, l_scr, selk_scr, os_scr, ow_scr):
    i = pl.program_id(2)
    tq, d, ck, hpg = NSA_TQ, NSA_HEAD_DIM, NSA_TQ, NSA_HPG
    nchunk = ks_ref.shape[3]
    nsel = kc_ref.shape[2] // 2
    neg = -jnp.inf
    qf = q_ref[0] * (d ** -0.5)
    qs = jnp.concatenate([qf[:, h * d:(h + 1) * d] for h in range(hpg)], axis=0).astype(BF16)

    tk = lax.broadcasted_iota(jnp.int32, (tq, ck), 0) - lax.broadcasted_iota(jnp.int32, (tq, ck), 1)

    def mask_heads(s, ok):
        return jnp.concatenate([jnp.where(ok, s[h * tq:(h + 1) * tq], neg) for h in range(hpg)], axis=0)

    def attend(k_ref, v_ref, n, ok_fn):
        m_scr[...] = jnp.full(m_scr.shape, neg, F32)
        for r in range(n):
            c = i - r
            cc = jnp.maximum(c, 0)
            s = _dot(qs, k_ref[0, 0, 0, cc])
            if r < 2:
                s = s + bt_ref[0, r]
            s = mask_heads(s, ok_fn(r, c, cc))
            s_scr[:, r * ck:(r + 1) * ck] = s
            m_scr[...] = jnp.maximum(m_scr[...], s)
        m = jnp.max(m_scr[...], axis=1, keepdims=True)
        m = jnp.where(m > neg, m, 0.0)
        l_scr[...] = jnp.zeros(l_scr.shape, F32)
        for r in range(n):
            p = jnp.exp(s_scr[:, r * ck:(r + 1) * ck] - m)
            l_scr[...] += p
            p_scr[:, r * ck:(r + 1) * ck] = p.astype(BF16)
        l = jnp.sum(l_scr[...], axis=1, keepdims=True)
        vt = jnp.concatenate([v_ref[0, 0, 0, jnp.maximum(i - r, 0)] for r in range(n)], axis=1)
        return _dot_nt(p_scr[:, 0:n * ck], vt) / jnp.maximum(l, 1e-30)

    nw = WINDOW // ck + 1

    def win_ok(r, c, cc):
        if r == 0:
            return tk >= 0
        if r == nw - 1:
            return tk < jnp.where(c >= 0, 0, -tq)
        return tk > jnp.where(c >= 0, -tq, tq)

    ow_scr[...] = attend(kw_ref, vw_ref, nw, win_ok)

    t_lane = i * tq + (lax.broadcasted_iota(jnp.int32, (1, hpg * tq), 1) & (tq - 1))
    s_c = _dot_nt(kc_ref[0, 0], qs) + bc_ref[0, 0]
    r_c = lax.broadcasted_iota(jnp.int32, (2 * nsel, hpg * tq), 0)
    n_c = jnp.where(r_c < nsel, 2 * r_c, 2 * (r_c - nsel) + 1)
    s_c = jnp.where(t_lane >= (n_c + 1) * CMP_BLOCK - 1, s_c, neg)
    m_c = jnp.max(s_c, axis=0, keepdims=True)
    m_c = jnp.where(m_c > neg, m_c, 0.0)
    p_c = jnp.exp(s_c - m_c)
    p_c = p_c / jnp.maximum(jnp.sum(p_c, axis=0, keepdims=True), 1e-30)
    o_c = _dot_tn(p_c.astype(BF16), vc_ref[0, 0])

    ph = p_c[:, 0:tq]
    for h in range(1, hpg):
        ph = ph + p_c[:, h * tq:(h + 1) * tq]
    imp = ph[0:nsel] + ph[nsel:2 * nsel]
    blk = lax.broadcasted_iota(jnp.int32, (nsel, tq), 0)
    cur = (i * tq + lax.broadcasted_iota(jnp.int32, (nsel, tq), 1)) >> int(math.log2(SEL_BLOCK))
    forced = (blk == 0) | (blk == cur) | (blk == cur - 1)
    imp = jnp.where(forced, FORCE_SCORE, jnp.where(blk <= cur, imp, neg))
    rank = jnp.zeros((nsel, tq), jnp.int32)
    for s2 in range(nsel):
        row = imp[s2:s2 + 1]
        beats = (row > imp) | ((row == imp) & (blk > s2))
        rank = rank + beats.astype(jnp.int32)
    chosen = (rank < N_SELECT).astype(BF16)
    selk = _dot_tn(chosen, e_ref[...])
    for c in range(nchunk):
        selk_scr[c] = selk[:, c * ck:(c + 1) * ck]

    def sel_ok(r, c, cc):
        ok = selk_scr[cc] > jnp.where(c >= 0, 0.5, 2.0)
        return ok & (tk >= 0) if r == 0 else ok

    sizes = list(range(NSA_CLASS, nchunk, NSA_CLASS)) + [nchunk]
    for lo, n in zip([0] + sizes[:-1], sizes):
        @pl.when((i >= lo) & (i < n))
        def _(n=n):
            os_scr[...] = attend(ks_ref, vs_ref, n, sel_ok)

    o_w = ow_scr[...]
    o_s = os_scr[...]

    sig = jax.nn.sigmoid(gt_ref[0, 0])
    outs = []
    for h in range(hpg):
        r = slice(h * tq, (h + 1) * tq)
        outs.append(sig[:, h:h + 1] * o_c[r] + sig[:, hpg + h:hpg + h + 1] * o_s[r]
                    + sig[:, 2 * hpg + h:2 * hpg + h + 1] * o_w[r])
    o_ref[0] = jnp.concatenate(outs, axis=1)


def _nsa_prompt(pr3, kc, vc, bias_c, kvt, bias_t, expand, gts):
    b, t = pr3.shape[:2]
    g, d, tq, hpg = NSA_KV_HEADS, NSA_HEAD_DIM, NSA_TQ, NSA_HPG
    nt = t // tq
    nc = kc.shape[2]
    rows = hpg * tq
    per_bg = lambda r, c: pl.BlockSpec((1, 1, r, c), lambda bi, gi, i: (bi, gi, 0, 0))
    kv = lambda br, z: pl.BlockSpec((None, 1, 1, 1, nt, d, tq), lambda bi, gi, i: (br, bi, z, gi, 0, 0, 0))
    return pl.pallas_call(
        _nsa_prompt_kernel,
        grid=(b, g, nt),
        in_specs=[pl.BlockSpec((1, tq, GROUP_W), lambda bi, gi, i: (bi, i, COL_Q // GROUP_W + gi)),
                  per_bg(nc, d), per_bg(nc, d),
                  pl.BlockSpec((1, 1, nc, rows), lambda bi, gi, i: (gi, i, 0, 0)),
                  kv(0, 0), kv(0, 1), kv(1, 0), kv(1, 1),
                  pl.BlockSpec((1, 2, rows, tq), lambda bi, gi, i: (gi, 0, 0, 0)),
                  pl.BlockSpec((nc // 2, t), lambda bi, gi, i: (0, 0)),
                  pl.BlockSpec((1, 1, tq, 128), lambda bi, gi, i: (bi, gi, i, 0))],
        out_specs=pl.BlockSpec((1, tq, GROUP_W), lambda bi, gi, i: (bi, i, gi)),
        out_shape=jax.ShapeDtypeStruct((b, t, Q_W), F32),
        scratch_shapes=[pltpu.VMEM((rows, t), F32), pltpu.VMEM((rows, t), BF16),
                        pltpu.VMEM((rows, tq), F32), pltpu.VMEM((rows, tq), F32),
                        pltpu.VMEM((nt, tq, tq), F32), pltpu.VMEM((rows, d), F32), pltpu.VMEM((rows, d), F32)],
        compiler_params=_cparams(("parallel", "parallel", "arbitrary")),
        name="nsa_prompt",
    )(pr3, kc, vc, bias_c, kvt, kvt, kvt, kvt, bias_t, expand, gts)


SMP_CPAD = 128
KEY_TILE = PAGE_SIZE


def _masked_softmax(s, ok):
    s = jnp.where(ok, s, -jnp.inf)
    m = jnp.max(s, axis=-1, keepdims=True)
    m = jnp.where(m > -jnp.inf, m, 0.0)
    p = jnp.exp(s - m)
    return p / jnp.maximum(jnp.sum(p, axis=-1, keepdims=True), 1e-30)


def _nsa_sample_kernel(pt_ref, q_ref, kc_ref, vc_ref, bc_ref, okc_ref, cur_ref, *rest, nb, n_pages, nsel):
    pages = rest[:nb * n_pages]
    sn_ref, bs_ref, oks_ref, e_ref, wb_ref, wn_ref, bw_ref, okw_ref, gt_ref, o_ref = rest[nb * n_pages:]
    del pt_ref
    d, hpg = NSA_HEAD_DIM, NSA_HPG
    rows = q_ref.shape[2]
    s_new = rows // hpg
    chains = [(n, g) for n in range(nb) for g in range(NSA_KV_HEADS)]
    bf = lambda x: x.astype(BF16)
    stack = lambda parts: jnp.concatenate(parts, axis=0)
    tile = lambda x: stack([x] * len(chains))
    part = lambda x, c: x[c * rows:(c + 1) * rows]
    qs = [bf(q_ref[n, g] * (d ** -0.5)) for n, g in chains]

    s_c = stack([_dot_nt(q, bf(kc_ref[n, g])) + bc_ref[g] for q, (n, g) in zip(qs, chains)])
    p_c = _masked_softmax(s_c, tile(okc_ref[...]) > 0.5)
    o_c = stack([_dot(bf(part(p_c, c)), bf(vc_ref[n, g])) for c, (n, g) in enumerate(chains)])

    ph = []
    for c in range(len(chains)):
        acc = p_c[c * rows:c * rows + s_new]
        for h in range(1, hpg):
            acc = acc + p_c[c * rows + h * s_new:c * rows + (h + 1) * s_new]
        ph.append(acc)
    ph = stack(ph)
    imp = ph[:, 0:SMP_CPAD] + ph[:, SMP_CPAD:2 * SMP_CPAD]
    blk = lax.broadcasted_iota(jnp.int32, imp.shape, 1)
    cur = tile(cur_ref[...])
    forced = (blk == 0) | (blk == cur) | (blk == cur - 1)
    imp = jnp.where(forced, FORCE_SCORE, jnp.where(blk <= cur, imp, -jnp.inf))
    rank = jnp.zeros(imp.shape, jnp.int32)
    for s2 in range(nsel):
        col = imp[:, s2:s2 + 1]
        beats = (col > imp) | ((col == imp) & (blk > s2))
        rank = rank + beats.astype(jnp.int32)
    chosen = bf((rank < N_SELECT) & (blk < nsel))
    chosen = stack([chosen[c * s_new:(c + 1) * s_new] for c in range(len(chains)) for _ in range(hpg)])
    sel_keys = _dot(chosen, e_ref[...])

    def kv_tiles(n, g, z):
        return [pg[0, 0, z, g] for pg in pages[n * n_pages:(n + 1) * n_pages]] + [sn_ref[n, z, g]]

    s_s = stack([jnp.concatenate([_dot(q, bf(kt)) for kt in kv_tiles(n, g, 0)], axis=1) + bs_ref[g]
                 for q, (n, g) in zip(qs, chains)])
    p_s = bf(_masked_softmax(s_s, (tile(oks_ref[...]) > 0.5) & (sel_keys > 0.5)))
    o_s = []
    for c, (n, g) in enumerate(chains):
        pc = part(p_s, c)
        acc = None
        for p, vt in enumerate(kv_tiles(n, g, 1)):
            term = _dot_nt(pc[:, p * KEY_TILE:(p + 1) * KEY_TILE], bf(vt))
            acc = term if acc is None else acc + term
        o_s.append(acc)
    o_s = stack(o_s)

    wkeys = wb_ref.shape[-1]
    s_w = stack([jnp.concatenate([_dot(q, bf(wb_ref[0, n, 0, g])), _dot(q, bf(wn_ref[n, 0, g]))], axis=1)
                 + bw_ref[g] for q, (n, g) in zip(qs, chains)])
    p_w = bf(_masked_softmax(s_w, tile(okw_ref[...]) > 0.5))
    o_w = stack([_dot_nt(part(p_w, c)[:, 0:wkeys], bf(wb_ref[0, n, 1, g]))
                 + _dot_nt(part(p_w, c)[:, wkeys:wkeys + KEY_TILE], bf(wn_ref[n, 1, g]))
                 for c, (n, g) in enumerate(chains)])

    gt = jax.nn.sigmoid(stack([gt_ref[n, g] for n, g in chains]))
    out = gt[:, 0:1] * o_c + gt[:, 1:2] * o_s + gt[:, 2:3] * o_w
    for c, (n, g) in enumerate(chains):
        o_ref[n, g] = part(out, c)


SMP_SEQS = 4


def _nsa_sample(layer, nsel, page_table, qg, kc, vc, bias_c, ok_c, cur, cache_sel, sel_new, bias_s, ok_s,
                expand, win_state, win_new, bias_w, ok_w, gts):
    b = qg.shape[0]
    g = NSA_KV_HEADS
    d = NSA_HEAD_DIM
    rows = qg.shape[2]
    n_pages = page_table.shape[1]
    wkeys = win_state.shape[-1]
    ks = (n_pages + 1) * KEY_TILE
    kw = wkeys + KEY_TILE
    nb = SMP_SEQS if b % SMP_SEQS == 0 else 1
    per_b = lambda *shape: pl.BlockSpec((nb,) + shape, lambda i, pt: (i,) + (0,) * len(shape))
    const = lambda *shape: pl.BlockSpec(shape, lambda i, pt: (0,) * len(shape))

    def page_spec(n, p):
        return pl.BlockSpec((1, 1, 2, g, d, PAGE_SIZE), lambda i, pt: (layer, pt[nb * i + n, p], 0, 0, 0, 0))

    in_specs = ([per_b(g, rows, d), per_b(g, 2 * SMP_CPAD, d), per_b(g, 2 * SMP_CPAD, d),
                 const(g, rows, 2 * SMP_CPAD), const(rows, 2 * SMP_CPAD), const(rows // NSA_HPG, SMP_CPAD)]
                + [page_spec(n, p) for n in range(nb) for p in range(n_pages)]
                + [per_b(2, g, d, KEY_TILE), const(g, rows, ks), const(rows, ks), const(SMP_CPAD, ks),
                   pl.BlockSpec((1, nb, 2, g, d, wkeys), lambda i, pt: (layer, i, 0, 0, 0, 0)),
                   per_b(2, g, d, KEY_TILE), const(g, rows, kw), const(rows, kw), per_b(g, rows, 128)])
    return pl.pallas_call(
        functools.partial(_nsa_sample_kernel, nb=nb, n_pages=n_pages, nsel=nsel),
        grid_spec=pltpu.PrefetchScalarGridSpec(
            num_scalar_prefetch=1,
            grid=(b // nb,),
            in_specs=in_specs,
            out_specs=pl.BlockSpec((nb, g, rows, d), lambda i, pt: (i, 0, 0, 0))),
        out_shape=jax.ShapeDtypeStruct((b, g, rows, d), F32),
        compiler_params=_cparams(("parallel",)),
        name="nsa_sample",
    )(page_table, qg, kc, vc, bias_c, ok_c, cur, *([cache_sel] * (nb * n_pages)), sel_new, bias_s, ok_s, expand,
      win_state, win_new, bias_w, ok_w, gts)


def _pad_rows(a, rows):
    return jnp.pad(a, ((0, 0), (0, rows - a.shape[1]), (0, 0)))


def _even_odd(n):
    return np.concatenate([np.arange(0, n, 2), np.arange(1, n, 2)])


def _prompt_bias_tables(rel_bias, t):
    g, hpg, tq = NSA_KV_HEADS, NSA_HPG, NSA_TQ
    nt = t // tq
    nc = t // CMP_BLOCK
    end_c = (_even_odd(nc) + 1) * CMP_BLOCK - 1
    dist_c = jnp.asarray(np.arange(t)[None, :] - end_c[:, None], jnp.int32)
    bias_c = _bias_lookup(rel_bias, dist_c, nc)
    bias_c = bias_c.reshape(g, hpg, nc, nt, tq).transpose(0, 3, 2, 1, 4).reshape(g, nt, nc, hpg * tq)
    off = np.arange(3)[:, None, None] * tq
    dist_t = off + np.arange(tq)[None, :, None] - np.arange(tq)[None, None, :]
    assert dist_t[2].min() >= REL_MAX_DIST
    bias_t = _bias_lookup(rel_bias, jnp.asarray(dist_t.reshape(3 * tq, tq), jnp.int32), 3 * tq)
    bias_t = bias_t.reshape(g, hpg, 3, tq, tq).transpose(0, 2, 1, 3, 4).reshape(g, 3, hpg * tq, tq)
    bias_t = bias_t[:, 0:2] - bias_t[:, 2:3]
    expand = np.arange(t)[None, :] // SEL_BLOCK == np.arange(t // SEL_BLOCK)[:, None]
    return bias_c, bias_t, jnp.asarray(expand, BF16)


def _sample_tables(rel_bias, past_len, s_new, n_win_keys):
    g, hpg = NSA_KV_HEADS, NSA_HPG
    tk = -(-(past_len + s_new) // SEL_BLOCK) * SEL_BLOCK
    nc = tk // CMP_BLOCK
    nsel = tk // SEL_BLOCK
    half = (nc + 1) // 2
    pos_q = past_len + np.arange(s_new)
    lane = np.arange(SMP_CPAD)
    n_of_lane = np.concatenate([2 * lane, 2 * lane + 1])
    real_c = np.concatenate([lane < half, lane < nc - half])
    dist_c = pos_q[:, None] - ((n_of_lane[None, :] + 1) * CMP_BLOCK - 1)
    ok_c = real_c[None, :] & (dist_c >= 0)
    n_keys_s = (past_len // PAGE_SIZE + 1) * KEY_TILE
    key = np.arange(n_keys_s)
    dist_s = pos_q[:, None] - key[None, :]
    ok_s = (key[None, :] < past_len + s_new) & (dist_s >= 0)
    n_keys_w = n_win_keys + KEY_TILE
    i = np.arange(n_keys_w)
    pos_kw = np.where(i < n_win_keys, past_len - n_win_keys + i, past_len + i - n_win_keys)
    dist_w = pos_q[:, None] - pos_kw[None, :]
    ok_w = (i[None, :] < n_win_keys + s_new) & (dist_w >= 0) & (dist_w < WINDOW) & (pos_kw[None, :] >= 0)
    dist = np.concatenate([dist_c, dist_s, dist_w], axis=1)
    dist = np.pad(dist, ((0, 8 - s_new), (0, 0)))
    bias = _bias_lookup(rel_bias, jnp.asarray(dist, jnp.int32), 8)[:, :s_new]
    bias = bias.reshape(g, hpg * s_new, dist.shape[1])
    c0, c1 = 2 * SMP_CPAD, 2 * SMP_CPAD + n_keys_s
    tile = lambda m: jnp.asarray(np.tile(m, (hpg, 1)), F32)
    expand = (key[None, :] // SEL_BLOCK == np.arange(SMP_CPAD)[:, None]) & (np.arange(SMP_CPAD)[:, None] < nsel)
    cur = np.broadcast_to((pos_q // SEL_BLOCK)[:, None], (s_new, SMP_CPAD))
    return dict(bias_c=bias[:, :, :c0], bias_s=bias[:, :, c0:c1], bias_w=bias[:, :, c1:],
                ok_c=tile(ok_c), ok_s=tile(ok_s), ok_w=tile(ok_w),
                expand=jnp.asarray(expand, BF16), cur=jnp.asarray(cur, jnp.int32), nc=nc, half=half, tk=tk)


def _flat_blocks(rows):
    b, tk = rows.shape[:2]
    nc = tk // CMP_BLOCK
    blk = rows.reshape(b, nc // 2, 2, CMP_BLOCK, 2, NSA_KV_HEADS, NSA_HEAD_DIM)
    return jnp.transpose(blk, (4, 0, 5, 2, 1, 3, 6)).reshape(2, b * NSA_KV_HEADS * nc, CMP_BLOCK * NSA_HEAD_DIM)


def _nsa_prompt_layer(pr3, lw, tabs):
    b, t = pr3.shape[:2]
    g, hpg, d, tq = NSA_KV_HEADS, NSA_HPG, NSA_HEAD_DIM, NSA_TQ
    nt = t // tq
    nc = t // CMP_BLOCK
    kv = pr3[:, :, COL_CMP:COL_CMP + 3 * KV_W].reshape(b, t, 3, 2, g, d)
    cmp_rows, sel_rows, win_rows = kv[:, :, 0], kv[:, :, 1], kv[:, :, 2]
    kcv = _compress(_flat_blocks(cmp_rows), lw['pe'], lw['cw1'], lw['cw2'], 512)
    kcv = kcv.reshape(2, b, g, nc, d).astype(BF16)
    kvt = kv[:, :, 1:].reshape(b, nt, tq, 2, 2, g, d).transpose(3, 0, 4, 5, 1, 6, 2).astype(BF16)
    gts = pr3[:, :, COL_MISC:COL_MISC + N_GATE].reshape(b, t, 3, g, hpg).transpose(0, 3, 1, 2, 4)
    gts = jnp.pad(gts.reshape(b, g, t, 3 * hpg), ((0, 0), (0, 0), (0, 0), (0, 128 - 3 * hpg)))
    o = _nsa_prompt(pr3, kcv[0], kcv[1], tabs[0], kvt, tabs[1], tabs[2], gts)
    return o.reshape(b * t, Q_W), cmp_rows, sel_rows, win_rows[:, t - min(WINDOW, t):]


def _to_native(rows, keys):
    b, n = rows.shape[:2]
    r = rows.reshape(b, n, 2, NSA_KV_HEADS, NSA_HEAD_DIM).transpose(0, 2, 3, 4, 1)
    return jnp.pad(r, ((0, 0),) * 4 + ((0, keys - n),))


def _nsa_sample_layer(pr3, lw, tabs, layer, cache_cmp, cache_sel, win_state, page_table):
    b, s_new = pr3.shape[:2]
    g, hpg, d = NSA_KV_HEADS, NSA_HPG, NSA_HEAD_DIM
    n_pages = page_table.shape[1]
    past_len = n_pages * PAGE_SIZE
    new = lambda c: pr3[:, :, c:c + KV_W]
    cmp_new, sel_new, win_new = new(COL_CMP), new(COL_SEL), new(COL_WIN)
    n_past, n_tail = past_len // CMP_BLOCK, tabs['nc'] - past_len // CMP_BLOCK
    pe3 = lw['pe'].reshape(2, CMP_BLOCK, d)
    past = _compress_pages(layer, page_table, cache_cmp, pe3, lw['cw1'].reshape(2, CMP_BLOCK, d, d), lw['cw2'])
    tail = jnp.pad(cmp_new, ((0, 0), (0, n_tail * CMP_BLOCK - s_new), (0, 0)))
    tail = _compress(_flat_blocks(tail.reshape(b, n_tail * CMP_BLOCK, 2, g, d)), lw['pe'], lw['cw1'], lw['cw2'], 512)
    tail = tail.reshape(2, b, g, 2, n_tail // 2, d).transpose(1, 0, 2, 3, 4, 5)
    kcv = jnp.concatenate([past, tail], axis=4)
    kcv = jnp.pad(kcv, ((0, 0),) * 4 + ((0, SMP_CPAD - kcv.shape[4]), (0, 0)))
    kcv = kcv.reshape(b, 2, g, 2 * SMP_CPAD, d).transpose(1, 0, 2, 3, 4)
    qg = pr3[:, :, COL_Q:COL_Q + Q_W].reshape(b, s_new, g, hpg, d).transpose(0, 2, 3, 1, 4)
    qg = qg.reshape(b, g, hpg * s_new, d)
    gts = pr3[:, :, COL_MISC:COL_MISC + N_GATE].reshape(b, s_new, 3, g, hpg).transpose(0, 3, 4, 1, 2)
    gts = jnp.pad(gts.reshape(b, g, hpg * s_new, 3), ((0, 0), (0, 0), (0, 0), (0, 128 - 3)))
    win_new_t = _to_native(win_new, KEY_TILE)
    o = _nsa_sample(layer, tabs['tk'] // SEL_BLOCK, page_table, qg, kcv[0], kcv[1], tabs['bias_c'], tabs['ok_c'],
                    tabs['cur'], cache_sel, _to_native(sel_new, KEY_TILE), tabs['bias_s'], tabs['ok_s'],
                    tabs['expand'], win_state, win_new_t, tabs['bias_w'], tabs['ok_w'], gts)
    o = o.reshape(b, g, hpg, s_new, d).transpose(0, 3, 1, 2, 4).reshape(b * s_new, Q_W)
    kv5 = lambda a: a.reshape(b, -1, 2, g, d)
    return o, kv5(cmp_new), kv5(sel_new), win_new_t[..., :s_new]


def _mixers(pr, b, t, lw, conv0, h0, s0, nsa_fn):
    pr3 = pr.reshape(b, t, PROJ_COLS)
    o_a, cmp_rows, sel_rows, win_rows = nsa_fn(pr3)
    lru_w = (lw['lcw'], lw['lcb'], lw['lgw'], lw['lgb'], lw['lam'])
    gla_w = (lw['gaw'], lw['gab'], lw['gng'])
    if t % 8 == 0:
        tc = min(t, 256)
        o_b, conv_n, h_n = _lru(pr3, COL_LRU_X // LRU_WIDTH, pr3, COL_LRU_G // LRU_WIDTH, conv0, h0[:, None],
                                *lru_w, tc, tc)
        tg = min(t, 256)
        o_c, s_n = _gla(pr3, COL_GLA_Q // GLA_QK_W, pr3, COL_GLA_K // GLA_QK_W, pr3, COL_GLA_V // GLA_V_W,
                         pr3, COL_GLA_G // GLA_V_W, pr3, COL_MISC // MISC_W, *gla_w, s0, tg,
                         min(tg, GLA_CHUNK), tg)
    else:
        tp = -(-t // 8) * 8
        cut = lambda c, w: _pad_rows(pr3[:, :, c:c + w], tp)
        if CONV_WIDTH - 1 + t <= LRU_SEG:
            o_b, conv_n, h_n = _lru_short(pr3[:, :, COL_LRU_X:COL_LRU_X + LRU_WIDTH],
                                          pr3[:, :, COL_LRU_G:COL_LRU_G + LRU_WIDTH], conv0, h0, *lru_w)
            h_n = h_n[:, None]
        else:
            o_b, conv_n, h_n = _lru(cut(COL_LRU_X, LRU_WIDTH), 0, cut(COL_LRU_G, LRU_WIDTH), 0, conv0,
                                    h0[:, None], *lru_w, tp, t)
        o_c, s_n = _gla(cut(COL_GLA_Q, GLA_QK_W), 0, cut(COL_GLA_K, GLA_QK_W), 0, cut(COL_GLA_V, GLA_V_W), 0,
                         cut(COL_GLA_G, GLA_V_W), 0, cut(COL_MISC, MISC_W), 0, *gla_w, s0, tp, tp, t)
        o_b, o_c = o_b[:, :t], o_c[:, :t]
    o_b = o_b.reshape(b * t, LRU_WIDTH)
    o_c = o_c.reshape(b * t, GLA_V_W)
    return (o_a, o_b, o_c), (cmp_rows, sel_rows, win_rows, conv_n, h_n[:, 0], s_n)


def _layer(x, b, t, lw, conv0, h0, s0, nsa_fn, tm):
    pr = _proj(x, lw['ng'], lw['w_in'], lw['b_in'], tm, PROJ_COLS // 5)
    (o_a, o_b, o_c), states = _mixers(pr, b, t, lw, conv0, h0, s0, nsa_fn)
    x = _merge(x, o_a, o_b, o_c, pr, lw['wb'], lw['wo'], min(tm, 512))
    x = _mlp(x, lw['mg'], lw['w1'], lw['w2'], tm, 1024)
    return x, states


def kernel(x_prompt, x_sample, cache_nsa_cmp_kv, cache_nsa_sel_kv, state_nsa_win_kv, state_lru_conv,
           state_lru_h, state_gla, page_table, rel_bias, norm_mix_g, norm_mlp_g, norm_final_g, w_in, b_in,
           nsa_cmp_pe, nsa_cmp_w1, nsa_cmp_w2, lru_gate_w, lru_gate_b, lru_lambda, lru_conv_w, lru_conv_b,
           gla_alpha_w, gla_alpha_b, gla_norm_g, w_branch, w_out, mlp_w1, mlp_w2):
    bp, tp = x_prompt.shape[:2]
    bs, ts = x_sample.shape[:2]
    depth = w_in.shape[0]
    n_pages = page_table.shape[1]
    w_buf = state_nsa_win_kv.shape[2]

    cols = [w_in[..., _SRC[n][0]:_SRC[n][0] + _SRC[n][1]] for n in _DST_ORDER]
    pad = PROJ_COLS - sum(c.shape[-1] for c in cols)
    w_in_p = jnp.concatenate(cols + [jnp.zeros(w_in.shape[:2] + (pad,), w_in.dtype)], axis=-1).astype(BF16)
    bcols = [b_in[..., _SRC[n][0]:_SRC[n][0] + _SRC[n][1]] for n in _DST_ORDER]
    b_in_p = jnp.concatenate(bcols + [jnp.zeros((depth, pad), b_in.dtype)], axis=-1)[:, None, :]
    eye = jnp.eye(LRU_BLOCKS, dtype=lru_gate_w.dtype)
    lgw = jnp.einsum('lznce,nm->lzncme', lru_gate_w, eye).reshape(depth, 2, LRU_WIDTH, LRU_WIDTH).astype(BF16)
    gaw = jnp.zeros((depth, MISC_W, GLA_QK_W), F32).at[:, N_GATE:N_GATE + GLA_RANK].set(gla_alpha_w).astype(BF16)
    pe = jnp.transpose(nsa_cmp_pe, (0, 2, 1, 3)).reshape(depth, 2, 1, CMP_BLOCK * NSA_HEAD_DIM)
    cw1 = nsa_cmp_w1.astype(BF16)
    cw2 = nsa_cmp_w2.astype(BF16)
    wb = w_branch.astype(BF16)
    wo = w_out.astype(BF16)
    w1 = mlp_w1.astype(BF16)
    w2 = mlp_w2.astype(BF16)
    cache_cmp = jnp.transpose(cache_nsa_cmp_kv, (0, 1, 3, 4, 5, 2))
    cache_sel = jnp.transpose(cache_nsa_sel_kv, (0, 1, 3, 4, 5, 2))
    win_state = jnp.transpose(state_nsa_win_kv, (0, 1, 3, 4, 5, 2))

    tabs_p = _prompt_bias_tables(rel_bias, tp)
    tabs_s = _sample_tables(rel_bias, n_pages * PAGE_SIZE, ts, w_buf)

    xp = x_prompt.reshape(bp * tp, D_MODEL)
    xs = x_sample.reshape(bs * ts, D_MODEL)
    conv0_p = jnp.zeros((bp, CONV_WIDTH - 1, LRU_WIDTH), F32)
    h0_p = jnp.zeros((bp, LRU_WIDTH), F32)
    s0_p = jnp.zeros((bp, GLA_HEADS, GLA_DK, GLA_DV), F32)
    outs_p = [[] for _ in range(6)]
    outs_s = [[] for _ in range(6)]
    for l in range(depth):
        lw = dict(ng=norm_mix_g[l][None], mg=norm_mlp_g[l][None], w_in=w_in_p[l], b_in=b_in_p[l],
                  pe=pe[l], cw1=cw1[l], cw2=cw2[l], lcw=lru_conv_w[l], lcb=lru_conv_b[l][None], lgw=lgw[l],
                  lgb=lru_gate_b[l], lam=lru_lambda[l][None], gaw=gaw[l], gab=gla_alpha_b[l][None],
                  gng=gla_norm_g[l][None], wb=wb[l], wo=wo[l], w1=w1[l], w2=w2[l])
        xp, st_p = _layer(xp, bp, tp, lw, conv0_p, h0_p, s0_p,
                          functools.partial(_nsa_prompt_layer, lw=lw, tabs=tabs_p), _row_tile(bp * tp, 1024))
        nsa_s = functools.partial(_nsa_sample_layer, lw=lw, tabs=tabs_s, layer=l, cache_cmp=cache_cmp,
                                  cache_sel=cache_sel, win_state=win_state, page_table=page_table)
        xs, st_s = _layer(xs, bs, ts, lw, state_lru_conv[l], state_lru_h[l], state_gla[l], nsa_s, bs * ts)
        for j in range(6):
            outs_p[j].append(st_p[j])
            outs_s[j].append(st_s[j])
    y_prompt = _final_norm(xp, norm_final_g[None], _row_tile(bp * tp, 1024)).reshape(bp, tp, D_MODEL)
    y_sample = _final_norm(xs, norm_final_g[None], bs * ts).reshape(bs, ts, D_MODEL)
    st = lambda outs, j: jnp.stack(outs[j])
    win_s = jnp.concatenate([win_state[..., ts:], st(outs_s, 2)], axis=-1).transpose(0, 1, 5, 2, 3, 4)
    return (y_prompt, y_sample, st(outs_p, 0), st(outs_s, 0), st(outs_p, 1), st(outs_s, 1),
            st(outs_p, 2), win_s, st(outs_p, 3), st(outs_s, 3), st(outs_p, 4), st(outs_s, 4),
            st(outs_p, 5), st(outs_s, 5))
```

```python
import functools
import math

import jax
import jax.numpy as jnp
import numpy as np
from jax import lax
from jax.experimental import pallas as pl
from jax.experimental.pallas import tpu as pltpu

F32 = jnp.float32
BF16 = jnp.bfloat16

D_MODEL = 1024
DEPTH = 4
PAGE_SIZE = 128
NSA_HEADS = 8
NSA_KV_HEADS = 2
NSA_HPG = NSA_HEADS // NSA_KV_HEADS
NSA_HEAD_DIM = 64
CMP_BLOCK = 32
SEL_BLOCK = 64
N_SELECT = 16
WINDOW = 512
FORCE_SCORE = 1e4
REL_BUCKETS = 32
REL_MAX_DIST = 128
LRU_WIDTH = 512
LRU_BLOCKS = 8
LRU_BLOCK_DIM = LRU_WIDTH // LRU_BLOCKS
CONV_WIDTH = 4
LRU_C = 8.0
GLA_HEADS = 4
GLA_DK = 64
GLA_DV = 128
GLA_RANK = 16
GLA_TAU = 16.0
GLA_CHUNK = 64
D_FF = 4 * D_MODEL
N_BRANCH = 3
BRANCH_WIDTH = 512
NORM_EPS = 1e-6

KV_W = 2 * NSA_KV_HEADS * NSA_HEAD_DIM
Q_W = NSA_HEADS * NSA_HEAD_DIM
GROUP_W = NSA_HPG * NSA_HEAD_DIM
GLA_QK_W = GLA_HEADS * GLA_DK
GLA_V_W = GLA_HEADS * GLA_DV
MISC_W = 128
N_GATE = 3 * NSA_HEADS

COL_MERGE = 0
COL_Q = COL_MERGE + N_BRANCH * D_MODEL
COL_LRU_X = COL_Q + Q_W
COL_LRU_G = COL_LRU_X + LRU_WIDTH
COL_GLA_V = COL_LRU_G + LRU_WIDTH
COL_GLA_G = COL_GLA_V + GLA_V_W
COL_CMP = COL_GLA_G + GLA_V_W
COL_SEL = COL_CMP + KV_W
COL_WIN = COL_SEL + KV_W
COL_GLA_Q = COL_WIN + KV_W
COL_GLA_K = COL_GLA_Q + GLA_QK_W
COL_MISC = COL_GLA_K + GLA_QK_W
PROJ_COLS = COL_MISC + MISC_W

_SRC = {}
_off = 0
for _name, _w in (('nsa_q', Q_W), ('nsa_cmp_kv', KV_W), ('nsa_sel_kv', KV_W), ('nsa_win_kv', KV_W),
                  ('nsa_gate', N_GATE), ('lru_x', LRU_WIDTH), ('lru_gate', LRU_WIDTH),
                  ('gla_q', GLA_QK_W), ('gla_k', GLA_QK_W), ('gla_v', GLA_V_W),
                  ('gla_alpha', GLA_RANK), ('gla_gate', GLA_V_W), ('merge_gate', N_BRANCH * D_MODEL)):
    _SRC[_name] = (_off, _w)
    _off += _w
_DST_ORDER = ('merge_gate', 'nsa_q', 'lru_x', 'lru_gate', 'gla_v', 'gla_gate', 'nsa_cmp_kv',
              'nsa_sel_kv', 'nsa_win_kv', 'gla_q', 'gla_k', 'nsa_gate', 'gla_alpha')

NSA_TQ = 128
NSA_LANES = NSA_HPG * NSA_TQ
NSA_CLASS = 2
LOG2E = math.log2(math.e)
ONES_ROWS = 16
VMEM_LIMIT = 56 * 1024 * 1024


def _cparams(sem):
    return pltpu.CompilerParams(dimension_semantics=sem, vmem_limit_bytes=VMEM_LIMIT)


def _gelu(x):
    return x * (0.5 * (1.0 + jnp.tanh(math.sqrt(2.0 / math.pi) * (x + 0.044715 * (x * x * x)))))


def _softplus(x):
    return jnp.maximum(x, 0.0) + jnp.log1p(jnp.exp(-jnp.abs(x)))


def _rms(x, g):
    return x * lax.rsqrt(jnp.mean(x * x, axis=-1, keepdims=True) + NORM_EPS) * g


def _dot(a, b):
    return jnp.dot(a, b, preferred_element_type=F32)


def _dot_nt(a, b):
    return lax.dot_general(a, b, (((1,), (1,)), ((), ())), preferred_element_type=F32)


def _dot_tn(a, b):
    return lax.dot_general(a, b, (((0,), (0,)), ((), ())), preferred_element_type=F32)


def _proj_kernel(x_ref, g_ref, w_ref, b_ref, o_ref, h_ref):
    @pl.when(pl.program_id(1) == 0)
    def _():
        h_ref[...] = _rms(x_ref[...], g_ref[...]).astype(BF16)

    o_ref[...] = _dot(h_ref[...], w_ref[...]) + b_ref[...]


def _proj(x, g, w, b, tm, tn):
    n = x.shape[0]
    return pl.pallas_call(
        _proj_kernel,
        grid=(n // tm, PROJ_COLS // tn),
        in_specs=[pl.BlockSpec((tm, D_MODEL), lambda i, j: (i, 0)),
                  pl.BlockSpec((1, D_MODEL), lambda i, j: (0, 0)),
                  pl.BlockSpec((D_MODEL, tn), lambda i, j: (0, j)),
                  pl.BlockSpec((1, tn), lambda i, j: (0, j))],
        out_specs=pl.BlockSpec((tm, tn), lambda i, j: (i, j)),
        out_shape=jax.ShapeDtypeStruct((n, PROJ_COLS), F32),
        scratch_shapes=[pltpu.VMEM((tm, D_MODEL), BF16)],
        compiler_params=_cparams(("parallel", "arbitrary")),
        name="proj",
    )(x, g, w, b)


def _merge_kernel(x_ref, oa_ref, ob_ref, oc_ref, g0_ref, g1_ref, g2_ref, wb_ref, wo_ref, o_ref):
    m = jax.nn.sigmoid(g0_ref[...]) * _dot(oa_ref[...].astype(BF16), wb_ref[0])
    m = m + jax.nn.sigmoid(g1_ref[...]) * _dot(ob_ref[...].astype(BF16), wb_ref[1])
    m = m + jax.nn.sigmoid(g2_ref[...]) * _dot(oc_ref[...].astype(BF16), wb_ref[2])
    o_ref[...] = x_ref[...] + _dot(m.astype(BF16), wo_ref[...])


def _merge(x, oa, ob, oc, pr, wb, wo, tm):
    n = x.shape[0]
    row = lambda w: pl.BlockSpec((tm, w), lambda i: (i, 0))
    gate = lambda z: pl.BlockSpec((tm, D_MODEL), lambda i: (i, COL_MERGE // D_MODEL + z))
    return pl.pallas_call(
        _merge_kernel,
        grid=(n // tm,),
        in_specs=[row(D_MODEL), row(BRANCH_WIDTH), row(BRANCH_WIDTH), row(BRANCH_WIDTH),
                  gate(0), gate(1), gate(2),
                  pl.BlockSpec((N_BRANCH, BRANCH_WIDTH, D_MODEL), lambda i: (0, 0, 0)),
                  pl.BlockSpec((D_MODEL, D_MODEL), lambda i: (0, 0))],
        out_specs=row(D_MODEL),
        out_shape=jax.ShapeDtypeStruct((n, D_MODEL), F32),
        compiler_params=_cparams(("parallel",)),
        name="merge",
    )(x, oa, ob, oc, pr, pr, pr, wb, wo)


def _mlp_kernel(x_ref, g_ref, w1_ref, w2_ref, o_ref, h_ref, acc_ref):
    f = pl.program_id(1)

    @pl.when(f == 0)
    def _():
        h_ref[...] = _rms(x_ref[...], g_ref[...]).astype(BF16)
        acc_ref[...] = jnp.zeros_like(acc_ref)

    a = jnp.maximum(_dot(h_ref[...], w1_ref[...]), 0.0)
    acc_ref[...] += _dot((a * a).astype(BF16), w2_ref[...])

    @pl.when(f == pl.num_programs(1) - 1)
    def _():
        o_ref[...] = x_ref[...] + acc_ref[...]


def _mlp(x, g, w1, w2, tm, tf):
    n = x.shape[0]
    return pl.pallas_call(
        _mlp_kernel,
        grid=(n // tm, D_FF // tf),
        in_specs=[pl.BlockSpec((tm, D_MODEL), lambda i, f: (i, 0)),
                  pl.BlockSpec((1, D_MODEL), lambda i, f: (0, 0)),
                  pl.BlockSpec((D_MODEL, tf), lambda i, f: (0, f)),
                  pl.BlockSpec((tf, D_MODEL), lambda i, f: (f, 0))],
        out_specs=pl.BlockSpec((tm, D_MODEL), lambda i, f: (i, 0)),
        out_shape=jax.ShapeDtypeStruct((n, D_MODEL), F32),
        scratch_shapes=[pltpu.VMEM((tm, D_MODEL), BF16), pltpu.VMEM((tm, D_MODEL), F32)],
        compiler_params=_cparams(("parallel", "arbitrary")),
        name="mlp",
    )(x, g, w1, w2)


def _norm_kernel(x_ref, g_ref, o_ref):
    o_ref[...] = _rms(x_ref[...], g_ref[...])


def _final_norm(x, g, tm):
    n = x.shape[0]
    return pl.pallas_call(
        _norm_kernel,
        grid=(n // tm,),
        in_specs=[pl.BlockSpec((tm, D_MODEL), lambda i: (i, 0)),
                  pl.BlockSpec((1, D_MODEL), lambda i: (0, 0))],
        out_specs=pl.BlockSpec((tm, D_MODEL), lambda i: (i, 0)),
        out_shape=jax.ShapeDtypeStruct((n, D_MODEL), F32),
        compiler_params=_cparams(("parallel",)),
        name="final_norm",
    )(x, g)


_XB = 8


def _lru_kernel(x_ref, gb_ref, conv0_ref, h0_ref, cw_ref, cb_ref, gw_ref, gbias_ref, lam_ref,
                o_ref, convn_ref, hn_ref, xbuf, hcar, *, tc, tv):
    @pl.when(pl.program_id(1) == 0)
    def _():
        xbuf[0:_XB, :] = jnp.zeros((_XB, LRU_WIDTH), F32)
        xbuf[_XB - 3:_XB, :] = conv0_ref[0]
        hcar[...] = h0_ref[0]

    x = x_ref[0]
    xbuf[_XB:_XB + tc, :] = x
    w = cw_ref[...]
    xc = cb_ref[...] + xbuf[_XB - 3:_XB - 3 + tc, :] * w[0:1]
    xc = xc + xbuf[_XB - 2:_XB - 2 + tc, :] * w[1:2]
    xc = xc + xbuf[_XB - 1:_XB - 1 + tc, :] * w[2:3]
    xc = xc + x * w[3:4]
    tail = xbuf[_XB - 3 + tv:_XB + tv, :]
    convn_ref[0] = tail
    xbuf[_XB - 3:_XB, :] = tail

    xcb = xc.astype(BF16)
    r = jax.nn.sigmoid(_dot(xcb, gw_ref[0]) + gbias_ref[0:1])
    i = jax.nn.sigmoid(_dot(xcb, gw_ref[1]) + gbias_ref[1:2])
    log_a = (-LRU_C * r) * _softplus(-lam_ref[...])
    a = jnp.exp(log_a)
    b = jnp.sqrt(-jnp.tanh(log_a) * (a * a + 1.0)) * (i * xc)

    rows = lax.broadcasted_iota(jnp.int32, (tc, LRU_WIDTH), 0)
    s = 1
    while s < tc:
        a_sh = pltpu.roll(a, s, 0)
        b_sh = pltpu.roll(b, s, 0)
        m = rows >= s
        b = jnp.where(m, a * b_sh + b, b)
        a = jnp.where(m, a * a_sh, a)
        s *= 2
    h = a * hcar[...] + b
    hlast = h[tv - 1:tv]
    hcar[...] = hlast
    hn_ref[0] = hlast
    o_ref[0] = _gelu(gb_ref[0]) * h


def _lru(x_arr, x_blk, gb_arr, gb_blk, conv0, h0, cw, cb, gw, gbias, lam, tc, tv):
    b, t = x_arr.shape[:2]
    r = LRU_WIDTH
    const2 = lambda shape: pl.BlockSpec(shape, lambda i, c: (0, 0))
    return pl.pallas_call(
        functools.partial(_lru_kernel, tc=tc, tv=tv),
        grid=(b, t // tc),
        in_specs=[pl.BlockSpec((1, tc, r), lambda i, c: (i, c, x_blk)),
                  pl.BlockSpec((1, tc, r), lambda i, c: (i, c, gb_blk)),
                  pl.BlockSpec((1, CONV_WIDTH - 1, r), lambda i, c: (i, 0, 0)),
                  pl.BlockSpec((1, 1, r), lambda i, c: (i, 0, 0)),
                  const2((CONV_WIDTH, r)), const2((1, r)),
                  pl.BlockSpec((2, r, r), lambda i, c: (0, 0, 0)),
                  const2((2, r)), const2((1, r))],
        out_specs=[pl.BlockSpec((1, tc, r), lambda i, c: (i, c, 0)),
                   pl.BlockSpec((1, CONV_WIDTH - 1, r), lambda i, c: (i, 0, 0)),
                   pl.BlockSpec((1, 1, r), lambda i, c: (i, 0, 0))],
        out_shape=[jax.ShapeDtypeStruct((b, t, r), F32),
                   jax.ShapeDtypeStruct((b, CONV_WIDTH - 1, r), F32),
                   jax.ShapeDtypeStruct((b, 1, r), F32)],
        scratch_shapes=[pltpu.VMEM((_XB + tc, r), F32), pltpu.VMEM((1, r), F32)],
        compiler_params=_cparams(("parallel", "arbitrary")),
        name="rglru",
    )(x_arr, gb_arr, conv0, h0, cw, cb, gw, gbias, lam)


LRU_SEG = 8


def _lru_short_kernel(x_ref, gb_ref, h0_ref, cw_ref, cb_ref, gw_ref, gbias_ref, lam_ref, o_ref, h_ref, *, t):
    x = x_ref[...]
    pos = lax.broadcasted_iota(jnp.int32, x.shape, 0) & (LRU_SEG - 1)
    w = cw_ref[...]
    xc = cb_ref[...] + x * w[CONV_WIDTH - 1:CONV_WIDTH]
    for k in range(1, CONV_WIDTH):
        xc = xc + pltpu.roll(x, k, 0) * w[CONV_WIDTH - 1 - k:CONV_WIDTH - k]
    xcb = xc.astype(BF16)
    r = jax.nn.sigmoid(_dot(xcb, gw_ref[0]) + gbias_ref[0:1])
    i = jax.nn.sigmoid(_dot(xcb, gw_ref[1]) + gbias_ref[1:2])
    log_a = (-LRU_C * r) * _softplus(-lam_ref[...])
    a = jnp.exp(log_a)
    b = jnp.sqrt(-jnp.tanh(log_a) * (a * a + 1.0)) * (i * xc)
    real = (pos >= CONV_WIDTH - 1) & (pos < CONV_WIDTH - 1 + t)
    a = jnp.where(real, a, 1.0)
    b = jnp.where(real, b, 0.0)
    s = 1
    while s < LRU_SEG:
        m = pos >= s
        b = jnp.where(m, a * pltpu.roll(b, s, 0) + b, b)
        a = jnp.where(m, a * pltpu.roll(a, s, 0), a)
        s *= 2
    h = a * h0_ref[...] + b
    h_ref[...] = h
    o_ref[...] = _gelu(gb_ref[...]) * h


def _lru_short(x, gb, conv0, h0, cw, cb, gw, gbias, lam):
    b, t, r = x.shape
    lead = CONV_WIDTH - 1
    seg = lambda head, body: jnp.concatenate(
        [head, body, jnp.zeros((b, LRU_SEG - lead - t, r), F32)], axis=1).reshape(b * LRU_SEG, r)
    xin = seg(conv0, x)
    rows = b * LRU_SEG
    tm = _row_tile(rows, 256)
    blk = pl.BlockSpec((tm, r), lambda i: (i, 0))
    const2 = lambda shape: pl.BlockSpec(shape, lambda i: (0, 0))
    o, h = pl.pallas_call(
        functools.partial(_lru_short_kernel, t=t),
        grid=(rows // tm,),
        in_specs=[blk, blk, blk, const2((CONV_WIDTH, r)), const2((1, r)),
                  pl.BlockSpec((2, r, r), lambda i: (0, 0, 0)), const2((2, r)), const2((1, r))],
        out_specs=[blk, blk],
        out_shape=[jax.ShapeDtypeStruct((rows, r), F32), jax.ShapeDtypeStruct((rows, r), F32)],
        compiler_params=_cparams(("parallel",)),
        name="rglru_short",
    )(xin, seg(jnp.zeros((b, lead, r), F32), gb), jnp.repeat(h0, LRU_SEG, axis=0), cw, cb, gw, gbias, lam)
    o = o.reshape(b, LRU_SEG, r)[:, lead:lead + t]
    h_new = h.reshape(b, LRU_SEG, r)[:, lead + t - 1]
    conv_new = xin.reshape(b, LRU_SEG, r)[:, t:t + lead]
    return o, conv_new, h_new


def _gla_kernel(q_ref, k_ref, v_ref, og_ref, misc_ref, aw_ref, ab_ref, ng_ref, s0_ref,
                o_ref, sn_ref, st, *, tg, ck, tv):
    @pl.when(pl.program_id(1) == 0)
    def _():
        for h in range(GLA_HEADS):
            st[h] = s0_ref[0, h].T

    pre = _dot(misc_ref[0].astype(BF16), aw_ref[...]) + ab_ref[...]
    g = -_softplus(-pre) * (1.0 / GLA_TAU)
    if tv < tg:
        g = jnp.where(lax.broadcasted_iota(jnp.int32, g.shape, 0) < tv, g, 0.0)
    q = q_ref[0] * (GLA_DK ** -0.5)
    k = k_ref[0]
    v = v_ref[0]
    og = og_ref[0]
    ng = ng_ref[...]
    rows = lax.broadcasted_iota(jnp.int32, (ck, GLA_QK_W), 0)
    tril = (lax.broadcasted_iota(jnp.int32, (ck, ck), 0) >= lax.broadcasted_iota(jnp.int32, (ck, ck), 1))
    for c in range(tg // ck):
        sl = slice(c * ck, (c + 1) * ck)
        bc = g[sl]
        s = 1
        while s < ck:
            bc = bc + jnp.where(rows >= s, pltpu.roll(bc, s, 0), 0.0)
            s *= 2
        bl = bc[ck - 1:ck]
        e = jnp.exp(bc)
        qi = (q[sl] * e).astype(BF16)
        ki = (k[sl] * jnp.exp(-bc)).astype(BF16)
        kd = (k[sl] * jnp.exp(bl - bc)).astype(BF16)
        dec = jnp.exp(bl)
        vb = v[sl].astype(BF16)
        outs = []
        for h in range(GLA_HEADS):
            ks = slice(h * GLA_DK, (h + 1) * GLA_DK)
            vs = slice(h * GLA_DV, (h + 1) * GLA_DV)
            att = jnp.where(tril, _dot_nt(qi[:, ks], ki[:, ks]), 0.0)
            s_prev = st[h]
            o = _dot(att.astype(BF16), vb[:, vs]) + _dot_nt(qi[:, ks], s_prev.astype(BF16))
            st[h] = s_prev * dec[:, ks] + _dot_tn(vb[:, vs], kd[:, ks])
            o = _rms(o, ng[:, vs])
            ogh = og[sl, vs]
            outs.append(o * (ogh * jax.nn.sigmoid(ogh)))
        o_ref[0, sl, :] = jnp.concatenate(outs, axis=-1)

    @pl.when(pl.program_id(1) == pl.num_programs(1) - 1)
    def _():
        for h in range(GLA_HEADS):
            sn_ref[0, h] = st[h].T


def _gla(q_arr, q_blk, k_arr, k_blk, v_arr, v_blk, og_arr, og_blk, misc_arr, misc_blk,
         aw, ab, ng, s0, tg, ck, tv):
    b, t = q_arr.shape[:2]
    col = lambda w, blk: pl.BlockSpec((1, tg, w), lambda i, c: (i, c, blk))
    const2 = lambda shape: pl.BlockSpec(shape, lambda i, c: (0, 0))
    state = pl.BlockSpec((1, GLA_HEADS, GLA_DK, GLA_DV), lambda i, c: (i, 0, 0, 0))
    return pl.pallas_call(
        functools.partial(_gla_kernel, tg=tg, ck=ck, tv=tv),
        grid=(b, t // tg),
        in_specs=[col(GLA_QK_W, q_blk), col(GLA_QK_W, k_blk), col(GLA_V_W, v_blk), col(GLA_V_W, og_blk),
                  col(MISC_W, misc_blk), const2((MISC_W, GLA_QK_W)), const2((1, GLA_QK_W)),
                  const2((1, GLA_V_W)), state],
        out_specs=[pl.BlockSpec((1, tg, GLA_V_W), lambda i, c: (i, c, 0)), state],
        out_shape=[jax.ShapeDtypeStruct((b, t, GLA_V_W), F32),
                   jax.ShapeDtypeStruct((b, GLA_HEADS, GLA_DK, GLA_DV), F32)],
        scratch_shapes=[pltpu.VMEM((GLA_HEADS, GLA_DV, GLA_DK), F32)],
        compiler_params=_cparams(("parallel", "arbitrary")),
        name="gla",
    )(q_arr, k_arr, v_arr, og_arr, misc_arr, aw, ab, ng, s0)


def _bias_kernel(tbl_ref, dist_ref, o_ref):
    h = pl.program_id(0)
    max_exact = REL_BUCKETS // 2
    n = jnp.maximum(dist_ref[...], 0)
    nf = jnp.maximum(n, 1).astype(F32)
    large = max_exact + (jnp.log(nf / max_exact) / math.log(REL_MAX_DIST / max_exact)
                         * (REL_BUCKETS - max_exact)).astype(jnp.int32)
    bucket = jnp.where(n < max_exact, n, jnp.minimum(large, REL_BUCKETS - 1))
    out = jnp.zeros(bucket.shape, F32)
    for kk in range(REL_BUCKETS):
        out = jnp.where(bucket == kk, tbl_ref[kk, h], out)
    o_ref[0] = out


def _bias_lookup(rel_bias, dist, tr):
    r, c = dist.shape
    return pl.pallas_call(
        _bias_kernel,
        grid=(NSA_HEADS, r // tr),
        in_specs=[pl.BlockSpec(memory_space=pltpu.SMEM),
                  pl.BlockSpec((tr, c), lambda h, i: (i, 0))],
        out_specs=pl.BlockSpec((1, tr, c), lambda h, i: (h, i, 0)),
        out_shape=jax.ShapeDtypeStruct((NSA_HEADS, r, c), F32),
        compiler_params=_cparams(("parallel", "parallel")),
        name="rel_bias",
    )(rel_bias, dist)


def _cmp_kernel(x_ref, pe_ref, w1_ref, w2_ref, o_ref):
    hid = _gelu(_dot((x_ref[0] + pe_ref[0]).astype(BF16), w1_ref[0]))
    o_ref[0] = _dot(hid.astype(BF16), w2_ref[0])


def _row_tile(m, cap):
    for tm in range(min(cap, m) // 8 * 8, 0, -8):
        if m % tm == 0:
            return tm
    return m


def _compress(flat, pe, w1, w2, cap):
    m = flat.shape[1]
    tm = _row_tile(m, cap)
    fw = CMP_BLOCK * NSA_HEAD_DIM
    d = NSA_HEAD_DIM
    return pl.pallas_call(
        _cmp_kernel,
        grid=(2, m // tm),
        in_specs=[pl.BlockSpec((1, tm, fw), lambda z, i: (z, i, 0)),
                  pl.BlockSpec((1, 1, fw), lambda z, i: (z, 0, 0)),
                  pl.BlockSpec((1, fw, d), lambda z, i: (z, 0, 0)),
                  pl.BlockSpec((1, d, d), lambda z, i: (z, 0, 0))],
        out_specs=pl.BlockSpec((1, tm, d), lambda z, i: (z, i, 0)),
        out_shape=jax.ShapeDtypeStruct((2, m, d), F32),
        compiler_params=_cparams(("parallel", "parallel")),
        name="nsa_compress",
    )(flat, pe, w1, w2)


def _cmp_pages_kernel(pt_ref, *refs, n_pages):
    del pt_ref
    pages = refs[:n_pages]
    pe_ref, w1_ref, w2_ref, o_ref, x_scr, h_scr = refs[n_pages:]
    gd = NSA_KV_HEADS * NSA_HEAD_DIM
    nblk = n_pages * PAGE_SIZE // CMP_BLOCK
    for z in range(2):
        for p in range(n_pages):
            x_scr[z, p * PAGE_SIZE:(p + 1) * PAGE_SIZE, :] = pages[p][0, 0, z].reshape(gd, PAGE_SIZE).T
        acc = jnp.zeros((nblk, gd), F32)
        for t in range(CMP_BLOCK):
            rows = x_scr[z, pl.ds(t, nblk, stride=CMP_BLOCK), :]
            acc = acc + _dot((rows + pe_ref[z, t:t + 1, :]).astype(BF16), w1_ref[z, t])
        h_scr[z] = _dot(_gelu(acc).astype(BF16), w2_ref[z])
        for par in range(2):
            o_ref[0, z, par] = h_scr[z, pl.ds(par, nblk // 2, stride=2), :]


def _compress_pages(layer, page_table, cache, pe, w1, w2):
    b, n_pages = page_table.shape
    g, d = NSA_KV_HEADS, NSA_HEAD_DIM
    gd = g * d
    nblk = n_pages * PAGE_SIZE // CMP_BLOCK
    eye = jnp.eye(g, dtype=w1.dtype)
    w1 = jnp.einsum('gh,ztje->ztgjhe', eye, w1).reshape(2, CMP_BLOCK, gd, gd)
    w2 = jnp.einsum('gh,zje->zgjhe', eye, w2).reshape(2, gd, gd)
    pe = jnp.tile(pe, (1, 1, g))
    const = lambda *shape: pl.BlockSpec(shape, lambda i, pt: (0,) * len(shape))

    def page_spec(p):
        return pl.BlockSpec((1, 1, 2, g, d, PAGE_SIZE), lambda i, pt: (layer, pt[i, p], 0, 0, 0, 0))

    out = pl.pallas_call(
        functools.partial(_cmp_pages_kernel, n_pages=n_pages),
        grid_spec=pltpu.PrefetchScalarGridSpec(
            num_scalar_prefetch=1,
            grid=(b,),
            in_specs=[page_spec(p) for p in range(n_pages)]
            + [const(2, CMP_BLOCK, gd), const(2, CMP_BLOCK, gd, gd), const(2, gd, gd)],
            out_specs=pl.BlockSpec((1, 2, 2, nblk // 2, gd), lambda i, pt: (i, 0, 0, 0, 0)),
            scratch_shapes=[pltpu.VMEM((2, n_pages * PAGE_SIZE, gd), F32), pltpu.VMEM((2, nblk, gd), F32)]),
        out_shape=jax.ShapeDtypeStruct((b, 2, 2, nblk // 2, gd), F32),
        compiler_params=_cparams(("parallel",)),
        name="nsa_compress_pages",
    )(page_table, *([cache] * n_pages), pe, w1, w2)
    return out.reshape(b, 2, 2, nblk // 2, g, d).transpose(0, 1, 4, 2, 3, 5)


def _nsa_prompt_kernel(q_ref, kc_ref, vc_ref, bc_ref, ks_ref, vs_ref, kw_ref, vw_ref, bt_ref, e_ref, gt_ref,
                       o_ref, s_scr, p_scr, m_scr, selk_scr, os_scr, ow_scr):
    i = pl.program_id(2)
    tq, d, ck, hpg = NSA_TQ, NSA_HEAD_DIM, NSA_TQ, NSA_HPG
    nchunk = ks_ref.shape[3]
    nsel = kc_ref.shape[2] // 2
    neg = -jnp.inf
    qf = q_ref[0] * (d ** -0.5 * LOG2E)
    qs = jnp.concatenate([qf[:, h * d:(h + 1) * d] for h in range(hpg)], axis=0).astype(BF16)

    tk = lax.broadcasted_iota(jnp.int32, (tq, ck), 0) - lax.broadcasted_iota(jnp.int32, (tq, ck), 1)

    def mask_heads(s, ok):
        return jnp.concatenate([jnp.where(ok, s[h * tq:(h + 1) * tq], neg) for h in range(hpg)], axis=0)

    def attend(k_ref, v_ref, n, ok_fn):
        m_scr[...] = jnp.full(m_scr.shape, neg, F32)
        for r in range(n):
            c = i - r
            cc = jnp.maximum(c, 0)
            s = _dot(qs, k_ref[0, 0, 0, cc])
            if r < 2:
                s = s + bt_ref[0, r]
            s = mask_heads(s, ok_fn(r, c, cc))
            s_scr[:, r * ck:(r + 1) * ck] = s
            m_scr[...] = jnp.maximum(m_scr[...], s)
        m = jnp.max(m_scr[...], axis=1, keepdims=True)
        m = jnp.where(m > neg, m, 0.0)
        for r in range(n):
            p_scr[:, r * ck:(r + 1) * ck] = jnp.exp2(s_scr[:, r * ck:(r + 1) * ck] - m).astype(BF16)
        vt = jnp.concatenate([v_ref[0, 0, 0, jnp.maximum(i - r, 0)] for r in range(n)], axis=1)
        vt = jnp.concatenate([vt, jnp.ones((ONES_ROWS, n * ck), BF16)], axis=0)
        acc = _dot_nt(p_scr[:, 0:n * ck], vt)
        return acc[:, 0:d] / jnp.maximum(acc[:, d:d + 1], 1e-30)

    nw = WINDOW // ck + 1

    def win_ok(r, c, cc):
        if r == 0:
            return tk >= 0
        if r == nw - 1:
            return tk < jnp.where(c >= 0, 0, -tq)
        return tk > jnp.where(c >= 0, -tq, tq)

    ow_scr[...] = attend(kw_ref, vw_ref, nw, win_ok)

    t_lane = i * tq + (lax.broadcasted_iota(jnp.int32, (1, hpg * tq), 1) & (tq - 1))
    s_c = _dot_nt(kc_ref[0, 0], qs) + bc_ref[0, 0]
    r_c = lax.broadcasted_iota(jnp.int32, (2 * nsel, hpg * tq), 0)
    n_c = jnp.where(r_c < nsel, 2 * r_c, 2 * (r_c - nsel) + 1)
    s_c = jnp.where(t_lane >= (n_c + 1) * CMP_BLOCK - 1, s_c, neg)
    m_c = jnp.max(s_c, axis=0, keepdims=True)
    m_c = jnp.where(m_c > neg, m_c, 0.0)
    p_c = jnp.exp2(s_c - m_c)
    p_c = p_c / jnp.maximum(jnp.sum(p_c, axis=0, keepdims=True), 1e-30)
    o_c = _dot_tn(p_c.astype(BF16), vc_ref[0, 0])

    ph = p_c[:, 0:tq]
    for h in range(1, hpg):
        ph = ph + p_c[:, h * tq:(h + 1) * tq]
    imp = ph[0:nsel] + ph[nsel:2 * nsel]
    blk = lax.broadcasted_iota(jnp.int32, (nsel, tq), 0)
    cur = (i * tq + lax.broadcasted_iota(jnp.int32, (nsel, tq), 1)) >> int(math.log2(SEL_BLOCK))
    forced = (blk == 0) | (blk == cur) | (blk == cur - 1)
    imp = jnp.where(forced, FORCE_SCORE, jnp.where(blk <= cur, imp, neg))
    rank = jnp.zeros((nsel, tq), jnp.int32)
    for s2 in range(nsel):
        row = imp[s2:s2 + 1]
        beats = (row > imp) | ((row == imp) & (blk > s2))
        rank = rank + beats.astype(jnp.int32)
    chosen = (rank < N_SELECT).astype(BF16)
    selk = _dot_tn(chosen, e_ref[...])
    for c in range(nchunk):
        selk_scr[c] = selk[:, c * ck:(c + 1) * ck]

    def sel_ok(r, c, cc):
        ok = selk_scr[cc] > jnp.where(c >= 0, 0.5, 2.0)
        return ok & (tk >= 0) if r == 0 else ok

    sizes = list(range(NSA_CLASS, nchunk, NSA_CLASS)) + [nchunk]
    for lo, n in zip([0] + sizes[:-1], sizes):
        @pl.when((i >= lo) & (i < n))
        def _(n=n):
            os_scr[...] = attend(ks_ref, vs_ref, n, sel_ok)

    o_w = ow_scr[...]
    o_s = os_scr[...]

    sig = jax.nn.sigmoid(gt_ref[0, 0])
    outs = []
    for h in range(hpg):
        r = slice(h * tq, (h + 1) * tq)
        outs.append(sig[:, h:h + 1] * o_c[r] + sig[:, hpg + h:hpg + h + 1] * o_s[r]
                    + sig[:, 2 * hpg + h:2 * hpg + h + 1] * o_w[r])
    o_ref[0] = jnp.concatenate(outs, axis=1)


def _nsa_prompt(pr3, kc, vc, bias_c, kvt, bias_t, expand, gts):
    b, t = pr3.shape[:2]
    g, d, tq, hpg = NSA_KV_HEADS, NSA_HEAD_DIM, NSA_TQ, NSA_HPG
    nt = t // tq
    nc = kc.shape[2]
    rows = hpg * tq
    per_bg = lambda r, c: pl.BlockSpec((1, 1, r, c), lambda bi, gi, i: (bi, gi, 0, 0))
    kv = lambda br, z: pl.BlockSpec((None, 1, 1, 1, nt, d, tq), lambda bi, gi, i: (br, bi, z, gi, 0, 0, 0))
    return pl.pallas_call(
        _nsa_prompt_kernel,
        grid=(b, g, nt),
        in_specs=[pl.BlockSpec((1, tq, GROUP_W), lambda bi, gi, i: (bi, i, COL_Q // GROUP_W + gi)),
                  per_bg(nc, d), per_bg(nc, d),
                  pl.BlockSpec((1, 1, nc, rows), lambda bi, gi, i: (gi, i, 0, 0)),
                  kv(0, 0), kv(0, 1), kv(1, 0), kv(1, 1),
                  pl.BlockSpec((1, 2, rows, tq), lambda bi, gi, i: (gi, 0, 0, 0)),
                  pl.BlockSpec((nc // 2, t), lambda bi, gi, i: (0, 0)),
                  pl.BlockSpec((1, 1, tq, 128), lambda bi, gi, i: (bi, gi, i, 0))],
        out_specs=pl.BlockSpec((1, tq, GROUP_W), lambda bi, gi, i: (bi, i, gi)),
        out_shape=jax.ShapeDtypeStruct((b, t, Q_W), F32),
        scratch_shapes=[pltpu.VMEM((rows, t), F32), pltpu.VMEM((rows, t), BF16),
                        pltpu.VMEM((rows, tq), F32),
                        pltpu.VMEM((nt, tq, tq), F32), pltpu.VMEM((rows, d), F32), pltpu.VMEM((rows, d), F32)],
        compiler_params=_cparams(("parallel", "parallel", "arbitrary")),
        name="nsa_prompt",
    )(pr3, kc, vc, bias_c, kvt, kvt, kvt, kvt, bias_t, expand, gts)


SMP_CPAD = 128
KEY_TILE = PAGE_SIZE


def _masked_softmax(s, ok):
    s = jnp.where(ok, s, -jnp.inf)
    m = jnp.max(s, axis=-1, keepdims=True)
    m = jnp.where(m > -jnp.inf, m, 0.0)
    p = jnp.exp(s - m)
    return p / jnp.maximum(jnp.sum(p, axis=-1, keepdims=True), 1e-30)


def _nsa_sample_kernel(pt_ref, q_ref, kc_ref, vc_ref, bc_ref, okc_ref, cur_ref, *rest, nb, n_pages, nsel):
    pages = rest[:nb * n_pages]
    (sn_ref, bs_ref, oks_ref, e_ref, wb_ref, wn_ref, bw_ref, okw_ref, gt_ref, acc_ref,
     o_ref, wo_ref) = rest[nb * n_pages:]
    del pt_ref, acc_ref
    d, hpg = NSA_HEAD_DIM, NSA_HPG
    rows = q_ref.shape[2]
    s_new = rows // hpg
    chains = [(n, g) for n in range(nb) for g in range(NSA_KV_HEADS)]
    bf = lambda x: x.astype(BF16)
    stack = lambda parts: jnp.concatenate(parts, axis=0)
    tile = lambda x: stack([x] * len(chains))
    part = lambda x, c: x[c * rows:(c + 1) * rows]
    qs = [bf(q_ref[n, g] * (d ** -0.5)) for n, g in chains]

    s_c = stack([_dot_nt(q, bf(kc_ref[n, g])) + bc_ref[g] for q, (n, g) in zip(qs, chains)])
    p_c = _masked_softmax(s_c, tile(okc_ref[...]) > 0.5)
    o_c = stack([_dot(bf(part(p_c, c)), bf(vc_ref[n, g])) for c, (n, g) in enumerate(chains)])

    ph = []
    for c in range(len(chains)):
        acc = p_c[c * rows:c * rows + s_new]
        for h in range(1, hpg):
            acc = acc + p_c[c * rows + h * s_new:c * rows + (h + 1) * s_new]
        ph.append(acc)
    ph = stack(ph)
    imp = ph[:, 0:SMP_CPAD] + ph[:, SMP_CPAD:2 * SMP_CPAD]
    blk = lax.broadcasted_iota(jnp.int32, imp.shape, 1)
    cur = tile(cur_ref[...])
    forced = (blk == 0) | (blk == cur) | (blk == cur - 1)
    imp = jnp.where(forced, FORCE_SCORE, jnp.where(blk <= cur, imp, -jnp.inf))
    rank = jnp.zeros(imp.shape, jnp.int32)
    for s2 in range(nsel):
        col = imp[:, s2:s2 + 1]
        beats = (col > imp) | ((col == imp) & (blk > s2))
        rank = rank + beats.astype(jnp.int32)
    chosen = bf((rank < N_SELECT) & (blk < nsel))
    chosen = stack([chosen[c * s_new:(c + 1) * s_new] for c in range(len(chains)) for _ in range(hpg)])
    sel_keys = _dot(chosen, e_ref[...])

    def kv_tiles(n, g, z):
        return [pg[0, 0, z, g] for pg in pages[n * n_pages:(n + 1) * n_pages]] + [sn_ref[n, z, g]]

    s_s = stack([jnp.concatenate([_dot(q, bf(kt)) for kt in kv_tiles(n, g, 0)], axis=1) + bs_ref[g]
                 for q, (n, g) in zip(qs, chains)])
    p_s = bf(_masked_softmax(s_s, (tile(oks_ref[...]) > 0.5) & (sel_keys > 0.5)))
    o_s = []
    for c, (n, g) in enumerate(chains):
        pc = part(p_s, c)
        acc = None
        for p, vt in enumerate(kv_tiles(n, g, 1)):
            term = _dot_nt(pc[:, p * KEY_TILE:(p + 1) * KEY_TILE], bf(vt))
            acc = term if acc is None else acc + term
        o_s.append(acc)
    o_s = stack(o_s)

    wkeys = wb_ref.shape[-1]
    s_w = stack([jnp.concatenate([_dot(q, bf(wb_ref[0, n, 0, g])), _dot(q, bf(wn_ref[n, 0, g]))], axis=1)
                 + bw_ref[g] for q, (n, g) in zip(qs, chains)])
    p_w = bf(_masked_softmax(s_w, tile(okw_ref[...]) > 0.5))
    o_w = stack([_dot_nt(part(p_w, c)[:, 0:wkeys], bf(wb_ref[0, n, 1, g]))
                 + _dot_nt(part(p_w, c)[:, wkeys:wkeys + KEY_TILE], bf(wn_ref[n, 1, g]))
                 for c, (n, g) in enumerate(chains)])

    gt = jax.nn.sigmoid(stack([gt_ref[n, g] for n, g in chains]))
    out = gt[:, 0:1] * o_c + gt[:, 1:2] * o_s + gt[:, 2:3] * o_w
    for c, (n, g) in enumerate(chains):
        o_ref[n, g] = part(out, c)

    lane = lax.broadcasted_iota(jnp.int32, (2 * NSA_KV_HEADS * d, KEY_TILE), 1)
    for n in range(nb):
        old = pltpu.roll(wb_ref[0, n].reshape(-1, wkeys), wkeys - s_new, 1)
        new = pltpu.roll(wn_ref[n].reshape(-1, KEY_TILE), KEY_TILE - s_new, 1)
        tail = jnp.where(lane >= KEY_TILE - s_new, new, old[:, wkeys - KEY_TILE:])
        wo_ref[0, n] = jnp.concatenate([old[:, 0:wkeys - KEY_TILE], tail], axis=1).reshape(wo_ref.shape[2:])


SMP_SEQS = 4


def _nsa_sample(layer, nsel, page_table, qg, kc, vc, bias_c, ok_c, cur, cache_sel, sel_new, bias_s, ok_s,
                expand, win_state, win_new, bias_w, ok_w, gts, win_acc):
    b = qg.shape[0]
    g = NSA_KV_HEADS
    d = NSA_HEAD_DIM
    rows = qg.shape[2]
    n_pages = page_table.shape[1]
    wkeys = win_state.shape[-1]
    ks = (n_pages + 1) * KEY_TILE
    kw = wkeys + KEY_TILE
    nb = SMP_SEQS if b % SMP_SEQS == 0 else 1
    per_b = lambda *shape: pl.BlockSpec((nb,) + shape, lambda i, pt: (i,) + (0,) * len(shape))
    const = lambda *shape: pl.BlockSpec(shape, lambda i, pt: (0,) * len(shape))

    def page_spec(n, p):
        return pl.BlockSpec((1, 1, 2, g, d, PAGE_SIZE), lambda i, pt: (layer, pt[nb * i + n, p], 0, 0, 0, 0))

    win_spec = pl.BlockSpec((1, nb, 2, g, d, wkeys), lambda i, pt: (layer, i, 0, 0, 0, 0))
    in_specs = ([per_b(g, rows, d), per_b(g, 2 * SMP_CPAD, d), per_b(g, 2 * SMP_CPAD, d),
                 const(g, rows, 2 * SMP_CPAD), const(rows, 2 * SMP_CPAD), const(rows // NSA_HPG, SMP_CPAD)]
                + [page_spec(n, p) for n in range(nb) for p in range(n_pages)]
                + [per_b(2, g, d, KEY_TILE), const(g, rows, ks), const(rows, ks), const(SMP_CPAD, ks),
                   win_spec, per_b(2, g, d, KEY_TILE), const(g, rows, kw), const(rows, kw), per_b(g, rows, 128),
                   pl.BlockSpec(memory_space=pl.ANY)])
    operands = (page_table, qg, kc, vc, bias_c, ok_c, cur, *([cache_sel] * (nb * n_pages)), sel_new, bias_s, ok_s,
                expand, win_state, win_new, bias_w, ok_w, gts, win_acc)
    return pl.pallas_call(
        functools.partial(_nsa_sample_kernel, nb=nb, n_pages=n_pages, nsel=nsel),
        grid_spec=pltpu.PrefetchScalarGridSpec(
            num_scalar_prefetch=1,
            grid=(b // nb,),
            in_specs=in_specs,
            out_specs=[pl.BlockSpec((nb, g, rows, d), lambda i, pt: (i, 0, 0, 0)), win_spec]),
        out_shape=[jax.ShapeDtypeStruct((b, g, rows, d), F32), jax.ShapeDtypeStruct(win_acc.shape, F32)],
        input_output_aliases={len(operands) - 1: 1},
        compiler_params=_cparams(("parallel",)),
        name="nsa_sample",
    )(*operands)


def _pad_rows(a, rows):
    return jnp.pad(a, ((0, 0), (0, rows - a.shape[1]), (0, 0)))


def _even_odd(n):
    return np.concatenate([np.arange(0, n, 2), np.arange(1, n, 2)])


def _prompt_bias_tables(rel_bias, t):
    g, hpg, tq = NSA_KV_HEADS, NSA_HPG, NSA_TQ
    nt = t // tq
    nc = t // CMP_BLOCK
    end_c = (_even_odd(nc) + 1) * CMP_BLOCK - 1
    dist_c = jnp.asarray(np.arange(t)[None, :] - end_c[:, None], jnp.int32)
    bias_c = _bias_lookup(rel_bias, dist_c, nc)
    bias_c = bias_c.reshape(g, hpg, nc, nt, tq).transpose(0, 3, 2, 1, 4).reshape(g, nt, nc, hpg * tq) * LOG2E
    off = np.arange(3)[:, None, None] * tq
    dist_t = off + np.arange(tq)[None, :, None] - np.arange(tq)[None, None, :]
    assert dist_t[2].min() >= REL_MAX_DIST
    bias_t = _bias_lookup(rel_bias, jnp.asarray(dist_t.reshape(3 * tq, tq), jnp.int32), 3 * tq)
    bias_t = bias_t.reshape(g, hpg, 3, tq, tq).transpose(0, 2, 1, 3, 4).reshape(g, 3, hpg * tq, tq)
    bias_t = (bias_t[:, 0:2] - bias_t[:, 2:3]) * LOG2E
    expand = np.arange(t)[None, :] // SEL_BLOCK == np.arange(t // SEL_BLOCK)[:, None]
    return bias_c, bias_t, jnp.asarray(expand, BF16)


def _sample_tables(rel_bias, past_len, s_new, n_win_keys):
    g, hpg = NSA_KV_HEADS, NSA_HPG
    tk = -(-(past_len + s_new) // SEL_BLOCK) * SEL_BLOCK
    nc = tk // CMP_BLOCK
    nsel = tk // SEL_BLOCK
    half = (nc + 1) // 2
    pos_q = past_len + np.arange(s_new)
    lane = np.arange(SMP_CPAD)
    n_of_lane = np.concatenate([2 * lane, 2 * lane + 1])
    real_c = np.concatenate([lane < half, lane < nc - half])
    dist_c = pos_q[:, None] - ((n_of_lane[None, :] + 1) * CMP_BLOCK - 1)
    ok_c = real_c[None, :] & (dist_c >= 0)
    n_keys_s = (past_len // PAGE_SIZE + 1) * KEY_TILE
    key = np.arange(n_keys_s)
    dist_s = pos_q[:, None] - key[None, :]
    ok_s = (key[None, :] < past_len + s_new) & (dist_s >= 0)
    n_keys_w = n_win_keys + KEY_TILE
    i = np.arange(n_keys_w)
    pos_kw = np.where(i < n_win_keys, past_len - n_win_keys + i, past_len + i - n_win_keys)
    dist_w = pos_q[:, None] - pos_kw[None, :]
    ok_w = (i[None, :] < n_win_keys + s_new) & (dist_w >= 0) & (dist_w < WINDOW) & (pos_kw[None, :] >= 0)
    dist = np.concatenate([dist_c, dist_s, dist_w], axis=1)
    dist = np.pad(dist, ((0, 8 - s_new), (0, 0)))
    bias = _bias_lookup(rel_bias, jnp.asarray(dist, jnp.int32), 8)[:, :s_new]
    bias = bias.reshape(g, hpg * s_new, dist.shape[1])
    c0, c1 = 2 * SMP_CPAD, 2 * SMP_CPAD + n_keys_s
    tile = lambda m: jnp.asarray(np.tile(m, (hpg, 1)), F32)
    expand = (key[None, :] // SEL_BLOCK == np.arange(SMP_CPAD)[:, None]) & (np.arange(SMP_CPAD)[:, None] < nsel)
    cur = np.broadcast_to((pos_q // SEL_BLOCK)[:, None], (s_new, SMP_CPAD))
    return dict(bias_c=bias[:, :, :c0], bias_s=bias[:, :, c0:c1], bias_w=bias[:, :, c1:],
                ok_c=tile(ok_c), ok_s=tile(ok_s), ok_w=tile(ok_w),
                expand=jnp.asarray(expand, BF16), cur=jnp.asarray(cur, jnp.int32), nc=nc, half=half, tk=tk)


def _flat_blocks(rows):
    b, tk = rows.shape[:2]
    nc = tk // CMP_BLOCK
    blk = rows.reshape(b, nc // 2, 2, CMP_BLOCK, 2, NSA_KV_HEADS, NSA_HEAD_DIM)
    return jnp.transpose(blk, (4, 0, 5, 2, 1, 3, 6)).reshape(2, b * NSA_KV_HEADS * nc, CMP_BLOCK * NSA_HEAD_DIM)


def _nsa_prompt_layer(pr3, lw, tabs):
    b, t = pr3.shape[:2]
    g, hpg, d, tq = NSA_KV_HEADS, NSA_HPG, NSA_HEAD_DIM, NSA_TQ
    nt = t // tq
    nc = t // CMP_BLOCK
    kv = pr3[:, :, COL_CMP:COL_CMP + 3 * KV_W].reshape(b, t, 3, 2, g, d)
    cmp_rows, sel_rows, win_rows = kv[:, :, 0], kv[:, :, 1], kv[:, :, 2]
    kcv = _compress(_flat_blocks(cmp_rows), lw['pe'], lw['cw1'], lw['cw2'], 512)
    kcv = kcv.reshape(2, b, g, nc, d).astype(BF16)
    kvt = kv[:, :, 1:].reshape(b, nt, tq, 2, 2, g, d).transpose(3, 0, 4, 5, 1, 6, 2).astype(BF16)
    gts = pr3[:, :, COL_MISC:COL_MISC + N_GATE].reshape(b, t, 3, g, hpg).transpose(0, 3, 1, 2, 4)
    gts = jnp.pad(gts.reshape(b, g, t, 3 * hpg), ((0, 0), (0, 0), (0, 0), (0, 128 - 3 * hpg)))
    o = _nsa_prompt(pr3, kcv[0], kcv[1], tabs[0], kvt, tabs[1], tabs[2], gts)
    return o.reshape(b * t, Q_W), cmp_rows, sel_rows, win_rows[:, t - min(WINDOW, t):]


def _to_native(rows, keys):
    b, n = rows.shape[:2]
    r = rows.reshape(b, n, 2, NSA_KV_HEADS, NSA_HEAD_DIM).transpose(0, 2, 3, 4, 1)
    return jnp.pad(r, ((0, 0),) * 4 + ((0, keys - n),))


def _nsa_sample_layer(pr3, lw, tabs, layer, cache_cmp, cache_sel, win_state, page_table, win_acc):
    b, s_new = pr3.shape[:2]
    g, hpg, d = NSA_KV_HEADS, NSA_HPG, NSA_HEAD_DIM
    n_pages = page_table.shape[1]
    past_len = n_pages * PAGE_SIZE
    new = lambda c: pr3[:, :, c:c + KV_W]
    cmp_new, sel_new, win_new = new(COL_CMP), new(COL_SEL), new(COL_WIN)
    n_past, n_tail = past_len // CMP_BLOCK, tabs['nc'] - past_len // CMP_BLOCK
    pe3 = lw['pe'].reshape(2, CMP_BLOCK, d)
    past = _compress_pages(layer, page_table, cache_cmp, pe3, lw['cw1'].reshape(2, CMP_BLOCK, d, d), lw['cw2'])
    tail = jnp.pad(cmp_new, ((0, 0), (0, n_tail * CMP_BLOCK - s_new), (0, 0)))
    tail = _compress(_flat_blocks(tail.reshape(b, n_tail * CMP_BLOCK, 2, g, d)), lw['pe'], lw['cw1'], lw['cw2'], 512)
    tail = tail.reshape(2, b, g, 2, n_tail // 2, d).transpose(1, 0, 2, 3, 4, 5)
    kcv = jnp.concatenate([past, tail], axis=4)
    kcv = jnp.pad(kcv, ((0, 0),) * 4 + ((0, SMP_CPAD - kcv.shape[4]), (0, 0)))
    kcv = kcv.reshape(b, 2, g, 2 * SMP_CPAD, d).transpose(1, 0, 2, 3, 4)
    qg = pr3[:, :, COL_Q:COL_Q + Q_W].reshape(b, s_new, g, hpg, d).transpose(0, 2, 3, 1, 4)
    qg = qg.reshape(b, g, hpg * s_new, d)
    gts = pr3[:, :, COL_MISC:COL_MISC + N_GATE].reshape(b, s_new, 3, g, hpg).transpose(0, 3, 4, 1, 2)
    gts = jnp.pad(gts.reshape(b, g, hpg * s_new, 3), ((0, 0), (0, 0), (0, 0), (0, 128 - 3)))
    o, win_acc = _nsa_sample(layer, tabs['tk'] // SEL_BLOCK, page_table, qg, kcv[0], kcv[1], tabs['bias_c'], tabs['ok_c'],
                    tabs['cur'], cache_sel, _to_native(sel_new, KEY_TILE), tabs['bias_s'], tabs['ok_s'],
                    tabs['expand'], win_state, _to_native(win_new, KEY_TILE), tabs['bias_w'], tabs['ok_w'], gts,
                    win_acc)
    o = o.reshape(b, g, hpg, s_new, d).transpose(0, 3, 1, 2, 4).reshape(b * s_new, Q_W)
    kv5 = lambda a: a.reshape(b, -1, 2, g, d)
    return o, kv5(cmp_new), kv5(sel_new), win_acc


def _mixers(pr, b, t, lw, conv0, h0, s0, nsa_fn):
    pr3 = pr.reshape(b, t, PROJ_COLS)
    o_a, cmp_rows, sel_rows, win_rows = nsa_fn(pr3)
    lru_w = (lw['lcw'], lw['lcb'], lw['lgw'], lw['lgb'], lw['lam'])
    gla_w = (lw['gaw'], lw['gab'], lw['gng'])
    if t % 8 == 0:
        tc = min(t, 256)
        o_b, conv_n, h_n = _lru(pr3, COL_LRU_X // LRU_WIDTH, pr3, COL_LRU_G // LRU_WIDTH, conv0, h0[:, None],
                                *lru_w, tc, tc)
        tg = min(t, 256)
        o_c, s_n = _gla(pr3, COL_GLA_Q // GLA_QK_W, pr3, COL_GLA_K // GLA_QK_W, pr3, COL_GLA_V // GLA_V_W,
                         pr3, COL_GLA_G // GLA_V_W, pr3, COL_MISC // MISC_W, *gla_w, s0, tg,
                         min(tg, GLA_CHUNK), tg)
    else:
        tp = -(-t // 8) * 8
        cut = lambda c, w: _pad_rows(pr3[:, :, c:c + w], tp)
        if CONV_WIDTH - 1 + t <= LRU_SEG:
            o_b, conv_n, h_n = _lru_short(pr3[:, :, COL_LRU_X:COL_LRU_X + LRU_WIDTH],
                                          pr3[:, :, COL_LRU_G:COL_LRU_G + LRU_WIDTH], conv0, h0, *lru_w)
            h_n = h_n[:, None]
        else:
            o_b, conv_n, h_n = _lru(cut(COL_LRU_X, LRU_WIDTH), 0, cut(COL_LRU_G, LRU_WIDTH), 0, conv0,
                                    h0[:, None], *lru_w, tp, t)
        o_c, s_n = _gla(cut(COL_GLA_Q, GLA_QK_W), 0, cut(COL_GLA_K, GLA_QK_W), 0, cut(COL_GLA_V, GLA_V_W), 0,
                         cut(COL_GLA_G, GLA_V_W), 0, cut(COL_MISC, MISC_W), 0, *gla_w, s0, tp, tp, t)
        o_b, o_c = o_b[:, :t], o_c[:, :t]
    o_b = o_b.reshape(b * t, LRU_WIDTH)
    o_c = o_c.reshape(b * t, GLA_V_W)
    return (o_a, o_b, o_c), (cmp_rows, sel_rows, win_rows, conv_n, h_n[:, 0], s_n)


def _layer(x, b, t, lw, conv0, h0, s0, nsa_fn, tm):
    pr = _proj(x, lw['ng'], lw['w_in'], lw['b_in'], _row_tile(x.shape[0], 2 * tm), PROJ_COLS // 11)
    (o_a, o_b, o_c), states = _mixers(pr, b, t, lw, conv0, h0, s0, nsa_fn)
    x = _merge(x, o_a, o_b, o_c, pr, lw['wb'], lw['wo'], min(tm, 512))
    x = _mlp(x, lw['mg'], lw['w1'], lw['w2'], tm, 1024)
    return x, states


def kernel(x_prompt, x_sample, cache_nsa_cmp_kv, cache_nsa_sel_kv, state_nsa_win_kv, state_lru_conv,
           state_lru_h, state_gla, page_table, rel_bias, norm_mix_g, norm_mlp_g, norm_final_g, w_in, b_in,
           nsa_cmp_pe, nsa_cmp_w1, nsa_cmp_w2, lru_gate_w, lru_gate_b, lru_lambda, lru_conv_w, lru_conv_b,
           gla_alpha_w, gla_alpha_b, gla_norm_g, w_branch, w_out, mlp_w1, mlp_w2):
    bp, tp = x_prompt.shape[:2]
    bs, ts = x_sample.shape[:2]
    depth = w_in.shape[0]
    n_pages = page_table.shape[1]
    w_buf = state_nsa_win_kv.shape[2]

    cols = [w_in[..., _SRC[n][0]:_SRC[n][0] + _SRC[n][1]] for n in _DST_ORDER]
    pad = PROJ_COLS - sum(c.shape[-1] for c in cols)
    w_in_p = jnp.concatenate(cols + [jnp.zeros(w_in.shape[:2] + (pad,), w_in.dtype)], axis=-1).astype(BF16)
    bcols = [b_in[..., _SRC[n][0]:_SRC[n][0] + _SRC[n][1]] for n in _DST_ORDER]
    b_in_p = jnp.concatenate(bcols + [jnp.zeros((depth, pad), b_in.dtype)], axis=-1)[:, None, :]
    eye = jnp.eye(LRU_BLOCKS, dtype=lru_gate_w.dtype)
    lgw = jnp.einsum('lznce,nm->lzncme', lru_gate_w, eye).reshape(depth, 2, LRU_WIDTH, LRU_WIDTH).astype(BF16)
    gaw = jnp.zeros((depth, MISC_W, GLA_QK_W), F32).at[:, N_GATE:N_GATE + GLA_RANK].set(gla_alpha_w).astype(BF16)
    pe = jnp.transpose(nsa_cmp_pe, (0, 2, 1, 3)).reshape(depth, 2, 1, CMP_BLOCK * NSA_HEAD_DIM)
    cw1 = nsa_cmp_w1.astype(BF16)
    cw2 = nsa_cmp_w2.astype(BF16)
    wb = w_branch.astype(BF16)
    wo = w_out.astype(BF16)
    w1 = mlp_w1.astype(BF16)
    w2 = mlp_w2.astype(BF16)
    cache_cmp = jnp.transpose(cache_nsa_cmp_kv, (0, 1, 3, 4, 5, 2))
    cache_sel = jnp.transpose(cache_nsa_sel_kv, (0, 1, 3, 4, 5, 2))
    win_state = jnp.transpose(state_nsa_win_kv, (0, 1, 3, 4, 5, 2))

    tabs_p = _prompt_bias_tables(rel_bias, tp)
    tabs_s = _sample_tables(rel_bias, n_pages * PAGE_SIZE, ts, w_buf)

    xp = x_prompt.reshape(bp * tp, D_MODEL)
    xs = x_sample.reshape(bs * ts, D_MODEL)
    conv0_p = jnp.zeros((bp, CONV_WIDTH - 1, LRU_WIDTH), F32)
    h0_p = jnp.zeros((bp, LRU_WIDTH), F32)
    s0_p = jnp.zeros((bp, GLA_HEADS, GLA_DK, GLA_DV), F32)
    outs_p = [[] for _ in range(6)]
    outs_s = [[] for _ in range(6)]
    win_acc = jnp.zeros(win_state.shape, F32)
    for l in range(depth):
        lw = dict(ng=norm_mix_g[l][None], mg=norm_mlp_g[l][None], w_in=w_in_p[l], b_in=b_in_p[l],
                  pe=pe[l], cw1=cw1[l], cw2=cw2[l], lcw=lru_conv_w[l], lcb=lru_conv_b[l][None], lgw=lgw[l],
                  lgb=lru_gate_b[l], lam=lru_lambda[l][None], gaw=gaw[l], gab=gla_alpha_b[l][None],
                  gng=gla_norm_g[l][None], wb=wb[l], wo=wo[l], w1=w1[l], w2=w2[l])
        xp, st_p = _layer(xp, bp, tp, lw, conv0_p, h0_p, s0_p,
                          functools.partial(_nsa_prompt_layer, lw=lw, tabs=tabs_p), _row_tile(bp * tp, 1024))
        nsa_s = functools.partial(_nsa_sample_layer, lw=lw, tabs=tabs_s, layer=l, cache_cmp=cache_cmp,
                                  cache_sel=cache_sel, win_state=win_state, page_table=page_table, win_acc=win_acc)
        xs, st_s = _layer(xs, bs, ts, lw, state_lru_conv[l], state_lru_h[l], state_gla[l], nsa_s, bs * ts)
        win_acc = st_s[2]
        for j in range(6):
            outs_p[j].append(st_p[j])
            outs_s[j].append(st_s[j])
    y_prompt = _final_norm(xp, norm_final_g[None], _row_tile(bp * tp, 1024)).reshape(bp, tp, D_MODEL)
    y_sample = _final_norm(xs, norm_final_g[None], bs * ts).reshape(bs, ts, D_MODEL)
    st = lambda outs, j: jnp.stack(outs[j])
    win_s = win_acc.transpose(0, 1, 5, 2, 3, 4)
    return (y_prompt, y_sample, st(outs_p, 0), st(outs_s, 0), st(outs_p, 1), st(outs_s, 1),
            st(outs_p, 2), win_s, st(outs_p, 3), st(outs_s, 3), st(outs_p, 4), st(outs_s, 4),
            st(outs_p, 5), st(outs_s, 5))
```

```python
import functools
import math

import jax
import jax.numpy as jnp
import numpy as np
from jax import lax
from jax.experimental import pallas as pl
from jax.experimental.pallas import tpu as pltpu

F32 = jnp.float32
BF16 = jnp.bfloat16

D_MODEL = 1024
DEPTH = 4
PAGE_SIZE = 128
NSA_HEADS = 8
NSA_KV_HEADS = 2
NSA_HPG = NSA_HEADS // NSA_KV_HEADS
NSA_HEAD_DIM = 64
CMP_BLOCK = 32
SEL_BLOCK = 64
N_SELECT = 16
WINDOW = 512
FORCE_SCORE = 1e4
REL_BUCKETS = 32
REL_MAX_DIST = 128
LRU_WIDTH = 512
LRU_BLOCKS = 8
LRU_BLOCK_DIM = LRU_WIDTH // LRU_BLOCKS
CONV_WIDTH = 4
LRU_C = 8.0
GLA_HEADS = 4
GLA_DK = 64
GLA_DV = 128
GLA_RANK = 16
GLA_TAU = 16.0
GLA_CHUNK = 64
D_FF = 4 * D_MODEL
N_BRANCH = 3
BRANCH_WIDTH = 512
NORM_EPS = 1e-6

KV_W = 2 * NSA_KV_HEADS * NSA_HEAD_DIM
Q_W = NSA_HEADS * NSA_HEAD_DIM
GROUP_W = NSA_HPG * NSA_HEAD_DIM
GLA_QK_W = GLA_HEADS * GLA_DK
GLA_V_W = GLA_HEADS * GLA_DV
MISC_W = 128
N_GATE = 3 * NSA_HEADS

COL_MERGE = 0
COL_Q = COL_MERGE + N_BRANCH * D_MODEL
COL_LRU_X = COL_Q + Q_W
COL_LRU_G = COL_LRU_X + LRU_WIDTH
COL_GLA_V = COL_LRU_G + LRU_WIDTH
COL_GLA_G = COL_GLA_V + GLA_V_W
COL_CMP = COL_GLA_G + GLA_V_W
COL_SEL = COL_CMP + KV_W
COL_WIN = COL_SEL + KV_W
COL_GLA_Q = COL_WIN + KV_W
COL_GLA_K = COL_GLA_Q + GLA_QK_W
COL_MISC = COL_GLA_K + GLA_QK_W
PROJ_COLS = COL_MISC + MISC_W

_SRC = {}
_off = 0
for _name, _w in (('nsa_q', Q_W), ('nsa_cmp_kv', KV_W), ('nsa_sel_kv', KV_W), ('nsa_win_kv', KV_W),
                  ('nsa_gate', N_GATE), ('lru_x', LRU_WIDTH), ('lru_gate', LRU_WIDTH),
                  ('gla_q', GLA_QK_W), ('gla_k', GLA_QK_W), ('gla_v', GLA_V_W),
                  ('gla_alpha', GLA_RANK), ('gla_gate', GLA_V_W), ('merge_gate', N_BRANCH * D_MODEL)):
    _SRC[_name] = (_off, _w)
    _off += _w
_DST_ORDER = ('merge_gate', 'nsa_q', 'lru_x', 'lru_gate', 'gla_v', 'gla_gate', 'nsa_cmp_kv',
              'nsa_sel_kv', 'nsa_win_kv', 'gla_q', 'gla_k', 'nsa_gate', 'gla_alpha')

NSA_TQ = 128
NSA_LANES = NSA_HPG * NSA_TQ
NSA_CLASS = 2
LOG2E = math.log2(math.e)
ONES_ROWS = 16
VMEM_LIMIT = 56 * 1024 * 1024


def _cparams(sem):
    return pltpu.CompilerParams(dimension_semantics=sem, vmem_limit_bytes=VMEM_LIMIT)


def _gelu(x):
    return x * (0.5 * (1.0 + jnp.tanh(math.sqrt(2.0 / math.pi) * (x + 0.044715 * (x * x * x)))))


def _softplus(x):
    return jnp.maximum(x, 0.0) + jnp.log1p(jnp.exp(-jnp.abs(x)))


def _rms(x, g):
    return x * lax.rsqrt(jnp.mean(x * x, axis=-1, keepdims=True) + NORM_EPS) * g


def _dot(a, b):
    return jnp.dot(a, b, preferred_element_type=F32)


def _dot_nt(a, b):
    return lax.dot_general(a, b, (((1,), (1,)), ((), ())), preferred_element_type=F32)


def _dot_tn(a, b):
    return lax.dot_general(a, b, (((0,), (0,)), ((), ())), preferred_element_type=F32)


def _proj_kernel(x_ref, g_ref, w_ref, b_ref, o_ref, h_ref):
    @pl.when(pl.program_id(1) == 0)
    def _():
        h_ref[...] = _rms(x_ref[...], g_ref[...]).astype(BF16)

    o_ref[...] = _dot(h_ref[...], w_ref[...]) + b_ref[...]


def _proj(x, g, w, b, tm, tn):
    n = x.shape[0]
    return pl.pallas_call(
        _proj_kernel,
        grid=(n // tm, PROJ_COLS // tn),
        in_specs=[pl.BlockSpec((tm, D_MODEL), lambda i, j: (i, 0)),
                  pl.BlockSpec((1, D_MODEL), lambda i, j: (0, 0)),
                  pl.BlockSpec((D_MODEL, tn), lambda i, j: (0, j)),
                  pl.BlockSpec((1, tn), lambda i, j: (0, j))],
        out_specs=pl.BlockSpec((tm, tn), lambda i, j: (i, j)),
        out_shape=jax.ShapeDtypeStruct((n, PROJ_COLS), F32),
        scratch_shapes=[pltpu.VMEM((tm, D_MODEL), BF16)],
        compiler_params=_cparams(("parallel", "arbitrary")),
        name="proj",
    )(x, g, w, b)


def _merge_kernel(x_ref, oa_ref, ob_ref, oc_ref, g0_ref, g1_ref, g2_ref, wb_ref, wo_ref, o_ref):
    m = jax.nn.sigmoid(g0_ref[...]) * _dot(oa_ref[...].astype(BF16), wb_ref[0])
    m = m + jax.nn.sigmoid(g1_ref[...]) * _dot(ob_ref[...].astype(BF16), wb_ref[1])
    m = m + jax.nn.sigmoid(g2_ref[...]) * _dot(oc_ref[...].astype(BF16), wb_ref[2])
    o_ref[...] = x_ref[...] + _dot(m.astype(BF16), wo_ref[...])


def _merge(x, oa, ob, oc, pr, wb, wo, tm):
    n = x.shape[0]
    row = lambda w: pl.BlockSpec((tm, w), lambda i: (i, 0))
    gate = lambda z: pl.BlockSpec((tm, D_MODEL), lambda i: (i, COL_MERGE // D_MODEL + z))
    return pl.pallas_call(
        _merge_kernel,
        grid=(n // tm,),
        in_specs=[row(D_MODEL), row(BRANCH_WIDTH), row(BRANCH_WIDTH), row(BRANCH_WIDTH),
                  gate(0), gate(1), gate(2),
                  pl.BlockSpec((N_BRANCH, BRANCH_WIDTH, D_MODEL), lambda i: (0, 0, 0)),
                  pl.BlockSpec((D_MODEL, D_MODEL), lambda i: (0, 0))],
        out_specs=row(D_MODEL),
        out_shape=jax.ShapeDtypeStruct((n, D_MODEL), F32),
        compiler_params=_cparams(("parallel",)),
        name="merge",
    )(x, oa, ob, oc, pr, pr, pr, wb, wo)


def _mlp_kernel(x_ref, g_ref, w1_ref, w2_ref, o_ref, h_ref, acc_ref):
    f = pl.program_id(1)

    @pl.when(f == 0)
    def _():
        h_ref[...] = _rms(x_ref[...], g_ref[...]).astype(BF16)
        acc_ref[...] = jnp.zeros_like(acc_ref)

    a = jnp.maximum(_dot(h_ref[...], w1_ref[...]), 0.0)
    acc_ref[...] += _dot((a * a).astype(BF16), w2_ref[...])

    @pl.when(f == pl.num_programs(1) - 1)
    def _():
        o_ref[...] = x_ref[...] + acc_ref[...]


def _mlp(x, g, w1, w2, tm, tf):
    n = x.shape[0]
    return pl.pallas_call(
        _mlp_kernel,
        grid=(n // tm, D_FF // tf),
        in_specs=[pl.BlockSpec((tm, D_MODEL), lambda i, f: (i, 0)),
                  pl.BlockSpec((1, D_MODEL), lambda i, f: (0, 0)),
                  pl.BlockSpec((D_MODEL, tf), lambda i, f: (0, f)),
                  pl.BlockSpec((tf, D_MODEL), lambda i, f: (f, 0))],
        out_specs=pl.BlockSpec((tm, D_MODEL), lambda i, f: (i, 0)),
        out_shape=jax.ShapeDtypeStruct((n, D_MODEL), F32),
        scratch_shapes=[pltpu.VMEM((tm, D_MODEL), BF16), pltpu.VMEM((tm, D_MODEL), F32)],
        compiler_params=_cparams(("parallel", "arbitrary")),
        name="mlp",
    )(x, g, w1, w2)


def _norm_kernel(x_ref, g_ref, o_ref):
    o_ref[...] = _rms(x_ref[...], g_ref[...])


def _final_norm(x, g, tm):
    n = x.shape[0]
    return pl.pallas_call(
        _norm_kernel,
        grid=(n // tm,),
        in_specs=[pl.BlockSpec((tm, D_MODEL), lambda i: (i, 0)),
                  pl.BlockSpec((1, D_MODEL), lambda i: (0, 0))],
        out_specs=pl.BlockSpec((tm, D_MODEL), lambda i: (i, 0)),
        out_shape=jax.ShapeDtypeStruct((n, D_MODEL), F32),
        compiler_params=_cparams(("parallel",)),
        name="final_norm",
    )(x, g)


_XB = 8


def _lru_kernel(x_ref, gb_ref, conv0_ref, h0_ref, cw_ref, cb_ref, gw_ref, gbias_ref, lam_ref,
                o_ref, convn_ref, hn_ref, xbuf, hcar, *, tc, tv):
    @pl.when(pl.program_id(1) == 0)
    def _():
        xbuf[0:_XB, :] = jnp.zeros((_XB, LRU_WIDTH), F32)
        xbuf[_XB - 3:_XB, :] = conv0_ref[0]
        hcar[...] = h0_ref[0]

    x = x_ref[0]
    xbuf[_XB:_XB + tc, :] = x
    w = cw_ref[...]
    xc = cb_ref[...] + xbuf[_XB - 3:_XB - 3 + tc, :] * w[0:1]
    xc = xc + xbuf[_XB - 2:_XB - 2 + tc, :] * w[1:2]
    xc = xc + xbuf[_XB - 1:_XB - 1 + tc, :] * w[2:3]
    xc = xc + x * w[3:4]
    tail = xbuf[_XB - 3 + tv:_XB + tv, :]
    convn_ref[0] = tail
    xbuf[_XB - 3:_XB, :] = tail

    xcb = xc.astype(BF16)
    r = jax.nn.sigmoid(_dot(xcb, gw_ref[0]) + gbias_ref[0:1])
    i = jax.nn.sigmoid(_dot(xcb, gw_ref[1]) + gbias_ref[1:2])
    log_a = (-LRU_C * r) * _softplus(-lam_ref[...])
    a = jnp.exp(log_a)
    b = jnp.sqrt(-jnp.tanh(log_a) * (a * a + 1.0)) * (i * xc)

    pos = lax.broadcasted_iota(jnp.int32, (tc, LRU_WIDTH), 0) & (LRU_SEG - 1)
    s = 1
    while s < LRU_SEG:
        m = pos >= s
        b = jnp.where(m, a * pltpu.roll(b, s, 0) + b, b)
        a = jnp.where(m, a * pltpu.roll(a, s, 0), a)
        s *= 2
    h_in = hcar[...]
    groups = []
    for j in range(tc // LRU_SEG):
        hj = a[j * LRU_SEG:(j + 1) * LRU_SEG] * h_in + b[j * LRU_SEG:(j + 1) * LRU_SEG]
        groups.append(hj)
        h_in = hj[LRU_SEG - 1:LRU_SEG]
    h = jnp.concatenate(groups, axis=0)
    hlast = h[tv - 1:tv]
    hcar[...] = hlast
    hn_ref[0] = hlast
    o_ref[0] = _gelu(gb_ref[0]) * h


def _lru(x_arr, x_blk, gb_arr, gb_blk, conv0, h0, cw, cb, gw, gbias, lam, tc, tv):
    b, t = x_arr.shape[:2]
    r = LRU_WIDTH
    const2 = lambda shape: pl.BlockSpec(shape, lambda i, c: (0, 0))
    return pl.pallas_call(
        functools.partial(_lru_kernel, tc=tc, tv=tv),
        grid=(b, t // tc),
        in_specs=[pl.BlockSpec((1, tc, r), lambda i, c: (i, c, x_blk)),
                  pl.BlockSpec((1, tc, r), lambda i, c: (i, c, gb_blk)),
                  pl.BlockSpec((1, CONV_WIDTH - 1, r), lambda i, c: (i, 0, 0)),
                  pl.BlockSpec((1, 1, r), lambda i, c: (i, 0, 0)),
                  const2((CONV_WIDTH, r)), const2((1, r)),
                  pl.BlockSpec((2, r, r), lambda i, c: (0, 0, 0)),
                  const2((2, r)), const2((1, r))],
        out_specs=[pl.BlockSpec((1, tc, r), lambda i, c: (i, c, 0)),
                   pl.BlockSpec((1, CONV_WIDTH - 1, r), lambda i, c: (i, 0, 0)),
                   pl.BlockSpec((1, 1, r), lambda i, c: (i, 0, 0))],
        out_shape=[jax.ShapeDtypeStruct((b, t, r), F32),
                   jax.ShapeDtypeStruct((b, CONV_WIDTH - 1, r), F32),
                   jax.ShapeDtypeStruct((b, 1, r), F32)],
        scratch_shapes=[pltpu.VMEM((_XB + tc, r), F32), pltpu.VMEM((1, r), F32)],
        compiler_params=_cparams(("parallel", "arbitrary")),
        name="rglru",
    )(x_arr, gb_arr, conv0, h0, cw, cb, gw, gbias, lam)


LRU_SEG = 8


def _lru_short_kernel(x_ref, gb_ref, h0_ref, cw_ref, cb_ref, gw_ref, gbias_ref, lam_ref, o_ref, h_ref, *, t):
    x = x_ref[...]
    pos = lax.broadcasted_iota(jnp.int32, x.shape, 0) & (LRU_SEG - 1)
    w = cw_ref[...]
    xc = cb_ref[...] + x * w[CONV_WIDTH - 1:CONV_WIDTH]
    for k in range(1, CONV_WIDTH):
        xc = xc + pltpu.roll(x, k, 0) * w[CONV_WIDTH - 1 - k:CONV_WIDTH - k]
    xcb = xc.astype(BF16)
    r = jax.nn.sigmoid(_dot(xcb, gw_ref[0]) + gbias_ref[0:1])
    i = jax.nn.sigmoid(_dot(xcb, gw_ref[1]) + gbias_ref[1:2])
    log_a = (-LRU_C * r) * _softplus(-lam_ref[...])
    a = jnp.exp(log_a)
    b = jnp.sqrt(-jnp.tanh(log_a) * (a * a + 1.0)) * (i * xc)
    real = (pos >= CONV_WIDTH - 1) & (pos < CONV_WIDTH - 1 + t)
    a = jnp.where(real, a, 1.0)
    b = jnp.where(real, b, 0.0)
    s = 1
    while s < LRU_SEG:
        m = pos >= s
        b = jnp.where(m, a * pltpu.roll(b, s, 0) + b, b)
        a = jnp.where(m, a * pltpu.roll(a, s, 0), a)
        s *= 2
    h = a * h0_ref[...] + b
    h_ref[...] = h
    o_ref[...] = _gelu(gb_ref[...]) * h


def _lru_short(x, gb, conv0, h0, cw, cb, gw, gbias, lam):
    b, t, r = x.shape
    lead = CONV_WIDTH - 1
    seg = lambda head, body: jnp.concatenate(
        [head, body, jnp.zeros((b, LRU_SEG - lead - t, r), F32)], axis=1).reshape(b * LRU_SEG, r)
    xin = seg(conv0, x)
    rows = b * LRU_SEG
    tm = _row_tile(rows, 256)
    blk = pl.BlockSpec((tm, r), lambda i: (i, 0))
    const2 = lambda shape: pl.BlockSpec(shape, lambda i: (0, 0))
    o, h = pl.pallas_call(
        functools.partial(_lru_short_kernel, t=t),
        grid=(rows // tm,),
        in_specs=[blk, blk, blk, const2((CONV_WIDTH, r)), const2((1, r)),
                  pl.BlockSpec((2, r, r), lambda i: (0, 0, 0)), const2((2, r)), const2((1, r))],
        out_specs=[blk, blk],
        out_shape=[jax.ShapeDtypeStruct((rows, r), F32), jax.ShapeDtypeStruct((rows, r), F32)],
        compiler_params=_cparams(("parallel",)),
        name="rglru_short",
    )(xin, seg(jnp.zeros((b, lead, r), F32), gb), jnp.repeat(h0, LRU_SEG, axis=0), cw, cb, gw, gbias, lam)
    o = o.reshape(b, LRU_SEG, r)[:, lead:lead + t]
    h_new = h.reshape(b, LRU_SEG, r)[:, lead + t - 1]
    conv_new = xin.reshape(b, LRU_SEG, r)[:, t:t + lead]
    return o, conv_new, h_new


def _gla_kernel(q_ref, k_ref, v_ref, og_ref, misc_ref, aw_ref, ab_ref, ng_ref, s0_ref,
                o_ref, sn_ref, st, *, tg, ck, tv):
    @pl.when(pl.program_id(1) == 0)
    def _():
        for h in range(GLA_HEADS):
            st[h] = s0_ref[0, h].T

    pre = _dot(misc_ref[0].astype(BF16), aw_ref[...]) + ab_ref[...]
    g = -_softplus(-pre) * (1.0 / GLA_TAU)
    if tv < tg:
        g = jnp.where(lax.broadcasted_iota(jnp.int32, g.shape, 0) < tv, g, 0.0)
    q = q_ref[0] * (GLA_DK ** -0.5)
    k = k_ref[0]
    v = v_ref[0]
    og = og_ref[0]
    ng = ng_ref[...]
    rows = lax.broadcasted_iota(jnp.int32, (ck, GLA_QK_W), 0)
    tril = (lax.broadcasted_iota(jnp.int32, (ck, ck), 0) >= lax.broadcasted_iota(jnp.int32, (ck, ck), 1))
    for c in range(tg // ck):
        sl = slice(c * ck, (c + 1) * ck)
        bc = g[sl]
        s = 1
        while s < ck:
            bc = bc + jnp.where(rows >= s, pltpu.roll(bc, s, 0), 0.0)
            s *= 2
        bl = bc[ck - 1:ck]
        e = jnp.exp(bc)
        qi = (q[sl] * e).astype(BF16)
        ki = (k[sl] * jnp.exp(-bc)).astype(BF16)
        kd = (k[sl] * jnp.exp(bl - bc)).astype(BF16)
        dec = jnp.exp(bl)
        vb = v[sl].astype(BF16)
        outs = []
        for h in range(GLA_HEADS):
            ks = slice(h * GLA_DK, (h + 1) * GLA_DK)
            vs = slice(h * GLA_DV, (h + 1) * GLA_DV)
            att = jnp.where(tril, _dot_nt(qi[:, ks], ki[:, ks]), 0.0)
            s_prev = st[h]
            o = _dot(att.astype(BF16), vb[:, vs]) + _dot_nt(qi[:, ks], s_prev.astype(BF16))
            st[h] = s_prev * dec[:, ks] + _dot_tn(vb[:, vs], kd[:, ks])
            o = _rms(o, ng[:, vs])
            ogh = og[sl, vs]
            outs.append(o * (ogh * jax.nn.sigmoid(ogh)))
        o_ref[0, sl, :] = jnp.concatenate(outs, axis=-1)

    @pl.when(pl.program_id(1) == pl.num_programs(1) - 1)
    def _():
        for h in range(GLA_HEADS):
            sn_ref[0, h] = st[h].T


def _gla(q_arr, q_blk, k_arr, k_blk, v_arr, v_blk, og_arr, og_blk, misc_arr, misc_blk,
         aw, ab, ng, s0, tg, ck, tv):
    b, t = q_arr.shape[:2]
    col = lambda w, blk: pl.BlockSpec((1, tg, w), lambda i, c: (i, c, blk))
    const2 = lambda shape: pl.BlockSpec(shape, lambda i, c: (0, 0))
    state = pl.BlockSpec((1, GLA_HEADS, GLA_DK, GLA_DV), lambda i, c: (i, 0, 0, 0))
    return pl.pallas_call(
        functools.partial(_gla_kernel, tg=tg, ck=ck, tv=tv),
        grid=(b, t // tg),
        in_specs=[col(GLA_QK_W, q_blk), col(GLA_QK_W, k_blk), col(GLA_V_W, v_blk), col(GLA_V_W, og_blk),
                  col(MISC_W, misc_blk), const2((MISC_W, GLA_QK_W)), const2((1, GLA_QK_W)),
                  const2((1, GLA_V_W)), state],
        out_specs=[pl.BlockSpec((1, tg, GLA_V_W), lambda i, c: (i, c, 0)), state],
        out_shape=[jax.ShapeDtypeStruct((b, t, GLA_V_W), F32),
                   jax.ShapeDtypeStruct((b, GLA_HEADS, GLA_DK, GLA_DV), F32)],
        scratch_shapes=[pltpu.VMEM((GLA_HEADS, GLA_DV, GLA_DK), F32)],
        compiler_params=_cparams(("parallel", "arbitrary")),
        name="gla",
    )(q_arr, k_arr, v_arr, og_arr, misc_arr, aw, ab, ng, s0)


GLA_SHORT_SEQS = 16


def _gla_short_kernel(q_ref, k_ref, v_ref, og_ref, misc_ref, aw_ref, ab_ref, ng_ref, s0_ref, o_ref, sn_ref,
                      *, nb, tv):
    seg = LRU_SEG
    rows = nb * seg
    flat = lambda ref: ref[...].reshape(rows, ref.shape[-1])
    part = lambda x, n: x[n * seg:(n + 1) * seg]
    pre = _dot(flat(misc_ref).astype(BF16), aw_ref[...]) + ab_ref[...]
    pos = lax.broadcasted_iota(jnp.int32, (rows, GLA_QK_W), 0) & (seg - 1)
    bc = jnp.where(pos < tv, -_softplus(-pre) * (1.0 / GLA_TAU), 0.0)
    s = 1
    while s < seg:
        bc = bc + jnp.where(pos >= s, pltpu.roll(bc, s, 0), 0.0)
        s *= 2
    last = bc.reshape(nb, seg, GLA_QK_W)[:, seg - 1:seg, :]
    bl = jnp.broadcast_to(last, (nb, seg, GLA_QK_W)).reshape(rows, GLA_QK_W)
    dec_t = jnp.exp(last.reshape(nb, GLA_QK_W)).T
    q = flat(q_ref) * (GLA_DK ** -0.5)
    k = flat(k_ref)
    qi = q * jnp.exp(bc)
    ki = k * jnp.exp(-bc)
    kd = k * jnp.exp(bl - bc)
    v = flat(v_ref)
    tril = lax.broadcasted_iota(jnp.int32, (seg, seg), 0) >= lax.broadcasted_iota(jnp.int32, (seg, seg), 1)
    pairs = [(n, h) for n in range(nb) for h in range(GLA_HEADS)]
    ks = lambda h: slice(h * GLA_DK, (h + 1) * GLA_DK)
    vs = lambda h: slice(h * GLA_DV, (h + 1) * GLA_DV)
    bf = lambda x: x.astype(BF16)
    qs = [bf(part(qi, n)[:, ks(h)]) for n, h in pairs]
    vb = [bf(part(v, n)[:, vs(h)]) for n, h in pairs]
    att = [bf(jnp.where(tril, _dot_nt(qp, bf(part(ki, n)[:, ks(h)])), 0.0)) for qp, (n, h) in zip(qs, pairs)]
    o = [_dot(a, vp) + _dot(qp, bf(s0_ref[n, h])) for a, vp, qp, (n, h) in zip(att, vb, qs, pairs)]
    for vp, (n, h) in zip(vb, pairs):
        sn_ref[n, h] = s0_ref[n, h] * dec_t[ks(h), n:n + 1] + _dot_tn(bf(part(kd, n)[:, ks(h)]), vp)
    o = jnp.concatenate([jnp.concatenate(o[n * GLA_HEADS:(n + 1) * GLA_HEADS], axis=1) for n in range(nb)], axis=0)
    ng = ng_ref[...]
    o = jnp.concatenate([_rms(o[:, vs(h)], ng[:, vs(h)]) for h in range(GLA_HEADS)], axis=1)
    og = flat(og_ref)
    o_ref[...] = (o * (og * jax.nn.sigmoid(og))).reshape(nb, seg, GLA_V_W)


def _gla_short(q, k, v, og, misc, aw, ab, ng, s0, tv):
    b = q.shape[0]
    nb = next((c for c in (GLA_SHORT_SEQS, 8) if b % c == 0), b)
    seq = lambda w: pl.BlockSpec((nb, LRU_SEG, w), lambda i: (i, 0, 0))
    const2 = lambda shape: pl.BlockSpec(shape, lambda i: (0, 0))
    state = pl.BlockSpec((nb, GLA_HEADS, GLA_DK, GLA_DV), lambda i: (i, 0, 0, 0))
    return pl.pallas_call(
        functools.partial(_gla_short_kernel, nb=nb, tv=tv),
        grid=(b // nb,),
        in_specs=[seq(GLA_QK_W), seq(GLA_QK_W), seq(GLA_V_W), seq(GLA_V_W), seq(MISC_W),
                  const2((MISC_W, GLA_QK_W)), const2((1, GLA_QK_W)), const2((1, GLA_V_W)), state],
        out_specs=[seq(GLA_V_W), state],
        out_shape=[jax.ShapeDtypeStruct((b, LRU_SEG, GLA_V_W), F32),
                   jax.ShapeDtypeStruct((b, GLA_HEADS, GLA_DK, GLA_DV), F32)],
        compiler_params=_cparams(("parallel",)),
        name="gla_short",
    )(q, k, v, og, misc, aw, ab, ng, s0)


def _bias_kernel(tbl_ref, dist_ref, o_ref):
    h = pl.program_id(0)
    max_exact = REL_BUCKETS // 2
    n = jnp.maximum(dist_ref[...], 0)
    nf = jnp.maximum(n, 1).astype(F32)
    large = max_exact + (jnp.log(nf / max_exact) / math.log(REL_MAX_DIST / max_exact)
                         * (REL_BUCKETS - max_exact)).astype(jnp.int32)
    bucket = jnp.where(n < max_exact, n, jnp.minimum(large, REL_BUCKETS - 1))
    out = jnp.zeros(bucket.shape, F32)
    for kk in range(REL_BUCKETS):
        out = jnp.where(bucket == kk, tbl_ref[kk, h], out)
    o_ref[0] = out


def _bias_lookup(rel_bias, dist, tr):
    r, c = dist.shape
    return pl.pallas_call(
        _bias_kernel,
        grid=(NSA_HEADS, r // tr),
        in_specs=[pl.BlockSpec(memory_space=pltpu.SMEM),
                  pl.BlockSpec((tr, c), lambda h, i: (i, 0))],
        out_specs=pl.BlockSpec((1, tr, c), lambda h, i: (h, i, 0)),
        out_shape=jax.ShapeDtypeStruct((NSA_HEADS, r, c), F32),
        compiler_params=_cparams(("parallel", "parallel")),
        name="rel_bias",
    )(rel_bias, dist)


def _cmp_kernel(x_ref, pe_ref, w1_ref, w2_ref, o_ref):
    hid = _gelu(_dot((x_ref[0] + pe_ref[0]).astype(BF16), w1_ref[0]))
    o_ref[0] = _dot(hid.astype(BF16), w2_ref[0])


def _row_tile(m, cap):
    for tm in range(min(cap, m) // 8 * 8, 0, -8):
        if m % tm == 0:
            return tm
    return m


def _compress(flat, pe, w1, w2, cap):
    m = flat.shape[1]
    tm = _row_tile(m, cap)
    fw = CMP_BLOCK * NSA_HEAD_DIM
    d = NSA_HEAD_DIM
    return pl.pallas_call(
        _cmp_kernel,
        grid=(2, m // tm),
        in_specs=[pl.BlockSpec((1, tm, fw), lambda z, i: (z, i, 0)),
                  pl.BlockSpec((1, 1, fw), lambda z, i: (z, 0, 0)),
                  pl.BlockSpec((1, fw, d), lambda z, i: (z, 0, 0)),
                  pl.BlockSpec((1, d, d), lambda z, i: (z, 0, 0))],
        out_specs=pl.BlockSpec((1, tm, d), lambda z, i: (z, i, 0)),
        out_shape=jax.ShapeDtypeStruct((2, m, d), F32),
        compiler_params=_cparams(("parallel", "parallel")),
        name="nsa_compress",
    )(flat, pe, w1, w2)


def _cmp_pages_kernel(pt_ref, *refs, n_pages):
    del pt_ref
    pages = refs[:n_pages]
    pe_ref, w1_ref, w2_ref, o_ref, x_scr, h_scr = refs[n_pages:]
    gd = NSA_KV_HEADS * NSA_HEAD_DIM
    nblk = n_pages * PAGE_SIZE // CMP_BLOCK
    for z in range(2):
        for p in range(n_pages):
            x_scr[z, p * PAGE_SIZE:(p + 1) * PAGE_SIZE, :] = pages[p][0, 0, z].reshape(gd, PAGE_SIZE).T
        acc = jnp.zeros((nblk, gd), F32)
        for t in range(CMP_BLOCK):
            rows = x_scr[z, pl.ds(t, nblk, stride=CMP_BLOCK), :]
            acc = acc + _dot((rows + pe_ref[z, t:t + 1, :]).astype(BF16), w1_ref[z, t])
        h_scr[z] = _dot(_gelu(acc).astype(BF16), w2_ref[z])
        for par in range(2):
            o_ref[0, z, par] = h_scr[z, pl.ds(par, nblk // 2, stride=2), :]


def _compress_pages(layer, page_table, cache, pe, w1, w2):
    b, n_pages = page_table.shape
    g, d = NSA_KV_HEADS, NSA_HEAD_DIM
    gd = g * d
    nblk = n_pages * PAGE_SIZE // CMP_BLOCK
    eye = jnp.eye(g, dtype=w1.dtype)
    w1 = jnp.einsum('gh,ztje->ztgjhe', eye, w1).reshape(2, CMP_BLOCK, gd, gd)
    w2 = jnp.einsum('gh,zje->zgjhe', eye, w2).reshape(2, gd, gd)
    pe = jnp.tile(pe, (1, 1, g))
    const = lambda *shape: pl.BlockSpec(shape, lambda i, pt: (0,) * len(shape))

    def page_spec(p):
        return pl.BlockSpec((1, 1, 2, g, d, PAGE_SIZE), lambda i, pt: (layer, pt[i, p], 0, 0, 0, 0))

    out = pl.pallas_call(
        functools.partial(_cmp_pages_kernel, n_pages=n_pages),
        grid_spec=pltpu.PrefetchScalarGridSpec(
            num_scalar_prefetch=1,
            grid=(b,),
            in_specs=[page_spec(p) for p in range(n_pages)]
            + [const(2, CMP_BLOCK, gd), const(2, CMP_BLOCK, gd, gd), const(2, gd, gd)],
            out_specs=pl.BlockSpec((1, 2, 2, nblk // 2, gd), lambda i, pt: (i, 0, 0, 0, 0)),
            scratch_shapes=[pltpu.VMEM((2, n_pages * PAGE_SIZE, gd), F32), pltpu.VMEM((2, nblk, gd), F32)]),
        out_shape=jax.ShapeDtypeStruct((b, 2, 2, nblk // 2, gd), F32),
        compiler_params=_cparams(("parallel",)),
        name="nsa_compress_pages",
    )(page_table, *([cache] * n_pages), pe, w1, w2)
    return out.reshape(b, 2, 2, nblk // 2, g, d).transpose(0, 1, 4, 2, 3, 5)


def _nsa_prompt_kernel(q_ref, kc_ref, vc_ref, bc_ref, ks_ref, vs_ref, kw_ref, vw_ref, bt_ref, e_ref, gt_ref,
                       o_ref, s_scr, p_scr, m_scr, selk_scr, os_scr, ow_scr):
    i = pl.program_id(2)
    tq, d, ck, hpg = NSA_TQ, NSA_HEAD_DIM, NSA_TQ, NSA_HPG
    nchunk = ks_ref.shape[3]
    nsel = kc_ref.shape[2] // 2
    neg = -jnp.inf
    qf = q_ref[0] * (d ** -0.5 * LOG2E)
    qs = jnp.concatenate([qf[:, h * d:(h + 1) * d] for h in range(hpg)], axis=0).astype(BF16)

    tk = lax.broadcasted_iota(jnp.int32, (tq, ck), 0) - lax.broadcasted_iota(jnp.int32, (tq, ck), 1)

    def mask_heads(s, ok):
        return jnp.concatenate([jnp.where(ok, s[h * tq:(h + 1) * tq], neg) for h in range(hpg)], axis=0)

    def attend(k_ref, v_ref, n, ok_fn):
        m_scr[...] = jnp.full(m_scr.shape, neg, F32)
        for r in range(n):
            c = i - r
            cc = jnp.maximum(c, 0)
            s = _dot(qs, k_ref[0, 0, 0, cc])
            if r < 2:
                s = s + bt_ref[0, r]
            s = mask_heads(s, ok_fn(r, c, cc))
            s_scr[:, r * ck:(r + 1) * ck] = s
            m_scr[...] = jnp.maximum(m_scr[...], s)
        m = jnp.max(m_scr[...], axis=1, keepdims=True)
        m = jnp.where(m > neg, m, 0.0)
        for r in range(n):
            p_scr[:, r * ck:(r + 1) * ck] = jnp.exp2(s_scr[:, r * ck:(r + 1) * ck] - m).astype(BF16)
        vt = jnp.concatenate([v_ref[0, 0, 0, jnp.maximum(i - r, 0)] for r in range(n)], axis=1)
        vt = jnp.concatenate([vt, jnp.ones((ONES_ROWS, n * ck), BF16)], axis=0)
        acc = _dot_nt(p_scr[:, 0:n * ck], vt)
        return acc[:, 0:d] / jnp.maximum(acc[:, d:d + 1], 1e-30)

    nw = WINDOW // ck + 1

    def win_ok(r, c, cc):
        if r == 0:
            return tk >= 0
        if r == nw - 1:
            return tk < jnp.where(c >= 0, 0, -tq)
        return tk > jnp.where(c >= 0, -tq, tq)

    ow_scr[...] = attend(kw_ref, vw_ref, nw, win_ok)

    t_lane = i * tq + (lax.broadcasted_iota(jnp.int32, (1, hpg * tq), 1) & (tq - 1))
    s_c = _dot_nt(kc_ref[0, 0], qs) + bc_ref[0, 0]
    r_c = lax.broadcasted_iota(jnp.int32, (2 * nsel, hpg * tq), 0)
    n_c = jnp.where(r_c < nsel, 2 * r_c, 2 * (r_c - nsel) + 1)
    s_c = jnp.where(t_lane >= (n_c + 1) * CMP_BLOCK - 1, s_c, neg)
    m_c = jnp.max(s_c, axis=0, keepdims=True)
    m_c = jnp.where(m_c > neg, m_c, 0.0)
    p_c = jnp.exp2(s_c - m_c)
    p_c = p_c / jnp.maximum(jnp.sum(p_c, axis=0, keepdims=True), 1e-30)
    o_c = _dot_tn(p_c.astype(BF16), vc_ref[0, 0])

    ph = p_c[:, 0:tq]
    for h in range(1, hpg):
        ph = ph + p_c[:, h * tq:(h + 1) * tq]
    imp = ph[0:nsel] + ph[nsel:2 * nsel]
    blk = lax.broadcasted_iota(jnp.int32, (nsel, tq), 0)
    cur = (i * tq + lax.broadcasted_iota(jnp.int32, (nsel, tq), 1)) >> int(math.log2(SEL_BLOCK))
    forced = (blk == 0) | (blk == cur) | (blk == cur - 1)
    imp = jnp.where(forced, FORCE_SCORE, jnp.where(blk <= cur, imp, neg))
    rank = jnp.zeros((nsel, tq), jnp.int32)
    for s2 in range(nsel):
        row = imp[s2:s2 + 1]
        beats = (row > imp) | ((row == imp) & (blk > s2))
        rank = rank + beats.astype(jnp.int32)
    chosen = (rank < N_SELECT).astype(BF16)
    selk = _dot_tn(chosen, e_ref[...])
    for c in range(nchunk):
        selk_scr[c] = selk[:, c * ck:(c + 1) * ck]

    def sel_ok(r, c, cc):
        ok = selk_scr[cc] > jnp.where(c >= 0, 0.5, 2.0)
        return ok & (tk >= 0) if r == 0 else ok

    sizes = list(range(NSA_CLASS, nchunk, NSA_CLASS)) + [nchunk]
    for lo, n in zip([0] + sizes[:-1], sizes):
        @pl.when((i >= lo) & (i < n))
        def _(n=n):
            os_scr[...] = attend(ks_ref, vs_ref, n, sel_ok)

    o_w = ow_scr[...]
    o_s = os_scr[...]

    sig = jax.nn.sigmoid(gt_ref[0, 0])
    outs = []
    for h in range(hpg):
        r = slice(h * tq, (h + 1) * tq)
        outs.append(sig[:, h:h + 1] * o_c[r] + sig[:, hpg + h:hpg + h + 1] * o_s[r]
                    + sig[:, 2 * hpg + h:2 * hpg + h + 1] * o_w[r])
    o_ref[0] = jnp.concatenate(outs, axis=1)


def _nsa_prompt(pr3, kc, vc, bias_c, kvt, bias_t, expand, gts):
    b, t = pr3.shape[:2]
    g, d, tq, hpg = NSA_KV_HEADS, NSA_HEAD_DIM, NSA_TQ, NSA_HPG
    nt = t // tq
    nc = kc.shape[2]
    rows = hpg * tq
    per_bg = lambda r, c: pl.BlockSpec((1, 1, r, c), lambda bi, gi, i: (bi, gi, 0, 0))
    kv = lambda br, z: pl.BlockSpec((None, 1, 1, 1, nt, d, tq), lambda bi, gi, i: (br, bi, z, gi, 0, 0, 0))
    return pl.pallas_call(
        _nsa_prompt_kernel,
        grid=(b, g, nt),
        in_specs=[pl.BlockSpec((1, tq, GROUP_W), lambda bi, gi, i: (bi, i, COL_Q // GROUP_W + gi)),
                  per_bg(nc, d), per_bg(nc, d),
                  pl.BlockSpec((1, 1, nc, rows), lambda bi, gi, i: (gi, i, 0, 0)),
                  kv(0, 0), kv(0, 1), kv(1, 0), kv(1, 1),
                  pl.BlockSpec((1, 2, rows, tq), lambda bi, gi, i: (gi, 0, 0, 0)),
                  pl.BlockSpec((nc // 2, t), lambda bi, gi, i: (0, 0)),
                  pl.BlockSpec((1, 1, tq, 128), lambda bi, gi, i: (bi, gi, i, 0))],
        out_specs=pl.BlockSpec((1, tq, GROUP_W), lambda bi, gi, i: (bi, i, gi)),
        out_shape=jax.ShapeDtypeStruct((b, t, Q_W), F32),
        scratch_shapes=[pltpu.VMEM((rows, t), F32), pltpu.VMEM((rows, t), BF16),
                        pltpu.VMEM((rows, tq), F32),
                        pltpu.VMEM((nt, tq, tq), F32), pltpu.VMEM((rows, d), F32), pltpu.VMEM((rows, d), F32)],
        compiler_params=_cparams(("parallel", "parallel", "arbitrary")),
        name="nsa_prompt",
    )(pr3, kc, vc, bias_c, kvt, kvt, kvt, kvt, bias_t, expand, gts)


SMP_CPAD = 128
KEY_TILE = PAGE_SIZE


def _masked_softmax(s, ok):
    s = jnp.where(ok, s, -jnp.inf)
    m = jnp.max(s, axis=-1, keepdims=True)
    m = jnp.where(m > -jnp.inf, m, 0.0)
    p = jnp.exp(s - m)
    return p / jnp.maximum(jnp.sum(p, axis=-1, keepdims=True), 1e-30)


def _nsa_sample_kernel(pt_ref, q_ref, kc_ref, vc_ref, bc_ref, okc_ref, cur_ref, *rest, nb, n_pages, nsel):
    pages = rest[:nb * n_pages]
    (sn_ref, bs_ref, oks_ref, e_ref, wb_ref, wn_ref, bw_ref, okw_ref, gt_ref, acc_ref,
     o_ref, wo_ref) = rest[nb * n_pages:]
    del pt_ref, acc_ref
    d, hpg = NSA_HEAD_DIM, NSA_HPG
    rows = q_ref.shape[2]
    s_new = rows // hpg
    chains = [(n, g) for n in range(nb) for g in range(NSA_KV_HEADS)]
    bf = lambda x: x.astype(BF16)
    stack = lambda parts: jnp.concatenate(parts, axis=0)
    tile = lambda x: stack([x] * len(chains))
    part = lambda x, c: x[c * rows:(c + 1) * rows]
    qs = [bf(q_ref[n, g] * (d ** -0.5)) for n, g in chains]

    s_c = stack([_dot_nt(q, bf(kc_ref[n, g])) + bc_ref[g] for q, (n, g) in zip(qs, chains)])
    p_c = _masked_softmax(s_c, tile(okc_ref[...]) > 0.5)
    o_c = stack([_dot(bf(part(p_c, c)), bf(vc_ref[n, g])) for c, (n, g) in enumerate(chains)])

    ph = []
    for c in range(len(chains)):
        acc = p_c[c * rows:c * rows + s_new]
        for h in range(1, hpg):
            acc = acc + p_c[c * rows + h * s_new:c * rows + (h + 1) * s_new]
        ph.append(acc)
    ph = stack(ph)
    imp = ph[:, 0:SMP_CPAD] + ph[:, SMP_CPAD:2 * SMP_CPAD]
    blk = lax.broadcasted_iota(jnp.int32, imp.shape, 1)
    cur = tile(cur_ref[...])
    forced = (blk == 0) | (blk == cur) | (blk == cur - 1)
    imp = jnp.where(forced, FORCE_SCORE, jnp.where(blk <= cur, imp, -jnp.inf))
    rank = jnp.zeros(imp.shape, jnp.int32)
    for s2 in range(nsel):
        col = imp[:, s2:s2 + 1]
        beats = (col > imp) | ((col == imp) & (blk > s2))
        rank = rank + beats.astype(jnp.int32)
    chosen = bf((rank < N_SELECT) & (blk < nsel))
    chosen = stack([chosen[c * s_new:(c + 1) * s_new] for c in range(len(chains)) for _ in range(hpg)])
    sel_keys = _dot(chosen, e_ref[...])

    def kv_tiles(n, g, z):
        return [pg[0, 0, z, g] for pg in pages[n * n_pages:(n + 1) * n_pages]] + [sn_ref[n, z, g]]

    s_s = stack([jnp.concatenate([_dot(q, bf(kt)) for kt in kv_tiles(n, g, 0)], axis=1) + bs_ref[g]
                 for q, (n, g) in zip(qs, chains)])
    p_s = bf(_masked_softmax(s_s, (tile(oks_ref[...]) > 0.5) & (sel_keys > 0.5)))
    o_s = []
    for c, (n, g) in enumerate(chains):
        pc = part(p_s, c)
        acc = None
        for p, vt in enumerate(kv_tiles(n, g, 1)):
            term = _dot_nt(pc[:, p * KEY_TILE:(p + 1) * KEY_TILE], bf(vt))
            acc = term if acc is None else acc + term
        o_s.append(acc)
    o_s = stack(o_s)

    wkeys = wb_ref.shape[-1]
    s_w = stack([jnp.concatenate([_dot(q, bf(wb_ref[0, n, 0, g])), _dot(q, bf(wn_ref[n, 0, g]))], axis=1)
                 + bw_ref[g] for q, (n, g) in zip(qs, chains)])
    p_w = bf(_masked_softmax(s_w, tile(okw_ref[...]) > 0.5))
    o_w = stack([_dot_nt(part(p_w, c)[:, 0:wkeys], bf(wb_ref[0, n, 1, g]))
                 + _dot_nt(part(p_w, c)[:, wkeys:wkeys + KEY_TILE], bf(wn_ref[n, 1, g]))
                 for c, (n, g) in enumerate(chains)])

    gt = jax.nn.sigmoid(stack([gt_ref[n, g] for n, g in chains]))
    out = gt[:, 0:1] * o_c + gt[:, 1:2] * o_s + gt[:, 2:3] * o_w
    for c, (n, g) in enumerate(chains):
        o_ref[n, g] = part(out, c)

    lane = lax.broadcasted_iota(jnp.int32, (2 * NSA_KV_HEADS * d, KEY_TILE), 1)
    for n in range(nb):
        old = pltpu.roll(wb_ref[0, n].reshape(-1, wkeys), wkeys - s_new, 1)
        new = pltpu.roll(wn_ref[n].reshape(-1, KEY_TILE), KEY_TILE - s_new, 1)
        tail = jnp.where(lane >= KEY_TILE - s_new, new, old[:, wkeys - KEY_TILE:])
        wo_ref[0, n] = jnp.concatenate([old[:, 0:wkeys - KEY_TILE], tail], axis=1).reshape(wo_ref.shape[2:])


SMP_SEQS = 4


def _nsa_sample(layer, nsel, page_table, qg, kc, vc, bias_c, ok_c, cur, cache_sel, sel_new, bias_s, ok_s,
                expand, win_state, win_new, bias_w, ok_w, gts, win_acc):
    b = qg.shape[0]
    g = NSA_KV_HEADS
    d = NSA_HEAD_DIM
    rows = qg.shape[2]
    n_pages = page_table.shape[1]
    wkeys = win_state.shape[-1]
    ks = (n_pages + 1) * KEY_TILE
    kw = wkeys + KEY_TILE
    nb = SMP_SEQS if b % SMP_SEQS == 0 else 1
    per_b = lambda *shape: pl.BlockSpec((nb,) + shape, lambda i, pt: (i,) + (0,) * len(shape))
    const = lambda *shape: pl.BlockSpec(shape, lambda i, pt: (0,) * len(shape))

    def page_spec(n, p):
        return pl.BlockSpec((1, 1, 2, g, d, PAGE_SIZE), lambda i, pt: (layer, pt[nb * i + n, p], 0, 0, 0, 0))

    win_spec = pl.BlockSpec((1, nb, 2, g, d, wkeys), lambda i, pt: (layer, i, 0, 0, 0, 0))
    in_specs = ([per_b(g, rows, d), per_b(g, 2 * SMP_CPAD, d), per_b(g, 2 * SMP_CPAD, d),
                 const(g, rows, 2 * SMP_CPAD), const(rows, 2 * SMP_CPAD), const(rows // NSA_HPG, SMP_CPAD)]
                + [page_spec(n, p) for n in range(nb) for p in range(n_pages)]
                + [per_b(2, g, d, KEY_TILE), const(g, rows, ks), const(rows, ks), const(SMP_CPAD, ks),
                   win_spec, per_b(2, g, d, KEY_TILE), const(g, rows, kw), const(rows, kw), per_b(g, rows, 128),
                   pl.BlockSpec(memory_space=pl.ANY)])
    operands = (page_table, qg, kc, vc, bias_c, ok_c, cur, *([cache_sel] * (nb * n_pages)), sel_new, bias_s, ok_s,
                expand, win_state, win_new, bias_w, ok_w, gts, win_acc)
    return pl.pallas_call(
        functools.partial(_nsa_sample_kernel, nb=nb, n_pages=n_pages, nsel=nsel),
        grid_spec=pltpu.PrefetchScalarGridSpec(
            num_scalar_prefetch=1,
            grid=(b // nb,),
            in_specs=in_specs,
            out_specs=[pl.BlockSpec((nb, g, rows, d), lambda i, pt: (i, 0, 0, 0)), win_spec]),
        out_shape=[jax.ShapeDtypeStruct((b, g, rows, d), F32), jax.ShapeDtypeStruct(win_acc.shape, F32)],
        input_output_aliases={len(operands) - 1: 1},
        compiler_params=_cparams(("parallel",)),
        name="nsa_sample",
    )(*operands)


def _pad_rows(a, rows):
    return jnp.pad(a, ((0, 0), (0, rows - a.shape[1]), (0, 0)))


def _even_odd(n):
    return np.concatenate([np.arange(0, n, 2), np.arange(1, n, 2)])


def _prompt_bias_tables(rel_bias, t):
    g, hpg, tq = NSA_KV_HEADS, NSA_HPG, NSA_TQ
    nt = t // tq
    nc = t // CMP_BLOCK
    end_c = (_even_odd(nc) + 1) * CMP_BLOCK - 1
    dist_c = jnp.asarray(np.arange(t)[None, :] - end_c[:, None], jnp.int32)
    bias_c = _bias_lookup(rel_bias, dist_c, nc)
    bias_c = bias_c.reshape(g, hpg, nc, nt, tq).transpose(0, 3, 2, 1, 4).reshape(g, nt, nc, hpg * tq) * LOG2E
    off = np.arange(3)[:, None, None] * tq
    dist_t = off + np.arange(tq)[None, :, None] - np.arange(tq)[None, None, :]
    assert dist_t[2].min() >= REL_MAX_DIST
    bias_t = _bias_lookup(rel_bias, jnp.asarray(dist_t.reshape(3 * tq, tq), jnp.int32), 3 * tq)
    bias_t = bias_t.reshape(g, hpg, 3, tq, tq).transpose(0, 2, 1, 3, 4).reshape(g, 3, hpg * tq, tq)
    bias_t = (bias_t[:, 0:2] - bias_t[:, 2:3]) * LOG2E
    expand = np.arange(t)[None, :] // SEL_BLOCK == np.arange(t // SEL_BLOCK)[:, None]
    return bias_c, bias_t, jnp.asarray(expand, BF16)


def _sample_tables(rel_bias, past_len, s_new, n_win_keys):
    g, hpg = NSA_KV_HEADS, NSA_HPG
    tk = -(-(past_len + s_new) // SEL_BLOCK) * SEL_BLOCK
    nc = tk // CMP_BLOCK
    nsel = tk // SEL_BLOCK
    half = (nc + 1) // 2
    pos_q = past_len + np.arange(s_new)
    lane = np.arange(SMP_CPAD)
    n_of_lane = np.concatenate([2 * lane, 2 * lane + 1])
    real_c = np.concatenate([lane < half, lane < nc - half])
    dist_c = pos_q[:, None] - ((n_of_lane[None, :] + 1) * CMP_BLOCK - 1)
    ok_c = real_c[None, :] & (dist_c >= 0)
    n_keys_s = (past_len // PAGE_SIZE + 1) * KEY_TILE
    key = np.arange(n_keys_s)
    dist_s = pos_q[:, None] - key[None, :]
    ok_s = (key[None, :] < past_len + s_new) & (dist_s >= 0)
    n_keys_w = n_win_keys + KEY_TILE
    i = np.arange(n_keys_w)
    pos_kw = np.where(i < n_win_keys, past_len - n_win_keys + i, past_len + i - n_win_keys)
    dist_w = pos_q[:, None] - pos_kw[None, :]
    ok_w = (i[None, :] < n_win_keys + s_new) & (dist_w >= 0) & (dist_w < WINDOW) & (pos_kw[None, :] >= 0)
    dist = np.concatenate([dist_c, dist_s, dist_w], axis=1)
    dist = np.pad(dist, ((0, 8 - s_new), (0, 0)))
    bias = _bias_lookup(rel_bias, jnp.asarray(dist, jnp.int32), 8)[:, :s_new]
    bias = bias.reshape(g, hpg * s_new, dist.shape[1])
    c0, c1 = 2 * SMP_CPAD, 2 * SMP_CPAD + n_keys_s
    tile = lambda m: jnp.asarray(np.tile(m, (hpg, 1)), F32)
    expand = (key[None, :] // SEL_BLOCK == np.arange(SMP_CPAD)[:, None]) & (np.arange(SMP_CPAD)[:, None] < nsel)
    cur = np.broadcast_to((pos_q // SEL_BLOCK)[:, None], (s_new, SMP_CPAD))
    return dict(bias_c=bias[:, :, :c0], bias_s=bias[:, :, c0:c1], bias_w=bias[:, :, c1:],
                ok_c=tile(ok_c), ok_s=tile(ok_s), ok_w=tile(ok_w),
                expand=jnp.asarray(expand, BF16), cur=jnp.asarray(cur, jnp.int32), nc=nc, half=half, tk=tk)


def _flat_blocks(rows):
    b, tk = rows.shape[:2]
    nc = tk // CMP_BLOCK
    blk = rows.reshape(b, nc // 2, 2, CMP_BLOCK, 2, NSA_KV_HEADS, NSA_HEAD_DIM)
    return jnp.transpose(blk, (4, 0, 5, 2, 1, 3, 6)).reshape(2, b * NSA_KV_HEADS * nc, CMP_BLOCK * NSA_HEAD_DIM)


def _nsa_prompt_layer(pr3, lw, tabs):
    b, t = pr3.shape[:2]
    g, hpg, d, tq = NSA_KV_HEADS, NSA_HPG, NSA_HEAD_DIM, NSA_TQ
    nt = t // tq
    nc = t // CMP_BLOCK
    kv = pr3[:, :, COL_CMP:COL_CMP + 3 * KV_W].reshape(b, t, 3, 2, g, d)
    cmp_rows, sel_rows, win_rows = kv[:, :, 0], kv[:, :, 1], kv[:, :, 2]
    kcv = _compress(_flat_blocks(cmp_rows), lw['pe'], lw['cw1'], lw['cw2'], 512)
    kcv = kcv.reshape(2, b, g, nc, d).astype(BF16)
    kvt = kv[:, :, 1:].reshape(b, nt, tq, 2, 2, g, d).transpose(3, 0, 4, 5, 1, 6, 2).astype(BF16)
    gts = pr3[:, :, COL_MISC:COL_MISC + N_GATE].reshape(b, t, 3, g, hpg).transpose(0, 3, 1, 2, 4)
    gts = jnp.pad(gts.reshape(b, g, t, 3 * hpg), ((0, 0), (0, 0), (0, 0), (0, 128 - 3 * hpg)))
    o = _nsa_prompt(pr3, kcv[0], kcv[1], tabs[0], kvt, tabs[1], tabs[2], gts)
    return o.reshape(b * t, Q_W), cmp_rows, sel_rows, win_rows[:, t - min(WINDOW, t):]


def _to_native(rows, keys):
    b, n = rows.shape[:2]
    r = rows.reshape(b, n, 2, NSA_KV_HEADS, NSA_HEAD_DIM).transpose(0, 2, 3, 4, 1)
    return jnp.pad(r, ((0, 0),) * 4 + ((0, keys - n),))


def _nsa_sample_layer(pr3, lw, tabs, layer, cache_cmp, cache_sel, win_state, page_table, win_acc):
    b, s_new = pr3.shape[:2]
    g, hpg, d = NSA_KV_HEADS, NSA_HPG, NSA_HEAD_DIM
    n_pages = page_table.shape[1]
    past_len = n_pages * PAGE_SIZE
    new = lambda c: pr3[:, :, c:c + KV_W]
    cmp_new, sel_new, win_new = new(COL_CMP), new(COL_SEL), new(COL_WIN)
    n_past, n_tail = past_len // CMP_BLOCK, tabs['nc'] - past_len // CMP_BLOCK
    pe3 = lw['pe'].reshape(2, CMP_BLOCK, d)
    past = _compress_pages(layer, page_table, cache_cmp, pe3, lw['cw1'].reshape(2, CMP_BLOCK, d, d), lw['cw2'])
    tail = jnp.pad(cmp_new, ((0, 0), (0, n_tail * CMP_BLOCK - s_new), (0, 0)))
    tail = _compress(_flat_blocks(tail.reshape(b, n_tail * CMP_BLOCK, 2, g, d)), lw['pe'], lw['cw1'], lw['cw2'], 512)
    tail = tail.reshape(2, b, g, 2, n_tail // 2, d).transpose(1, 0, 2, 3, 4, 5)
    kcv = jnp.concatenate([past, tail], axis=4)
    kcv = jnp.pad(kcv, ((0, 0),) * 4 + ((0, SMP_CPAD - kcv.shape[4]), (0, 0)))
    kcv = kcv.reshape(b, 2, g, 2 * SMP_CPAD, d).transpose(1, 0, 2, 3, 4)
    qg = pr3[:, :, COL_Q:COL_Q + Q_W].reshape(b, s_new, g, hpg, d).transpose(0, 2, 3, 1, 4)
    qg = qg.reshape(b, g, hpg * s_new, d)
    gts = pr3[:, :, COL_MISC:COL_MISC + N_GATE].reshape(b, s_new, 3, g, hpg).transpose(0, 3, 4, 1, 2)
    gts = jnp.pad(gts.reshape(b, g, hpg * s_new, 3), ((0, 0), (0, 0), (0, 0), (0, 128 - 3)))
    o, win_acc = _nsa_sample(layer, tabs['tk'] // SEL_BLOCK, page_table, qg, kcv[0], kcv[1], tabs['bias_c'], tabs['ok_c'],
                    tabs['cur'], cache_sel, _to_native(sel_new, KEY_TILE), tabs['bias_s'], tabs['ok_s'],
                    tabs['expand'], win_state, _to_native(win_new, KEY_TILE), tabs['bias_w'], tabs['ok_w'], gts,
                    win_acc)
    o = o.reshape(b, g, hpg, s_new, d).transpose(0, 3, 1, 2, 4).reshape(b * s_new, Q_W)
    kv5 = lambda a: a.reshape(b, -1, 2, g, d)
    return o, kv5(cmp_new), kv5(sel_new), win_acc


def _mixers(pr, b, t, lw, conv0, h0, s0, nsa_fn):
    pr3 = pr.reshape(b, t, PROJ_COLS)
    o_a, cmp_rows, sel_rows, win_rows = nsa_fn(pr3)
    lru_w = (lw['lcw'], lw['lcb'], lw['lgw'], lw['lgb'], lw['lam'])
    gla_w = (lw['gaw'], lw['gab'], lw['gng'])
    if t % 8 == 0:
        tc = min(t, 256)
        o_b, conv_n, h_n = _lru(pr3, COL_LRU_X // LRU_WIDTH, pr3, COL_LRU_G // LRU_WIDTH, conv0, h0[:, None],
                                *lru_w, tc, tc)
        tg = min(t, 256)
        o_c, s_n = _gla(pr3, COL_GLA_Q // GLA_QK_W, pr3, COL_GLA_K // GLA_QK_W, pr3, COL_GLA_V // GLA_V_W,
                         pr3, COL_GLA_G // GLA_V_W, pr3, COL_MISC // MISC_W, *gla_w, s0, tg,
                         min(tg, GLA_CHUNK), tg)
    else:
        tp = -(-t // 8) * 8
        cut = lambda c, w: _pad_rows(pr3[:, :, c:c + w], tp)
        if CONV_WIDTH - 1 + t <= LRU_SEG:
            o_b, conv_n, h_n = _lru_short(pr3[:, :, COL_LRU_X:COL_LRU_X + LRU_WIDTH],
                                          pr3[:, :, COL_LRU_G:COL_LRU_G + LRU_WIDTH], conv0, h0, *lru_w)
            h_n = h_n[:, None]
        else:
            o_b, conv_n, h_n = _lru(cut(COL_LRU_X, LRU_WIDTH), 0, cut(COL_LRU_G, LRU_WIDTH), 0, conv0,
                                    h0[:, None], *lru_w, tp, t)
        gla_in = (cut(COL_GLA_Q, GLA_QK_W), cut(COL_GLA_K, GLA_QK_W), cut(COL_GLA_V, GLA_V_W),
                  cut(COL_GLA_G, GLA_V_W), cut(COL_MISC, MISC_W))
        if tp == LRU_SEG:
            o_c, s_n = _gla_short(*gla_in, *gla_w, s0, t)
        else:
            o_c, s_n = _gla(*(a for x in gla_in for a in (x, 0)), *gla_w, s0, tp, tp, t)
        o_b, o_c = o_b[:, :t], o_c[:, :t]
    o_b = o_b.reshape(b * t, LRU_WIDTH)
    o_c = o_c.reshape(b * t, GLA_V_W)
    return (o_a, o_b, o_c), (cmp_rows, sel_rows, win_rows, conv_n, h_n[:, 0], s_n)


def _layer(x, b, t, lw, conv0, h0, s0, nsa_fn, tm):
    pr = _proj(x, lw['ng'], lw['w_in'], lw['b_in'], tm, PROJ_COLS // 5)
    (o_a, o_b, o_c), states = _mixers(pr, b, t, lw, conv0, h0, s0, nsa_fn)
    x = _merge(x, o_a, o_b, o_c, pr, lw['wb'], lw['wo'], min(tm, 512))
    x = _mlp(x, lw['mg'], lw['w1'], lw['w2'], tm, 1024)
    return x, states


def kernel(x_prompt, x_sample, cache_nsa_cmp_kv, cache_nsa_sel_kv, state_nsa_win_kv, state_lru_conv,
           state_lru_h, state_gla, page_table, rel_bias, norm_mix_g, norm_mlp_g, norm_final_g, w_in, b_in,
           nsa_cmp_pe, nsa_cmp_w1, nsa_cmp_w2, lru_gate_w, lru_gate_b, lru_lambda, lru_conv_w, lru_conv_b,
           gla_alpha_w, gla_alpha_b, gla_norm_g, w_branch, w_out, mlp_w1, mlp_w2):
    bp, tp = x_prompt.shape[:2]
    bs, ts = x_sample.shape[:2]
    depth = w_in.shape[0]
    n_pages = page_table.shape[1]
    w_buf = state_nsa_win_kv.shape[2]

    cols = [w_in[..., _SRC[n][0]:_SRC[n][0] + _SRC[n][1]] for n in _DST_ORDER]
    pad = PROJ_COLS - sum(c.shape[-1] for c in cols)
    w_in_p = jnp.concatenate(cols + [jnp.zeros(w_in.shape[:2] + (pad,), w_in.dtype)], axis=-1).astype(BF16)
    bcols = [b_in[..., _SRC[n][0]:_SRC[n][0] + _SRC[n][1]] for n in _DST_ORDER]
    b_in_p = jnp.concatenate(bcols + [jnp.zeros((depth, pad), b_in.dtype)], axis=-1)[:, None, :]
    eye = jnp.eye(LRU_BLOCKS, dtype=lru_gate_w.dtype)
    lgw = jnp.einsum('lznce,nm->lzncme', lru_gate_w, eye).reshape(depth, 2, LRU_WIDTH, LRU_WIDTH).astype(BF16)
    gaw = jnp.zeros((depth, MISC_W, GLA_QK_W), F32).at[:, N_GATE:N_GATE + GLA_RANK].set(gla_alpha_w).astype(BF16)
    pe = jnp.transpose(nsa_cmp_pe, (0, 2, 1, 3)).reshape(depth, 2, 1, CMP_BLOCK * NSA_HEAD_DIM)
    cw1 = nsa_cmp_w1.astype(BF16)
    cw2 = nsa_cmp_w2.astype(BF16)
    wb = w_branch.astype(BF16)
    wo = w_out.astype(BF16)
    w1 = mlp_w1.astype(BF16)
    w2 = mlp_w2.astype(BF16)
    cache_cmp = jnp.transpose(cache_nsa_cmp_kv, (0, 1, 3, 4, 5, 2))
    cache_sel = jnp.transpose(cache_nsa_sel_kv, (0, 1, 3, 4, 5, 2))
    win_state = jnp.transpose(state_nsa_win_kv, (0, 1, 3, 4, 5, 2))

    tabs_p = _prompt_bias_tables(rel_bias, tp)
    tabs_s = _sample_tables(rel_bias, n_pages * PAGE_SIZE, ts, w_buf)

    xp = x_prompt.reshape(bp * tp, D_MODEL)
    xs = x_sample.reshape(bs * ts, D_MODEL)
    conv0_p = jnp.zeros((bp, CONV_WIDTH - 1, LRU_WIDTH), F32)
    h0_p = jnp.zeros((bp, LRU_WIDTH), F32)
    s0_p = jnp.zeros((bp, GLA_HEADS, GLA_DK, GLA_DV), F32)
    outs_p = [[] for _ in range(6)]
    outs_s = [[] for _ in range(6)]
    win_acc = jnp.zeros(win_state.shape, F32)
    for l in range(depth):
        lw = dict(ng=norm_mix_g[l][None], mg=norm_mlp_g[l][None], w_in=w_in_p[l], b_in=b_in_p[l],
                  pe=pe[l], cw1=cw1[l], cw2=cw2[l], lcw=lru_conv_w[l], lcb=lru_conv_b[l][None], lgw=lgw[l],
                  lgb=lru_gate_b[l], lam=lru_lambda[l][None], gaw=gaw[l], gab=gla_alpha_b[l][None],
                  gng=gla_norm_g[l][None], wb=wb[l], wo=wo[l], w1=w1[l], w2=w2[l])
        xp, st_p = _layer(xp, bp, tp, lw, conv0_p, h0_p, s0_p,
                          functools.partial(_nsa_prompt_layer, lw=lw, tabs=tabs_p), _row_tile(bp * tp, 1024))
        nsa_s = functools.partial(_nsa_sample_layer, lw=lw, tabs=tabs_s, layer=l, cache_cmp=cache_cmp,
                                  cache_sel=cache_sel, win_state=win_state, page_table=page_table, win_acc=win_acc)
        xs, st_s = _layer(xs, bs, ts, lw, state_lru_conv[l], state_lru_h[l], state_gla[l], nsa_s, bs * ts)
        win_acc = st_s[2]
        for j in range(6):
            outs_p[j].append(st_p[j])
            outs_s[j].append(st_s[j])
    y_prompt = _final_norm(xp, norm_final_g[None], _row_tile(bp * tp, 1024)).reshape(bp, tp, D_MODEL)
    y_sample = _final_norm(xs, norm_final_g[None], bs * ts).reshape(bs, ts, D_MODEL)
    st = lambda outs, j: jnp.stack(outs[j])
    win_s = win_acc.transpose(0, 1, 5, 2, 3, 4)
    return (y_prompt, y_sample, st(outs_p, 0), st(outs_s, 0), st(outs_p, 1), st(outs_s, 1),
            st(outs_p, 2), win_s, st(outs_p, 3), st(outs_s, 3), st(outs_p, 4), st(outs_s, 4),
            st(outs_p, 5), st(outs_s, 5))
```

```python
import functools
import math

import jax
import jax.numpy as jnp
import numpy as np
from jax import lax
from jax.experimental import pallas as pl
from jax.experimental.pallas import tpu as pltpu

F32 = jnp.float32
BF16 = jnp.bfloat16

D_MODEL = 1024
DEPTH = 4
PAGE_SIZE = 128
NSA_HEADS = 8
NSA_KV_HEADS = 2
NSA_HPG = NSA_HEADS // NSA_KV_HEADS
NSA_HEAD_DIM = 64
CMP_BLOCK = 32
SEL_BLOCK = 64
N_SELECT = 16
WINDOW = 512
FORCE_SCORE = 1e4
REL_BUCKETS = 32
REL_MAX_DIST = 128
LRU_WIDTH = 512
LRU_BLOCKS = 8
LRU_BLOCK_DIM = LRU_WIDTH // LRU_BLOCKS
CONV_WIDTH = 4
LRU_C = 8.0
GLA_HEADS = 4
GLA_DK = 64
GLA_DV = 128
GLA_RANK = 16
GLA_TAU = 16.0
GLA_CHUNK = 64
D_FF = 4 * D_MODEL
N_BRANCH = 3
BRANCH_WIDTH = 512
NORM_EPS = 1e-6

KV_W = 2 * NSA_KV_HEADS * NSA_HEAD_DIM
Q_W = NSA_HEADS * NSA_HEAD_DIM
GROUP_W = NSA_HPG * NSA_HEAD_DIM
GLA_QK_W = GLA_HEADS * GLA_DK
GLA_V_W = GLA_HEADS * GLA_DV
MISC_W = 128
N_GATE = 3 * NSA_HEADS

COL_MERGE = 0
COL_Q = COL_MERGE + N_BRANCH * D_MODEL
COL_LRU_X = COL_Q + Q_W
COL_LRU_G = COL_LRU_X + LRU_WIDTH
COL_GLA_V = COL_LRU_G + LRU_WIDTH
COL_GLA_G = COL_GLA_V + GLA_V_W
COL_CMP = COL_GLA_G + GLA_V_W
COL_SEL = COL_CMP + KV_W
COL_WIN = COL_SEL + KV_W
COL_GLA_Q = COL_WIN + KV_W
COL_GLA_K = COL_GLA_Q + GLA_QK_W
COL_MISC = COL_GLA_K + GLA_QK_W
PROJ_COLS = COL_MISC + MISC_W

_SRC = {}
_off = 0
for _name, _w in (('nsa_q', Q_W), ('nsa_cmp_kv', KV_W), ('nsa_sel_kv', KV_W), ('nsa_win_kv', KV_W),
                  ('nsa_gate', N_GATE), ('lru_x', LRU_WIDTH), ('lru_gate', LRU_WIDTH),
                  ('gla_q', GLA_QK_W), ('gla_k', GLA_QK_W), ('gla_v', GLA_V_W),
                  ('gla_alpha', GLA_RANK), ('gla_gate', GLA_V_W), ('merge_gate', N_BRANCH * D_MODEL)):
    _SRC[_name] = (_off, _w)
    _off += _w
_DST_ORDER = ('merge_gate', 'nsa_q', 'lru_x', 'lru_gate', 'gla_v', 'gla_gate', 'nsa_cmp_kv',
              'nsa_sel_kv', 'nsa_win_kv', 'gla_q', 'gla_k', 'nsa_gate', 'gla_alpha')

NSA_TQ = 128
NSA_LANES = NSA_HPG * NSA_TQ
NSA_CLASS = 2
LOG2E = math.log2(math.e)
ONES_ROWS = 16
VMEM_LIMIT = 56 * 1024 * 1024


def _cparams(sem):
    return pltpu.CompilerParams(dimension_semantics=sem, vmem_limit_bytes=VMEM_LIMIT)


def _gelu(x):
    return x * (0.5 * (1.0 + jnp.tanh(math.sqrt(2.0 / math.pi) * (x + 0.044715 * (x * x * x)))))


def _softplus(x):
    return jnp.maximum(x, 0.0) + jnp.log1p(jnp.exp(-jnp.abs(x)))


def _rms(x, g):
    return x * lax.rsqrt(jnp.mean(x * x, axis=-1, keepdims=True) + NORM_EPS) * g


def _dot(a, b):
    return jnp.dot(a, b, preferred_element_type=F32)


def _dot_nt(a, b):
    return lax.dot_general(a, b, (((1,), (1,)), ((), ())), preferred_element_type=F32)


def _dot_tn(a, b):
    return lax.dot_general(a, b, (((0,), (0,)), ((), ())), preferred_element_type=F32)


def _proj_kernel(x_ref, g_ref, w_ref, b_ref, o_ref, h_ref):
    @pl.when(pl.program_id(1) == 0)
    def _():
        h_ref[...] = _rms(x_ref[...], g_ref[...]).astype(BF16)

    o_ref[...] = _dot(h_ref[...], w_ref[...]) + b_ref[...]


def _proj(x, g, w, b, tm, tn):
    n = x.shape[0]
    return pl.pallas_call(
        _proj_kernel,
        grid=(n // tm, PROJ_COLS // tn),
        in_specs=[pl.BlockSpec((tm, D_MODEL), lambda i, j: (i, 0)),
                  pl.BlockSpec((1, D_MODEL), lambda i, j: (0, 0)),
                  pl.BlockSpec((D_MODEL, tn), lambda i, j: (0, j)),
                  pl.BlockSpec((1, tn), lambda i, j: (0, j))],
        out_specs=pl.BlockSpec((tm, tn), lambda i, j: (i, j)),
        out_shape=jax.ShapeDtypeStruct((n, PROJ_COLS), F32),
        scratch_shapes=[pltpu.VMEM((tm, D_MODEL), BF16)],
        compiler_params=_cparams(("parallel", "arbitrary")),
        name="proj",
    )(x, g, w, b)


def _merge_kernel(x_ref, oa_ref, ob_ref, oc_ref, g0_ref, g1_ref, g2_ref, wb_ref, wo_ref, o_ref):
    m = jax.nn.sigmoid(g0_ref[...]) * _dot(oa_ref[...].astype(BF16), wb_ref[0])
    m = m + jax.nn.sigmoid(g1_ref[...]) * _dot(ob_ref[...].astype(BF16), wb_ref[1])
    m = m + jax.nn.sigmoid(g2_ref[...]) * _dot(oc_ref[...].astype(BF16), wb_ref[2])
    o_ref[...] = x_ref[...] + _dot(m.astype(BF16), wo_ref[...])


def _merge(x, oa, ob, oc, pr, wb, wo, tm):
    n = x.shape[0]
    row = lambda w: pl.BlockSpec((tm, w), lambda i: (i, 0))
    gate = lambda z: pl.BlockSpec((tm, D_MODEL), lambda i: (i, COL_MERGE // D_MODEL + z))
    return pl.pallas_call(
        _merge_kernel,
        grid=(n // tm,),
        in_specs=[row(D_MODEL), row(BRANCH_WIDTH), row(BRANCH_WIDTH), row(BRANCH_WIDTH),
                  gate(0), gate(1), gate(2),
                  pl.BlockSpec((N_BRANCH, BRANCH_WIDTH, D_MODEL), lambda i: (0, 0, 0)),
                  pl.BlockSpec((D_MODEL, D_MODEL), lambda i: (0, 0))],
        out_specs=row(D_MODEL),
        out_shape=jax.ShapeDtypeStruct((n, D_MODEL), F32),
        compiler_params=_cparams(("parallel",)),
        name="merge",
    )(x, oa, ob, oc, pr, pr, pr, wb, wo)


def _mlp_kernel(x_ref, g_ref, w1_ref, w2_ref, o_ref, h_ref, acc_ref):
    f = pl.program_id(1)

    @pl.when(f == 0)
    def _():
        h_ref[...] = _rms(x_ref[...], g_ref[...]).astype(BF16)
        acc_ref[...] = jnp.zeros_like(acc_ref)

    a = jnp.maximum(_dot(h_ref[...], w1_ref[...]), 0.0)
    acc_ref[...] += _dot((a * a).astype(BF16), w2_ref[...])

    @pl.when(f == pl.num_programs(1) - 1)
    def _():
        o_ref[...] = x_ref[...] + acc_ref[...]


def _mlp(x, g, w1, w2, tm, tf):
    n = x.shape[0]
    return pl.pallas_call(
        _mlp_kernel,
        grid=(n // tm, D_FF // tf),
        in_specs=[pl.BlockSpec((tm, D_MODEL), lambda i, f: (i, 0)),
                  pl.BlockSpec((1, D_MODEL), lambda i, f: (0, 0)),
                  pl.BlockSpec((D_MODEL, tf), lambda i, f: (0, f)),
                  pl.BlockSpec((tf, D_MODEL), lambda i, f: (f, 0))],
        out_specs=pl.BlockSpec((tm, D_MODEL), lambda i, f: (i, 0)),
        out_shape=jax.ShapeDtypeStruct((n, D_MODEL), F32),
        scratch_shapes=[pltpu.VMEM((tm, D_MODEL), BF16), pltpu.VMEM((tm, D_MODEL), F32)],
        compiler_params=_cparams(("parallel", "arbitrary")),
        name="mlp",
    )(x, g, w1, w2)


def _norm_kernel(x_ref, g_ref, o_ref):
    o_ref[...] = _rms(x_ref[...], g_ref[...])


def _final_norm(x, g, tm):
    n = x.shape[0]
    return pl.pallas_call(
        _norm_kernel,
        grid=(n // tm,),
        in_specs=[pl.BlockSpec((tm, D_MODEL), lambda i: (i, 0)),
                  pl.BlockSpec((1, D_MODEL), lambda i: (0, 0))],
        out_specs=pl.BlockSpec((tm, D_MODEL), lambda i: (i, 0)),
        out_shape=jax.ShapeDtypeStruct((n, D_MODEL), F32),
        compiler_params=_cparams(("parallel",)),
        name="final_norm",
    )(x, g)


_XB = 8


def _lru_kernel(x_ref, gb_ref, conv0_ref, h0_ref, cw_ref, cb_ref, gw_ref, gbias_ref, lam_ref,
                o_ref, convn_ref, hn_ref, xbuf, hcar, *, tc, tv):
    @pl.when(pl.program_id(1) == 0)
    def _():
        xbuf[0:_XB, :] = jnp.zeros((_XB, LRU_WIDTH), F32)
        xbuf[_XB - 3:_XB, :] = conv0_ref[0]
        hcar[...] = h0_ref[0]

    x = x_ref[0]
    xbuf[_XB:_XB + tc, :] = x
    w = cw_ref[...]
    xc = cb_ref[...] + xbuf[_XB - 3:_XB - 3 + tc, :] * w[0:1]
    xc = xc + xbuf[_XB - 2:_XB - 2 + tc, :] * w[1:2]
    xc = xc + xbuf[_XB - 1:_XB - 1 + tc, :] * w[2:3]
    xc = xc + x * w[3:4]
    tail = xbuf[_XB - 3 + tv:_XB + tv, :]
    convn_ref[0] = tail
    xbuf[_XB - 3:_XB, :] = tail

    xcb = xc.astype(BF16)
    r = jax.nn.sigmoid(_dot(xcb, gw_ref[0]) + gbias_ref[0:1])
    i = jax.nn.sigmoid(_dot(xcb, gw_ref[1]) + gbias_ref[1:2])
    log_a = (-LRU_C * r) * _softplus(-lam_ref[...])
    a = jnp.exp(log_a)
    b = jnp.sqrt(-jnp.tanh(log_a) * (a * a + 1.0)) * (i * xc)

    pos = lax.broadcasted_iota(jnp.int32, (tc, LRU_WIDTH), 0) & (LRU_SEG - 1)
    s = 1
    while s < LRU_SEG:
        m = pos >= s
        b = jnp.where(m, a * pltpu.roll(b, s, 0) + b, b)
        a = jnp.where(m, a * pltpu.roll(a, s, 0), a)
        s *= 2
    h_in = hcar[...]
    groups = []
    for j in range(tc // LRU_SEG):
        hj = a[j * LRU_SEG:(j + 1) * LRU_SEG] * h_in + b[j * LRU_SEG:(j + 1) * LRU_SEG]
        groups.append(hj)
        h_in = hj[LRU_SEG - 1:LRU_SEG]
    h = jnp.concatenate(groups, axis=0)
    hlast = h[tv - 1:tv]
    hcar[...] = hlast
    hn_ref[0] = hlast
    o_ref[0] = _gelu(gb_ref[0]) * h


def _lru(x_arr, x_blk, gb_arr, gb_blk, conv0, h0, cw, cb, gw, gbias, lam, tc, tv):
    b, t = x_arr.shape[:2]
    r = LRU_WIDTH
    const2 = lambda shape: pl.BlockSpec(shape, lambda i, c: (0, 0))
    return pl.pallas_call(
        functools.partial(_lru_kernel, tc=tc, tv=tv),
        grid=(b, t // tc),
        in_specs=[pl.BlockSpec((1, tc, r), lambda i, c: (i, c, x_blk)),
                  pl.BlockSpec((1, tc, r), lambda i, c: (i, c, gb_blk)),
                  pl.BlockSpec((1, CONV_WIDTH - 1, r), lambda i, c: (i, 0, 0)),
                  pl.BlockSpec((1, 1, r), lambda i, c: (i, 0, 0)),
                  const2((CONV_WIDTH, r)), const2((1, r)),
                  pl.BlockSpec((2, r, r), lambda i, c: (0, 0, 0)),
                  const2((2, r)), const2((1, r))],
        out_specs=[pl.BlockSpec((1, tc, r), lambda i, c: (i, c, 0)),
                   pl.BlockSpec((1, CONV_WIDTH - 1, r), lambda i, c: (i, 0, 0)),
                   pl.BlockSpec((1, 1, r), lambda i, c: (i, 0, 0))],
        out_shape=[jax.ShapeDtypeStruct((b, t, r), F32),
                   jax.ShapeDtypeStruct((b, CONV_WIDTH - 1, r), F32),
                   jax.ShapeDtypeStruct((b, 1, r), F32)],
        scratch_shapes=[pltpu.VMEM((_XB + tc, r), F32), pltpu.VMEM((1, r), F32)],
        compiler_params=_cparams(("parallel", "arbitrary")),
        name="rglru",
    )(x_arr, gb_arr, conv0, h0, cw, cb, gw, gbias, lam)


LRU_SEG = 8


def _lru_short_kernel(x_ref, gb_ref, h0_ref, cw_ref, cb_ref, gw_ref, gbias_ref, lam_ref, o_ref, h_ref, *, t):
    x = x_ref[...]
    pos = lax.broadcasted_iota(jnp.int32, x.shape, 0) & (LRU_SEG - 1)
    w = cw_ref[...]
    xc = cb_ref[...] + x * w[CONV_WIDTH - 1:CONV_WIDTH]
    for k in range(1, CONV_WIDTH):
        xc = xc + pltpu.roll(x, k, 0) * w[CONV_WIDTH - 1 - k:CONV_WIDTH - k]
    xcb = xc.astype(BF16)
    r = jax.nn.sigmoid(_dot(xcb, gw_ref[0]) + gbias_ref[0:1])
    i = jax.nn.sigmoid(_dot(xcb, gw_ref[1]) + gbias_ref[1:2])
    log_a = (-LRU_C * r) * _softplus(-lam_ref[...])
    a = jnp.exp(log_a)
    b = jnp.sqrt(-jnp.tanh(log_a) * (a * a + 1.0)) * (i * xc)
    real = (pos >= CONV_WIDTH - 1) & (pos < CONV_WIDTH - 1 + t)
    a = jnp.where(real, a, 1.0)
    b = jnp.where(real, b, 0.0)
    s = 1
    while s < LRU_SEG:
        m = pos >= s
        b = jnp.where(m, a * pltpu.roll(b, s, 0) + b, b)
        a = jnp.where(m, a * pltpu.roll(a, s, 0), a)
        s *= 2
    h = a * h0_ref[...] + b
    h_ref[...] = h
    o_ref[...] = _gelu(gb_ref[...]) * h


def _lru_short(x, gb, conv0, h0, cw, cb, gw, gbias, lam):
    b, t, r = x.shape
    lead = CONV_WIDTH - 1
    seg = lambda head, body: jnp.concatenate(
        [head, body, jnp.zeros((b, LRU_SEG - lead - t, r), F32)], axis=1).reshape(b * LRU_SEG, r)
    xin = seg(conv0, x)
    rows = b * LRU_SEG
    tm = _row_tile(rows, 256)
    blk = pl.BlockSpec((tm, r), lambda i: (i, 0))
    const2 = lambda shape: pl.BlockSpec(shape, lambda i: (0, 0))
    o, h = pl.pallas_call(
        functools.partial(_lru_short_kernel, t=t),
        grid=(rows // tm,),
        in_specs=[blk, blk, blk, const2((CONV_WIDTH, r)), const2((1, r)),
                  pl.BlockSpec((2, r, r), lambda i: (0, 0, 0)), const2((2, r)), const2((1, r))],
        out_specs=[blk, blk],
        out_shape=[jax.ShapeDtypeStruct((rows, r), F32), jax.ShapeDtypeStruct((rows, r), F32)],
        compiler_params=_cparams(("parallel",)),
        name="rglru_short",
    )(xin, seg(jnp.zeros((b, lead, r), F32), gb), jnp.repeat(h0, LRU_SEG, axis=0), cw, cb, gw, gbias, lam)
    o = o.reshape(b, LRU_SEG, r)[:, lead:lead + t]
    h_new = h.reshape(b, LRU_SEG, r)[:, lead + t - 1]
    conv_new = xin.reshape(b, LRU_SEG, r)[:, t:t + lead]
    return o, conv_new, h_new


def _gla_kernel(q_ref, k_ref, v_ref, og_ref, misc_ref, aw_ref, ab_ref, ng_ref, s0_ref,
                o_ref, sn_ref, st, *, tg, ck, tv):
    @pl.when(pl.program_id(1) == 0)
    def _():
        for h in range(GLA_HEADS):
            st[h] = s0_ref[0, h].T

    pre = _dot(misc_ref[0].astype(BF16), aw_ref[...]) + ab_ref[...]
    g = -_softplus(-pre) * (1.0 / GLA_TAU)
    if tv < tg:
        g = jnp.where(lax.broadcasted_iota(jnp.int32, g.shape, 0) < tv, g, 0.0)
    q = q_ref[0] * (GLA_DK ** -0.5)
    k = k_ref[0]
    v = v_ref[0]
    og = og_ref[0]
    ng = ng_ref[...]
    rows = lax.broadcasted_iota(jnp.int32, (ck, GLA_QK_W), 0)
    tril = (lax.broadcasted_iota(jnp.int32, (ck, ck), 0) >= lax.broadcasted_iota(jnp.int32, (ck, ck), 1))
    for c in range(tg // ck):
        sl = slice(c * ck, (c + 1) * ck)
        bc = g[sl]
        s = 1
        while s < ck:
            bc = bc + jnp.where(rows >= s, pltpu.roll(bc, s, 0), 0.0)
            s *= 2
        bl = bc[ck - 1:ck]
        e = jnp.exp(bc)
        qi = (q[sl] * e).astype(BF16)
        ki = (k[sl] * jnp.exp(-bc)).astype(BF16)
        kd = (k[sl] * jnp.exp(bl - bc)).astype(BF16)
        dec = jnp.exp(bl)
        vb = v[sl].astype(BF16)
        outs = []
        for h in range(GLA_HEADS):
            ks = slice(h * GLA_DK, (h + 1) * GLA_DK)
            vs = slice(h * GLA_DV, (h + 1) * GLA_DV)
            att = jnp.where(tril, _dot_nt(qi[:, ks], ki[:, ks]), 0.0)
            s_prev = st[h]
            o = _dot(att.astype(BF16), vb[:, vs]) + _dot_nt(qi[:, ks], s_prev.astype(BF16))
            st[h] = s_prev * dec[:, ks] + _dot_tn(vb[:, vs], kd[:, ks])
            o = _rms(o, ng[:, vs])
            ogh = og[sl, vs]
            outs.append(o * (ogh * jax.nn.sigmoid(ogh)))
        o_ref[0, sl, :] = jnp.concatenate(outs, axis=-1)

    @pl.when(pl.program_id(1) == pl.num_programs(1) - 1)
    def _():
        for h in range(GLA_HEADS):
            sn_ref[0, h] = st[h].T


def _gla(q_arr, q_blk, k_arr, k_blk, v_arr, v_blk, og_arr, og_blk, misc_arr, misc_blk,
         aw, ab, ng, s0, tg, ck, tv):
    b, t = q_arr.shape[:2]
    col = lambda w, blk: pl.BlockSpec((1, tg, w), lambda i, c: (i, c, blk))
    const2 = lambda shape: pl.BlockSpec(shape, lambda i, c: (0, 0))
    state = pl.BlockSpec((1, GLA_HEADS, GLA_DK, GLA_DV), lambda i, c: (i, 0, 0, 0))
    return pl.pallas_call(
        functools.partial(_gla_kernel, tg=tg, ck=ck, tv=tv),
        grid=(b, t // tg),
        in_specs=[col(GLA_QK_W, q_blk), col(GLA_QK_W, k_blk), col(GLA_V_W, v_blk), col(GLA_V_W, og_blk),
                  col(MISC_W, misc_blk), const2((MISC_W, GLA_QK_W)), const2((1, GLA_QK_W)),
                  const2((1, GLA_V_W)), state],
        out_specs=[pl.BlockSpec((1, tg, GLA_V_W), lambda i, c: (i, c, 0)), state],
        out_shape=[jax.ShapeDtypeStruct((b, t, GLA_V_W), F32),
                   jax.ShapeDtypeStruct((b, GLA_HEADS, GLA_DK, GLA_DV), F32)],
        scratch_shapes=[pltpu.VMEM((GLA_HEADS, GLA_DV, GLA_DK), F32)],
        compiler_params=_cparams(("parallel", "arbitrary")),
        name="gla",
    )(q_arr, k_arr, v_arr, og_arr, misc_arr, aw, ab, ng, s0)


GLA_SHORT_SEQS = 16


def _gla_short_kernel(q_ref, k_ref, v_ref, og_ref, misc_ref, aw_ref, ab_ref, ng_ref, s0_ref, o_ref, sn_ref,
                      *, nb, tv):
    seg = LRU_SEG
    rows = nb * seg
    flat = lambda ref: ref[...].reshape(rows, ref.shape[-1])
    part = lambda x, n: x[n * seg:(n + 1) * seg]
    pre = _dot(flat(misc_ref).astype(BF16), aw_ref[...]) + ab_ref[...]
    pos = lax.broadcasted_iota(jnp.int32, (rows, GLA_QK_W), 0) & (seg - 1)
    bc = jnp.where(pos < tv, -_softplus(-pre) * (1.0 / GLA_TAU), 0.0)
    s = 1
    while s < seg:
        bc = bc + jnp.where(pos >= s, pltpu.roll(bc, s, 0), 0.0)
        s *= 2
    last = bc.reshape(nb, seg, GLA_QK_W)[:, seg - 1:seg, :]
    bl = jnp.broadcast_to(last, (nb, seg, GLA_QK_W)).reshape(rows, GLA_QK_W)
    dec_t = jnp.exp(last.reshape(nb, GLA_QK_W)).T
    q = flat(q_ref) * (GLA_DK ** -0.5)
    k = flat(k_ref)
    qi = q * jnp.exp(bc)
    ki = k * jnp.exp(-bc)
    kd = k * jnp.exp(bl - bc)
    v = flat(v_ref)
    tril = lax.broadcasted_iota(jnp.int32, (seg, seg), 0) >= lax.broadcasted_iota(jnp.int32, (seg, seg), 1)
    pairs = [(n, h) for n in range(nb) for h in range(GLA_HEADS)]
    ks = lambda h: slice(h * GLA_DK, (h + 1) * GLA_DK)
    vs = lambda h: slice(h * GLA_DV, (h + 1) * GLA_DV)
    bf = lambda x: x.astype(BF16)
    qs = [bf(part(qi, n)[:, ks(h)]) for n, h in pairs]
    vb = [bf(part(v, n)[:, vs(h)]) for n, h in pairs]
    att = [bf(jnp.where(tril, _dot_nt(qp, bf(part(ki, n)[:, ks(h)])), 0.0)) for qp, (n, h) in zip(qs, pairs)]
    o = [_dot(a, vp) + _dot(qp, bf(s0_ref[n, h])) for a, vp, qp, (n, h) in zip(att, vb, qs, pairs)]
    for vp, (n, h) in zip(vb, pairs):
        sn_ref[n, h] = s0_ref[n, h] * dec_t[ks(h), n:n + 1] + _dot_tn(bf(part(kd, n)[:, ks(h)]), vp)
    o = jnp.concatenate([jnp.concatenate(o[n * GLA_HEADS:(n + 1) * GLA_HEADS], axis=1) for n in range(nb)], axis=0)
    ng = ng_ref[...]
    o = jnp.concatenate([_rms(o[:, vs(h)], ng[:, vs(h)]) for h in range(GLA_HEADS)], axis=1)
    og = flat(og_ref)
    o_ref[...] = (o * (og * jax.nn.sigmoid(og))).reshape(nb, seg, GLA_V_W)


def _gla_short(q, k, v, og, misc, aw, ab, ng, s0, tv):
    b = q.shape[0]
    nb = next((c for c in (GLA_SHORT_SEQS, 8) if b % c == 0), b)
    seq = lambda w: pl.BlockSpec((nb, LRU_SEG, w), lambda i: (i, 0, 0))
    const2 = lambda shape: pl.BlockSpec(shape, lambda i: (0, 0))
    state = pl.BlockSpec((nb, GLA_HEADS, GLA_DK, GLA_DV), lambda i: (i, 0, 0, 0))
    return pl.pallas_call(
        functools.partial(_gla_short_kernel, nb=nb, tv=tv),
        grid=(b // nb,),
        in_specs=[seq(GLA_QK_W), seq(GLA_QK_W), seq(GLA_V_W), seq(GLA_V_W), seq(MISC_W),
                  const2((MISC_W, GLA_QK_W)), const2((1, GLA_QK_W)), const2((1, GLA_V_W)), state],
        out_specs=[seq(GLA_V_W), state],
        out_shape=[jax.ShapeDtypeStruct((b, LRU_SEG, GLA_V_W), F32),
                   jax.ShapeDtypeStruct((b, GLA_HEADS, GLA_DK, GLA_DV), F32)],
        compiler_params=_cparams(("parallel",)),
        name="gla_short",
    )(q, k, v, og, misc, aw, ab, ng, s0)


def _bias_kernel(tbl_ref, dist_ref, o_ref):
    h = pl.program_id(0)
    max_exact = REL_BUCKETS // 2
    n = jnp.maximum(dist_ref[...], 0)
    nf = jnp.maximum(n, 1).astype(F32)
    large = max_exact + (jnp.log(nf / max_exact) / math.log(REL_MAX_DIST / max_exact)
                         * (REL_BUCKETS - max_exact)).astype(jnp.int32)
    bucket = jnp.where(n < max_exact, n, jnp.minimum(large, REL_BUCKETS - 1))
    out = jnp.zeros(bucket.shape, F32)
    for kk in range(REL_BUCKETS):
        out = jnp.where(bucket == kk, tbl_ref[kk, h], out)
    o_ref[0] = out


def _bias_lookup(rel_bias, dist, tr):
    r, c = dist.shape
    return pl.pallas_call(
        _bias_kernel,
        grid=(NSA_HEADS, r // tr),
        in_specs=[pl.BlockSpec(memory_space=pltpu.SMEM),
                  pl.BlockSpec((tr, c), lambda h, i: (i, 0))],
        out_specs=pl.BlockSpec((1, tr, c), lambda h, i: (h, i, 0)),
        out_shape=jax.ShapeDtypeStruct((NSA_HEADS, r, c), F32),
        compiler_params=_cparams(("parallel", "parallel")),
        name="rel_bias",
    )(rel_bias, dist)


def _cmp_kernel(x_ref, pe_ref, w1_ref, w2_ref, o_ref):
    hid = _gelu(_dot((x_ref[0] + pe_ref[0]).astype(BF16), w1_ref[0]))
    o_ref[0] = _dot(hid.astype(BF16), w2_ref[0])


def _row_tile(m, cap):
    for tm in range(min(cap, m) // 8 * 8, 0, -8):
        if m % tm == 0:
            return tm
    return m


def _compress(flat, pe, w1, w2, cap):
    m = flat.shape[1]
    tm = _row_tile(m, cap)
    fw = CMP_BLOCK * NSA_HEAD_DIM
    d = NSA_HEAD_DIM
    return pl.pallas_call(
        _cmp_kernel,
        grid=(2, m // tm),
        in_specs=[pl.BlockSpec((1, tm, fw), lambda z, i: (z, i, 0)),
                  pl.BlockSpec((1, 1, fw), lambda z, i: (z, 0, 0)),
                  pl.BlockSpec((1, fw, d), lambda z, i: (z, 0, 0)),
                  pl.BlockSpec((1, d, d), lambda z, i: (z, 0, 0))],
        out_specs=pl.BlockSpec((1, tm, d), lambda z, i: (z, i, 0)),
        out_shape=jax.ShapeDtypeStruct((2, m, d), F32),
        compiler_params=_cparams(("parallel", "parallel")),
        name="nsa_compress",
    )(flat, pe, w1, w2)


CMP_PITCH = CMP_BLOCK + 4


def _cmp_pages_kernel(pt_ref, *refs, nb, n_pages):
    del pt_ref
    pages = refs[:nb * n_pages]
    pe_ref, w1_ref, w2_ref, o_ref, x_scr, h_scr = refs[nb * n_pages:]
    gd = NSA_KV_HEADS * NSA_HEAD_DIM
    per_page = PAGE_SIZE // CMP_BLOCK
    nblk = n_pages * per_page
    for z in range(2):
        for n in range(nb):
            for p in range(n_pages):
                xt = pages[n * n_pages + p][0, 0, z].reshape(gd, PAGE_SIZE).T
                for j in range(per_page):
                    r0 = (n * nblk + p * per_page + j) * CMP_PITCH
                    x_scr[z, r0:r0 + CMP_BLOCK, :] = xt[j * CMP_BLOCK:(j + 1) * CMP_BLOCK]
        acc = jnp.zeros((nb * nblk, gd), F32)
        for t in range(CMP_BLOCK):
            rows = x_scr[z, pl.ds(t, nb * nblk, stride=CMP_PITCH), :]
            acc = acc + _dot((rows + pe_ref[z, t:t + 1, :]).astype(BF16), w1_ref[z, t])
        h_scr[z] = _dot(_gelu(acc).astype(BF16), w2_ref[z])
        for n in range(nb):
            for par in range(2):
                o_ref[n, z, par] = h_scr[z, pl.ds(n * nblk + par, nblk // 2, stride=2), :]


def _compress_pages(layer, page_table, cache, pe, w1, w2):
    b, n_pages = page_table.shape
    g, d = NSA_KV_HEADS, NSA_HEAD_DIM
    gd = g * d
    nblk = n_pages * PAGE_SIZE // CMP_BLOCK
    eye = jnp.eye(g, dtype=w1.dtype)
    w1 = jnp.einsum('gh,ztje->ztgjhe', eye, w1).reshape(2, CMP_BLOCK, gd, gd)
    w2 = jnp.einsum('gh,zje->zgjhe', eye, w2).reshape(2, gd, gd)
    pe = jnp.tile(pe, (1, 1, g))
    const = lambda *shape: pl.BlockSpec(shape, lambda i, pt: (0,) * len(shape))
    nb = SMP_SEQS if b % SMP_SEQS == 0 else 1

    def page_spec(n, p):
        return pl.BlockSpec((1, 1, 2, g, d, PAGE_SIZE), lambda i, pt: (layer, pt[nb * i + n, p], 0, 0, 0, 0))

    out = pl.pallas_call(
        functools.partial(_cmp_pages_kernel, nb=nb, n_pages=n_pages),
        grid_spec=pltpu.PrefetchScalarGridSpec(
            num_scalar_prefetch=1,
            grid=(b // nb,),
            in_specs=[page_spec(n, p) for n in range(nb) for p in range(n_pages)]
            + [const(2, CMP_BLOCK, gd), const(2, CMP_BLOCK, gd, gd), const(2, gd, gd)],
            out_specs=pl.BlockSpec((nb, 2, 2, nblk // 2, gd), lambda i, pt: (i, 0, 0, 0, 0)),
            scratch_shapes=[pltpu.VMEM((2, nb * nblk * CMP_PITCH, gd), F32), pltpu.VMEM((2, nb * nblk, gd), F32)]),
        out_shape=jax.ShapeDtypeStruct((b, 2, 2, nblk // 2, gd), F32),
        compiler_params=_cparams(("parallel",)),
        name="nsa_compress_pages",
    )(page_table, *([cache] * (nb * n_pages)), pe, w1, w2)
    return out.reshape(b, 2, 2, nblk // 2, g, d).transpose(0, 1, 4, 2, 3, 5)


def _nsa_prompt_kernel(q_ref, kc_ref, vc_ref, bc_ref, ks_ref, vs_ref, kw_ref, vw_ref, bt_ref, e_ref, gt_ref,
                       o_ref, s_scr, p_scr, m_scr, selk_scr, os_scr, ow_scr):
    i = pl.program_id(2)
    tq, d, ck, hpg = NSA_TQ, NSA_HEAD_DIM, NSA_TQ, NSA_HPG
    nchunk = ks_ref.shape[3]
    nsel = kc_ref.shape[2] // 2
    neg = -jnp.inf
    qf = q_ref[0] * (d ** -0.5 * LOG2E)
    qs = jnp.concatenate([qf[:, h * d:(h + 1) * d] for h in range(hpg)], axis=0).astype(BF16)

    tk = lax.broadcasted_iota(jnp.int32, (tq, ck), 0) - lax.broadcasted_iota(jnp.int32, (tq, ck), 1)

    def mask_heads(s, ok):
        return jnp.concatenate([jnp.where(ok, s[h * tq:(h + 1) * tq], neg) for h in range(hpg)], axis=0)

    def attend(k_ref, v_ref, n, ok_fn):
        m_scr[...] = jnp.full(m_scr.shape, neg, F32)
        for r in range(n):
            c = i - r
            cc = jnp.maximum(c, 0)
            s = _dot(qs, k_ref[0, 0, 0, cc])
            if r < 2:
                s = s + bt_ref[0, r]
            s = mask_heads(s, ok_fn(r, c, cc))
            s_scr[:, r * ck:(r + 1) * ck] = s
            m_scr[...] = jnp.maximum(m_scr[...], s)
        m = jnp.max(m_scr[...], axis=1, keepdims=True)
        m = jnp.where(m > neg, m, 0.0)
        for r in range(n):
            p_scr[:, r * ck:(r + 1) * ck] = jnp.exp2(s_scr[:, r * ck:(r + 1) * ck] - m).astype(BF16)
        vt = jnp.concatenate([v_ref[0, 0, 0, jnp.maximum(i - r, 0)] for r in range(n)], axis=1)
        vt = jnp.concatenate([vt, jnp.ones((ONES_ROWS, n * ck), BF16)], axis=0)
        acc = _dot_nt(p_scr[:, 0:n * ck], vt)
        return acc[:, 0:d] / jnp.maximum(acc[:, d:d + 1], 1e-30)

    nw = WINDOW // ck + 1

    def win_ok(r, c, cc):
        if r == 0:
            return tk >= 0
        if r == nw - 1:
            return tk < jnp.where(c >= 0, 0, -tq)
        return tk > jnp.where(c >= 0, -tq, tq)

    ow_scr[...] = attend(kw_ref, vw_ref, nw, win_ok)

    t_lane = i * tq + (lax.broadcasted_iota(jnp.int32, (1, hpg * tq), 1) & (tq - 1))
    s_c = _dot_nt(kc_ref[0, 0], qs) + bc_ref[0, 0]
    r_c = lax.broadcasted_iota(jnp.int32, (2 * nsel, hpg * tq), 0)
    n_c = jnp.where(r_c < nsel, 2 * r_c, 2 * (r_c - nsel) + 1)
    s_c = jnp.where(t_lane >= (n_c + 1) * CMP_BLOCK - 1, s_c, neg)
    m_c = jnp.max(s_c, axis=0, keepdims=True)
    m_c = jnp.where(m_c > neg, m_c, 0.0)
    p_c = jnp.exp2(s_c - m_c)
    p_c = p_c / jnp.maximum(jnp.sum(p_c, axis=0, keepdims=True), 1e-30)
    o_c = _dot_tn(p_c.astype(BF16), vc_ref[0, 0])

    ph = p_c[:, 0:tq]
    for h in range(1, hpg):
        ph = ph + p_c[:, h * tq:(h + 1) * tq]
    imp = ph[0:nsel] + ph[nsel:2 * nsel]
    blk = lax.broadcasted_iota(jnp.int32, (nsel, tq), 0)
    cur = (i * tq + lax.broadcasted_iota(jnp.int32, (nsel, tq), 1)) >> int(math.log2(SEL_BLOCK))
    forced = (blk == 0) | (blk == cur) | (blk == cur - 1)
    imp = jnp.where(forced, FORCE_SCORE, jnp.where(blk <= cur, imp, neg))
    rank = jnp.zeros((nsel, tq), jnp.int32)
    for s2 in range(nsel):
        row = imp[s2:s2 + 1]
        beats = (row > imp) | ((row == imp) & (blk > s2))
        rank = rank + beats.astype(jnp.int32)
    chosen = (rank < N_SELECT).astype(BF16)
    selk = _dot_tn(chosen, e_ref[...])
    for c in range(nchunk):
        selk_scr[c] = selk[:, c * ck:(c + 1) * ck]

    def sel_ok(r, c, cc):
        ok = selk_scr[cc] > jnp.where(c >= 0, 0.5, 2.0)
        return ok & (tk >= 0) if r == 0 else ok

    sizes = list(range(NSA_CLASS, nchunk, NSA_CLASS)) + [nchunk]
    for lo, n in zip([0] + sizes[:-1], sizes):
        @pl.when((i >= lo) & (i < n))
        def _(n=n):
            os_scr[...] = attend(ks_ref, vs_ref, n, sel_ok)

    o_w = ow_scr[...]
    o_s = os_scr[...]

    sig = jax.nn.sigmoid(gt_ref[0, 0])
    outs = []
    for h in range(hpg):
        r = slice(h * tq, (h + 1) * tq)
        outs.append(sig[:, h:h + 1] * o_c[r] + sig[:, hpg + h:hpg + h + 1] * o_s[r]
                    + sig[:, 2 * hpg + h:2 * hpg + h + 1] * o_w[r])
    o_ref[0] = jnp.concatenate(outs, axis=1)


def _nsa_prompt(pr3, kc, vc, bias_c, kvt, bias_t, expand, gts):
    b, t = pr3.shape[:2]
    g, d, tq, hpg = NSA_KV_HEADS, NSA_HEAD_DIM, NSA_TQ, NSA_HPG
    nt = t // tq
    nc = kc.shape[2]
    rows = hpg * tq
    per_bg = lambda r, c: pl.BlockSpec((1, 1, r, c), lambda bi, gi, i: (bi, gi, 0, 0))
    kv = lambda br, z: pl.BlockSpec((None, 1, 1, 1, nt, d, tq), lambda bi, gi, i: (br, bi, z, gi, 0, 0, 0))
    return pl.pallas_call(
        _nsa_prompt_kernel,
        grid=(b, g, nt),
        in_specs=[pl.BlockSpec((1, tq, GROUP_W), lambda bi, gi, i: (bi, i, COL_Q // GROUP_W + gi)),
                  per_bg(nc, d), per_bg(nc, d),
                  pl.BlockSpec((1, 1, nc, rows), lambda bi, gi, i: (gi, i, 0, 0)),
                  kv(0, 0), kv(0, 1), kv(1, 0), kv(1, 1),
                  pl.BlockSpec((1, 2, rows, tq), lambda bi, gi, i: (gi, 0, 0, 0)),
                  pl.BlockSpec((nc // 2, t), lambda bi, gi, i: (0, 0)),
                  pl.BlockSpec((1, 1, tq, 128), lambda bi, gi, i: (bi, gi, i, 0))],
        out_specs=pl.BlockSpec((1, tq, GROUP_W), lambda bi, gi, i: (bi, i, gi)),
        out_shape=jax.ShapeDtypeStruct((b, t, Q_W), F32),
        scratch_shapes=[pltpu.VMEM((rows, t), F32), pltpu.VMEM((rows, t), BF16),
                        pltpu.VMEM((rows, tq), F32),
                        pltpu.VMEM((nt, tq, tq), F32), pltpu.VMEM((rows, d), F32), pltpu.VMEM((rows, d), F32)],
        compiler_params=_cparams(("parallel", "parallel", "arbitrary")),
        name="nsa_prompt",
    )(pr3, kc, vc, bias_c, kvt, kvt, kvt, kvt, bias_t, expand, gts)


SMP_CPAD = 128
KEY_TILE = PAGE_SIZE


def _masked_softmax(s, ok):
    s = jnp.where(ok, s, -jnp.inf)
    m = jnp.max(s, axis=-1, keepdims=True)
    m = jnp.where(m > -jnp.inf, m, 0.0)
    p = jnp.exp(s - m)
    return p / jnp.maximum(jnp.sum(p, axis=-1, keepdims=True), 1e-30)


def _nsa_sample_kernel(pt_ref, q_ref, kc_ref, vc_ref, bc_ref, okc_ref, cur_ref, *rest, nb, n_pages, nsel):
    pages = rest[:nb * n_pages]
    (sn_ref, bs_ref, oks_ref, e_ref, wb_ref, wn_ref, bw_ref, okw_ref, gt_ref, acc_ref,
     o_ref, wo_ref) = rest[nb * n_pages:]
    del pt_ref, acc_ref
    d, hpg = NSA_HEAD_DIM, NSA_HPG
    rows = q_ref.shape[2]
    s_new = rows // hpg
    chains = [(n, g) for n in range(nb) for g in range(NSA_KV_HEADS)]
    bf = lambda x: x.astype(BF16)
    stack = lambda parts: jnp.concatenate(parts, axis=0)
    tile = lambda x: stack([x] * len(chains))
    part = lambda x, c: x[c * rows:(c + 1) * rows]
    qs = [bf(q_ref[n, g] * (d ** -0.5)) for n, g in chains]

    s_c = stack([_dot_nt(q, bf(kc_ref[n, g])) + bc_ref[g] for q, (n, g) in zip(qs, chains)])
    p_c = _masked_softmax(s_c, tile(okc_ref[...]) > 0.5)
    o_c = stack([_dot(bf(part(p_c, c)), bf(vc_ref[n, g])) for c, (n, g) in enumerate(chains)])

    ph = []
    for c in range(len(chains)):
        acc = p_c[c * rows:c * rows + s_new]
        for h in range(1, hpg):
            acc = acc + p_c[c * rows + h * s_new:c * rows + (h + 1) * s_new]
        ph.append(acc)
    ph = stack(ph)
    imp = ph[:, 0:SMP_CPAD] + ph[:, SMP_CPAD:2 * SMP_CPAD]
    blk = lax.broadcasted_iota(jnp.int32, imp.shape, 1)
    cur = tile(cur_ref[...])
    forced = (blk == 0) | (blk == cur) | (blk == cur - 1)
    imp = jnp.where(forced, FORCE_SCORE, jnp.where(blk <= cur, imp, -jnp.inf))
    rank = jnp.zeros(imp.shape, jnp.int32)
    for s2 in range(nsel):
        col = imp[:, s2:s2 + 1]
        beats = (col > imp) | ((col == imp) & (blk > s2))
        rank = rank + beats.astype(jnp.int32)
    chosen = bf((rank < N_SELECT) & (blk < nsel))
    chosen = stack([chosen[c * s_new:(c + 1) * s_new] for c in range(len(chains)) for _ in range(hpg)])
    sel_keys = _dot(chosen, e_ref[...])

    def kv_tiles(n, g, z):
        return [pg[0, 0, z, g] for pg in pages[n * n_pages:(n + 1) * n_pages]] + [sn_ref[n, z, g]]

    s_s = stack([jnp.concatenate([_dot(q, bf(kt)) for kt in kv_tiles(n, g, 0)], axis=1) + bs_ref[g]
                 for q, (n, g) in zip(qs, chains)])
    p_s = bf(_masked_softmax(s_s, (tile(oks_ref[...]) > 0.5) & (sel_keys > 0.5)))
    o_s = []
    for c, (n, g) in enumerate(chains):
        pc = part(p_s, c)
        acc = None
        for p, vt in enumerate(kv_tiles(n, g, 1)):
            term = _dot_nt(pc[:, p * KEY_TILE:(p + 1) * KEY_TILE], bf(vt))
            acc = term if acc is None else acc + term
        o_s.append(acc)
    o_s = stack(o_s)

    wkeys = wb_ref.shape[-1]
    s_w = stack([jnp.concatenate([_dot(q, bf(wb_ref[0, n, 0, g])), _dot(q, bf(wn_ref[n, 0, g]))], axis=1)
                 + bw_ref[g] for q, (n, g) in zip(qs, chains)])
    p_w = bf(_masked_softmax(s_w, tile(okw_ref[...]) > 0.5))
    o_w = stack([_dot_nt(part(p_w, c)[:, 0:wkeys], bf(wb_ref[0, n, 1, g]))
                 + _dot_nt(part(p_w, c)[:, wkeys:wkeys + KEY_TILE], bf(wn_ref[n, 1, g]))
                 for c, (n, g) in enumerate(chains)])

    gt = jax.nn.sigmoid(stack([gt_ref[n, g] for n, g in chains]))
    out = gt[:, 0:1] * o_c + gt[:, 1:2] * o_s + gt[:, 2:3] * o_w
    for c, (n, g) in enumerate(chains):
        o_ref[n, g] = part(out, c)

    lane = lax.broadcasted_iota(jnp.int32, (2 * NSA_KV_HEADS * d, KEY_TILE), 1)
    for n in range(nb):
        old = pltpu.roll(wb_ref[0, n].reshape(-1, wkeys), wkeys - s_new, 1)
        new = pltpu.roll(wn_ref[n].reshape(-1, KEY_TILE), KEY_TILE - s_new, 1)
        tail = jnp.where(lane >= KEY_TILE - s_new, new, old[:, wkeys - KEY_TILE:])
        wo_ref[0, n] = jnp.concatenate([old[:, 0:wkeys - KEY_TILE], tail], axis=1).reshape(wo_ref.shape[2:])


SMP_SEQS = 4


def _nsa_sample(layer, nsel, page_table, qg, kc, vc, bias_c, ok_c, cur, cache_sel, sel_new, bias_s, ok_s,
                expand, win_state, win_new, bias_w, ok_w, gts, win_acc):
    b = qg.shape[0]
    g = NSA_KV_HEADS
    d = NSA_HEAD_DIM
    rows = qg.shape[2]
    n_pages = page_table.shape[1]
    wkeys = win_state.shape[-1]
    ks = (n_pages + 1) * KEY_TILE
    kw = wkeys + KEY_TILE
    nb = SMP_SEQS if b % SMP_SEQS == 0 else 1
    per_b = lambda *shape: pl.BlockSpec((nb,) + shape, lambda i, pt: (i,) + (0,) * len(shape))
    const = lambda *shape: pl.BlockSpec(shape, lambda i, pt: (0,) * len(shape))

    def page_spec(n, p):
        return pl.BlockSpec((1, 1, 2, g, d, PAGE_SIZE), lambda i, pt: (layer, pt[nb * i + n, p], 0, 0, 0, 0))

    win_spec = pl.BlockSpec((1, nb, 2, g, d, wkeys), lambda i, pt: (layer, i, 0, 0, 0, 0))
    in_specs = ([per_b(g, rows, d), per_b(g, 2 * SMP_CPAD, d), per_b(g, 2 * SMP_CPAD, d),
                 const(g, rows, 2 * SMP_CPAD), const(rows, 2 * SMP_CPAD), const(rows // NSA_HPG, SMP_CPAD)]
                + [page_spec(n, p) for n in range(nb) for p in range(n_pages)]
                + [per_b(2, g, d, KEY_TILE), const(g, rows, ks), const(rows, ks), const(SMP_CPAD, ks),
                   win_spec, per_b(2, g, d, KEY_TILE), const(g, rows, kw), const(rows, kw), per_b(g, rows, 128),
                   pl.BlockSpec(memory_space=pl.ANY)])
    operands = (page_table, qg, kc, vc, bias_c, ok_c, cur, *([cache_sel] * (nb * n_pages)), sel_new, bias_s, ok_s,
                expand, win_state, win_new, bias_w, ok_w, gts, win_acc)
    return pl.pallas_call(
        functools.partial(_nsa_sample_kernel, nb=nb, n_pages=n_pages, nsel=nsel),
        grid_spec=pltpu.PrefetchScalarGridSpec(
            num_scalar_prefetch=1,
            grid=(b // nb,),
            in_specs=in_specs,
            out_specs=[pl.BlockSpec((nb, g, rows, d), lambda i, pt: (i, 0, 0, 0)), win_spec]),
        out_shape=[jax.ShapeDtypeStruct((b, g, rows, d), F32), jax.ShapeDtypeStruct(win_acc.shape, F32)],
        input_output_aliases={len(operands) - 1: 1},
        compiler_params=_cparams(("parallel",)),
        name="nsa_sample",
    )(*operands)


def _pad_rows(a, rows):
    return jnp.pad(a, ((0, 0), (0, rows - a.shape[1]), (0, 0)))


def _even_odd(n):
    return np.concatenate([np.arange(0, n, 2), np.arange(1, n, 2)])


def _prompt_bias_tables(rel_bias, t):
    g, hpg, tq = NSA_KV_HEADS, NSA_HPG, NSA_TQ
    nt = t // tq
    nc = t // CMP_BLOCK
    end_c = (_even_odd(nc) + 1) * CMP_BLOCK - 1
    dist_c = jnp.asarray(np.arange(t)[None, :] - end_c[:, None], jnp.int32)
    bias_c = _bias_lookup(rel_bias, dist_c, nc)
    bias_c = bias_c.reshape(g, hpg, nc, nt, tq).transpose(0, 3, 2, 1, 4).reshape(g, nt, nc, hpg * tq) * LOG2E
    off = np.arange(3)[:, None, None] * tq
    dist_t = off + np.arange(tq)[None, :, None] - np.arange(tq)[None, None, :]
    assert dist_t[2].min() >= REL_MAX_DIST
    bias_t = _bias_lookup(rel_bias, jnp.asarray(dist_t.reshape(3 * tq, tq), jnp.int32), 3 * tq)
    bias_t = bias_t.reshape(g, hpg, 3, tq, tq).transpose(0, 2, 1, 3, 4).reshape(g, 3, hpg * tq, tq)
    bias_t = (bias_t[:, 0:2] - bias_t[:, 2:3]) * LOG2E
    expand = np.arange(t)[None, :] // SEL_BLOCK == np.arange(t // SEL_BLOCK)[:, None]
    return bias_c, bias_t, jnp.asarray(expand, BF16)


def _sample_tables(rel_bias, past_len, s_new, n_win_keys):
    g, hpg = NSA_KV_HEADS, NSA_HPG
    tk = -(-(past_len + s_new) // SEL_BLOCK) * SEL_BLOCK
    nc = tk // CMP_BLOCK
    nsel = tk // SEL_BLOCK
    half = (nc + 1) // 2
    pos_q = past_len + np.arange(s_new)
    lane = np.arange(SMP_CPAD)
    n_of_lane = np.concatenate([2 * lane, 2 * lane + 1])
    real_c = np.concatenate([lane < half, lane < nc - half])
    dist_c = pos_q[:, None] - ((n_of_lane[None, :] + 1) * CMP_BLOCK - 1)
    ok_c = real_c[None, :] & (dist_c >= 0)
    n_keys_s = (past_len // PAGE_SIZE + 1) * KEY_TILE
    key = np.arange(n_keys_s)
    dist_s = pos_q[:, None] - key[None, :]
    ok_s = (key[None, :] < past_len + s_new) & (dist_s >= 0)
    n_keys_w = n_win_keys + KEY_TILE
    i = np.arange(n_keys_w)
    pos_kw = np.where(i < n_win_keys, past_len - n_win_keys + i, past_len + i - n_win_keys)
    dist_w = pos_q[:, None] - pos_kw[None, :]
    ok_w = (i[None, :] < n_win_keys + s_new) & (dist_w >= 0) & (dist_w < WINDOW) & (pos_kw[None, :] >= 0)
    dist = np.concatenate([dist_c, dist_s, dist_w], axis=1)
    dist = np.pad(dist, ((0, 8 - s_new), (0, 0)))
    bias = _bias_lookup(rel_bias, jnp.asarray(dist, jnp.int32), 8)[:, :s_new]
    bias = bias.reshape(g, hpg * s_new, dist.shape[1])
    c0, c1 = 2 * SMP_CPAD, 2 * SMP_CPAD + n_keys_s
    tile = lambda m: jnp.asarray(np.tile(m, (hpg, 1)), F32)
    expand = (key[None, :] // SEL_BLOCK == np.arange(SMP_CPAD)[:, None]) & (np.arange(SMP_CPAD)[:, None] < nsel)
    cur = np.broadcast_to((pos_q // SEL_BLOCK)[:, None], (s_new, SMP_CPAD))
    return dict(bias_c=bias[:, :, :c0], bias_s=bias[:, :, c0:c1], bias_w=bias[:, :, c1:],
                ok_c=tile(ok_c), ok_s=tile(ok_s), ok_w=tile(ok_w),
                expand=jnp.asarray(expand, BF16), cur=jnp.asarray(cur, jnp.int32), nc=nc, half=half, tk=tk)


def _flat_blocks(rows):
    b, tk = rows.shape[:2]
    nc = tk // CMP_BLOCK
    blk = rows.reshape(b, nc // 2, 2, CMP_BLOCK, 2, NSA_KV_HEADS, NSA_HEAD_DIM)
    return jnp.transpose(blk, (4, 0, 5, 2, 1, 3, 6)).reshape(2, b * NSA_KV_HEADS * nc, CMP_BLOCK * NSA_HEAD_DIM)


def _nsa_prompt_layer(pr3, lw, tabs):
    b, t = pr3.shape[:2]
    g, hpg, d, tq = NSA_KV_HEADS, NSA_HPG, NSA_HEAD_DIM, NSA_TQ
    nt = t // tq
    nc = t // CMP_BLOCK
    kv = pr3[:, :, COL_CMP:COL_CMP + 3 * KV_W].reshape(b, t, 3, 2, g, d)
    cmp_rows, sel_rows, win_rows = kv[:, :, 0], kv[:, :, 1], kv[:, :, 2]
    kcv = _compress(_flat_blocks(cmp_rows), lw['pe'], lw['cw1'], lw['cw2'], 512)
    kcv = kcv.reshape(2, b, g, nc, d).astype(BF16)
    kvt = kv[:, :, 1:].reshape(b, nt, tq, 2, 2, g, d).transpose(3, 0, 4, 5, 1, 6, 2).astype(BF16)
    gts = pr3[:, :, COL_MISC:COL_MISC + N_GATE].reshape(b, t, 3, g, hpg).transpose(0, 3, 1, 2, 4)
    gts = jnp.pad(gts.reshape(b, g, t, 3 * hpg), ((0, 0), (0, 0), (0, 0), (0, 128 - 3 * hpg)))
    o = _nsa_prompt(pr3, kcv[0], kcv[1], tabs[0], kvt, tabs[1], tabs[2], gts)
    return o.reshape(b * t, Q_W), cmp_rows, sel_rows, win_rows[:, t - min(WINDOW, t):]


def _to_native(rows, keys):
    b, n = rows.shape[:2]
    r = rows.reshape(b, n, 2, NSA_KV_HEADS, NSA_HEAD_DIM).transpose(0, 2, 3, 4, 1)
    return jnp.pad(r, ((0, 0),) * 4 + ((0, keys - n),))


def _nsa_sample_layer(pr3, lw, tabs, layer, cache_cmp, cache_sel, win_state, page_table, win_acc):
    b, s_new = pr3.shape[:2]
    g, hpg, d = NSA_KV_HEADS, NSA_HPG, NSA_HEAD_DIM
    n_pages = page_table.shape[1]
    past_len = n_pages * PAGE_SIZE
    new = lambda c: pr3[:, :, c:c + KV_W]
    cmp_new, sel_new, win_new = new(COL_CMP), new(COL_SEL), new(COL_WIN)
    n_past, n_tail = past_len // CMP_BLOCK, tabs['nc'] - past_len // CMP_BLOCK
    pe3 = lw['pe'].reshape(2, CMP_BLOCK, d)
    past = _compress_pages(layer, page_table, cache_cmp, pe3, lw['cw1'].reshape(2, CMP_BLOCK, d, d), lw['cw2'])
    tail = jnp.pad(cmp_new, ((0, 0), (0, n_tail * CMP_BLOCK - s_new), (0, 0)))
    tail = _compress(_flat_blocks(tail.reshape(b, n_tail * CMP_BLOCK, 2, g, d)), lw['pe'], lw['cw1'], lw['cw2'], 512)
    tail = tail.reshape(2, b, g, 2, n_tail // 2, d).transpose(1, 0, 2, 3, 4, 5)
    kcv = jnp.concatenate([past, tail], axis=4)
    kcv = jnp.pad(kcv, ((0, 0),) * 4 + ((0, SMP_CPAD - kcv.shape[4]), (0, 0)))
    kcv = kcv.reshape(b, 2, g, 2 * SMP_CPAD, d).transpose(1, 0, 2, 3, 4)
    qg = pr3[:, :, COL_Q:COL_Q + Q_W].reshape(b, s_new, g, hpg, d).transpose(0, 2, 3, 1, 4)
    qg = qg.reshape(b, g, hpg * s_new, d)
    gts = pr3[:, :, COL_MISC:COL_MISC + N_GATE].reshape(b, s_new, 3, g, hpg).transpose(0, 3, 4, 1, 2)
    gts = jnp.pad(gts.reshape(b, g, hpg * s_new, 3), ((0, 0), (0, 0), (0, 0), (0, 128 - 3)))
    o, win_acc = _nsa_sample(layer, tabs['tk'] // SEL_BLOCK, page_table, qg, kcv[0], kcv[1], tabs['bias_c'], tabs['ok_c'],
                    tabs['cur'], cache_sel, _to_native(sel_new, KEY_TILE), tabs['bias_s'], tabs['ok_s'],
                    tabs['expand'], win_state, _to_native(win_new, KEY_TILE), tabs['bias_w'], tabs['ok_w'], gts,
                    win_acc)
    o = o.reshape(b, g, hpg, s_new, d).transpose(0, 3, 1, 2, 4).reshape(b * s_new, Q_W)
    kv5 = lambda a: a.reshape(b, -1, 2, g, d)
    return o, kv5(cmp_new), kv5(sel_new), win_acc


def _mixers(pr, b, t, lw, conv0, h0, s0, nsa_fn):
    pr3 = pr.reshape(b, t, PROJ_COLS)
    o_a, cmp_rows, sel_rows, win_rows = nsa_fn(pr3)
    lru_w = (lw['lcw'], lw['lcb'], lw['lgw'], lw['lgb'], lw['lam'])
    gla_w = (lw['gaw'], lw['gab'], lw['gng'])
    if t % 8 == 0:
        tc = min(t, 256)
        o_b, conv_n, h_n = _lru(pr3, COL_LRU_X // LRU_WIDTH, pr3, COL_LRU_G // LRU_WIDTH, conv0, h0[:, None],
                                *lru_w, tc, tc)
        tg = min(t, 256)
        o_c, s_n = _gla(pr3, COL_GLA_Q // GLA_QK_W, pr3, COL_GLA_K // GLA_QK_W, pr3, COL_GLA_V // GLA_V_W,
                         pr3, COL_GLA_G // GLA_V_W, pr3, COL_MISC // MISC_W, *gla_w, s0, tg,
                         min(tg, GLA_CHUNK), tg)
    else:
        tp = -(-t // 8) * 8
        cut = lambda c, w: _pad_rows(pr3[:, :, c:c + w], tp)
        if CONV_WIDTH - 1 + t <= LRU_SEG:
            o_b, conv_n, h_n = _lru_short(pr3[:, :, COL_LRU_X:COL_LRU_X + LRU_WIDTH],
                                          pr3[:, :, COL_LRU_G:COL_LRU_G + LRU_WIDTH], conv0, h0, *lru_w)
            h_n = h_n[:, None]
        else:
            o_b, conv_n, h_n = _lru(cut(COL_LRU_X, LRU_WIDTH), 0, cut(COL_LRU_G, LRU_WIDTH), 0, conv0,
                                    h0[:, None], *lru_w, tp, t)
        gla_in = (cut(COL_GLA_Q, GLA_QK_W), cut(COL_GLA_K, GLA_QK_W), cut(COL_GLA_V, GLA_V_W),
                  cut(COL_GLA_G, GLA_V_W), cut(COL_MISC, MISC_W))
        if tp == LRU_SEG:
            o_c, s_n = _gla_short(*gla_in, *gla_w, s0, t)
        else:
            o_c, s_n = _gla(*(a for x in gla_in for a in (x, 0)), *gla_w, s0, tp, tp, t)
        o_b, o_c = o_b[:, :t], o_c[:, :t]
    o_b = o_b.reshape(b * t, LRU_WIDTH)
    o_c = o_c.reshape(b * t, GLA_V_W)
    return (o_a, o_b, o_c), (cmp_rows, sel_rows, win_rows, conv_n, h_n[:, 0], s_n)


def _layer(x, b, t, lw, conv0, h0, s0, nsa_fn, tm):
    pr = _proj(x, lw['ng'], lw['w_in'], lw['b_in'], tm, PROJ_COLS // 5)
    (o_a, o_b, o_c), states = _mixers(pr, b, t, lw, conv0, h0, s0, nsa_fn)
    x = _merge(x, o_a, o_b, o_c, pr, lw['wb'], lw['wo'], min(tm, 512))
    x = _mlp(x, lw['mg'], lw['w1'], lw['w2'], tm, 1024)
    return x, states


def kernel(x_prompt, x_sample, cache_nsa_cmp_kv, cache_nsa_sel_kv, state_nsa_win_kv, state_lru_conv,
           state_lru_h, state_gla, page_table, rel_bias, norm_mix_g, norm_mlp_g, norm_final_g, w_in, b_in,
           nsa_cmp_pe, nsa_cmp_w1, nsa_cmp_w2, lru_gate_w, lru_gate_b, lru_lambda, lru_conv_w, lru_conv_b,
           gla_alpha_w, gla_alpha_b, gla_norm_g, w_branch, w_out, mlp_w1, mlp_w2):
    bp, tp = x_prompt.shape[:2]
    bs, ts = x_sample.shape[:2]
    depth = w_in.shape[0]
    n_pages = page_table.shape[1]
    w_buf = state_nsa_win_kv.shape[2]

    cols = [w_in[..., _SRC[n][0]:_SRC[n][0] + _SRC[n][1]] for n in _DST_ORDER]
    pad = PROJ_COLS - sum(c.shape[-1] for c in cols)
    w_in_p = jnp.concatenate(cols + [jnp.zeros(w_in.shape[:2] + (pad,), w_in.dtype)], axis=-1).astype(BF16)
    bcols = [b_in[..., _SRC[n][0]:_SRC[n][0] + _SRC[n][1]] for n in _DST_ORDER]
    b_in_p = jnp.concatenate(bcols + [jnp.zeros((depth, pad), b_in.dtype)], axis=-1)[:, None, :]
    eye = jnp.eye(LRU_BLOCKS, dtype=lru_gate_w.dtype)
    lgw = jnp.einsum('lznce,nm->lzncme', lru_gate_w, eye).reshape(depth, 2, LRU_WIDTH, LRU_WIDTH).astype(BF16)
    gaw = jnp.zeros((depth, MISC_W, GLA_QK_W), F32).at[:, N_GATE:N_GATE + GLA_RANK].set(gla_alpha_w).astype(BF16)
    pe = jnp.transpose(nsa_cmp_pe, (0, 2, 1, 3)).reshape(depth, 2, 1, CMP_BLOCK * NSA_HEAD_DIM)
    cw1 = nsa_cmp_w1.astype(BF16)
    cw2 = nsa_cmp_w2.astype(BF16)
    wb = w_branch.astype(BF16)
    wo = w_out.astype(BF16)
    w1 = mlp_w1.astype(BF16)
    w2 = mlp_w2.astype(BF16)
    cache_cmp = jnp.transpose(cache_nsa_cmp_kv, (0, 1, 3, 4, 5, 2))
    cache_sel = jnp.transpose(cache_nsa_sel_kv, (0, 1, 3, 4, 5, 2))
    win_state = jnp.transpose(state_nsa_win_kv, (0, 1, 3, 4, 5, 2))

    tabs_p = _prompt_bias_tables(rel_bias, tp)
    tabs_s = _sample_tables(rel_bias, n_pages * PAGE_SIZE, ts, w_buf)

    xp = x_prompt.reshape(bp * tp, D_MODEL)
    xs = x_sample.reshape(bs * ts, D_MODEL)
    conv0_p = jnp.zeros((bp, CONV_WIDTH - 1, LRU_WIDTH), F32)
    h0_p = jnp.zeros((bp, LRU_WIDTH), F32)
    s0_p = jnp.zeros((bp, GLA_HEADS, GLA_DK, GLA_DV), F32)
    outs_p = [[] for _ in range(6)]
    outs_s = [[] for _ in range(6)]
    win_acc = jnp.zeros(win_state.shape, F32)
    for l in range(depth):
        lw = dict(ng=norm_mix_g[l][None], mg=norm_mlp_g[l][None], w_in=w_in_p[l], b_in=b_in_p[l],
                  pe=pe[l], cw1=cw1[l], cw2=cw2[l], lcw=lru_conv_w[l], lcb=lru_conv_b[l][None], lgw=lgw[l],
                  lgb=lru_gate_b[l], lam=lru_lambda[l][None], gaw=gaw[l], gab=gla_alpha_b[l][None],
                  gng=gla_norm_g[l][None], wb=wb[l], wo=wo[l], w1=w1[l], w2=w2[l])
        xp, st_p = _layer(xp, bp, tp, lw, conv0_p, h0_p, s0_p,
                          functools.partial(_nsa_prompt_layer, lw=lw, tabs=tabs_p), _row_tile(bp * tp, 1024))
        nsa_s = functools.partial(_nsa_sample_layer, lw=lw, tabs=tabs_s, layer=l, cache_cmp=cache_cmp,
                                  cache_sel=cache_sel, win_state=win_state, page_table=page_table, win_acc=win_acc)
        xs, st_s = _layer(xs, bs, ts, lw, state_lru_conv[l], state_lru_h[l], state_gla[l], nsa_s, bs * ts)
        win_acc = st_s[2]
        for j in range(6):
            outs_p[j].append(st_p[j])
            outs_s[j].append(st_s[j])
    y_prompt = _final_norm(xp, norm_final_g[None], _row_tile(bp * tp, 1024)).reshape(bp, tp, D_MODEL)
    y_sample = _final_norm(xs, norm_final_g[None], bs * ts).reshape(bs, ts, D_MODEL)
    st = lambda outs, j: jnp.stack(outs[j])
    win_s = win_acc.transpose(0, 1, 5, 2, 3, 4)
    return (y_prompt, y_sample, st(outs_p, 0), st(outs_s, 0), st(outs_p, 1), st(outs_s, 1),
            st(outs_p, 2), win_s, st(outs_p, 3), st(outs_s, 3), st(outs_p, 4), st(outs_s, 4),
            st(outs_p, 5), st(outs_s, 5))
```

```python
import functools
import math

import jax
import jax.numpy as jnp
import numpy as np
from jax import lax
from jax.experimental import pallas as pl
from jax.experimental.pallas import tpu as pltpu

F32 = jnp.float32
BF16 = jnp.bfloat16

D_MODEL = 1024
DEPTH = 4
PAGE_SIZE = 128
NSA_HEADS = 8
NSA_KV_HEADS = 2
NSA_HPG = NSA_HEADS // NSA_KV_HEADS
NSA_HEAD_DIM = 64
CMP_BLOCK = 32
SEL_BLOCK = 64
N_SELECT = 16
WINDOW = 512
FORCE_SCORE = 1e4
REL_BUCKETS = 32
REL_MAX_DIST = 128
LRU_WIDTH = 512
LRU_BLOCKS = 8
LRU_BLOCK_DIM = LRU_WIDTH // LRU_BLOCKS
CONV_WIDTH = 4
LRU_C = 8.0
GLA_HEADS = 4
GLA_DK = 64
GLA_DV = 128
GLA_RANK = 16
GLA_TAU = 16.0
GLA_CHUNK = 64
D_FF = 4 * D_MODEL
N_BRANCH = 3
BRANCH_WIDTH = 512
NORM_EPS = 1e-6

KV_W = 2 * NSA_KV_HEADS * NSA_HEAD_DIM
Q_W = NSA_HEADS * NSA_HEAD_DIM
GROUP_W = NSA_HPG * NSA_HEAD_DIM
GLA_QK_W = GLA_HEADS * GLA_DK
GLA_V_W = GLA_HEADS * GLA_DV
MISC_W = 128
N_GATE = 3 * NSA_HEADS

COL_MERGE = 0
COL_Q = COL_MERGE + N_BRANCH * D_MODEL
COL_LRU_X = COL_Q + Q_W
COL_LRU_G = COL_LRU_X + LRU_WIDTH
COL_GLA_V = COL_LRU_G + LRU_WIDTH
COL_GLA_G = COL_GLA_V + GLA_V_W
COL_CMP = COL_GLA_G + GLA_V_W
COL_SEL = COL_CMP + KV_W
COL_WIN = COL_SEL + KV_W
COL_GLA_Q = COL_WIN + KV_W
COL_GLA_K = COL_GLA_Q + GLA_QK_W
COL_MISC = COL_GLA_K + GLA_QK_W
PROJ_COLS = COL_MISC + MISC_W

_SRC = {}
_off = 0
for _name, _w in (('nsa_q', Q_W), ('nsa_cmp_kv', KV_W), ('nsa_sel_kv', KV_W), ('nsa_win_kv', KV_W),
                  ('nsa_gate', N_GATE), ('lru_x', LRU_WIDTH), ('lru_gate', LRU_WIDTH),
                  ('gla_q', GLA_QK_W), ('gla_k', GLA_QK_W), ('gla_v', GLA_V_W),
                  ('gla_alpha', GLA_RANK), ('gla_gate', GLA_V_W), ('merge_gate', N_BRANCH * D_MODEL)):
    _SRC[_name] = (_off, _w)
    _off += _w
_DST_ORDER = ('merge_gate', 'nsa_q', 'lru_x', 'lru_gate', 'gla_v', 'gla_gate', 'nsa_cmp_kv',
              'nsa_sel_kv', 'nsa_win_kv', 'gla_q', 'gla_k', 'nsa_gate', 'gla_alpha')

NSA_TQ = 128
NSA_LANES = NSA_HPG * NSA_TQ
NSA_CLASS = 2
LOG2E = math.log2(math.e)
ONES_ROWS = 16
VMEM_LIMIT = 56 * 1024 * 1024


def _cparams(sem):
    return pltpu.CompilerParams(dimension_semantics=sem, vmem_limit_bytes=VMEM_LIMIT)


def _gelu(x):
    return x * (0.5 * (1.0 + jnp.tanh(math.sqrt(2.0 / math.pi) * (x + 0.044715 * (x * x * x)))))


def _softplus(x):
    return jnp.maximum(x, 0.0) + jnp.log1p(jnp.exp(-jnp.abs(x)))


def _rms(x, g):
    return x * lax.rsqrt(jnp.mean(x * x, axis=-1, keepdims=True) + NORM_EPS) * g


def _dot(a, b):
    return jnp.dot(a, b, preferred_element_type=F32)


def _dot_nt(a, b):
    return lax.dot_general(a, b, (((1,), (1,)), ((), ())), preferred_element_type=F32)


def _dot_tn(a, b):
    return lax.dot_general(a, b, (((0,), (0,)), ((), ())), preferred_element_type=F32)


def _proj_kernel(x_ref, g_ref, w_ref, b_ref, o_ref, h_ref):
    @pl.when(pl.program_id(1) == 0)
    def _():
        h_ref[...] = _rms(x_ref[...], g_ref[...]).astype(BF16)

    o_ref[...] = _dot(h_ref[...], w_ref[...]) + b_ref[...]


def _proj(x, g, w, b, layer, tm, tn):
    n = x.shape[0]
    return pl.pallas_call(
        _proj_kernel,
        grid=(n // tm, PROJ_COLS // tn),
        in_specs=[pl.BlockSpec((tm, D_MODEL), lambda i, j: (i, 0)),
                  pl.BlockSpec((1, D_MODEL), lambda i, j: (0, 0)),
                  pl.BlockSpec((None, D_MODEL, tn), lambda i, j: (layer, 0, j)),
                  pl.BlockSpec((None, 1, tn), lambda i, j: (layer, 0, j))],
        out_specs=pl.BlockSpec((tm, tn), lambda i, j: (i, j)),
        out_shape=jax.ShapeDtypeStruct((n, PROJ_COLS), F32),
        scratch_shapes=[pltpu.VMEM((tm, D_MODEL), BF16)],
        compiler_params=_cparams(("parallel", "arbitrary")),
        name="proj",
    )(x, g, w, b)


def _merge_kernel(x_ref, oa_ref, ob_ref, oc_ref, g0_ref, g1_ref, g2_ref, wb_ref, wo_ref, o_ref):
    m = jax.nn.sigmoid(g0_ref[...]) * _dot(oa_ref[...].astype(BF16), wb_ref[0])
    m = m + jax.nn.sigmoid(g1_ref[...]) * _dot(ob_ref[...].astype(BF16), wb_ref[1])
    m = m + jax.nn.sigmoid(g2_ref[...]) * _dot(oc_ref[...].astype(BF16), wb_ref[2])
    o_ref[...] = x_ref[...] + _dot(m.astype(BF16), wo_ref[...])


def _merge(x, oa, ob, oc, pr, wb, wo, layer, tm):
    n = x.shape[0]
    row = lambda w: pl.BlockSpec((tm, w), lambda i: (i, 0))
    gate = lambda z: pl.BlockSpec((tm, D_MODEL), lambda i: (i, COL_MERGE // D_MODEL + z))
    return pl.pallas_call(
        _merge_kernel,
        grid=(n // tm,),
        in_specs=[row(D_MODEL), row(BRANCH_WIDTH), row(BRANCH_WIDTH), row(BRANCH_WIDTH),
                  gate(0), gate(1), gate(2),
                  pl.BlockSpec((None, N_BRANCH, BRANCH_WIDTH, D_MODEL), lambda i: (layer, 0, 0, 0)),
                  pl.BlockSpec((None, D_MODEL, D_MODEL), lambda i: (layer, 0, 0))],
        out_specs=row(D_MODEL),
        out_shape=jax.ShapeDtypeStruct((n, D_MODEL), F32),
        compiler_params=_cparams(("parallel",)),
        name="merge",
    )(x, oa, ob, oc, pr, pr, pr, wb, wo)


def _mlp_kernel(x_ref, g_ref, w1_ref, w2_ref, o_ref, h_ref, acc_ref):
    f = pl.program_id(1)

    @pl.when(f == 0)
    def _():
        h_ref[...] = _rms(x_ref[...], g_ref[...]).astype(BF16)
        acc_ref[...] = jnp.zeros_like(acc_ref)

    a = jnp.maximum(_dot(h_ref[...], w1_ref[...]), 0.0)
    acc_ref[...] += _dot((a * a).astype(BF16), w2_ref[...])

    @pl.when(f == pl.num_programs(1) - 1)
    def _():
        o_ref[...] = x_ref[...] + acc_ref[...]


def _mlp(x, g, w1, w2, layer, tm, tf):
    n = x.shape[0]
    return pl.pallas_call(
        _mlp_kernel,
        grid=(n // tm, D_FF // tf),
        in_specs=[pl.BlockSpec((tm, D_MODEL), lambda i, f: (i, 0)),
                  pl.BlockSpec((1, D_MODEL), lambda i, f: (0, 0)),
                  pl.BlockSpec((None, D_MODEL, tf), lambda i, f: (layer, 0, f)),
                  pl.BlockSpec((None, tf, D_MODEL), lambda i, f: (layer, f, 0))],
        out_specs=pl.BlockSpec((tm, D_MODEL), lambda i, f: (i, 0)),
        out_shape=jax.ShapeDtypeStruct((n, D_MODEL), F32),
        scratch_shapes=[pltpu.VMEM((tm, D_MODEL), BF16), pltpu.VMEM((tm, D_MODEL), F32)],
        compiler_params=_cparams(("parallel", "arbitrary")),
        name="mlp",
    )(x, g, w1, w2)


def _norm_kernel(x_ref, g_ref, o_ref):
    o_ref[...] = _rms(x_ref[...], g_ref[...])


def _final_norm(x, g, tm):
    n = x.shape[0]
    return pl.pallas_call(
        _norm_kernel,
        grid=(n // tm,),
        in_specs=[pl.BlockSpec((tm, D_MODEL), lambda i: (i, 0)),
                  pl.BlockSpec((1, D_MODEL), lambda i: (0, 0))],
        out_specs=pl.BlockSpec((tm, D_MODEL), lambda i: (i, 0)),
        out_shape=jax.ShapeDtypeStruct((n, D_MODEL), F32),
        compiler_params=_cparams(("parallel",)),
        name="final_norm",
    )(x, g)


_XB = 8


def _lru_kernel(x_ref, gb_ref, conv0_ref, h0_ref, cw_ref, cb_ref, gw_ref, gbias_ref, lam_ref,
                o_ref, convn_ref, hn_ref, xbuf, hcar, *, tc, tv):
    @pl.when(pl.program_id(1) == 0)
    def _():
        xbuf[0:_XB, :] = jnp.zeros((_XB, LRU_WIDTH), F32)
        xbuf[_XB - 3:_XB, :] = conv0_ref[0]
        hcar[...] = h0_ref[0]

    x = x_ref[0]
    xbuf[_XB:_XB + tc, :] = x
    w = cw_ref[...]
    xc = cb_ref[...] + xbuf[_XB - 3:_XB - 3 + tc, :] * w[0:1]
    xc = xc + xbuf[_XB - 2:_XB - 2 + tc, :] * w[1:2]
    xc = xc + xbuf[_XB - 1:_XB - 1 + tc, :] * w[2:3]
    xc = xc + x * w[3:4]
    tail = xbuf[_XB - 3 + tv:_XB + tv, :]
    convn_ref[0] = tail
    xbuf[_XB - 3:_XB, :] = tail

    xcb = xc.astype(BF16)
    r = jax.nn.sigmoid(_dot(xcb, gw_ref[0]) + gbias_ref[0:1])
    i = jax.nn.sigmoid(_dot(xcb, gw_ref[1]) + gbias_ref[1:2])
    log_a = (-LRU_C * r) * _softplus(-lam_ref[...])
    a = jnp.exp(log_a)
    b = jnp.sqrt(-jnp.tanh(log_a) * (a * a + 1.0)) * (i * xc)

    pos = lax.broadcasted_iota(jnp.int32, (tc, LRU_WIDTH), 0) & (LRU_SEG - 1)
    s = 1
    while s < LRU_SEG:
        m = pos >= s
        b = jnp.where(m, a * pltpu.roll(b, s, 0) + b, b)
        a = jnp.where(m, a * pltpu.roll(a, s, 0), a)
        s *= 2
    h_in = hcar[...]
    groups = []
    for j in range(tc // LRU_SEG):
        hj = a[j * LRU_SEG:(j + 1) * LRU_SEG] * h_in + b[j * LRU_SEG:(j + 1) * LRU_SEG]
        groups.append(hj)
        h_in = hj[LRU_SEG - 1:LRU_SEG]
    h = jnp.concatenate(groups, axis=0)
    hlast = h[tv - 1:tv]
    hcar[...] = hlast
    hn_ref[0] = hlast
    o_ref[0] = _gelu(gb_ref[0]) * h


def _lru(x_arr, x_blk, gb_arr, gb_blk, conv0, h0, cw, cb, gw, gbias, lam, tc, tv):
    b, t = x_arr.shape[:2]
    r = LRU_WIDTH
    const2 = lambda shape: pl.BlockSpec(shape, lambda i, c: (0, 0))
    return pl.pallas_call(
        functools.partial(_lru_kernel, tc=tc, tv=tv),
        grid=(b, t // tc),
        in_specs=[pl.BlockSpec((1, tc, r), lambda i, c: (i, c, x_blk)),
                  pl.BlockSpec((1, tc, r), lambda i, c: (i, c, gb_blk)),
                  pl.BlockSpec((1, CONV_WIDTH - 1, r), lambda i, c: (i, 0, 0)),
                  pl.BlockSpec((1, 1, r), lambda i, c: (i, 0, 0)),
                  const2((CONV_WIDTH, r)), const2((1, r)),
                  pl.BlockSpec((2, r, r), lambda i, c: (0, 0, 0)),
                  const2((2, r)), const2((1, r))],
        out_specs=[pl.BlockSpec((1, tc, r), lambda i, c: (i, c, 0)),
                   pl.BlockSpec((1, CONV_WIDTH - 1, r), lambda i, c: (i, 0, 0)),
                   pl.BlockSpec((1, 1, r), lambda i, c: (i, 0, 0))],
        out_shape=[jax.ShapeDtypeStruct((b, t, r), F32),
                   jax.ShapeDtypeStruct((b, CONV_WIDTH - 1, r), F32),
                   jax.ShapeDtypeStruct((b, 1, r), F32)],
        scratch_shapes=[pltpu.VMEM((_XB + tc, r), F32), pltpu.VMEM((1, r), F32)],
        compiler_params=_cparams(("parallel", "arbitrary")),
        name="rglru",
    )(x_arr, gb_arr, conv0, h0, cw, cb, gw, gbias, lam)


LRU_SEG = 8


def _lru_short_kernel(x_ref, gb_ref, h0_ref, cw_ref, cb_ref, gw_ref, gbias_ref, lam_ref, o_ref, h_ref, *, t):
    x = x_ref[...]
    pos = lax.broadcasted_iota(jnp.int32, x.shape, 0) & (LRU_SEG - 1)
    w = cw_ref[...]
    xc = cb_ref[...] + x * w[CONV_WIDTH - 1:CONV_WIDTH]
    for k in range(1, CONV_WIDTH):
        xc = xc + pltpu.roll(x, k, 0) * w[CONV_WIDTH - 1 - k:CONV_WIDTH - k]
    xcb = xc.astype(BF16)
    r = jax.nn.sigmoid(_dot(xcb, gw_ref[0]) + gbias_ref[0:1])
    i = jax.nn.sigmoid(_dot(xcb, gw_ref[1]) + gbias_ref[1:2])
    log_a = (-LRU_C * r) * _softplus(-lam_ref[...])
    a = jnp.exp(log_a)
    b = jnp.sqrt(-jnp.tanh(log_a) * (a * a + 1.0)) * (i * xc)
    real = (pos >= CONV_WIDTH - 1) & (pos < CONV_WIDTH - 1 + t)
    a = jnp.where(real, a, 1.0)
    b = jnp.where(real, b, 0.0)
    s = 1
    while s < LRU_SEG:
        m = pos >= s
        b = jnp.where(m, a * pltpu.roll(b, s, 0) + b, b)
        a = jnp.where(m, a * pltpu.roll(a, s, 0), a)
        s *= 2
    h = a * h0_ref[...] + b
    h_ref[...] = h
    o_ref[...] = _gelu(gb_ref[...]) * h


def _lru_short(x, gb, conv0, h0, cw, cb, gw, gbias, lam):
    b, t, r = x.shape
    lead = CONV_WIDTH - 1
    seg = lambda head, body: jnp.concatenate(
        [head, body, jnp.zeros((b, LRU_SEG - lead - t, r), F32)], axis=1).reshape(b * LRU_SEG, r)
    xin = seg(conv0, x)
    rows = b * LRU_SEG
    tm = _row_tile(rows, 256)
    blk = pl.BlockSpec((tm, r), lambda i: (i, 0))
    const2 = lambda shape: pl.BlockSpec(shape, lambda i: (0, 0))
    o, h = pl.pallas_call(
        functools.partial(_lru_short_kernel, t=t),
        grid=(rows // tm,),
        in_specs=[blk, blk, blk, const2((CONV_WIDTH, r)), const2((1, r)),
                  pl.BlockSpec((2, r, r), lambda i: (0, 0, 0)), const2((2, r)), const2((1, r))],
        out_specs=[blk, blk],
        out_shape=[jax.ShapeDtypeStruct((rows, r), F32), jax.ShapeDtypeStruct((rows, r), F32)],
        compiler_params=_cparams(("parallel",)),
        name="rglru_short",
    )(xin, seg(jnp.zeros((b, lead, r), F32), gb), jnp.repeat(h0, LRU_SEG, axis=0), cw, cb, gw, gbias, lam)
    o = o.reshape(b, LRU_SEG, r)[:, lead:lead + t]
    h_new = h.reshape(b, LRU_SEG, r)[:, lead + t - 1]
    conv_new = xin.reshape(b, LRU_SEG, r)[:, t:t + lead]
    return o, conv_new, h_new


def _gla_kernel(q_ref, k_ref, v_ref, og_ref, misc_ref, aw_ref, ab_ref, ng_ref, s0_ref,
                o_ref, sn_ref, st, *, tg, ck, tv):
    @pl.when(pl.program_id(1) == 0)
    def _():
        for h in range(GLA_HEADS):
            st[h] = s0_ref[0, h].T

    pre = _dot(misc_ref[0].astype(BF16), aw_ref[...]) + ab_ref[...]
    g = -_softplus(-pre) * (1.0 / GLA_TAU)
    if tv < tg:
        g = jnp.where(lax.broadcasted_iota(jnp.int32, g.shape, 0) < tv, g, 0.0)
    q = q_ref[0] * (GLA_DK ** -0.5)
    k = k_ref[0]
    v = v_ref[0]
    og = og_ref[0]
    ng = ng_ref[...]
    rows = lax.broadcasted_iota(jnp.int32, (ck, GLA_QK_W), 0)
    tril = (lax.broadcasted_iota(jnp.int32, (ck, ck), 0) >= lax.broadcasted_iota(jnp.int32, (ck, ck), 1))
    for c in range(tg // ck):
        sl = slice(c * ck, (c + 1) * ck)
        bc = g[sl]
        s = 1
        while s < ck:
            bc = bc + jnp.where(rows >= s, pltpu.roll(bc, s, 0), 0.0)
            s *= 2
        bl = bc[ck - 1:ck]
        e = jnp.exp(bc)
        qi = (q[sl] * e).astype(BF16)
        ki = (k[sl] * jnp.exp(-bc)).astype(BF16)
        kd = (k[sl] * jnp.exp(bl - bc)).astype(BF16)
        dec = jnp.exp(bl)
        vb = v[sl].astype(BF16)
        outs = []
        for h in range(GLA_HEADS):
            ks = slice(h * GLA_DK, (h + 1) * GLA_DK)
            vs = slice(h * GLA_DV, (h + 1) * GLA_DV)
            att = jnp.where(tril, _dot_nt(qi[:, ks], ki[:, ks]), 0.0)
            s_prev = st[h]
            o = _dot(att.astype(BF16), vb[:, vs]) + _dot_nt(qi[:, ks], s_prev.astype(BF16))
            st[h] = s_prev * dec[:, ks] + _dot_tn(vb[:, vs], kd[:, ks])
            o = _rms(o, ng[:, vs])
            ogh = og[sl, vs]
            outs.append(o * (ogh * jax.nn.sigmoid(ogh)))
        o_ref[0, sl, :] = jnp.concatenate(outs, axis=-1)

    @pl.when(pl.program_id(1) == pl.num_programs(1) - 1)
    def _():
        for h in range(GLA_HEADS):
            sn_ref[0, h] = st[h].T


def _gla(q_arr, q_blk, k_arr, k_blk, v_arr, v_blk, og_arr, og_blk, misc_arr, misc_blk,
         aw, ab, ng, s0, tg, ck, tv):
    b, t = q_arr.shape[:2]
    col = lambda w, blk: pl.BlockSpec((1, tg, w), lambda i, c: (i, c, blk))
    const2 = lambda shape: pl.BlockSpec(shape, lambda i, c: (0, 0))
    state = pl.BlockSpec((1, GLA_HEADS, GLA_DK, GLA_DV), lambda i, c: (i, 0, 0, 0))
    return pl.pallas_call(
        functools.partial(_gla_kernel, tg=tg, ck=ck, tv=tv),
        grid=(b, t // tg),
        in_specs=[col(GLA_QK_W, q_blk), col(GLA_QK_W, k_blk), col(GLA_V_W, v_blk), col(GLA_V_W, og_blk),
                  col(MISC_W, misc_blk), const2((MISC_W, GLA_QK_W)), const2((1, GLA_QK_W)),
                  const2((1, GLA_V_W)), state],
        out_specs=[pl.BlockSpec((1, tg, GLA_V_W), lambda i, c: (i, c, 0)), state],
        out_shape=[jax.ShapeDtypeStruct((b, t, GLA_V_W), F32),
                   jax.ShapeDtypeStruct((b, GLA_HEADS, GLA_DK, GLA_DV), F32)],
        scratch_shapes=[pltpu.VMEM((GLA_HEADS, GLA_DV, GLA_DK), F32)],
        compiler_params=_cparams(("parallel", "arbitrary")),
        name="gla",
    )(q_arr, k_arr, v_arr, og_arr, misc_arr, aw, ab, ng, s0)


GLA_SHORT_SEQS = 16


def _gla_short_kernel(q_ref, k_ref, v_ref, og_ref, misc_ref, aw_ref, ab_ref, ng_ref, s0_ref, o_ref, sn_ref,
                      *, nb, tv):
    seg = LRU_SEG
    rows = nb * seg
    flat = lambda ref: ref[...].reshape(rows, ref.shape[-1])
    part = lambda x, n: x[n * seg:(n + 1) * seg]
    pre = _dot(flat(misc_ref).astype(BF16), aw_ref[...]) + ab_ref[...]
    pos = lax.broadcasted_iota(jnp.int32, (rows, GLA_QK_W), 0) & (seg - 1)
    bc = jnp.where(pos < tv, -_softplus(-pre) * (1.0 / GLA_TAU), 0.0)
    s = 1
    while s < seg:
        bc = bc + jnp.where(pos >= s, pltpu.roll(bc, s, 0), 0.0)
        s *= 2
    last = bc.reshape(nb, seg, GLA_QK_W)[:, seg - 1:seg, :]
    bl = jnp.broadcast_to(last, (nb, seg, GLA_QK_W)).reshape(rows, GLA_QK_W)
    dec_t = jnp.exp(last.reshape(nb, GLA_QK_W)).T
    q = flat(q_ref) * (GLA_DK ** -0.5)
    k = flat(k_ref)
    qi = q * jnp.exp(bc)
    ki = k * jnp.exp(-bc)
    kd = k * jnp.exp(bl - bc)
    v = flat(v_ref)
    tril = lax.broadcasted_iota(jnp.int32, (seg, seg), 0) >= lax.broadcasted_iota(jnp.int32, (seg, seg), 1)
    pairs = [(n, h) for n in range(nb) for h in range(GLA_HEADS)]
    ks = lambda h: slice(h * GLA_DK, (h + 1) * GLA_DK)
    vs = lambda h: slice(h * GLA_DV, (h + 1) * GLA_DV)
    bf = lambda x: x.astype(BF16)
    qs = [bf(part(qi, n)[:, ks(h)]) for n, h in pairs]
    vb = [bf(part(v, n)[:, vs(h)]) for n, h in pairs]
    att = [bf(jnp.where(tril, _dot_nt(qp, bf(part(ki, n)[:, ks(h)])), 0.0)) for qp, (n, h) in zip(qs, pairs)]
    o = [_dot(a, vp) + _dot(qp, bf(s0_ref[n, h])) for a, vp, qp, (n, h) in zip(att, vb, qs, pairs)]
    for vp, (n, h) in zip(vb, pairs):
        sn_ref[n, h] = s0_ref[n, h] * dec_t[ks(h), n:n + 1] + _dot_tn(bf(part(kd, n)[:, ks(h)]), vp)
    o = jnp.concatenate([jnp.concatenate(o[n * GLA_HEADS:(n + 1) * GLA_HEADS], axis=1) for n in range(nb)], axis=0)
    ng = ng_ref[...]
    o = jnp.concatenate([_rms(o[:, vs(h)], ng[:, vs(h)]) for h in range(GLA_HEADS)], axis=1)
    og = flat(og_ref)
    o_ref[...] = (o * (og * jax.nn.sigmoid(og))).reshape(nb, seg, GLA_V_W)


def _gla_short(q, k, v, og, misc, aw, ab, ng, s0, tv):
    b = q.shape[0]
    nb = next((c for c in (GLA_SHORT_SEQS, 8) if b % c == 0), b)
    seq = lambda w: pl.BlockSpec((nb, LRU_SEG, w), lambda i: (i, 0, 0))
    const2 = lambda shape: pl.BlockSpec(shape, lambda i: (0, 0))
    state = pl.BlockSpec((nb, GLA_HEADS, GLA_DK, GLA_DV), lambda i: (i, 0, 0, 0))
    return pl.pallas_call(
        functools.partial(_gla_short_kernel, nb=nb, tv=tv),
        grid=(b // nb,),
        in_specs=[seq(GLA_QK_W), seq(GLA_QK_W), seq(GLA_V_W), seq(GLA_V_W), seq(MISC_W),
                  const2((MISC_W, GLA_QK_W)), const2((1, GLA_QK_W)), const2((1, GLA_V_W)), state],
        out_specs=[seq(GLA_V_W), state],
        out_shape=[jax.ShapeDtypeStruct((b, LRU_SEG, GLA_V_W), F32),
                   jax.ShapeDtypeStruct((b, GLA_HEADS, GLA_DK, GLA_DV), F32)],
        compiler_params=_cparams(("parallel",)),
        name="gla_short",
    )(q, k, v, og, misc, aw, ab, ng, s0)


def _bias_kernel(tbl_ref, dist_ref, o_ref):
    h = pl.program_id(0)
    max_exact = REL_BUCKETS // 2
    n = jnp.maximum(dist_ref[...], 0)
    nf = jnp.maximum(n, 1).astype(F32)
    large = max_exact + (jnp.log(nf / max_exact) / math.log(REL_MAX_DIST / max_exact)
                         * (REL_BUCKETS - max_exact)).astype(jnp.int32)
    bucket = jnp.where(n < max_exact, n, jnp.minimum(large, REL_BUCKETS - 1))
    out = jnp.zeros(bucket.shape, F32)
    for kk in range(REL_BUCKETS):
        out = jnp.where(bucket == kk, tbl_ref[kk, h], out)
    o_ref[0] = out


def _bias_lookup(rel_bias, dist, tr):
    r, c = dist.shape
    return pl.pallas_call(
        _bias_kernel,
        grid=(NSA_HEADS, r // tr),
        in_specs=[pl.BlockSpec(memory_space=pltpu.SMEM),
                  pl.BlockSpec((tr, c), lambda h, i: (i, 0))],
        out_specs=pl.BlockSpec((1, tr, c), lambda h, i: (h, i, 0)),
        out_shape=jax.ShapeDtypeStruct((NSA_HEADS, r, c), F32),
        compiler_params=_cparams(("parallel", "parallel")),
        name="rel_bias",
    )(rel_bias, dist)


def _cmp_kernel(x_ref, pe_ref, w1_ref, w2_ref, o_ref):
    hid = _gelu(_dot((x_ref[0] + pe_ref[0]).astype(BF16), w1_ref[0]))
    o_ref[0] = _dot(hid.astype(BF16), w2_ref[0])


def _row_tile(m, cap):
    for tm in range(min(cap, m) // 8 * 8, 0, -8):
        if m % tm == 0:
            return tm
    return m


def _compress(flat, pe, w1, w2, cap):
    m = flat.shape[1]
    tm = _row_tile(m, cap)
    fw = CMP_BLOCK * NSA_HEAD_DIM
    d = NSA_HEAD_DIM
    return pl.pallas_call(
        _cmp_kernel,
        grid=(2, m // tm),
        in_specs=[pl.BlockSpec((1, tm, fw), lambda z, i: (z, i, 0)),
                  pl.BlockSpec((1, 1, fw), lambda z, i: (z, 0, 0)),
                  pl.BlockSpec((1, fw, d), lambda z, i: (z, 0, 0)),
                  pl.BlockSpec((1, d, d), lambda z, i: (z, 0, 0))],
        out_specs=pl.BlockSpec((1, tm, d), lambda z, i: (z, i, 0)),
        out_shape=jax.ShapeDtypeStruct((2, m, d), F32),
        compiler_params=_cparams(("parallel", "parallel")),
        name="nsa_compress",
    )(flat, pe, w1, w2)


CMP_PITCH = CMP_BLOCK + 4


def _cmp_pages_kernel(pt_ref, *refs, nb, n_pages):
    del pt_ref
    pages = refs[:nb * n_pages]
    pe_ref, w1_ref, w2_ref, o_ref, x_scr, h_scr = refs[nb * n_pages:]
    gd = NSA_KV_HEADS * NSA_HEAD_DIM
    per_page = PAGE_SIZE // CMP_BLOCK
    nblk = n_pages * per_page
    for z in range(2):
        for n in range(nb):
            for p in range(n_pages):
                xt = pages[n * n_pages + p][0, 0, z].reshape(gd, PAGE_SIZE).T
                for j in range(per_page):
                    r0 = (n * nblk + p * per_page + j) * CMP_PITCH
                    x_scr[z, r0:r0 + CMP_BLOCK, :] = xt[j * CMP_BLOCK:(j + 1) * CMP_BLOCK]
        acc = jnp.zeros((nb * nblk, gd), F32)
        for t in range(CMP_BLOCK):
            rows = x_scr[z, pl.ds(t, nb * nblk, stride=CMP_PITCH), :]
            acc = acc + _dot((rows + pe_ref[z, t:t + 1, :]).astype(BF16), w1_ref[z, t])
        h_scr[z] = _dot(_gelu(acc).astype(BF16), w2_ref[z])
        for n in range(nb):
            for par in range(2):
                o_ref[n, z, par] = h_scr[z, pl.ds(n * nblk + par, nblk // 2, stride=2), :]


def _compress_pages(layer, page_table, cache, pe, w1, w2):
    b, n_pages = page_table.shape
    g, d = NSA_KV_HEADS, NSA_HEAD_DIM
    gd = g * d
    nblk = n_pages * PAGE_SIZE // CMP_BLOCK
    eye = jnp.eye(g, dtype=w1.dtype)
    w1 = jnp.einsum('gh,ztje->ztgjhe', eye, w1).reshape(2, CMP_BLOCK, gd, gd)
    w2 = jnp.einsum('gh,zje->zgjhe', eye, w2).reshape(2, gd, gd)
    pe = jnp.tile(pe, (1, 1, g))
    const = lambda *shape: pl.BlockSpec(shape, lambda i, pt: (0,) * len(shape))
    nb = SMP_SEQS if b % SMP_SEQS == 0 else 1

    def page_spec(n, p):
        return pl.BlockSpec((1, 1, 2, g, d, PAGE_SIZE), lambda i, pt: (layer, pt[nb * i + n, p], 0, 0, 0, 0))

    out = pl.pallas_call(
        functools.partial(_cmp_pages_kernel, nb=nb, n_pages=n_pages),
        grid_spec=pltpu.PrefetchScalarGridSpec(
            num_scalar_prefetch=1,
            grid=(b // nb,),
            in_specs=[page_spec(n, p) for n in range(nb) for p in range(n_pages)]
            + [const(2, CMP_BLOCK, gd), const(2, CMP_BLOCK, gd, gd), const(2, gd, gd)],
            out_specs=pl.BlockSpec((nb, 2, 2, nblk // 2, gd), lambda i, pt: (i, 0, 0, 0, 0)),
            scratch_shapes=[pltpu.VMEM((2, nb * nblk * CMP_PITCH, gd), F32), pltpu.VMEM((2, nb * nblk, gd), F32)]),
        out_shape=jax.ShapeDtypeStruct((b, 2, 2, nblk // 2, gd), F32),
        compiler_params=_cparams(("parallel",)),
        name="nsa_compress_pages",
    )(page_table, *([cache] * (nb * n_pages)), pe, w1, w2)
    return out.reshape(b, 2, 2, nblk // 2, g, d).transpose(0, 1, 4, 2, 3, 5)


def _nsa_prompt_kernel(q_ref, kc_ref, vc_ref, bc_ref, ks_ref, vs_ref, kw_ref, vw_ref, bt_ref, e_ref, gt_ref,
                       o_ref, s_scr, p_scr, m_scr, selk_scr, os_scr, ow_scr):
    i = pl.program_id(2)
    tq, d, ck, hpg = NSA_TQ, NSA_HEAD_DIM, NSA_TQ, NSA_HPG
    nchunk = ks_ref.shape[3]
    nsel = kc_ref.shape[2] // 2
    neg = -jnp.inf
    qf = q_ref[0] * (d ** -0.5 * LOG2E)
    qs = jnp.concatenate([qf[:, h * d:(h + 1) * d] for h in range(hpg)], axis=0).astype(BF16)

    tk = lax.broadcasted_iota(jnp.int32, (tq, ck), 0) - lax.broadcasted_iota(jnp.int32, (tq, ck), 1)

    def mask_heads(s, ok):
        return jnp.concatenate([jnp.where(ok, s[h * tq:(h + 1) * tq], neg) for h in range(hpg)], axis=0)

    def attend(k_ref, v_ref, n, ok_fn):
        m_scr[...] = jnp.full(m_scr.shape, neg, F32)
        for r in range(n):
            c = i - r
            cc = jnp.maximum(c, 0)
            s = _dot(qs, k_ref[0, 0, 0, cc])
            if r < 2:
                s = s + bt_ref[0, r]
            s = mask_heads(s, ok_fn(r, c, cc))
            s_scr[:, r * ck:(r + 1) * ck] = s
            m_scr[...] = jnp.maximum(m_scr[...], s)
        m = jnp.max(m_scr[...], axis=1, keepdims=True)
        m = jnp.where(m > neg, m, 0.0)
        for r in range(n):
            p_scr[:, r * ck:(r + 1) * ck] = jnp.exp2(s_scr[:, r * ck:(r + 1) * ck] - m).astype(BF16)
        vt = jnp.concatenate([v_ref[0, 0, 0, jnp.maximum(i - r, 0)] for r in range(n)], axis=1)
        vt = jnp.concatenate([vt, jnp.ones((ONES_ROWS, n * ck), BF16)], axis=0)
        acc = _dot_nt(p_scr[:, 0:n * ck], vt)
        return acc[:, 0:d] / jnp.maximum(acc[:, d:d + 1], 1e-30)

    nw = WINDOW // ck + 1

    def win_ok(r, c, cc):
        if r == 0:
            return tk >= 0
        if r == nw - 1:
            return tk < jnp.where(c >= 0, 0, -tq)
        return tk > jnp.where(c >= 0, -tq, tq)

    ow_scr[...] = attend(kw_ref, vw_ref, nw, win_ok)

    t_lane = i * tq + (lax.broadcasted_iota(jnp.int32, (1, hpg * tq), 1) & (tq - 1))
    s_c = _dot_nt(kc_ref[0, 0], qs) + bc_ref[0, 0]
    r_c = lax.broadcasted_iota(jnp.int32, (2 * nsel, hpg * tq), 0)
    n_c = jnp.where(r_c < nsel, 2 * r_c, 2 * (r_c - nsel) + 1)
    s_c = jnp.where(t_lane >= (n_c + 1) * CMP_BLOCK - 1, s_c, neg)
    m_c = jnp.max(s_c, axis=0, keepdims=True)
    m_c = jnp.where(m_c > neg, m_c, 0.0)
    p_c = jnp.exp2(s_c - m_c)
    p_c = p_c / jnp.maximum(jnp.sum(p_c, axis=0, keepdims=True), 1e-30)
    o_c = _dot_tn(p_c.astype(BF16), vc_ref[0, 0])

    ph = p_c[:, 0:tq]
    for h in range(1, hpg):
        ph = ph + p_c[:, h * tq:(h + 1) * tq]
    imp = ph[0:nsel] + ph[nsel:2 * nsel]
    blk = lax.broadcasted_iota(jnp.int32, (nsel, tq), 0)
    cur = (i * tq + lax.broadcasted_iota(jnp.int32, (nsel, tq), 1)) >> int(math.log2(SEL_BLOCK))
    forced = (blk == 0) | (blk == cur) | (blk == cur - 1)
    imp = jnp.where(forced, FORCE_SCORE, jnp.where(blk <= cur, imp, neg))
    rank = jnp.zeros((nsel, tq), jnp.int32)
    for s2 in range(nsel):
        row = imp[s2:s2 + 1]
        beats = (row > imp) | ((row == imp) & (blk > s2))
        rank = rank + beats.astype(jnp.int32)
    chosen = (rank < N_SELECT).astype(BF16)
    selk = _dot_tn(chosen, e_ref[...])
    for c in range(nchunk):
        selk_scr[c] = selk[:, c * ck:(c + 1) * ck]

    def sel_ok(r, c, cc):
        ok = selk_scr[cc] > jnp.where(c >= 0, 0.5, 2.0)
        return ok & (tk >= 0) if r == 0 else ok

    sizes = list(range(NSA_CLASS, nchunk, NSA_CLASS)) + [nchunk]
    for lo, n in zip([0] + sizes[:-1], sizes):
        @pl.when((i >= lo) & (i < n))
        def _(n=n):
            os_scr[...] = attend(ks_ref, vs_ref, n, sel_ok)

    o_w = ow_scr[...]
    o_s = os_scr[...]

    sig = jax.nn.sigmoid(gt_ref[0, 0])
    outs = []
    for h in range(hpg):
        r = slice(h * tq, (h + 1) * tq)
        outs.append(sig[:, h:h + 1] * o_c[r] + sig[:, hpg + h:hpg + h + 1] * o_s[r]
                    + sig[:, 2 * hpg + h:2 * hpg + h + 1] * o_w[r])
    o_ref[0] = jnp.concatenate(outs, axis=1)


def _nsa_prompt(pr3, kc, vc, bias_c, kvt, bias_t, expand, gts):
    b, t = pr3.shape[:2]
    g, d, tq, hpg = NSA_KV_HEADS, NSA_HEAD_DIM, NSA_TQ, NSA_HPG
    nt = t // tq
    nc = kc.shape[2]
    rows = hpg * tq
    per_bg = lambda r, c: pl.BlockSpec((1, 1, r, c), lambda bi, gi, i: (bi, gi, 0, 0))
    kv = lambda br, z: pl.BlockSpec((None, 1, 1, 1, nt, d, tq), lambda bi, gi, i: (br, bi, z, gi, 0, 0, 0))
    return pl.pallas_call(
        _nsa_prompt_kernel,
        grid=(b, g, nt),
        in_specs=[pl.BlockSpec((1, tq, GROUP_W), lambda bi, gi, i: (bi, i, COL_Q // GROUP_W + gi)),
                  per_bg(nc, d), per_bg(nc, d),
                  pl.BlockSpec((1, 1, nc, rows), lambda bi, gi, i: (gi, i, 0, 0)),
                  kv(0, 0), kv(0, 1), kv(1, 0), kv(1, 1),
                  pl.BlockSpec((1, 2, rows, tq), lambda bi, gi, i: (gi, 0, 0, 0)),
                  pl.BlockSpec((nc // 2, t), lambda bi, gi, i: (0, 0)),
                  pl.BlockSpec((1, 1, tq, 128), lambda bi, gi, i: (bi, gi, i, 0))],
        out_specs=pl.BlockSpec((1, tq, GROUP_W), lambda bi, gi, i: (bi, i, gi)),
        out_shape=jax.ShapeDtypeStruct((b, t, Q_W), F32),
        scratch_shapes=[pltpu.VMEM((rows, t), F32), pltpu.VMEM((rows, t), BF16),
                        pltpu.VMEM((rows, tq), F32),
                        pltpu.VMEM((nt, tq, tq), F32), pltpu.VMEM((rows, d), F32), pltpu.VMEM((rows, d), F32)],
        compiler_params=_cparams(("parallel", "parallel", "arbitrary")),
        name="nsa_prompt",
    )(pr3, kc, vc, bias_c, kvt, kvt, kvt, kvt, bias_t, expand, gts)


SMP_CPAD = 128
KEY_TILE = PAGE_SIZE


def _masked_softmax(s, ok):
    s = jnp.where(ok, s, -jnp.inf)
    m = jnp.max(s, axis=-1, keepdims=True)
    m = jnp.where(m > -jnp.inf, m, 0.0)
    p = jnp.exp(s - m)
    return p / jnp.maximum(jnp.sum(p, axis=-1, keepdims=True), 1e-30)


def _nsa_sample_kernel(pt_ref, q_ref, kc_ref, vc_ref, bc_ref, okc_ref, cur_ref, *rest, nb, n_pages, nsel):
    pages = rest[:nb * n_pages]
    (sn_ref, bs_ref, oks_ref, e_ref, wb_ref, wn_ref, bw_ref, okw_ref, gt_ref, acc_ref,
     o_ref, wo_ref) = rest[nb * n_pages:]
    del pt_ref, acc_ref
    d, hpg = NSA_HEAD_DIM, NSA_HPG
    rows = q_ref.shape[2]
    s_new = rows // hpg
    chains = [(n, g) for n in range(nb) for g in range(NSA_KV_HEADS)]
    bf = lambda x: x.astype(BF16)
    stack = lambda parts: jnp.concatenate(parts, axis=0)
    tile = lambda x: stack([x] * len(chains))
    part = lambda x, c: x[c * rows:(c + 1) * rows]
    qs = [bf(q_ref[n, g] * (d ** -0.5)) for n, g in chains]

    s_c = stack([_dot_nt(q, bf(kc_ref[n, g])) + bc_ref[g] for q, (n, g) in zip(qs, chains)])
    p_c = _masked_softmax(s_c, tile(okc_ref[...]) > 0.5)
    o_c = stack([_dot(bf(part(p_c, c)), bf(vc_ref[n, g])) for c, (n, g) in enumerate(chains)])

    ph = []
    for c in range(len(chains)):
        acc = p_c[c * rows:c * rows + s_new]
        for h in range(1, hpg):
            acc = acc + p_c[c * rows + h * s_new:c * rows + (h + 1) * s_new]
        ph.append(acc)
    ph = stack(ph)
    imp = ph[:, 0:SMP_CPAD] + ph[:, SMP_CPAD:2 * SMP_CPAD]
    blk = lax.broadcasted_iota(jnp.int32, imp.shape, 1)
    cur = tile(cur_ref[...])
    forced = (blk == 0) | (blk == cur) | (blk == cur - 1)
    imp = jnp.where(forced, FORCE_SCORE, jnp.where(blk <= cur, imp, -jnp.inf))
    rank = jnp.zeros(imp.shape, jnp.int32)
    for s2 in range(nsel):
        col = imp[:, s2:s2 + 1]
        beats = (col > imp) | ((col == imp) & (blk > s2))
        rank = rank + beats.astype(jnp.int32)
    chosen = bf((rank < N_SELECT) & (blk < nsel))
    chosen = stack([chosen[c * s_new:(c + 1) * s_new] for c in range(len(chains)) for _ in range(hpg)])
    sel_keys = _dot(chosen, e_ref[...])

    def kv_tiles(n, g, z):
        return [pg[0, 0, z, g] for pg in pages[n * n_pages:(n + 1) * n_pages]] + [sn_ref[n, z, g]]

    s_s = stack([jnp.concatenate([_dot(q, bf(kt)) for kt in kv_tiles(n, g, 0)], axis=1) + bs_ref[g]
                 for q, (n, g) in zip(qs, chains)])
    p_s = bf(_masked_softmax(s_s, (tile(oks_ref[...]) > 0.5) & (sel_keys > 0.5)))
    o_s = []
    for c, (n, g) in enumerate(chains):
        pc = part(p_s, c)
        acc = None
        for p, vt in enumerate(kv_tiles(n, g, 1)):
            term = _dot_nt(pc[:, p * KEY_TILE:(p + 1) * KEY_TILE], bf(vt))
            acc = term if acc is None else acc + term
        o_s.append(acc)
    o_s = stack(o_s)

    wkeys = wb_ref.shape[-1]
    s_w = stack([jnp.concatenate([_dot(q, bf(wb_ref[0, n, 0, g])), _dot(q, bf(wn_ref[n, 0, g]))], axis=1)
                 + bw_ref[g] for q, (n, g) in zip(qs, chains)])
    p_w = bf(_masked_softmax(s_w, tile(okw_ref[...]) > 0.5))
    o_w = stack([_dot_nt(part(p_w, c)[:, 0:wkeys], bf(wb_ref[0, n, 1, g]))
                 + _dot_nt(part(p_w, c)[:, wkeys:wkeys + KEY_TILE], bf(wn_ref[n, 1, g]))
                 for c, (n, g) in enumerate(chains)])

    gt = jax.nn.sigmoid(stack([gt_ref[n, g] for n, g in chains]))
    out = gt[:, 0:1] * o_c + gt[:, 1:2] * o_s + gt[:, 2:3] * o_w
    for c, (n, g) in enumerate(chains):
        o_ref[n, g] = part(out, c)

    lane = lax.broadcasted_iota(jnp.int32, (2 * NSA_KV_HEADS * d, KEY_TILE), 1)
    for n in range(nb):
        old = pltpu.roll(wb_ref[0, n].reshape(-1, wkeys), wkeys - s_new, 1)
        new = pltpu.roll(wn_ref[n].reshape(-1, KEY_TILE), KEY_TILE - s_new, 1)
        tail = jnp.where(lane >= KEY_TILE - s_new, new, old[:, wkeys - KEY_TILE:])
        wo_ref[0, n] = jnp.concatenate([old[:, 0:wkeys - KEY_TILE], tail], axis=1).reshape(wo_ref.shape[2:])


SMP_SEQS = 4


def _nsa_sample(layer, nsel, page_table, qg, kc, vc, bias_c, ok_c, cur, cache_sel, sel_new, bias_s, ok_s,
                expand, win_state, win_new, bias_w, ok_w, gts, win_acc):
    b = qg.shape[0]
    g = NSA_KV_HEADS
    d = NSA_HEAD_DIM
    rows = qg.shape[2]
    n_pages = page_table.shape[1]
    wkeys = win_state.shape[-1]
    ks = (n_pages + 1) * KEY_TILE
    kw = wkeys + KEY_TILE
    nb = SMP_SEQS if b % SMP_SEQS == 0 else 1
    per_b = lambda *shape: pl.BlockSpec((nb,) + shape, lambda i, pt: (i,) + (0,) * len(shape))
    const = lambda *shape: pl.BlockSpec(shape, lambda i, pt: (0,) * len(shape))

    def page_spec(n, p):
        return pl.BlockSpec((1, 1, 2, g, d, PAGE_SIZE), lambda i, pt: (layer, pt[nb * i + n, p], 0, 0, 0, 0))

    win_spec = pl.BlockSpec((1, nb, 2, g, d, wkeys), lambda i, pt: (layer, i, 0, 0, 0, 0))
    in_specs = ([per_b(g, rows, d), per_b(g, 2 * SMP_CPAD, d), per_b(g, 2 * SMP_CPAD, d),
                 const(g, rows, 2 * SMP_CPAD), const(rows, 2 * SMP_CPAD), const(rows // NSA_HPG, SMP_CPAD)]
                + [page_spec(n, p) for n in range(nb) for p in range(n_pages)]
                + [per_b(2, g, d, KEY_TILE), const(g, rows, ks), const(rows, ks), const(SMP_CPAD, ks),
                   win_spec, per_b(2, g, d, KEY_TILE), const(g, rows, kw), const(rows, kw), per_b(g, rows, 128),
                   pl.BlockSpec(memory_space=pl.ANY)])
    operands = (page_table, qg, kc, vc, bias_c, ok_c, cur, *([cache_sel] * (nb * n_pages)), sel_new, bias_s, ok_s,
                expand, win_state, win_new, bias_w, ok_w, gts, win_acc)
    return pl.pallas_call(
        functools.partial(_nsa_sample_kernel, nb=nb, n_pages=n_pages, nsel=nsel),
        grid_spec=pltpu.PrefetchScalarGridSpec(
            num_scalar_prefetch=1,
            grid=(b // nb,),
            in_specs=in_specs,
            out_specs=[pl.BlockSpec((nb, g, rows, d), lambda i, pt: (i, 0, 0, 0)), win_spec]),
        out_shape=[jax.ShapeDtypeStruct((b, g, rows, d), F32), jax.ShapeDtypeStruct(win_acc.shape, F32)],
        input_output_aliases={len(operands) - 1: 1},
        compiler_params=_cparams(("parallel",)),
        name="nsa_sample",
    )(*operands)


def _pad_rows(a, rows):
    return jnp.pad(a, ((0, 0), (0, rows - a.shape[1]), (0, 0)))


def _even_odd(n):
    return np.concatenate([np.arange(0, n, 2), np.arange(1, n, 2)])


def _prompt_bias_tables(rel_bias, t):
    g, hpg, tq = NSA_KV_HEADS, NSA_HPG, NSA_TQ
    nt = t // tq
    nc = t // CMP_BLOCK
    end_c = (_even_odd(nc) + 1) * CMP_BLOCK - 1
    dist_c = jnp.asarray(np.arange(t)[None, :] - end_c[:, None], jnp.int32)
    bias_c = _bias_lookup(rel_bias, dist_c, nc)
    bias_c = bias_c.reshape(g, hpg, nc, nt, tq).transpose(0, 3, 2, 1, 4).reshape(g, nt, nc, hpg * tq) * LOG2E
    off = np.arange(3)[:, None, None] * tq
    dist_t = off + np.arange(tq)[None, :, None] - np.arange(tq)[None, None, :]
    assert dist_t[2].min() >= REL_MAX_DIST
    bias_t = _bias_lookup(rel_bias, jnp.asarray(dist_t.reshape(3 * tq, tq), jnp.int32), 3 * tq)
    bias_t = bias_t.reshape(g, hpg, 3, tq, tq).transpose(0, 2, 1, 3, 4).reshape(g, 3, hpg * tq, tq)
    bias_t = (bias_t[:, 0:2] - bias_t[:, 2:3]) * LOG2E
    expand = np.arange(t)[None, :] // SEL_BLOCK == np.arange(t // SEL_BLOCK)[:, None]
    return bias_c, bias_t, jnp.asarray(expand, BF16)


def _sample_tables(rel_bias, past_len, s_new, n_win_keys):
    g, hpg = NSA_KV_HEADS, NSA_HPG
    tk = -(-(past_len + s_new) // SEL_BLOCK) * SEL_BLOCK
    nc = tk // CMP_BLOCK
    nsel = tk // SEL_BLOCK
    half = (nc + 1) // 2
    pos_q = past_len + np.arange(s_new)
    lane = np.arange(SMP_CPAD)
    n_of_lane = np.concatenate([2 * lane, 2 * lane + 1])
    real_c = np.concatenate([lane < half, lane < nc - half])
    dist_c = pos_q[:, None] - ((n_of_lane[None, :] + 1) * CMP_BLOCK - 1)
    ok_c = real_c[None, :] & (dist_c >= 0)
    n_keys_s = (past_len // PAGE_SIZE + 1) * KEY_TILE
    key = np.arange(n_keys_s)
    dist_s = pos_q[:, None] - key[None, :]
    ok_s = (key[None, :] < past_len + s_new) & (dist_s >= 0)
    n_keys_w = n_win_keys + KEY_TILE
    i = np.arange(n_keys_w)
    pos_kw = np.where(i < n_win_keys, past_len - n_win_keys + i, past_len + i - n_win_keys)
    dist_w = pos_q[:, None] - pos_kw[None, :]
    ok_w = (i[None, :] < n_win_keys + s_new) & (dist_w >= 0) & (dist_w < WINDOW) & (pos_kw[None, :] >= 0)
    dist = np.concatenate([dist_c, dist_s, dist_w], axis=1)
    dist = np.pad(dist, ((0, 8 - s_new), (0, 0)))
    bias = _bias_lookup(rel_bias, jnp.asarray(dist, jnp.int32), 8)[:, :s_new]
    bias = bias.reshape(g, hpg * s_new, dist.shape[1])
    c0, c1 = 2 * SMP_CPAD, 2 * SMP_CPAD + n_keys_s
    tile = lambda m: jnp.asarray(np.tile(m, (hpg, 1)), F32)
    expand = (key[None, :] // SEL_BLOCK == np.arange(SMP_CPAD)[:, None]) & (np.arange(SMP_CPAD)[:, None] < nsel)
    cur = np.broadcast_to((pos_q // SEL_BLOCK)[:, None], (s_new, SMP_CPAD))
    return dict(bias_c=bias[:, :, :c0], bias_s=bias[:, :, c0:c1], bias_w=bias[:, :, c1:],
                ok_c=tile(ok_c), ok_s=tile(ok_s), ok_w=tile(ok_w),
                expand=jnp.asarray(expand, BF16), cur=jnp.asarray(cur, jnp.int32), nc=nc, half=half, tk=tk)


def _flat_blocks(rows):
    b, tk = rows.shape[:2]
    nc = tk // CMP_BLOCK
    blk = rows.reshape(b, nc // 2, 2, CMP_BLOCK, 2, NSA_KV_HEADS, NSA_HEAD_DIM)
    return jnp.transpose(blk, (4, 0, 5, 2, 1, 3, 6)).reshape(2, b * NSA_KV_HEADS * nc, CMP_BLOCK * NSA_HEAD_DIM)


def _nsa_prompt_layer(pr3, lw, tabs):
    b, t = pr3.shape[:2]
    g, hpg, d, tq = NSA_KV_HEADS, NSA_HPG, NSA_HEAD_DIM, NSA_TQ
    nt = t // tq
    nc = t // CMP_BLOCK
    kv = pr3[:, :, COL_CMP:COL_CMP + 3 * KV_W].reshape(b, t, 3, 2, g, d)
    cmp_rows, sel_rows, win_rows = kv[:, :, 0], kv[:, :, 1], kv[:, :, 2]
    kcv = _compress(_flat_blocks(cmp_rows), lw['pe'], lw['cw1'], lw['cw2'], 512)
    kcv = kcv.reshape(2, b, g, nc, d).astype(BF16)
    kvt = kv[:, :, 1:].reshape(b, nt, tq, 2, 2, g, d).transpose(3, 0, 4, 5, 1, 6, 2).astype(BF16)
    gts = pr3[:, :, COL_MISC:COL_MISC + N_GATE].reshape(b, t, 3, g, hpg).transpose(0, 3, 1, 2, 4)
    gts = jnp.pad(gts.reshape(b, g, t, 3 * hpg), ((0, 0), (0, 0), (0, 0), (0, 128 - 3 * hpg)))
    o = _nsa_prompt(pr3, kcv[0], kcv[1], tabs[0], kvt, tabs[1], tabs[2], gts)
    return o.reshape(b * t, Q_W), cmp_rows, sel_rows, win_rows[:, t - min(WINDOW, t):]


def _to_native(rows, keys):
    b, n = rows.shape[:2]
    r = rows.reshape(b, n, 2, NSA_KV_HEADS, NSA_HEAD_DIM).transpose(0, 2, 3, 4, 1)
    return jnp.pad(r, ((0, 0),) * 4 + ((0, keys - n),))


def _nsa_sample_layer(pr3, lw, tabs, layer, cache_cmp, cache_sel, win_state, page_table, win_acc):
    b, s_new = pr3.shape[:2]
    g, hpg, d = NSA_KV_HEADS, NSA_HPG, NSA_HEAD_DIM
    n_pages = page_table.shape[1]
    past_len = n_pages * PAGE_SIZE
    new = lambda c: pr3[:, :, c:c + KV_W]
    cmp_new, sel_new, win_new = new(COL_CMP), new(COL_SEL), new(COL_WIN)
    n_past, n_tail = past_len // CMP_BLOCK, tabs['nc'] - past_len // CMP_BLOCK
    pe3 = lw['pe'].reshape(2, CMP_BLOCK, d)
    past = _compress_pages(layer, page_table, cache_cmp, pe3, lw['cw1'].reshape(2, CMP_BLOCK, d, d), lw['cw2'])
    tail = jnp.pad(cmp_new, ((0, 0), (0, n_tail * CMP_BLOCK - s_new), (0, 0)))
    tail = _compress(_flat_blocks(tail.reshape(b, n_tail * CMP_BLOCK, 2, g, d)), lw['pe'], lw['cw1'], lw['cw2'], 512)
    tail = tail.reshape(2, b, g, 2, n_tail // 2, d).transpose(1, 0, 2, 3, 4, 5)
    kcv = jnp.concatenate([past, tail], axis=4)
    kcv = jnp.pad(kcv, ((0, 0),) * 4 + ((0, SMP_CPAD - kcv.shape[4]), (0, 0)))
    kcv = kcv.reshape(b, 2, g, 2 * SMP_CPAD, d).transpose(1, 0, 2, 3, 4)
    qg = pr3[:, :, COL_Q:COL_Q + Q_W].reshape(b, s_new, g, hpg, d).transpose(0, 2, 3, 1, 4)
    qg = qg.reshape(b, g, hpg * s_new, d)
    gts = pr3[:, :, COL_MISC:COL_MISC + N_GATE].reshape(b, s_new, 3, g, hpg).transpose(0, 3, 4, 1, 2)
    gts = jnp.pad(gts.reshape(b, g, hpg * s_new, 3), ((0, 0), (0, 0), (0, 0), (0, 128 - 3)))
    o, win_acc = _nsa_sample(layer, tabs['tk'] // SEL_BLOCK, page_table, qg, kcv[0], kcv[1], tabs['bias_c'], tabs['ok_c'],
                    tabs['cur'], cache_sel, _to_native(sel_new, KEY_TILE), tabs['bias_s'], tabs['ok_s'],
                    tabs['expand'], win_state, _to_native(win_new, KEY_TILE), tabs['bias_w'], tabs['ok_w'], gts,
                    win_acc)
    o = o.reshape(b, g, hpg, s_new, d).transpose(0, 3, 1, 2, 4).reshape(b * s_new, Q_W)
    kv5 = lambda a: a.reshape(b, -1, 2, g, d)
    return o, kv5(cmp_new), kv5(sel_new), win_acc


def _mixers(pr, b, t, lw, conv0, h0, s0, nsa_fn):
    pr3 = pr.reshape(b, t, PROJ_COLS)
    o_a, cmp_rows, sel_rows, win_rows = nsa_fn(pr3)
    lru_w = (lw['lcw'], lw['lcb'], lw['lgw'], lw['lgb'], lw['lam'])
    gla_w = (lw['gaw'], lw['gab'], lw['gng'])
    if t % 8 == 0:
        tc = min(t, 256)
        o_b, conv_n, h_n = _lru(pr3, COL_LRU_X // LRU_WIDTH, pr3, COL_LRU_G // LRU_WIDTH, conv0, h0[:, None],
                                *lru_w, tc, tc)
        tg = min(t, 256)
        o_c, s_n = _gla(pr3, COL_GLA_Q // GLA_QK_W, pr3, COL_GLA_K // GLA_QK_W, pr3, COL_GLA_V // GLA_V_W,
                         pr3, COL_GLA_G // GLA_V_W, pr3, COL_MISC // MISC_W, *gla_w, s0, tg,
                         min(tg, GLA_CHUNK), tg)
    else:
        tp = -(-t // 8) * 8
        cut = lambda c, w: _pad_rows(pr3[:, :, c:c + w], tp)
        if CONV_WIDTH - 1 + t <= LRU_SEG:
            o_b, conv_n, h_n = _lru_short(pr3[:, :, COL_LRU_X:COL_LRU_X + LRU_WIDTH],
                                          pr3[:, :, COL_LRU_G:COL_LRU_G + LRU_WIDTH], conv0, h0, *lru_w)
            h_n = h_n[:, None]
        else:
            o_b, conv_n, h_n = _lru(cut(COL_LRU_X, LRU_WIDTH), 0, cut(COL_LRU_G, LRU_WIDTH), 0, conv0,
                                    h0[:, None], *lru_w, tp, t)
        gla_in = (cut(COL_GLA_Q, GLA_QK_W), cut(COL_GLA_K, GLA_QK_W), cut(COL_GLA_V, GLA_V_W),
                  cut(COL_GLA_G, GLA_V_W), cut(COL_MISC, MISC_W))
        if tp == LRU_SEG:
            o_c, s_n = _gla_short(*gla_in, *gla_w, s0, t)
        else:
            o_c, s_n = _gla(*(a for x in gla_in for a in (x, 0)), *gla_w, s0, tp, tp, t)
        o_b, o_c = o_b[:, :t], o_c[:, :t]
    o_b = o_b.reshape(b * t, LRU_WIDTH)
    o_c = o_c.reshape(b * t, GLA_V_W)
    return (o_a, o_b, o_c), (cmp_rows, sel_rows, win_rows, conv_n, h_n[:, 0], s_n)


def _layer(x, b, t, lw, conv0, h0, s0, nsa_fn, tm):
    pr = _proj(x, lw['ng'], lw['w_in'], lw['b_in'], lw['layer'], tm, PROJ_COLS // 5)
    (o_a, o_b, o_c), states = _mixers(pr, b, t, lw, conv0, h0, s0, nsa_fn)
    x = _merge(x, o_a, o_b, o_c, pr, lw['wb'], lw['wo'], lw['layer'], min(tm, 512))
    x = _mlp(x, lw['mg'], lw['w1'], lw['w2'], lw['layer'], tm, 1024)
    return x, states


def kernel(x_prompt, x_sample, cache_nsa_cmp_kv, cache_nsa_sel_kv, state_nsa_win_kv, state_lru_conv,
           state_lru_h, state_gla, page_table, rel_bias, norm_mix_g, norm_mlp_g, norm_final_g, w_in, b_in,
           nsa_cmp_pe, nsa_cmp_w1, nsa_cmp_w2, lru_gate_w, lru_gate_b, lru_lambda, lru_conv_w, lru_conv_b,
           gla_alpha_w, gla_alpha_b, gla_norm_g, w_branch, w_out, mlp_w1, mlp_w2):
    bp, tp = x_prompt.shape[:2]
    bs, ts = x_sample.shape[:2]
    depth = w_in.shape[0]
    n_pages = page_table.shape[1]
    w_buf = state_nsa_win_kv.shape[2]

    cols = [w_in[..., _SRC[n][0]:_SRC[n][0] + _SRC[n][1]] for n in _DST_ORDER]
    pad = PROJ_COLS - sum(c.shape[-1] for c in cols)
    w_in_p = jnp.concatenate(cols + [jnp.zeros(w_in.shape[:2] + (pad,), w_in.dtype)], axis=-1).astype(BF16)
    bcols = [b_in[..., _SRC[n][0]:_SRC[n][0] + _SRC[n][1]] for n in _DST_ORDER]
    b_in_p = jnp.concatenate(bcols + [jnp.zeros((depth, pad), b_in.dtype)], axis=-1)[:, None, :]
    eye = jnp.eye(LRU_BLOCKS, dtype=lru_gate_w.dtype)
    lgw = jnp.einsum('lznce,nm->lzncme', lru_gate_w, eye).reshape(depth, 2, LRU_WIDTH, LRU_WIDTH).astype(BF16)
    gaw = jnp.zeros((depth, MISC_W, GLA_QK_W), F32).at[:, N_GATE:N_GATE + GLA_RANK].set(gla_alpha_w).astype(BF16)
    pe = jnp.transpose(nsa_cmp_pe, (0, 2, 1, 3)).reshape(depth, 2, 1, CMP_BLOCK * NSA_HEAD_DIM)
    cw1 = nsa_cmp_w1.astype(BF16)
    cw2 = nsa_cmp_w2.astype(BF16)
    wb = w_branch.astype(BF16)
    wo = w_out.astype(BF16)
    w1 = mlp_w1.astype(BF16)
    w2 = mlp_w2.astype(BF16)
    cache_cmp = jnp.transpose(cache_nsa_cmp_kv, (0, 1, 3, 4, 5, 2))
    cache_sel = jnp.transpose(cache_nsa_sel_kv, (0, 1, 3, 4, 5, 2))
    win_state = jnp.transpose(state_nsa_win_kv, (0, 1, 3, 4, 5, 2))

    tabs_p = _prompt_bias_tables(rel_bias, tp)
    tabs_s = _sample_tables(rel_bias, n_pages * PAGE_SIZE, ts, w_buf)

    xp = x_prompt.reshape(bp * tp, D_MODEL)
    xs = x_sample.reshape(bs * ts, D_MODEL)
    conv0_p = jnp.zeros((bp, CONV_WIDTH - 1, LRU_WIDTH), F32)
    h0_p = jnp.zeros((bp, LRU_WIDTH), F32)
    s0_p = jnp.zeros((bp, GLA_HEADS, GLA_DK, GLA_DV), F32)
    outs_p = [[] for _ in range(6)]
    outs_s = [[] for _ in range(6)]
    win_acc = jnp.zeros(win_state.shape, F32)
    for l in range(depth):
        lw = dict(layer=l, ng=norm_mix_g[l][None], mg=norm_mlp_g[l][None], w_in=w_in_p, b_in=b_in_p,
                  pe=pe[l], cw1=cw1[l], cw2=cw2[l], lcw=lru_conv_w[l], lcb=lru_conv_b[l][None], lgw=lgw[l],
                  lgb=lru_gate_b[l], lam=lru_lambda[l][None], gaw=gaw[l], gab=gla_alpha_b[l][None],
                  gng=gla_norm_g[l][None], wb=wb, wo=wo, w1=w1, w2=w2)
        xp, st_p = _layer(xp, bp, tp, lw, conv0_p, h0_p, s0_p,
                          functools.partial(_nsa_prompt_layer, lw=lw, tabs=tabs_p), _row_tile(bp * tp, 1024))
        nsa_s = functools.partial(_nsa_sample_layer, lw=lw, tabs=tabs_s, layer=l, cache_cmp=cache_cmp,
                                  cache_sel=cache_sel, win_state=win_state, page_table=page_table, win_acc=win_acc)
        xs, st_s = _layer(xs, bs, ts, lw, state_lru_conv[l], state_lru_h[l], state_gla[l], nsa_s, bs * ts)
        win_acc = st_s[2]
        for j in range(6):
            outs_p[j].append(st_p[j])
            outs_s[j].append(st_s[j])
    y_prompt = _final_norm(xp, norm_final_g[None], _row_tile(bp * tp, 1024)).reshape(bp, tp, D_MODEL)
    y_sample = _final_norm(xs, norm_final_g[None], bs * ts).reshape(bs, ts, D_MODEL)
    st = lambda outs, j: jnp.stack(outs[j])
    win_s = win_acc.transpose(0, 1, 5, 2, 3, 4)
    return (y_prompt, y_sample, st(outs_p, 0), st(outs_s, 0), st(outs_p, 1), st(outs_s, 1),
            st(outs_p, 2), win_s, st(outs_p, 3), st(outs_s, 3), st(outs_p, 4), st(outs_s, 4),
            st(outs_p, 5), st(outs_s, 5))
```

```python
import functools
import math

import jax
import jax.numpy as jnp
import numpy as np
from jax import lax
from jax.experimental import pallas as pl
from jax.experimental.pallas import tpu as pltpu

F32 = jnp.float32
BF16 = jnp.bfloat16

D_MODEL = 1024
DEPTH = 4
PAGE_SIZE = 128
NSA_HEADS = 8
NSA_KV_HEADS = 2
NSA_HPG = NSA_HEADS // NSA_KV_HEADS
NSA_HEAD_DIM = 64
CMP_BLOCK = 32
SEL_BLOCK = 64
N_SELECT = 16
WINDOW = 512
FORCE_SCORE = 1e4
REL_BUCKETS = 32
REL_MAX_DIST = 128
LRU_WIDTH = 512
LRU_BLOCKS = 8
LRU_BLOCK_DIM = LRU_WIDTH // LRU_BLOCKS
CONV_WIDTH = 4
LRU_C = 8.0
GLA_HEADS = 4
GLA_DK = 64
GLA_DV = 128
GLA_RANK = 16
GLA_TAU = 16.0
GLA_CHUNK = 64
D_FF = 4 * D_MODEL
N_BRANCH = 3
BRANCH_WIDTH = 512
NORM_EPS = 1e-6

KV_W = 2 * NSA_KV_HEADS * NSA_HEAD_DIM
Q_W = NSA_HEADS * NSA_HEAD_DIM
GROUP_W = NSA_HPG * NSA_HEAD_DIM
GLA_QK_W = GLA_HEADS * GLA_DK
GLA_V_W = GLA_HEADS * GLA_DV
MISC_W = 128
N_GATE = 3 * NSA_HEADS

COL_Q = 0
COL_LRU_X = COL_Q + Q_W
COL_LRU_G = COL_LRU_X + LRU_WIDTH
COL_GLA_V = COL_LRU_G + LRU_WIDTH
COL_GLA_G = COL_GLA_V + GLA_V_W
COL_CMP = COL_GLA_G + GLA_V_W
COL_SEL = COL_CMP + KV_W
COL_WIN = COL_SEL + KV_W
COL_GLA_Q = COL_WIN + KV_W
COL_GLA_K = COL_GLA_Q + GLA_QK_W
COL_MISC = COL_GLA_K + GLA_QK_W
PROJ_COLS = 4096

_SRC = {}
_off = 0
for _name, _w in (('nsa_q', Q_W), ('nsa_cmp_kv', KV_W), ('nsa_sel_kv', KV_W), ('nsa_win_kv', KV_W),
                  ('nsa_gate', N_GATE), ('lru_x', LRU_WIDTH), ('lru_gate', LRU_WIDTH),
                  ('gla_q', GLA_QK_W), ('gla_k', GLA_QK_W), ('gla_v', GLA_V_W),
                  ('gla_alpha', GLA_RANK), ('gla_gate', GLA_V_W), ('merge_gate', N_BRANCH * D_MODEL)):
    _SRC[_name] = (_off, _w)
    _off += _w
_DST_ORDER = ('nsa_q', 'lru_x', 'lru_gate', 'gla_v', 'gla_gate', 'nsa_cmp_kv',
              'nsa_sel_kv', 'nsa_win_kv', 'gla_q', 'gla_k', 'nsa_gate', 'gla_alpha')

NSA_TQ = 128
NSA_LANES = NSA_HPG * NSA_TQ
NSA_CLASS = 2
LOG2E = math.log2(math.e)
ONES_ROWS = 16
VMEM_LIMIT = 56 * 1024 * 1024


def _cparams(sem):
    return pltpu.CompilerParams(dimension_semantics=sem, vmem_limit_bytes=VMEM_LIMIT)


def _gelu(x):
    return x * (0.5 * (1.0 + jnp.tanh(math.sqrt(2.0 / math.pi) * (x + 0.044715 * (x * x * x)))))


def _softplus(x):
    return jnp.maximum(x, 0.0) + jnp.log1p(jnp.exp(-jnp.abs(x)))


def _rms(x, g):
    return x * lax.rsqrt(jnp.mean(x * x, axis=-1, keepdims=True) + NORM_EPS) * g


def _dot(a, b):
    return jnp.dot(a, b, preferred_element_type=F32)


def _dot_nt(a, b):
    return lax.dot_general(a, b, (((1,), (1,)), ((), ())), preferred_element_type=F32)


def _dot_tn(a, b):
    return lax.dot_general(a, b, (((0,), (0,)), ((), ())), preferred_element_type=F32)


def _proj_kernel(x_ref, g_ref, w_ref, b_ref, o_ref, h_ref):
    @pl.when(pl.program_id(1) == 0)
    def _():
        h_ref[...] = _rms(x_ref[...], g_ref[...]).astype(BF16)

    o_ref[...] = _dot(h_ref[...], w_ref[...]) + b_ref[...]


def _proj(x, g, w, b, layer, tm, tn):
    n = x.shape[0]
    return pl.pallas_call(
        _proj_kernel,
        grid=(n // tm, PROJ_COLS // tn),
        in_specs=[pl.BlockSpec((tm, D_MODEL), lambda i, j: (i, 0)),
                  pl.BlockSpec((1, D_MODEL), lambda i, j: (0, 0)),
                  pl.BlockSpec((None, D_MODEL, tn), lambda i, j: (layer, 0, j)),
                  pl.BlockSpec((None, 1, tn), lambda i, j: (layer, 0, j))],
        out_specs=pl.BlockSpec((tm, tn), lambda i, j: (i, j)),
        out_shape=jax.ShapeDtypeStruct((n, PROJ_COLS), F32),
        scratch_shapes=[pltpu.VMEM((tm, D_MODEL), BF16)],
        compiler_params=_cparams(("parallel", "arbitrary")),
        name="proj",
    )(x, g, w, b)


def _merge_kernel(x_ref, oa_ref, ob_ref, oc_ref, ng_ref, wg_ref, bg_ref, wb_ref, wo_ref, o_ref):
    h = _rms(x_ref[...], ng_ref[...]).astype(BF16)
    m = None
    for z, o_z in enumerate((oa_ref, ob_ref, oc_ref)):
        cols = slice(z * D_MODEL, (z + 1) * D_MODEL)
        gate = jax.nn.sigmoid(_dot(h, wg_ref[:, cols]) + bg_ref[:, cols])
        term = gate * _dot(o_z[...].astype(BF16), wb_ref[z])
        m = term if m is None else m + term
    o_ref[...] = x_ref[...] + _dot(m.astype(BF16), wo_ref[...])


def _merge(x, oa, ob, oc, ng, wg, bg, wb, wo, layer, tm):
    n = x.shape[0]
    row = lambda w: pl.BlockSpec((tm, w), lambda i: (i, 0))
    return pl.pallas_call(
        _merge_kernel,
        grid=(n // tm,),
        in_specs=[row(D_MODEL), row(BRANCH_WIDTH), row(BRANCH_WIDTH), row(BRANCH_WIDTH),
                  pl.BlockSpec((1, D_MODEL), lambda i: (0, 0)),
                  pl.BlockSpec((None, D_MODEL, N_BRANCH * D_MODEL), lambda i: (layer, 0, 0)),
                  pl.BlockSpec((None, 1, N_BRANCH * D_MODEL), lambda i: (layer, 0, 0)),
                  pl.BlockSpec((None, N_BRANCH, BRANCH_WIDTH, D_MODEL), lambda i: (layer, 0, 0, 0)),
                  pl.BlockSpec((None, D_MODEL, D_MODEL), lambda i: (layer, 0, 0))],
        out_specs=row(D_MODEL),
        out_shape=jax.ShapeDtypeStruct((n, D_MODEL), F32),
        compiler_params=_cparams(("parallel",)),
        name="merge",
    )(x, oa, ob, oc, ng, wg, bg, wb, wo)


def _mlp_kernel(x_ref, g_ref, w1_ref, w2_ref, o_ref, h_ref, acc_ref):
    f = pl.program_id(1)

    @pl.when(f == 0)
    def _():
        h_ref[...] = _rms(x_ref[...], g_ref[...]).astype(BF16)
        acc_ref[...] = jnp.zeros_like(acc_ref)

    a = jnp.maximum(_dot(h_ref[...], w1_ref[...]), 0.0)
    acc_ref[...] += _dot((a * a).astype(BF16), w2_ref[...])

    @pl.when(f == pl.num_programs(1) - 1)
    def _():
        o_ref[...] = x_ref[...] + acc_ref[...]


def _mlp(x, g, w1, w2, layer, tm, tf):
    n = x.shape[0]
    return pl.pallas_call(
        _mlp_kernel,
        grid=(n // tm, D_FF // tf),
        in_specs=[pl.BlockSpec((tm, D_MODEL), lambda i, f: (i, 0)),
                  pl.BlockSpec((1, D_MODEL), lambda i, f: (0, 0)),
                  pl.BlockSpec((None, D_MODEL, tf), lambda i, f: (layer, 0, f)),
                  pl.BlockSpec((None, tf, D_MODEL), lambda i, f: (layer, f, 0))],
        out_specs=pl.BlockSpec((tm, D_MODEL), lambda i, f: (i, 0)),
        out_shape=jax.ShapeDtypeStruct((n, D_MODEL), F32),
        scratch_shapes=[pltpu.VMEM((tm, D_MODEL), BF16), pltpu.VMEM((tm, D_MODEL), F32)],
        compiler_params=_cparams(("parallel", "arbitrary")),
        name="mlp",
    )(x, g, w1, w2)


def _norm_kernel(x_ref, g_ref, o_ref):
    o_ref[...] = _rms(x_ref[...], g_ref[...])


def _final_norm(x, g, tm):
    n = x.shape[0]
    return pl.pallas_call(
        _norm_kernel,
        grid=(n // tm,),
        in_specs=[pl.BlockSpec((tm, D_MODEL), lambda i: (i, 0)),
                  pl.BlockSpec((1, D_MODEL), lambda i: (0, 0))],
        out_specs=pl.BlockSpec((tm, D_MODEL), lambda i: (i, 0)),
        out_shape=jax.ShapeDtypeStruct((n, D_MODEL), F32),
        compiler_params=_cparams(("parallel",)),
        name="final_norm",
    )(x, g)


_XB = 8


def _lru_kernel(x_ref, gb_ref, conv0_ref, h0_ref, cw_ref, cb_ref, gw_ref, gbias_ref, lam_ref,
                o_ref, convn_ref, hn_ref, xbuf, hcar, *, tc, tv):
    @pl.when(pl.program_id(1) == 0)
    def _():
        xbuf[0:_XB, :] = jnp.zeros((_XB, LRU_WIDTH), F32)
        xbuf[_XB - 3:_XB, :] = conv0_ref[0]
        hcar[...] = h0_ref[0]

    x = x_ref[0]
    xbuf[_XB:_XB + tc, :] = x
    w = cw_ref[...]
    xc = cb_ref[...] + xbuf[_XB - 3:_XB - 3 + tc, :] * w[0:1]
    xc = xc + xbuf[_XB - 2:_XB - 2 + tc, :] * w[1:2]
    xc = xc + xbuf[_XB - 1:_XB - 1 + tc, :] * w[2:3]
    xc = xc + x * w[3:4]
    tail = xbuf[_XB - 3 + tv:_XB + tv, :]
    convn_ref[0] = tail
    xbuf[_XB - 3:_XB, :] = tail

    xcb = xc.astype(BF16)
    r = jax.nn.sigmoid(_dot(xcb, gw_ref[0]) + gbias_ref[0:1])
    i = jax.nn.sigmoid(_dot(xcb, gw_ref[1]) + gbias_ref[1:2])
    log_a = (-LRU_C * r) * _softplus(-lam_ref[...])
    a = jnp.exp(log_a)
    b = jnp.sqrt(-jnp.tanh(log_a) * (a * a + 1.0)) * (i * xc)

    pos = lax.broadcasted_iota(jnp.int32, (tc, LRU_WIDTH), 0) & (LRU_SEG - 1)
    s = 1
    while s < LRU_SEG:
        m = pos >= s
        b = jnp.where(m, a * pltpu.roll(b, s, 0) + b, b)
        a = jnp.where(m, a * pltpu.roll(a, s, 0), a)
        s *= 2
    h_in = hcar[...]
    groups = []
    for j in range(tc // LRU_SEG):
        hj = a[j * LRU_SEG:(j + 1) * LRU_SEG] * h_in + b[j * LRU_SEG:(j + 1) * LRU_SEG]
        groups.append(hj)
        h_in = hj[LRU_SEG - 1:LRU_SEG]
    h = jnp.concatenate(groups, axis=0)
    hlast = h[tv - 1:tv]
    hcar[...] = hlast
    hn_ref[0] = hlast
    o_ref[0] = _gelu(gb_ref[0]) * h


def _lru(x_arr, x_blk, gb_arr, gb_blk, conv0, h0, cw, cb, gw, gbias, lam, tc, tv):
    b, t = x_arr.shape[:2]
    r = LRU_WIDTH
    const2 = lambda shape: pl.BlockSpec(shape, lambda i, c: (0, 0))
    return pl.pallas_call(
        functools.partial(_lru_kernel, tc=tc, tv=tv),
        grid=(b, t // tc),
        in_specs=[pl.BlockSpec((1, tc, r), lambda i, c: (i, c, x_blk)),
                  pl.BlockSpec((1, tc, r), lambda i, c: (i, c, gb_blk)),
                  pl.BlockSpec((1, CONV_WIDTH - 1, r), lambda i, c: (i, 0, 0)),
                  pl.BlockSpec((1, 1, r), lambda i, c: (i, 0, 0)),
                  const2((CONV_WIDTH, r)), const2((1, r)),
                  pl.BlockSpec((2, r, r), lambda i, c: (0, 0, 0)),
                  const2((2, r)), const2((1, r))],
        out_specs=[pl.BlockSpec((1, tc, r), lambda i, c: (i, c, 0)),
                   pl.BlockSpec((1, CONV_WIDTH - 1, r), lambda i, c: (i, 0, 0)),
                   pl.BlockSpec((1, 1, r), lambda i, c: (i, 0, 0))],
        out_shape=[jax.ShapeDtypeStruct((b, t, r), F32),
                   jax.ShapeDtypeStruct((b, CONV_WIDTH - 1, r), F32),
                   jax.ShapeDtypeStruct((b, 1, r), F32)],
        scratch_shapes=[pltpu.VMEM((_XB + tc, r), F32), pltpu.VMEM((1, r), F32)],
        compiler_params=_cparams(("parallel", "arbitrary")),
        name="rglru",
    )(x_arr, gb_arr, conv0, h0, cw, cb, gw, gbias, lam)


LRU_SEG = 8


def _lru_short_kernel(x_ref, gb_ref, h0_ref, cw_ref, cb_ref, gw_ref, gbias_ref, lam_ref, o_ref, h_ref, *, t):
    x = x_ref[...]
    pos = lax.broadcasted_iota(jnp.int32, x.shape, 0) & (LRU_SEG - 1)
    w = cw_ref[...]
    xc = cb_ref[...] + x * w[CONV_WIDTH - 1:CONV_WIDTH]
    for k in range(1, CONV_WIDTH):
        xc = xc + pltpu.roll(x, k, 0) * w[CONV_WIDTH - 1 - k:CONV_WIDTH - k]
    xcb = xc.astype(BF16)
    r = jax.nn.sigmoid(_dot(xcb, gw_ref[0]) + gbias_ref[0:1])
    i = jax.nn.sigmoid(_dot(xcb, gw_ref[1]) + gbias_ref[1:2])
    log_a = (-LRU_C * r) * _softplus(-lam_ref[...])
    a = jnp.exp(log_a)
    b = jnp.sqrt(-jnp.tanh(log_a) * (a * a + 1.0)) * (i * xc)
    real = (pos >= CONV_WIDTH - 1) & (pos < CONV_WIDTH - 1 + t)
    a = jnp.where(real, a, 1.0)
    b = jnp.where(real, b, 0.0)
    s = 1
    while s < LRU_SEG:
        m = pos >= s
        b = jnp.where(m, a * pltpu.roll(b, s, 0) + b, b)
        a = jnp.where(m, a * pltpu.roll(a, s, 0), a)
        s *= 2
    h = a * h0_ref[...] + b
    h_ref[...] = h
    o_ref[...] = _gelu(gb_ref[...]) * h


def _lru_short(x, gb, conv0, h0, cw, cb, gw, gbias, lam):
    b, t, r = x.shape
    lead = CONV_WIDTH - 1
    seg = lambda head, body: jnp.concatenate(
        [head, body, jnp.zeros((b, LRU_SEG - lead - t, r), F32)], axis=1).reshape(b * LRU_SEG, r)
    xin = seg(conv0, x)
    rows = b * LRU_SEG
    tm = _row_tile(rows, 256)
    blk = pl.BlockSpec((tm, r), lambda i: (i, 0))
    const2 = lambda shape: pl.BlockSpec(shape, lambda i: (0, 0))
    o, h = pl.pallas_call(
        functools.partial(_lru_short_kernel, t=t),
        grid=(rows // tm,),
        in_specs=[blk, blk, blk, const2((CONV_WIDTH, r)), const2((1, r)),
                  pl.BlockSpec((2, r, r), lambda i: (0, 0, 0)), const2((2, r)), const2((1, r))],
        out_specs=[blk, blk],
        out_shape=[jax.ShapeDtypeStruct((rows, r), F32), jax.ShapeDtypeStruct((rows, r), F32)],
        compiler_params=_cparams(("parallel",)),
        name="rglru_short",
    )(xin, seg(jnp.zeros((b, lead, r), F32), gb), jnp.repeat(h0, LRU_SEG, axis=0), cw, cb, gw, gbias, lam)
    o = o.reshape(b, LRU_SEG, r)[:, lead:lead + t]
    h_new = h.reshape(b, LRU_SEG, r)[:, lead + t - 1]
    conv_new = xin.reshape(b, LRU_SEG, r)[:, t:t + lead]
    return o, conv_new, h_new


def _gla_kernel(q_ref, k_ref, v_ref, og_ref, misc_ref, aw_ref, ab_ref, ng_ref, s0_ref,
                o_ref, sn_ref, st, *, tg, ck, tv):
    @pl.when(pl.program_id(1) == 0)
    def _():
        for h in range(GLA_HEADS):
            st[h] = s0_ref[0, h].T

    pre = _dot(misc_ref[0].astype(BF16), aw_ref[...]) + ab_ref[...]
    g = -_softplus(-pre) * (1.0 / GLA_TAU)
    if tv < tg:
        g = jnp.where(lax.broadcasted_iota(jnp.int32, g.shape, 0) < tv, g, 0.0)
    q = q_ref[0] * (GLA_DK ** -0.5)
    k = k_ref[0]
    v = v_ref[0]
    og = og_ref[0]
    ng = ng_ref[...]
    rows = lax.broadcasted_iota(jnp.int32, (ck, GLA_QK_W), 0)
    tril = (lax.broadcasted_iota(jnp.int32, (ck, ck), 0) >= lax.broadcasted_iota(jnp.int32, (ck, ck), 1))
    for c in range(tg // ck):
        sl = slice(c * ck, (c + 1) * ck)
        bc = g[sl]
        s = 1
        while s < ck:
            bc = bc + jnp.where(rows >= s, pltpu.roll(bc, s, 0), 0.0)
            s *= 2
        bl = bc[ck - 1:ck]
        e = jnp.exp(bc)
        qi = (q[sl] * e).astype(BF16)
        ki = (k[sl] * jnp.exp(-bc)).astype(BF16)
        kd = (k[sl] * jnp.exp(bl - bc)).astype(BF16)
        dec = jnp.exp(bl)
        vb = v[sl].astype(BF16)
        outs = []
        for h in range(GLA_HEADS):
            ks = slice(h * GLA_DK, (h + 1) * GLA_DK)
            vs = slice(h * GLA_DV, (h + 1) * GLA_DV)
            att = jnp.where(tril, _dot_nt(qi[:, ks], ki[:, ks]), 0.0)
            s_prev = st[h]
            o = _dot(att.astype(BF16), vb[:, vs]) + _dot_nt(qi[:, ks], s_prev.astype(BF16))
            st[h] = s_prev * dec[:, ks] + _dot_tn(vb[:, vs], kd[:, ks])
            o = _rms(o, ng[:, vs])
            ogh = og[sl, vs]
            outs.append(o * (ogh * jax.nn.sigmoid(ogh)))
        o_ref[0, sl, :] = jnp.concatenate(outs, axis=-1)

    @pl.when(pl.program_id(1) == pl.num_programs(1) - 1)
    def _():
        for h in range(GLA_HEADS):
            sn_ref[0, h] = st[h].T


def _gla(q_arr, q_blk, k_arr, k_blk, v_arr, v_blk, og_arr, og_blk, misc_arr, misc_blk,
         aw, ab, ng, s0, tg, ck, tv):
    b, t = q_arr.shape[:2]
    col = lambda w, blk: pl.BlockSpec((1, tg, w), lambda i, c: (i, c, blk))
    const2 = lambda shape: pl.BlockSpec(shape, lambda i, c: (0, 0))
    state = pl.BlockSpec((1, GLA_HEADS, GLA_DK, GLA_DV), lambda i, c: (i, 0, 0, 0))
    return pl.pallas_call(
        functools.partial(_gla_kernel, tg=tg, ck=ck, tv=tv),
        grid=(b, t // tg),
        in_specs=[col(GLA_QK_W, q_blk), col(GLA_QK_W, k_blk), col(GLA_V_W, v_blk), col(GLA_V_W, og_blk),
                  col(MISC_W, misc_blk), const2((MISC_W, GLA_QK_W)), const2((1, GLA_QK_W)),
                  const2((1, GLA_V_W)), state],
        out_specs=[pl.BlockSpec((1, tg, GLA_V_W), lambda i, c: (i, c, 0)), state],
        out_shape=[jax.ShapeDtypeStruct((b, t, GLA_V_W), F32),
                   jax.ShapeDtypeStruct((b, GLA_HEADS, GLA_DK, GLA_DV), F32)],
        scratch_shapes=[pltpu.VMEM((GLA_HEADS, GLA_DV, GLA_DK), F32)],
        compiler_params=_cparams(("parallel", "arbitrary")),
        name="gla",
    )(q_arr, k_arr, v_arr, og_arr, misc_arr, aw, ab, ng, s0)


GLA_SHORT_SEQS = 16


def _gla_short_kernel(q_ref, k_ref, v_ref, og_ref, misc_ref, aw_ref, ab_ref, ng_ref, s0_ref, o_ref, sn_ref,
                      *, nb, tv):
    seg = LRU_SEG
    rows = nb * seg
    flat = lambda ref: ref[...].reshape(rows, ref.shape[-1])
    part = lambda x, n: x[n * seg:(n + 1) * seg]
    pre = _dot(flat(misc_ref).astype(BF16), aw_ref[...]) + ab_ref[...]
    pos = lax.broadcasted_iota(jnp.int32, (rows, GLA_QK_W), 0) & (seg - 1)
    bc = jnp.where(pos < tv, -_softplus(-pre) * (1.0 / GLA_TAU), 0.0)
    s = 1
    while s < seg:
        bc = bc + jnp.where(pos >= s, pltpu.roll(bc, s, 0), 0.0)
        s *= 2
    last = bc.reshape(nb, seg, GLA_QK_W)[:, seg - 1:seg, :]
    bl = jnp.broadcast_to(last, (nb, seg, GLA_QK_W)).reshape(rows, GLA_QK_W)
    dec_t = jnp.exp(last.reshape(nb, GLA_QK_W)).T
    q = flat(q_ref) * (GLA_DK ** -0.5)
    k = flat(k_ref)
    qi = q * jnp.exp(bc)
    ki = k * jnp.exp(-bc)
    kd = k * jnp.exp(bl - bc)
    v = flat(v_ref)
    tril = lax.broadcasted_iota(jnp.int32, (seg, seg), 0) >= lax.broadcasted_iota(jnp.int32, (seg, seg), 1)
    pairs = [(n, h) for n in range(nb) for h in range(GLA_HEADS)]
    ks = lambda h: slice(h * GLA_DK, (h + 1) * GLA_DK)
    vs = lambda h: slice(h * GLA_DV, (h + 1) * GLA_DV)
    bf = lambda x: x.astype(BF16)
    qs = [bf(part(qi, n)[:, ks(h)]) for n, h in pairs]
    vb = [bf(part(v, n)[:, vs(h)]) for n, h in pairs]
    att = [bf(jnp.where(tril, _dot_nt(qp, bf(part(ki, n)[:, ks(h)])), 0.0)) for qp, (n, h) in zip(qs, pairs)]
    o = [_dot(a, vp) + _dot(qp, bf(s0_ref[n, h])) for a, vp, qp, (n, h) in zip(att, vb, qs, pairs)]
    for vp, (n, h) in zip(vb, pairs):
        sn_ref[n, h] = s0_ref[n, h] * dec_t[ks(h), n:n + 1] + _dot_tn(bf(part(kd, n)[:, ks(h)]), vp)
    o = jnp.concatenate([jnp.concatenate(o[n * GLA_HEADS:(n + 1) * GLA_HEADS], axis=1) for n in range(nb)], axis=0)
    ng = ng_ref[...]
    o = jnp.concatenate([_rms(o[:, vs(h)], ng[:, vs(h)]) for h in range(GLA_HEADS)], axis=1)
    og = flat(og_ref)
    o_ref[...] = (o * (og * jax.nn.sigmoid(og))).reshape(nb, seg, GLA_V_W)


def _gla_short(q, k, v, og, misc, aw, ab, ng, s0, tv):
    b = q.shape[0]
    nb = next((c for c in (GLA_SHORT_SEQS, 8) if b % c == 0), b)
    seq = lambda w: pl.BlockSpec((nb, LRU_SEG, w), lambda i: (i, 0, 0))
    const2 = lambda shape: pl.BlockSpec(shape, lambda i: (0, 0))
    state = pl.BlockSpec((nb, GLA_HEADS, GLA_DK, GLA_DV), lambda i: (i, 0, 0, 0))
    return pl.pallas_call(
        functools.partial(_gla_short_kernel, nb=nb, tv=tv),
        grid=(b // nb,),
        in_specs=[seq(GLA_QK_W), seq(GLA_QK_W), seq(GLA_V_W), seq(GLA_V_W), seq(MISC_W),
                  const2((MISC_W, GLA_QK_W)), const2((1, GLA_QK_W)), const2((1, GLA_V_W)), state],
        out_specs=[seq(GLA_V_W), state],
        out_shape=[jax.ShapeDtypeStruct((b, LRU_SEG, GLA_V_W), F32),
                   jax.ShapeDtypeStruct((b, GLA_HEADS, GLA_DK, GLA_DV), F32)],
        compiler_params=_cparams(("parallel",)),
        name="gla_short",
    )(q, k, v, og, misc, aw, ab, ng, s0)


def _bias_kernel(tbl_ref, dist_ref, o_ref):
    h = pl.program_id(0)
    max_exact = REL_BUCKETS // 2
    n = jnp.maximum(dist_ref[...], 0)
    nf = jnp.maximum(n, 1).astype(F32)
    large = max_exact + (jnp.log(nf / max_exact) / math.log(REL_MAX_DIST / max_exact)
                         * (REL_BUCKETS - max_exact)).astype(jnp.int32)
    bucket = jnp.where(n < max_exact, n, jnp.minimum(large, REL_BUCKETS - 1))
    out = jnp.zeros(bucket.shape, F32)
    for kk in range(REL_BUCKETS):
        out = jnp.where(bucket == kk, tbl_ref[kk, h], out)
    o_ref[0] = out


def _bias_lookup(rel_bias, dist, tr):
    r, c = dist.shape
    return pl.pallas_call(
        _bias_kernel,
        grid=(NSA_HEADS, r // tr),
        in_specs=[pl.BlockSpec(memory_space=pltpu.SMEM),
                  pl.BlockSpec((tr, c), lambda h, i: (i, 0))],
        out_specs=pl.BlockSpec((1, tr, c), lambda h, i: (h, i, 0)),
        out_shape=jax.ShapeDtypeStruct((NSA_HEADS, r, c), F32),
        compiler_params=_cparams(("parallel", "parallel")),
        name="rel_bias",
    )(rel_bias, dist)


def _cmp_kernel(x_ref, pe_ref, w1_ref, w2_ref, o_ref):
    hid = _gelu(_dot((x_ref[0] + pe_ref[0]).astype(BF16), w1_ref[0]))
    o_ref[0] = _dot(hid.astype(BF16), w2_ref[0])


def _row_tile(m, cap):
    for tm in range(min(cap, m) // 8 * 8, 0, -8):
        if m % tm == 0:
            return tm
    return m


def _compress(flat, pe, w1, w2, cap):
    m = flat.shape[1]
    tm = _row_tile(m, cap)
    fw = CMP_BLOCK * NSA_HEAD_DIM
    d = NSA_HEAD_DIM
    return pl.pallas_call(
        _cmp_kernel,
        grid=(2, m // tm),
        in_specs=[pl.BlockSpec((1, tm, fw), lambda z, i: (z, i, 0)),
                  pl.BlockSpec((1, 1, fw), lambda z, i: (z, 0, 0)),
                  pl.BlockSpec((1, fw, d), lambda z, i: (z, 0, 0)),
                  pl.BlockSpec((1, d, d), lambda z, i: (z, 0, 0))],
        out_specs=pl.BlockSpec((1, tm, d), lambda z, i: (z, i, 0)),
        out_shape=jax.ShapeDtypeStruct((2, m, d), F32),
        compiler_params=_cparams(("parallel", "parallel")),
        name="nsa_compress",
    )(flat, pe, w1, w2)


CMP_PITCH = CMP_BLOCK + 4


def _cmp_pages_kernel(pt_ref, *refs, nb, n_pages):
    del pt_ref
    pages = refs[:nb * n_pages]
    pe_ref, w1_ref, w2_ref, o_ref, x_scr, h_scr = refs[nb * n_pages:]
    gd = NSA_KV_HEADS * NSA_HEAD_DIM
    per_page = PAGE_SIZE // CMP_BLOCK
    nblk = n_pages * per_page
    for z in range(2):
        for n in range(nb):
            for p in range(n_pages):
                xt = pages[n * n_pages + p][0, 0, z].reshape(gd, PAGE_SIZE).T
                for j in range(per_page):
                    r0 = (n * nblk + p * per_page + j) * CMP_PITCH
                    x_scr[z, r0:r0 + CMP_BLOCK, :] = xt[j * CMP_BLOCK:(j + 1) * CMP_BLOCK]
        acc = jnp.zeros((nb * nblk, gd), F32)
        for t in range(CMP_BLOCK):
            rows = x_scr[z, pl.ds(t, nb * nblk, stride=CMP_PITCH), :]
            acc = acc + _dot((rows + pe_ref[z, t:t + 1, :]).astype(BF16), w1_ref[z, t])
        h_scr[z] = _dot(_gelu(acc).astype(BF16), w2_ref[z])
        for n in range(nb):
            for par in range(2):
                o_ref[n, z, par] = h_scr[z, pl.ds(n * nblk + par, nblk // 2, stride=2), :]


def _compress_pages(layer, page_table, cache, pe, w1, w2):
    b, n_pages = page_table.shape
    g, d = NSA_KV_HEADS, NSA_HEAD_DIM
    gd = g * d
    nblk = n_pages * PAGE_SIZE // CMP_BLOCK
    eye = jnp.eye(g, dtype=w1.dtype)
    w1 = jnp.einsum('gh,ztje->ztgjhe', eye, w1).reshape(2, CMP_BLOCK, gd, gd)
    w2 = jnp.einsum('gh,zje->zgjhe', eye, w2).reshape(2, gd, gd)
    pe = jnp.tile(pe, (1, 1, g))
    const = lambda *shape: pl.BlockSpec(shape, lambda i, pt: (0,) * len(shape))
    nb = SMP_SEQS if b % SMP_SEQS == 0 else 1

    def page_spec(n, p):
        return pl.BlockSpec((1, 1, 2, g, d, PAGE_SIZE), lambda i, pt: (layer, pt[nb * i + n, p], 0, 0, 0, 0))

    out = pl.pallas_call(
        functools.partial(_cmp_pages_kernel, nb=nb, n_pages=n_pages),
        grid_spec=pltpu.PrefetchScalarGridSpec(
            num_scalar_prefetch=1,
            grid=(b // nb,),
            in_specs=[page_spec(n, p) for n in range(nb) for p in range(n_pages)]
            + [const(2, CMP_BLOCK, gd), const(2, CMP_BLOCK, gd, gd), const(2, gd, gd)],
            out_specs=pl.BlockSpec((nb, 2, 2, nblk // 2, gd), lambda i, pt: (i, 0, 0, 0, 0)),
            scratch_shapes=[pltpu.VMEM((2, nb * nblk * CMP_PITCH, gd), F32), pltpu.VMEM((2, nb * nblk, gd), F32)]),
        out_shape=jax.ShapeDtypeStruct((b, 2, 2, nblk // 2, gd), F32),
        compiler_params=_cparams(("parallel",)),
        name="nsa_compress_pages",
    )(page_table, *([cache] * (nb * n_pages)), pe, w1, w2)
    return out.reshape(b, 2, 2, nblk // 2, g, d).transpose(0, 1, 4, 2, 3, 5)


def _nsa_prompt_kernel(q_ref, kc_ref, vc_ref, bc_ref, ks_ref, vs_ref, kw_ref, vw_ref, bt_ref, e_ref, gt_ref,
                       o_ref, s_scr, p_scr, m_scr, selk_scr, os_scr, ow_scr):
    i = pl.program_id(2)
    tq, d, ck, hpg = NSA_TQ, NSA_HEAD_DIM, NSA_TQ, NSA_HPG
    nchunk = ks_ref.shape[3]
    nsel = kc_ref.shape[2] // 2
    neg = -jnp.inf
    qf = q_ref[0] * (d ** -0.5 * LOG2E)
    qs = jnp.concatenate([qf[:, h * d:(h + 1) * d] for h in range(hpg)], axis=0).astype(BF16)

    tk = lax.broadcasted_iota(jnp.int32, (tq, ck), 0) - lax.broadcasted_iota(jnp.int32, (tq, ck), 1)

    def mask_heads(s, ok):
        return jnp.concatenate([jnp.where(ok, s[h * tq:(h + 1) * tq], neg) for h in range(hpg)], axis=0)

    def attend(k_ref, v_ref, n, ok_fn):
        m_scr[...] = jnp.full(m_scr.shape, neg, F32)
        for r in range(n):
            c = i - r
            cc = jnp.maximum(c, 0)
            s = _dot(qs, k_ref[0, 0, 0, cc])
            if r < 2:
                s = s + bt_ref[0, r]
            s = mask_heads(s, ok_fn(r, c, cc))
            s_scr[:, r * ck:(r + 1) * ck] = s
            m_scr[...] = jnp.maximum(m_scr[...], s)
        m = jnp.max(m_scr[...], axis=1, keepdims=True)
        m = jnp.where(m > neg, m, 0.0)
        for r in range(n):
            p_scr[:, r * ck:(r + 1) * ck] = jnp.exp2(s_scr[:, r * ck:(r + 1) * ck] - m).astype(BF16)
        vt = jnp.concatenate([v_ref[0, 0, 0, jnp.maximum(i - r, 0)] for r in range(n)], axis=1)
        vt = jnp.concatenate([vt, jnp.ones((ONES_ROWS, n * ck), BF16)], axis=0)
        acc = _dot_nt(p_scr[:, 0:n * ck], vt)
        return acc[:, 0:d] / jnp.maximum(acc[:, d:d + 1], 1e-30)

    nw = WINDOW // ck + 1

    def win_ok(r, c, cc):
        if r == 0:
            return tk >= 0
        if r == nw - 1:
            return tk < jnp.where(c >= 0, 0, -tq)
        return tk > jnp.where(c >= 0, -tq, tq)

    ow_scr[...] = attend(kw_ref, vw_ref, nw, win_ok)

    t_lane = i * tq + (lax.broadcasted_iota(jnp.int32, (1, hpg * tq), 1) & (tq - 1))
    s_c = _dot_nt(kc_ref[0, 0], qs) + bc_ref[0, 0]
    r_c = lax.broadcasted_iota(jnp.int32, (2 * nsel, hpg * tq), 0)
    n_c = jnp.where(r_c < nsel, 2 * r_c, 2 * (r_c - nsel) + 1)
    s_c = jnp.where(t_lane >= (n_c + 1) * CMP_BLOCK - 1, s_c, neg)
    m_c = jnp.max(s_c, axis=0, keepdims=True)
    m_c = jnp.where(m_c > neg, m_c, 0.0)
    p_c = jnp.exp2(s_c - m_c)
    p_c = p_c / jnp.maximum(jnp.sum(p_c, axis=0, keepdims=True), 1e-30)
    o_c = _dot_tn(p_c.astype(BF16), vc_ref[0, 0])

    ph = p_c[:, 0:tq]
    for h in range(1, hpg):
        ph = ph + p_c[:, h * tq:(h + 1) * tq]
    imp = ph[0:nsel] + ph[nsel:2 * nsel]
    blk = lax.broadcasted_iota(jnp.int32, (nsel, tq), 0)
    cur = (i * tq + lax.broadcasted_iota(jnp.int32, (nsel, tq), 1)) >> int(math.log2(SEL_BLOCK))
    forced = (blk == 0) | (blk == cur) | (blk == cur - 1)
    imp = jnp.where(forced, FORCE_SCORE, jnp.where(blk <= cur, imp, neg))
    rank = jnp.zeros((nsel, tq), jnp.int32)
    for s2 in range(nsel):
        row = imp[s2:s2 + 1]
        beats = (row > imp) | ((row == imp) & (blk > s2))
        rank = rank + beats.astype(jnp.int32)
    chosen = (rank < N_SELECT).astype(BF16)
    selk = _dot_tn(chosen, e_ref[...])
    for c in range(nchunk):
        selk_scr[c] = selk[:, c * ck:(c + 1) * ck]

    def sel_ok(r, c, cc):
        ok = selk_scr[cc] > jnp.where(c >= 0, 0.5, 2.0)
        return ok & (tk >= 0) if r == 0 else ok

    sizes = list(range(NSA_CLASS, nchunk, NSA_CLASS)) + [nchunk]
    for lo, n in zip([0] + sizes[:-1], sizes):
        @pl.when((i >= lo) & (i < n))
        def _(n=n):
            os_scr[...] = attend(ks_ref, vs_ref, n, sel_ok)

    o_w = ow_scr[...]
    o_s = os_scr[...]

    sig = jax.nn.sigmoid(gt_ref[0, 0])
    outs = []
    for h in range(hpg):
        r = slice(h * tq, (h + 1) * tq)
        outs.append(sig[:, h:h + 1] * o_c[r] + sig[:, hpg + h:hpg + h + 1] * o_s[r]
                    + sig[:, 2 * hpg + h:2 * hpg + h + 1] * o_w[r])
    o_ref[0] = jnp.concatenate(outs, axis=1)


def _nsa_prompt(pr3, kc, vc, bias_c, kvt, bias_t, expand, gts):
    b, t = pr3.shape[:2]
    g, d, tq, hpg = NSA_KV_HEADS, NSA_HEAD_DIM, NSA_TQ, NSA_HPG
    nt = t // tq
    nc = kc.shape[2]
    rows = hpg * tq
    per_bg = lambda r, c: pl.BlockSpec((1, 1, r, c), lambda bi, gi, i: (bi, gi, 0, 0))
    kv = lambda br, z: pl.BlockSpec((None, 1, 1, 1, nt, d, tq), lambda bi, gi, i: (br, bi, z, gi, 0, 0, 0))
    return pl.pallas_call(
        _nsa_prompt_kernel,
        grid=(b, g, nt),
        in_specs=[pl.BlockSpec((1, tq, GROUP_W), lambda bi, gi, i: (bi, i, COL_Q // GROUP_W + gi)),
                  per_bg(nc, d), per_bg(nc, d),
                  pl.BlockSpec((1, 1, nc, rows), lambda bi, gi, i: (gi, i, 0, 0)),
                  kv(0, 0), kv(0, 1), kv(1, 0), kv(1, 1),
                  pl.BlockSpec((1, 2, rows, tq), lambda bi, gi, i: (gi, 0, 0, 0)),
                  pl.BlockSpec((nc // 2, t), lambda bi, gi, i: (0, 0)),
                  pl.BlockSpec((1, 1, tq, 128), lambda bi, gi, i: (bi, gi, i, 0))],
        out_specs=pl.BlockSpec((1, tq, GROUP_W), lambda bi, gi, i: (bi, i, gi)),
        out_shape=jax.ShapeDtypeStruct((b, t, Q_W), F32),
        scratch_shapes=[pltpu.VMEM((rows, t), F32), pltpu.VMEM((rows, t), BF16),
                        pltpu.VMEM((rows, tq), F32),
                        pltpu.VMEM((nt, tq, tq), F32), pltpu.VMEM((rows, d), F32), pltpu.VMEM((rows, d), F32)],
        compiler_params=_cparams(("parallel", "parallel", "arbitrary")),
        name="nsa_prompt",
    )(pr3, kc, vc, bias_c, kvt, kvt, kvt, kvt, bias_t, expand, gts)


SMP_CPAD = 128
KEY_TILE = PAGE_SIZE


def _masked_softmax(s, ok):
    s = jnp.where(ok, s, -jnp.inf)
    m = jnp.max(s, axis=-1, keepdims=True)
    m = jnp.where(m > -jnp.inf, m, 0.0)
    p = jnp.exp(s - m)
    return p / jnp.maximum(jnp.sum(p, axis=-1, keepdims=True), 1e-30)


def _nsa_sample_kernel(pt_ref, q_ref, kc_ref, vc_ref, bc_ref, okc_ref, cur_ref, *rest, nb, n_pages, nsel):
    pages = rest[:nb * n_pages]
    (sn_ref, bs_ref, oks_ref, e_ref, wb_ref, wn_ref, bw_ref, okw_ref, gt_ref, acc_ref,
     o_ref, wo_ref) = rest[nb * n_pages:]
    del pt_ref, acc_ref
    d, hpg = NSA_HEAD_DIM, NSA_HPG
    rows = q_ref.shape[2]
    s_new = rows // hpg
    chains = [(n, g) for n in range(nb) for g in range(NSA_KV_HEADS)]
    bf = lambda x: x.astype(BF16)
    stack = lambda parts: jnp.concatenate(parts, axis=0)
    tile = lambda x: stack([x] * len(chains))
    part = lambda x, c: x[c * rows:(c + 1) * rows]
    qs = [bf(q_ref[n, g] * (d ** -0.5)) for n, g in chains]

    s_c = stack([_dot_nt(q, bf(kc_ref[n, g])) + bc_ref[g] for q, (n, g) in zip(qs, chains)])
    p_c = _masked_softmax(s_c, tile(okc_ref[...]) > 0.5)
    o_c = stack([_dot(bf(part(p_c, c)), bf(vc_ref[n, g])) for c, (n, g) in enumerate(chains)])

    ph = []
    for c in range(len(chains)):
        acc = p_c[c * rows:c * rows + s_new]
        for h in range(1, hpg):
            acc = acc + p_c[c * rows + h * s_new:c * rows + (h + 1) * s_new]
        ph.append(acc)
    ph = stack(ph)
    imp = ph[:, 0:SMP_CPAD] + ph[:, SMP_CPAD:2 * SMP_CPAD]
    blk = lax.broadcasted_iota(jnp.int32, imp.shape, 1)
    cur = tile(cur_ref[...])
    forced = (blk == 0) | (blk == cur) | (blk == cur - 1)
    imp = jnp.where(forced, FORCE_SCORE, jnp.where(blk <= cur, imp, -jnp.inf))
    rank = jnp.zeros(imp.shape, jnp.int32)
    for s2 in range(nsel):
        col = imp[:, s2:s2 + 1]
        beats = (col > imp) | ((col == imp) & (blk > s2))
        rank = rank + beats.astype(jnp.int32)
    chosen = bf((rank < N_SELECT) & (blk < nsel))
    chosen = stack([chosen[c * s_new:(c + 1) * s_new] for c in range(len(chains)) for _ in range(hpg)])
    sel_keys = _dot(chosen, e_ref[...])

    def kv_tiles(n, g, z):
        return [pg[0, 0, z, g] for pg in pages[n * n_pages:(n + 1) * n_pages]] + [sn_ref[n, z, g]]

    s_s = stack([jnp.concatenate([_dot(q, bf(kt)) for kt in kv_tiles(n, g, 0)], axis=1) + bs_ref[g]
                 for q, (n, g) in zip(qs, chains)])
    p_s = bf(_masked_softmax(s_s, (tile(oks_ref[...]) > 0.5) & (sel_keys > 0.5)))
    o_s = []
    for c, (n, g) in enumerate(chains):
        pc = part(p_s, c)
        acc = None
        for p, vt in enumerate(kv_tiles(n, g, 1)):
            term = _dot_nt(pc[:, p * KEY_TILE:(p + 1) * KEY_TILE], bf(vt))
            acc = term if acc is None else acc + term
        o_s.append(acc)
    o_s = stack(o_s)

    wkeys = wb_ref.shape[-1]
    s_w = stack([jnp.concatenate([_dot(q, bf(wb_ref[0, n, 0, g])), _dot(q, bf(wn_ref[n, 0, g]))], axis=1)
                 + bw_ref[g] for q, (n, g) in zip(qs, chains)])
    p_w = bf(_masked_softmax(s_w, tile(okw_ref[...]) > 0.5))
    o_w = stack([_dot_nt(part(p_w, c)[:, 0:wkeys], bf(wb_ref[0, n, 1, g]))
                 + _dot_nt(part(p_w, c)[:, wkeys:wkeys + KEY_TILE], bf(wn_ref[n, 1, g]))
                 for c, (n, g) in enumerate(chains)])

    gt = jax.nn.sigmoid(stack([gt_ref[n, g] for n, g in chains]))
    out = gt[:, 0:1] * o_c + gt[:, 1:2] * o_s + gt[:, 2:3] * o_w
    for c, (n, g) in enumerate(chains):
        o_ref[n, g] = part(out, c)

    lane = lax.broadcasted_iota(jnp.int32, (2 * NSA_KV_HEADS * d, KEY_TILE), 1)
    for n in range(nb):
        old = pltpu.roll(wb_ref[0, n].reshape(-1, wkeys), wkeys - s_new, 1)
        new = pltpu.roll(wn_ref[n].reshape(-1, KEY_TILE), KEY_TILE - s_new, 1)
        tail = jnp.where(lane >= KEY_TILE - s_new, new, old[:, wkeys - KEY_TILE:])
        wo_ref[0, n] = jnp.concatenate([old[:, 0:wkeys - KEY_TILE], tail], axis=1).reshape(wo_ref.shape[2:])


SMP_SEQS = 4


def _nsa_sample(layer, nsel, page_table, qg, kc, vc, bias_c, ok_c, cur, cache_sel, sel_new, bias_s, ok_s,
                expand, win_state, win_new, bias_w, ok_w, gts, win_acc):
    b = qg.shape[0]
    g = NSA_KV_HEADS
    d = NSA_HEAD_DIM
    rows = qg.shape[2]
    n_pages = page_table.shape[1]
    wkeys = win_state.shape[-1]
    ks = (n_pages + 1) * KEY_TILE
    kw = wkeys + KEY_TILE
    nb = SMP_SEQS if b % SMP_SEQS == 0 else 1
    per_b = lambda *shape: pl.BlockSpec((nb,) + shape, lambda i, pt: (i,) + (0,) * len(shape))
    const = lambda *shape: pl.BlockSpec(shape, lambda i, pt: (0,) * len(shape))

    def page_spec(n, p):
        return pl.BlockSpec((1, 1, 2, g, d, PAGE_SIZE), lambda i, pt: (layer, pt[nb * i + n, p], 0, 0, 0, 0))

    win_spec = pl.BlockSpec((1, nb, 2, g, d, wkeys), lambda i, pt: (layer, i, 0, 0, 0, 0))
    in_specs = ([per_b(g, rows, d), per_b(g, 2 * SMP_CPAD, d), per_b(g, 2 * SMP_CPAD, d),
                 const(g, rows, 2 * SMP_CPAD), const(rows, 2 * SMP_CPAD), const(rows // NSA_HPG, SMP_CPAD)]
                + [page_spec(n, p) for n in range(nb) for p in range(n_pages)]
                + [per_b(2, g, d, KEY_TILE), const(g, rows, ks), const(rows, ks), const(SMP_CPAD, ks),
                   win_spec, per_b(2, g, d, KEY_TILE), const(g, rows, kw), const(rows, kw), per_b(g, rows, 128),
                   pl.BlockSpec(memory_space=pl.ANY)])
    operands = (page_table, qg, kc, vc, bias_c, ok_c, cur, *([cache_sel] * (nb * n_pages)), sel_new, bias_s, ok_s,
                expand, win_state, win_new, bias_w, ok_w, gts, win_acc)
    return pl.pallas_call(
        functools.partial(_nsa_sample_kernel, nb=nb, n_pages=n_pages, nsel=nsel),
        grid_spec=pltpu.PrefetchScalarGridSpec(
            num_scalar_prefetch=1,
            grid=(b // nb,),
            in_specs=in_specs,
            out_specs=[pl.BlockSpec((nb, g, rows, d), lambda i, pt: (i, 0, 0, 0)), win_spec]),
        out_shape=[jax.ShapeDtypeStruct((b, g, rows, d), F32), jax.ShapeDtypeStruct(win_acc.shape, F32)],
        input_output_aliases={len(operands) - 1: 1},
        compiler_params=_cparams(("parallel",)),
        name="nsa_sample",
    )(*operands)


def _pad_rows(a, rows):
    return jnp.pad(a, ((0, 0), (0, rows - a.shape[1]), (0, 0)))


def _even_odd(n):
    return np.concatenate([np.arange(0, n, 2), np.arange(1, n, 2)])


def _prompt_bias_tables(rel_bias, t):
    g, hpg, tq = NSA_KV_HEADS, NSA_HPG, NSA_TQ
    nt = t // tq
    nc = t // CMP_BLOCK
    end_c = (_even_odd(nc) + 1) * CMP_BLOCK - 1
    dist_c = jnp.asarray(np.arange(t)[None, :] - end_c[:, None], jnp.int32)
    bias_c = _bias_lookup(rel_bias, dist_c, nc)
    bias_c = bias_c.reshape(g, hpg, nc, nt, tq).transpose(0, 3, 2, 1, 4).reshape(g, nt, nc, hpg * tq) * LOG2E
    off = np.arange(3)[:, None, None] * tq
    dist_t = off + np.arange(tq)[None, :, None] - np.arange(tq)[None, None, :]
    assert dist_t[2].min() >= REL_MAX_DIST
    bias_t = _bias_lookup(rel_bias, jnp.asarray(dist_t.reshape(3 * tq, tq), jnp.int32), 3 * tq)
    bias_t = bias_t.reshape(g, hpg, 3, tq, tq).transpose(0, 2, 1, 3, 4).reshape(g, 3, hpg * tq, tq)
    bias_t = (bias_t[:, 0:2] - bias_t[:, 2:3]) * LOG2E
    expand = np.arange(t)[None, :] // SEL_BLOCK == np.arange(t // SEL_BLOCK)[:, None]
    return bias_c, bias_t, jnp.asarray(expand, BF16)


def _sample_tables(rel_bias, past_len, s_new, n_win_keys):
    g, hpg = NSA_KV_HEADS, NSA_HPG
    tk = -(-(past_len + s_new) // SEL_BLOCK) * SEL_BLOCK
    nc = tk // CMP_BLOCK
    nsel = tk // SEL_BLOCK
    half = (nc + 1) // 2
    pos_q = past_len + np.arange(s_new)
    lane = np.arange(SMP_CPAD)
    n_of_lane = np.concatenate([2 * lane, 2 * lane + 1])
    real_c = np.concatenate([lane < half, lane < nc - half])
    dist_c = pos_q[:, None] - ((n_of_lane[None, :] + 1) * CMP_BLOCK - 1)
    ok_c = real_c[None, :] & (dist_c >= 0)
    n_keys_s = (past_len // PAGE_SIZE + 1) * KEY_TILE
    key = np.arange(n_keys_s)
    dist_s = pos_q[:, None] - key[None, :]
    ok_s = (key[None, :] < past_len + s_new) & (dist_s >= 0)
    n_keys_w = n_win_keys + KEY_TILE
    i = np.arange(n_keys_w)
    pos_kw = np.where(i < n_win_keys, past_len - n_win_keys + i, past_len + i - n_win_keys)
    dist_w = pos_q[:, None] - pos_kw[None, :]
    ok_w = (i[None, :] < n_win_keys + s_new) & (dist_w >= 0) & (dist_w < WINDOW) & (pos_kw[None, :] >= 0)
    dist = np.concatenate([dist_c, dist_s, dist_w], axis=1)
    dist = np.pad(dist, ((0, 8 - s_new), (0, 0)))
    bias = _bias_lookup(rel_bias, jnp.asarray(dist, jnp.int32), 8)[:, :s_new]
    bias = bias.reshape(g, hpg * s_new, dist.shape[1])
    c0, c1 = 2 * SMP_CPAD, 2 * SMP_CPAD + n_keys_s
    tile = lambda m: jnp.asarray(np.tile(m, (hpg, 1)), F32)
    expand = (key[None, :] // SEL_BLOCK == np.arange(SMP_CPAD)[:, None]) & (np.arange(SMP_CPAD)[:, None] < nsel)
    cur = np.broadcast_to((pos_q // SEL_BLOCK)[:, None], (s_new, SMP_CPAD))
    return dict(bias_c=bias[:, :, :c0], bias_s=bias[:, :, c0:c1], bias_w=bias[:, :, c1:],
                ok_c=tile(ok_c), ok_s=tile(ok_s), ok_w=tile(ok_w),
                expand=jnp.asarray(expand, BF16), cur=jnp.asarray(cur, jnp.int32), nc=nc, half=half, tk=tk)


def _flat_blocks(rows):
    b, tk = rows.shape[:2]
    nc = tk // CMP_BLOCK
    blk = rows.reshape(b, nc // 2, 2, CMP_BLOCK, 2, NSA_KV_HEADS, NSA_HEAD_DIM)
    return jnp.transpose(blk, (4, 0, 5, 2, 1, 3, 6)).reshape(2, b * NSA_KV_HEADS * nc, CMP_BLOCK * NSA_HEAD_DIM)


def _nsa_prompt_layer(pr3, lw, tabs):
    b, t = pr3.shape[:2]
    g, hpg, d, tq = NSA_KV_HEADS, NSA_HPG, NSA_HEAD_DIM, NSA_TQ
    nt = t // tq
    nc = t // CMP_BLOCK
    kv = pr3[:, :, COL_CMP:COL_CMP + 3 * KV_W].reshape(b, t, 3, 2, g, d)
    cmp_rows, sel_rows, win_rows = kv[:, :, 0], kv[:, :, 1], kv[:, :, 2]
    kcv = _compress(_flat_blocks(cmp_rows), lw['pe'], lw['cw1'], lw['cw2'], 512)
    kcv = kcv.reshape(2, b, g, nc, d).astype(BF16)
    kvt = kv[:, :, 1:].reshape(b, nt, tq, 2, 2, g, d).transpose(3, 0, 4, 5, 1, 6, 2).astype(BF16)
    gts = pr3[:, :, COL_MISC:COL_MISC + N_GATE].reshape(b, t, 3, g, hpg).transpose(0, 3, 1, 2, 4)
    gts = jnp.pad(gts.reshape(b, g, t, 3 * hpg), ((0, 0), (0, 0), (0, 0), (0, 128 - 3 * hpg)))
    o = _nsa_prompt(pr3, kcv[0], kcv[1], tabs[0], kvt, tabs[1], tabs[2], gts)
    return o.reshape(b * t, Q_W), cmp_rows, sel_rows, win_rows[:, t - min(WINDOW, t):]


def _to_native(rows, keys):
    b, n = rows.shape[:2]
    r = rows.reshape(b, n, 2, NSA_KV_HEADS, NSA_HEAD_DIM).transpose(0, 2, 3, 4, 1)
    return jnp.pad(r, ((0, 0),) * 4 + ((0, keys - n),))


def _nsa_sample_layer(pr3, lw, tabs, layer, cache_cmp, cache_sel, win_state, page_table, win_acc):
    b, s_new = pr3.shape[:2]
    g, hpg, d = NSA_KV_HEADS, NSA_HPG, NSA_HEAD_DIM
    n_pages = page_table.shape[1]
    past_len = n_pages * PAGE_SIZE
    new = lambda c: pr3[:, :, c:c + KV_W]
    cmp_new, sel_new, win_new = new(COL_CMP), new(COL_SEL), new(COL_WIN)
    n_past, n_tail = past_len // CMP_BLOCK, tabs['nc'] - past_len // CMP_BLOCK
    pe3 = lw['pe'].reshape(2, CMP_BLOCK, d)
    past = _compress_pages(layer, page_table, cache_cmp, pe3, lw['cw1'].reshape(2, CMP_BLOCK, d, d), lw['cw2'])
    tail = jnp.pad(cmp_new, ((0, 0), (0, n_tail * CMP_BLOCK - s_new), (0, 0)))
    tail = _compress(_flat_blocks(tail.reshape(b, n_tail * CMP_BLOCK, 2, g, d)), lw['pe'], lw['cw1'], lw['cw2'], 512)
    tail = tail.reshape(2, b, g, 2, n_tail // 2, d).transpose(1, 0, 2, 3, 4, 5)
    kcv = jnp.concatenate([past, tail], axis=4)
    kcv = jnp.pad(kcv, ((0, 0),) * 4 + ((0, SMP_CPAD - kcv.shape[4]), (0, 0)))
    kcv = kcv.reshape(b, 2, g, 2 * SMP_CPAD, d).transpose(1, 0, 2, 3, 4)
    qg = pr3[:, :, COL_Q:COL_Q + Q_W].reshape(b, s_new, g, hpg, d).transpose(0, 2, 3, 1, 4)
    qg = qg.reshape(b, g, hpg * s_new, d)
    gts = pr3[:, :, COL_MISC:COL_MISC + N_GATE].reshape(b, s_new, 3, g, hpg).transpose(0, 3, 4, 1, 2)
    gts = jnp.pad(gts.reshape(b, g, hpg * s_new, 3), ((0, 0), (0, 0), (0, 0), (0, 128 - 3)))
    o, win_acc = _nsa_sample(layer, tabs['tk'] // SEL_BLOCK, page_table, qg, kcv[0], kcv[1], tabs['bias_c'], tabs['ok_c'],
                    tabs['cur'], cache_sel, _to_native(sel_new, KEY_TILE), tabs['bias_s'], tabs['ok_s'],
                    tabs['expand'], win_state, _to_native(win_new, KEY_TILE), tabs['bias_w'], tabs['ok_w'], gts,
                    win_acc)
    o = o.reshape(b, g, hpg, s_new, d).transpose(0, 3, 1, 2, 4).reshape(b * s_new, Q_W)
    kv5 = lambda a: a.reshape(b, -1, 2, g, d)
    return o, kv5(cmp_new), kv5(sel_new), win_acc


def _mixers(pr, b, t, lw, conv0, h0, s0, nsa_fn):
    pr3 = pr.reshape(b, t, PROJ_COLS)
    o_a, cmp_rows, sel_rows, win_rows = nsa_fn(pr3)
    lru_w = (lw['lcw'], lw['lcb'], lw['lgw'], lw['lgb'], lw['lam'])
    gla_w = (lw['gaw'], lw['gab'], lw['gng'])
    if t % 8 == 0:
        tc = min(t, 256)
        o_b, conv_n, h_n = _lru(pr3, COL_LRU_X // LRU_WIDTH, pr3, COL_LRU_G // LRU_WIDTH, conv0, h0[:, None],
                                *lru_w, tc, tc)
        tg = min(t, 256)
        o_c, s_n = _gla(pr3, COL_GLA_Q // GLA_QK_W, pr3, COL_GLA_K // GLA_QK_W, pr3, COL_GLA_V // GLA_V_W,
                         pr3, COL_GLA_G // GLA_V_W, pr3, COL_MISC // MISC_W, *gla_w, s0, tg,
                         min(tg, GLA_CHUNK), tg)
    else:
        tp = -(-t // 8) * 8
        cut = lambda c, w: _pad_rows(pr3[:, :, c:c + w], tp)
        if CONV_WIDTH - 1 + t <= LRU_SEG:
            o_b, conv_n, h_n = _lru_short(pr3[:, :, COL_LRU_X:COL_LRU_X + LRU_WIDTH],
                                          pr3[:, :, COL_LRU_G:COL_LRU_G + LRU_WIDTH], conv0, h0, *lru_w)
            h_n = h_n[:, None]
        else:
            o_b, conv_n, h_n = _lru(cut(COL_LRU_X, LRU_WIDTH), 0, cut(COL_LRU_G, LRU_WIDTH), 0, conv0,
                                    h0[:, None], *lru_w, tp, t)
        gla_in = (cut(COL_GLA_Q, GLA_QK_W), cut(COL_GLA_K, GLA_QK_W), cut(COL_GLA_V, GLA_V_W),
                  cut(COL_GLA_G, GLA_V_W), cut(COL_MISC, MISC_W))
        if tp == LRU_SEG:
            o_c, s_n = _gla_short(*gla_in, *gla_w, s0, t)
        else:
            o_c, s_n = _gla(*(a for x in gla_in for a in (x, 0)), *gla_w, s0, tp, tp, t)
        o_b, o_c = o_b[:, :t], o_c[:, :t]
    o_b = o_b.reshape(b * t, LRU_WIDTH)
    o_c = o_c.reshape(b * t, GLA_V_W)
    return (o_a, o_b, o_c), (cmp_rows, sel_rows, win_rows, conv_n, h_n[:, 0], s_n)


def _layer(x, b, t, lw, conv0, h0, s0, nsa_fn, tm):
    pr = _proj(x, lw['ng'], lw['w_in'], lw['b_in'], lw['layer'], tm, PROJ_COLS // 4)
    (o_a, o_b, o_c), states = _mixers(pr, b, t, lw, conv0, h0, s0, nsa_fn)
    x = _merge(x, o_a, o_b, o_c, lw['ng'], lw['wg'], lw['bg'], lw['wb'], lw['wo'], lw['layer'], min(tm, 512))
    x = _mlp(x, lw['mg'], lw['w1'], lw['w2'], lw['layer'], tm, 1024)
    return x, states


def kernel(x_prompt, x_sample, cache_nsa_cmp_kv, cache_nsa_sel_kv, state_nsa_win_kv, state_lru_conv,
           state_lru_h, state_gla, page_table, rel_bias, norm_mix_g, norm_mlp_g, norm_final_g, w_in, b_in,
           nsa_cmp_pe, nsa_cmp_w1, nsa_cmp_w2, lru_gate_w, lru_gate_b, lru_lambda, lru_conv_w, lru_conv_b,
           gla_alpha_w, gla_alpha_b, gla_norm_g, w_branch, w_out, mlp_w1, mlp_w2):
    bp, tp = x_prompt.shape[:2]
    bs, ts = x_sample.shape[:2]
    depth = w_in.shape[0]
    n_pages = page_table.shape[1]
    w_buf = state_nsa_win_kv.shape[2]

    cols = [w_in[..., _SRC[n][0]:_SRC[n][0] + _SRC[n][1]] for n in _DST_ORDER]
    pad = PROJ_COLS - sum(c.shape[-1] for c in cols)
    w_in_p = jnp.concatenate(cols + [jnp.zeros(w_in.shape[:2] + (pad,), w_in.dtype)], axis=-1).astype(BF16)
    bcols = [b_in[..., _SRC[n][0]:_SRC[n][0] + _SRC[n][1]] for n in _DST_ORDER]
    b_in_p = jnp.concatenate(bcols + [jnp.zeros((depth, pad), b_in.dtype)], axis=-1)[:, None, :]
    g0, gw = _SRC['merge_gate']
    w_gate = w_in[..., g0:g0 + gw].astype(BF16)
    b_gate = b_in[:, None, g0:g0 + gw]
    eye = jnp.eye(LRU_BLOCKS, dtype=lru_gate_w.dtype)
    lgw = jnp.einsum('lznce,nm->lzncme', lru_gate_w, eye).reshape(depth, 2, LRU_WIDTH, LRU_WIDTH).astype(BF16)
    gaw = jnp.zeros((depth, MISC_W, GLA_QK_W), F32).at[:, N_GATE:N_GATE + GLA_RANK].set(gla_alpha_w).astype(BF16)
    pe = jnp.transpose(nsa_cmp_pe, (0, 2, 1, 3)).reshape(depth, 2, 1, CMP_BLOCK * NSA_HEAD_DIM)
    cw1 = nsa_cmp_w1.astype(BF16)
    cw2 = nsa_cmp_w2.astype(BF16)
    wb = w_branch.astype(BF16)
    wo = w_out.astype(BF16)
    w1 = mlp_w1.astype(BF16)
    w2 = mlp_w2.astype(BF16)
    cache_cmp = jnp.transpose(cache_nsa_cmp_kv, (0, 1, 3, 4, 5, 2))
    cache_sel = jnp.transpose(cache_nsa_sel_kv, (0, 1, 3, 4, 5, 2))
    win_state = jnp.transpose(state_nsa_win_kv, (0, 1, 3, 4, 5, 2))

    tabs_p = _prompt_bias_tables(rel_bias, tp)
    tabs_s = _sample_tables(rel_bias, n_pages * PAGE_SIZE, ts, w_buf)

    xp = x_prompt.reshape(bp * tp, D_MODEL)
    xs = x_sample.reshape(bs * ts, D_MODEL)
    conv0_p = jnp.zeros((bp, CONV_WIDTH - 1, LRU_WIDTH), F32)
    h0_p = jnp.zeros((bp, LRU_WIDTH), F32)
    s0_p = jnp.zeros((bp, GLA_HEADS, GLA_DK, GLA_DV), F32)
    outs_p = [[] for _ in range(6)]
    outs_s = [[] for _ in range(6)]
    win_acc = jnp.zeros(win_state.shape, F32)
    for l in range(depth):
        lw = dict(layer=l, ng=norm_mix_g[l][None], mg=norm_mlp_g[l][None], w_in=w_in_p, b_in=b_in_p,
                  pe=pe[l], cw1=cw1[l], cw2=cw2[l], lcw=lru_conv_w[l], lcb=lru_conv_b[l][None], lgw=lgw[l],
                  lgb=lru_gate_b[l], lam=lru_lambda[l][None], gaw=gaw[l], gab=gla_alpha_b[l][None],
                  gng=gla_norm_g[l][None], wg=w_gate, bg=b_gate, wb=wb, wo=wo, w1=w1, w2=w2)
        xp, st_p = _layer(xp, bp, tp, lw, conv0_p, h0_p, s0_p,
                          functools.partial(_nsa_prompt_layer, lw=lw, tabs=tabs_p), _row_tile(bp * tp, 1024))
        nsa_s = functools.partial(_nsa_sample_layer, lw=lw, tabs=tabs_s, layer=l, cache_cmp=cache_cmp,
                                  cache_sel=cache_sel, win_state=win_state, page_table=page_table, win_acc=win_acc)
        xs, st_s = _layer(xs, bs, ts, lw, state_lru_conv[l], state_lru_h[l], state_gla[l], nsa_s, bs * ts)
        win_acc = st_s[2]
        for j in range(6):
            outs_p[j].append(st_p[j])
            outs_s[j].append(st_s[j])
    y_prompt = _final_norm(xp, norm_final_g[None], _row_tile(bp * tp, 1024)).reshape(bp, tp, D_MODEL)
    y_sample = _final_norm(xs, norm_final_g[None], bs * ts).reshape(bs, ts, D_MODEL)
    st = lambda outs, j: jnp.stack(outs[j])
    win_s = win_acc.transpose(0, 1, 5, 2, 3, 4)
    return (y_prompt, y_sample, st(outs_p, 0), st(outs_s, 0), st(outs_p, 1), st(outs_s, 1),
            st(outs_p, 2), win_s, st(outs_p, 3), st(outs_s, 3), st(outs_p, 4), st(outs_s, 4),
            st(outs_p, 5), st(outs_s, 5))
```
